```python
import math
import jax, jax.numpy as jnp
from jax import lax
import numpy as np

D_MODEL = 1024
BATCH = 1
SEQ = 16384
DEPTH = 1

NSA_HEADS = 8
NSA_KV_GROUPS = 2
NSA_HPG = NSA_HEADS // NSA_KV_GROUPS
NSA_HEAD_DIM = 64
CMP_BLOCK = 32
CMP_STRIDE = 16
CMP_HIDDEN = 256
SLC_BLOCK = 64
SLC_COUNT = 16
WINDOW = 512
Q_BLOCK = 128
ROPE_THETA = 500000.0
ROPE_DIM = NSA_HEAD_DIM // 4
RET_HEADS = 4
RET_HEAD_DIM = 128
RET_CHUNK = 128
RET_ROT_BASE = 10000.0
MLP_HIDDEN = 4 * D_MODEL
NORM_EPS = 1e-6
NEG = -1e30
BIG = 1e30

NSA_Q_W = NSA_HEADS * NSA_HEAD_DIM
NSA_KV_W = NSA_KV_GROUPS * NSA_HEAD_DIM
RET_W = RET_HEADS * RET_HEAD_DIM
IN_SIZES = (NSA_Q_W, NSA_KV_W, NSA_KV_W, NSA_KV_W, NSA_KV_W, NSA_KV_W, NSA_KV_W, 3 * NSA_HEADS,
            RET_W, RET_W, RET_W, RET_W, D_MODEL, D_MODEL)
IN_WIDTH = sum(IN_SIZES)

kernel_name = "hybrid_nsa_retention_gated_block"


def rmsnorm(x, g):
    xf = x.astype(jnp.float32)
    y = xf * lax.rsqrt(jnp.mean(xf * xf, axis=-1, keepdims=True) + NORM_EPS) * g.astype(jnp.float32)
    return y.astype(x.dtype)


def rotary(x, pos, rot_dim, base):
    half = rot_dim // 2
    inv = jnp.exp(-math.log(base) * jnp.arange(half, dtype=jnp.float32) * 2.0 / rot_dim)
    ang = pos.astype(jnp.float32)[..., None] * inv
    cos, sin = jnp.cos(ang)[:, :, None, :], jnp.sin(ang)[:, :, None, :]
    xf = x.astype(jnp.float32)
    x1, x2 = xf[..., :half], xf[..., half:rot_dim]
    out = jnp.concatenate([x1 * cos - x2 * sin, x1 * sin + x2 * cos, xf[..., rot_dim:]], axis=-1)
    return out.astype(x.dtype)


def nsa_mixer(q, kc, vc, ks, vs, kw, vw, gates, pe_k, pe_v, ck_w1, ck_w2, cv_w1, cv_w2):
    S = q.shape[0]
    dt = q.dtype
    G, HPG, dh = NSA_KV_GROUPS, NSA_HPG, NSA_HEAD_DIM
    n_cmp = (S - CMP_BLOCK) // CMP_STRIDE + 1
    n_slc = S // SLC_BLOCK
    n_sel = min(SLC_COUNT, n_slc)
    n_qb = S // Q_BLOCK
    scale = dh ** -0.5

    cmp_idx = jnp.arange(n_cmp)[:, None] * CMP_STRIDE + jnp.arange(CMP_BLOCK)[None, :]

    def compress(k, pe, w1, w2):
        blk = k[cmp_idx] + pe[None, :, None, :]
        blk = blk.transpose(0, 2, 1, 3).reshape(n_cmp, G, CMP_BLOCK * dh)
        return jax.nn.silu(blk @ w1) @ w2

    k_cmp = compress(kc, pe_k, ck_w1, ck_w2)
    v_cmp = compress(vc, pe_v, cv_w1, cv_w2)
    cmp_start = cmp_idx[:, 0]
    cmp_end = cmp_idx[:, -1]
    slc_start = jnp.arange(n_slc) * SLC_BLOCK
    overlap = jnp.clip(jnp.minimum(cmp_start[:, None] + CMP_BLOCK, slc_start[None, :] + SLC_BLOCK)
                       - jnp.maximum(cmp_start[:, None], slc_start[None, :]), 0, None)
    overlap = overlap.astype(jnp.float32) / CMP_BLOCK

    ks_b = ks.reshape(n_slc, SLC_BLOCK, G, dh).transpose(2, 0, 1, 3)
    vs_b = vs.reshape(n_slc, SLC_BLOCK, G, dh).transpose(2, 0, 1, 3)
    kw_pad = jnp.pad(kw, ((WINDOW, 0), (0, 0), (0, 0)))
    vw_pad = jnp.pad(vw, ((WINDOW, 0), (0, 0), (0, 0)))

    qg = q.reshape(S, G, HPG, dh)
    gg = jax.nn.sigmoid(gates.astype(jnp.float32)).reshape(S, G, HPG, 3)
    g_idx = jnp.arange(G)[:, None, None]
    blk_j = jnp.arange(n_slc)

    def block(b):
        s0 = b * Q_BLOCK
        qi = lax.dynamic_slice_in_dim(qg, s0, Q_BLOCK, 0)
        gi = lax.dynamic_slice_in_dim(gg, s0, Q_BLOCK, 0)
        t = s0 + jnp.arange(Q_BLOCK)

        sc = jnp.einsum('qghd,cgd->gqhc', qi, k_cmp).astype(jnp.float32) * scale
        valid = cmp_end[None, :] <= t[:, None]
        p_c = jax.nn.softmax(jnp.where(valid[None, :, None, :], sc, NEG), axis=-1)
        p_c = p_c * jnp.any(valid, axis=-1)[None, :, None, None].astype(jnp.float32)
        o_c = jnp.einsum('gqhc,cgd->qghd', p_c.astype(dt), v_cmp)

        imp = jnp.einsum('gqhc,cj->gqj', p_c, overlap)
        cur = t // SLC_BLOCK
        forced = (blk_j[None] == 0) | (blk_j[None] == cur[:, None]) | (blk_j[None] == cur[:, None] - 1)
        future = blk_j[None] > cur[:, None]
        imp = jnp.where(forced[None], BIG, jnp.where(future[None], NEG, imp))
        _, sel = lax.top_k(imp, n_sel)

        k_sel = ks_b[g_idx, sel].reshape(G, Q_BLOCK, n_sel * SLC_BLOCK, dh)
        v_sel = vs_b[g_idx, sel].reshape(G, Q_BLOCK, n_sel * SLC_BLOCK, dh)
        k_pos = (sel[..., None] * SLC_BLOCK + jnp.arange(SLC_BLOCK)).reshape(G, Q_BLOCK, n_sel * SLC_BLOCK)
        ss = jnp.einsum('qghd,gqkd->gqhk', qi, k_sel).astype(jnp.float32) * scale
        smask = (k_pos <= t[None, :, None])[:, :, None, :]
        p_s = jax.nn.softmax(jnp.where(smask, ss, NEG), axis=-1)
        o_s = jnp.einsum('gqhk,gqkd->qghd', p_s.astype(dt), v_sel)

        k_win = lax.dynamic_slice_in_dim(kw_pad, s0, WINDOW + Q_BLOCK, 0)
        v_win = lax.dynamic_slice_in_dim(vw_pad, s0, WINDOW + Q_BLOCK, 0)
        w_pos = s0 - WINDOW + jnp.arange(WINDOW + Q_BLOCK)
        wmask = (w_pos[None] <= t[:, None]) & (w_pos[None] > t[:, None] - WINDOW) & (w_pos[None] >= 0)
        sw = jnp.einsum('qghd,kgd->gqhk', qi, k_win).astype(jnp.float32) * scale
        p_w = jax.nn.softmax(jnp.where(wmask[None, :, None, :], sw, NEG), axis=-1)
        o_w = jnp.einsum('gqhk,kgd->qghd', p_w.astype(dt), v_win)

        o = gi[..., 0:1] * o_c + gi[..., 1:2] * o_s + gi[..., 2:3] * o_w
        return o.reshape(Q_BLOCK, NSA_HEADS * dh).astype(dt)

    out = lax.map(block, jnp.arange(n_qb))
    return out.reshape(S, NSA_HEADS * dh)


def retention(q, k, v, pos):
    B, S = q.shape[0], q.shape[1]
    H, d, C = RET_HEADS, RET_HEAD_DIM, RET_CHUNK
    q = rotary(q, pos, d, RET_ROT_BASE).astype(jnp.float32)
    k = rotary(k, pos, d, RET_ROT_BASE).astype(jnp.float32) * (d ** -0.5)
    v = v.astype(jnp.float32)
    n_c = S // C

    def chunks(a):
        return a.reshape(B, n_c, C, H, d).transpose(1, 0, 3, 2, 4)

    log_g = jnp.log(1.0 - jnp.exp2(-5.0 - jnp.arange(H, dtype=jnp.float32)))
    n = jnp.arange(C, dtype=jnp.float32)
    rel = n[:, None] - n[None, :]
    dmat = jnp.where(rel >= 0, jnp.exp(log_g[:, None, None] * jnp.maximum(rel, 0.0)), 0.0)
    xi = jnp.exp(log_g[:, None] * (n + 1.0))
    zeta = jnp.exp(log_g[:, None] * (C - 1.0 - n))
    chunk_decay = jnp.exp(log_g * C)

    def step(state, inp):
        qc, kc, vc = inp
        inner = jnp.einsum('bhnm,bhme->bhne', jnp.einsum('bhnd,bhmd->bhnm', qc, kc) * dmat, vc)
        cross = jnp.einsum('bhnd,bhde->bhne', qc, state) * xi[None, :, :, None]
        state = chunk_decay[None, :, None, None] * state + jnp.einsum(
            'bhmd,bhme->bhde', kc * zeta[None, :, :, None], vc)
        return state, inner + cross

    state0 = jnp.zeros((B, H, d, d), jnp.float32)
    _, o = lax.scan(step, state0, (chunks(q), chunks(k), chunks(v)))
    o = o.transpose(1, 0, 3, 2, 4).reshape(B, S, H, d)
    return o * lax.rsqrt(jnp.mean(o * o, axis=-1, keepdims=True) + NORM_EPS)


def setup_inputs(seed: int = 0) -> dict:
    key = jax.random.key(seed)
    ks = jax.random.split(key, 20)
    f32 = jnp.float32
    nrm = lambda k, shape, s: jax.random.normal(k, shape, f32) * s
    L = DEPTH
    x = jax.random.normal(ks[0], (BATCH, SEQ, D_MODEL), f32)
    offset = jax.random.randint(ks[1], (BATCH, 1), 0, 4096, dtype=jnp.int32)
    positions = (offset + jnp.arange(SEQ, dtype=jnp.int32)[None, :]).astype(jnp.int32)
    return {
        "x": x,
        "positions": positions,
        "norm_mix": 1.0 + nrm(ks[2], (L, D_MODEL), 0.02),
        "w_in": nrm(ks[3], (L, D_MODEL, IN_WIDTH), D_MODEL ** -0.5),
        "cmp_pos_k": nrm(ks[4], (L, CMP_BLOCK, NSA_HEAD_DIM), 0.1),
        "cmp_pos_v": nrm(ks[5], (L, CMP_BLOCK, NSA_HEAD_DIM), 0.1),
        "cmp_k_w1": nrm(ks[6], (L, CMP_BLOCK * NSA_HEAD_DIM, CMP_HIDDEN), (CMP_BLOCK * NSA_HEAD_DIM) ** -0.5),
        "cmp_k_w2": nrm(ks[7], (L, CMP_HIDDEN, NSA_HEAD_DIM), CMP_HIDDEN ** -0.5),
        "cmp_v_w1": nrm(ks[8], (L, CMP_BLOCK * NSA_HEAD_DIM, CMP_HIDDEN), (CMP_BLOCK * NSA_HEAD_DIM) ** -0.5),
        "cmp_v_w2": nrm(ks[9], (L, CMP_HIDDEN, NSA_HEAD_DIM), CMP_HIDDEN ** -0.5),
        "w_proj_a": nrm(ks[10], (L, NSA_Q_W, D_MODEL), NSA_Q_W ** -0.5),
        "w_proj_b": nrm(ks[11], (L, RET_W, D_MODEL), RET_W ** -0.5),
        "w_out": nrm(ks[12], (L, D_MODEL, D_MODEL), D_MODEL ** -0.5),
        "norm_mlp": 1.0 + nrm(ks[13], (L, D_MODEL), 0.02),
        "w_up": nrm(ks[14], (L, D_MODEL, MLP_HIDDEN), D_MODEL ** -0.5),
        "w_down": nrm(ks[15], (L, MLP_HIDDEN, D_MODEL), MLP_HIDDEN ** -0.5),
        "norm_final": 1.0 + nrm(ks[16], (D_MODEL,), 0.02),
    }


def reference(x, positions, norm_mix, w_in, cmp_pos_k, cmp_pos_v, cmp_k_w1, cmp_k_w2, cmp_v_w1, cmp_v_w2,
              w_proj_a, w_proj_b, w_out, norm_mlp, w_up, w_down, norm_final):
    B, S, _ = x.shape
    offsets = np.cumsum(IN_SIZES)[:-1].tolist()
    nsa_batched = jax.vmap(nsa_mixer, in_axes=(0,) * 8 + (None,) * 6)
    for layer in range(DEPTH):
        h = rmsnorm(x, norm_mix[layer])
        (q, kc, vc, ks, vs, kw, vw, nsa_g, rq, rk, rv, rg, gate_a, gate_b) = jnp.split(
            h @ w_in[layer], offsets, axis=-1)
        heads = lambda a, n_h, d: a.reshape(B, S, n_h, d)
        q = rotary(heads(q, NSA_HEADS, NSA_HEAD_DIM), positions, ROPE_DIM, ROPE_THETA)
        kc = rotary(heads(kc, NSA_KV_GROUPS, NSA_HEAD_DIM), positions, ROPE_DIM, ROPE_THETA)
        ks = rotary(heads(ks, NSA_KV_GROUPS, NSA_HEAD_DIM), positions, ROPE_DIM, ROPE_THETA)
        kw = rotary(heads(kw, NSA_KV_GROUPS, NSA_HEAD_DIM), positions, ROPE_DIM, ROPE_THETA)
        vc = heads(vc, NSA_KV_GROUPS, NSA_HEAD_DIM)
        vs = heads(vs, NSA_KV_GROUPS, NSA_HEAD_DIM)
        vw = heads(vw, NSA_KV_GROUPS, NSA_HEAD_DIM)
        nsa_out = nsa_batched(q, kc, vc, ks, vs, kw, vw, nsa_g.reshape(B, S, NSA_HEADS, 3),
                              cmp_pos_k[layer], cmp_pos_v[layer], cmp_k_w1[layer], cmp_k_w2[layer],
                              cmp_v_w1[layer], cmp_v_w2[layer])
        ret = retention(heads(rq, RET_HEADS, RET_HEAD_DIM), heads(rk, RET_HEADS, RET_HEAD_DIM),
                        heads(rv, RET_HEADS, RET_HEAD_DIM), positions)
        ret_out = (jax.nn.silu(rg.astype(jnp.float32)) * ret.reshape(B, S, RET_W)).astype(x.dtype)
        mix = (jax.nn.sigmoid(gate_a) * (nsa_out @ w_proj_a[layer])
               + jax.nn.sigmoid(gate_b) * (ret_out @ w_proj_b[layer]))
        x = x + mix @ w_out[layer]
        h = rmsnorm(x, norm_mlp[layer])
        x = x + jnp.square(jax.nn.relu(h @ w_up[layer])) @ w_down[layer]
    return rmsnorm(x, norm_final)
```

```python
import functools
import math

import numpy as np
import jax
import jax.numpy as jnp
from jax import lax
from jax.experimental import pallas as pl
from jax.experimental.pallas import tpu as pltpu

F32 = jnp.float32
BF16 = jnp.bfloat16

D_MODEL = 1024
NSA_HEADS = 8
NSA_GROUPS = 2
HPG = NSA_HEADS // NSA_GROUPS
DH = 64
CMP_BLOCK = 32
CMP_STRIDE = 16
CMP_HIDDEN = 256
SLC_BLOCK = 64
SLC_COUNT = 16
WINDOW = 512
QB = 128
ROPE_THETA = 500000.0
ROPE_DIM = DH // 4
RET_HEADS = 4
RET_D = 128
RET_C = 128
RET_BASE = 10000.0
MLP_HIDDEN = 4 * D_MODEL
EPS = 1e-6
NEG = -1e30
BIG = 1e30

LANES = 128
TK = 512
BLK_PER_TILE = TK // SLC_BLOCK
VROWS = 80
WIN_KEYS = WINDOW + QB
VMEM_LIMIT = 56 * 1024 * 1024

NT_DIMS = (((1,), (1,)), ((), ()))
TN_DIMS = (((0,), (0,)), ((), ()))


def _sigmoid(v):
    return 1.0 / (1.0 + jnp.exp(-v))


def _rms(x, g):
    return x * lax.rsqrt(jnp.mean(x * x, axis=-1, keepdims=True) + EPS) * g


def _params(sem):
    return pltpu.CompilerParams(dimension_semantics=sem, vmem_limit_bytes=VMEM_LIMIT)


_C_ROT = (0, 896)
_C_V = (896, 1280)
_C_G = (1280, 1408)
_C_RQ = (1408, 1920)
_C_RK = (1920, 2432)
_C_RV = (2432, 2944)
_C_RG = (2944, 3456)
PROJ_W = 3456


def _proj_kernel(x_ref, g_ref, w_ref, cn_ref, sa_ref, sb_ref, cr_ref, sr_ref,
                 q_ref, kcvc_ref, ks_ref, kw_ref, vsvw_ref, gate_ref, rq_ref, rk_ref, rv_ref, rg_ref,
                 *, tm):
    h = _rms(x_ref[...], g_ref[...]).astype(BF16)
    cn, sa, sb = cn_ref[...], sa_ref[...], sb_ref[...]
    cr, sr = cr_ref[...], sr_ref[...]

    def dot(c):
        return jnp.dot(h, w_ref[:, c[0]:c[1]], preferred_element_type=F32)

    def tile(y, i):
        return y[:, i * LANES:(i + 1) * LANES]

    def rope_n(t):
        return t * cn + pltpu.roll(t, LANES - ROPE_DIM // 2, 1) * sa + pltpu.roll(t, ROPE_DIM // 2, 1) * sb

    def rope_r(t):
        return t * cr + pltpu.roll(t, RET_D // 2, 1) * sr

    lane = lax.broadcasted_iota(jnp.int32, (tm, LANES), 1)
    lo = lane < DH

    def split(t):
        return jnp.where(lo, t, 0.0), jnp.where(lo, pltpu.roll(t, DH, 1), 0.0)

    yr = dot(_C_ROT)
    for i in range(4):
        a, b = split(rope_n(tile(yr, i)) * (DH ** -0.5))
        q_ref[2 * i] = a.astype(BF16)
        q_ref[2 * i + 1] = b.astype(BF16)
    kcvc_ref[:, 0:LANES] = rope_n(tile(yr, 4)).astype(BF16)
    row = pl.program_id(0) * tm + lax.broadcasted_iota(jnp.int32, (tm, LANES), 0)
    onehot = jnp.where((lane - DH) == ((row >> 6) & (BLK_PER_TILE - 1)), 1.0, 0.0)
    a, b = split(rope_n(tile(yr, 5)))
    ks_ref[0] = jnp.where(lo, a, onehot).astype(BF16)
    ks_ref[1] = jnp.where(lo, b, onehot).astype(BF16)
    a, b = split(rope_n(tile(yr, 6)))
    kw_ref[0] = a.astype(BF16)
    kw_ref[1] = b.astype(BF16)

    yv = dot(_C_V)
    kcvc_ref[:, LANES:2 * LANES] = tile(yv, 0).astype(BF16)
    vsvw_ref[...] = yv[:, LANES:3 * LANES].astype(BF16)
    gate_ref[...] = _sigmoid(dot(_C_G))

    yq = dot(_C_RQ)
    yk = dot(_C_RK)
    for i in range(RET_HEADS):
        sl = slice(i * LANES, (i + 1) * LANES)
        rq_ref[:, sl] = rope_r(tile(yq, i)).astype(BF16)
        rk_ref[:, sl] = (rope_r(tile(yk, i)) * (RET_D ** -0.5)).astype(BF16)
    rv_ref[...] = dot(_C_RV).astype(BF16)
    rg_ref[...] = dot(_C_RG).astype(BF16)


def _proj(x, g, w, tables, tm=512):
    S = x.shape[0]
    row = lambda i: (i, 0)
    const = lambda i: (0, 0)
    tab = pl.BlockSpec((tm, LANES), row)
    out_shape = (
        jax.ShapeDtypeStruct((NSA_HEADS, S, LANES), BF16),
        jax.ShapeDtypeStruct((S, 2 * LANES), BF16),
        jax.ShapeDtypeStruct((NSA_GROUPS, S, LANES), BF16),
        jax.ShapeDtypeStruct((NSA_GROUPS, S, LANES), BF16),
        jax.ShapeDtypeStruct((S, 2 * LANES), BF16),
        jax.ShapeDtypeStruct((S, LANES), F32),
        jax.ShapeDtypeStruct((S, RET_HEADS * RET_D), BF16),
        jax.ShapeDtypeStruct((S, RET_HEADS * RET_D), BF16),
        jax.ShapeDtypeStruct((S, RET_HEADS * RET_D), BF16),
        jax.ShapeDtypeStruct((S, RET_HEADS * RET_D), BF16),
    )
    out_specs = (
        pl.BlockSpec((NSA_HEADS, tm, LANES), lambda i: (0, i, 0)),
        pl.BlockSpec((tm, 2 * LANES), row),
        pl.BlockSpec((NSA_GROUPS, tm, LANES), lambda i: (0, i, 0)),
        pl.BlockSpec((NSA_GROUPS, tm, LANES), lambda i: (0, i, 0)),
        pl.BlockSpec((tm, 2 * LANES), row),
        pl.BlockSpec((tm, LANES), row),
        pl.BlockSpec((tm, 512), row),
        pl.BlockSpec((tm, 512), row),
        pl.BlockSpec((tm, 512), row),
        pl.BlockSpec((tm, 512), row),
    )
    return pl.pallas_call(
        functools.partial(_proj_kernel, tm=tm),
        grid=(S // tm,),
        in_specs=[pl.BlockSpec((tm, D_MODEL), row), pl.BlockSpec((1, D_MODEL), const),
                  pl.BlockSpec((D_MODEL, PROJ_W), const), tab, tab, tab, tab, tab],
        out_specs=out_specs, out_shape=out_shape,
        compiler_params=_params(("arbitrary",)), name="proj",
    )(x, g, w, *tables)


def _compress_kernel(x_ref, w1_ref, pe_ref, w2_ref, o_ref):
    x = x_ref[...]
    half = CMP_STRIDE * DH
    a = jnp.dot(x, w1_ref[0:half, :], preferred_element_type=F32)
    b = jnp.dot(x, w1_ref[half:2 * half, :], preferred_element_type=F32)
    peb = jnp.dot(pe_ref[...], w1_ref[...], preferred_element_type=F32)[0:1, :]
    hid = a + pltpu.roll(b, x.shape[0] - 1, 0) + peb
    hid = hid * _sigmoid(hid)
    o_ref[...] = jnp.dot(hid.astype(BF16), w2_ref[...], preferred_element_type=F32)


def _compress(xh, w1, pe, w2):
    n_half = xh.shape[1]
    return pl.pallas_call(
        _compress_kernel,
        grid=(2 * NSA_GROUPS,),
        in_specs=[pl.BlockSpec((None, n_half, CMP_STRIDE * DH), lambda i: (i, 0, 0)),
                  pl.BlockSpec((None, CMP_BLOCK * DH, CMP_HIDDEN), lambda i: (i // NSA_GROUPS, 0, 0)),
                  pl.BlockSpec((None, 16, CMP_BLOCK * DH), lambda i: (i // NSA_GROUPS, 0, 0)),
                  pl.BlockSpec((None, CMP_HIDDEN, LANES), lambda i: (i // NSA_GROUPS, 0, 0))],
        out_specs=pl.BlockSpec((None, n_half, LANES), lambda i: (i, 0, 0)),
        out_shape=jax.ShapeDtypeStruct((2 * NSA_GROUPS, n_half, LANES), F32),
        compiler_params=_params(("arbitrary",)), name="compress",
    )(xh, w1, pe, w2)


def _nsa_kernel(q_ref, kcmp_ref, vcmp_ref, ks_ref, vs_ref, kw_ref, vw_ref, gt_ref, ovl_ref, pbig_ref,
                out_ref, baug_ref, m_ref, acc_ref, *, S):
    NC = S // CMP_STRIDE
    NB = S // SLC_BLOCK
    NT = S // TK
    NQ = HPG * QB
    b = pl.program_id(1)
    s0 = b * QB
    q4 = q_ref[...]
    qcat = q4.reshape(NQ, LANES)
    t_row = s0 + (lax.broadcasted_iota(jnp.int32, (1, NQ), 1) & (QB - 1))

    sc = lax.dot_general(kcmp_ref[...], qcat, NT_DIMS, preferred_element_type=F32)
    c_end = lax.broadcasted_iota(jnp.int32, (NC, NQ), 0) * CMP_STRIDE + (CMP_BLOCK - 1)
    sc = jnp.where(c_end <= t_row, sc, NEG)
    pc = jnp.exp(sc - jnp.max(sc, axis=0, keepdims=True))
    any_valid = jnp.where(t_row >= CMP_BLOCK - 1, 1.0, 0.0)
    pc = pc * (any_valid / jnp.sum(pc, axis=0, keepdims=True))
    oc = jnp.dot(vcmp_ref[...], pc.astype(BF16), preferred_element_type=F32)

    p4 = pc[:, 0:QB] + pc[:, QB:2 * QB] + pc[:, 2 * QB:3 * QB] + pc[:, 3 * QB:4 * QB]
    p_hi = p4.astype(BF16)
    p_lo = (p4 - p_hi.astype(F32)).astype(BF16)
    ovl = ovl_ref[...]
    imp = (jnp.dot(ovl, p_hi, preferred_element_type=F32)
           + jnp.dot(ovl, p_lo, preferred_element_type=F32))

    jidx = lax.broadcasted_iota(jnp.int32, (NB, QB), 0)
    cur = (s0 + lax.broadcasted_iota(jnp.int32, (1, QB), 1)) >> 6
    forced = (jidx == 0) | (jidx == cur) | (jidx == cur - 1)
    v = jnp.where(forced, BIG, jnp.where(jidx > cur, NEG, imp))
    bias_t = jnp.full((NB, QB), NEG, F32)
    for _ in range(min(SLC_COUNT, NB)):
        mx = jnp.max(v, axis=0, keepdims=True)
        first = jnp.min(jnp.where(v == mx, jidx, NB), axis=0, keepdims=True)
        hit = jidx == first
        bias_t = jnp.where(hit, 0.0, bias_t)
        v = jnp.where(hit, -jnp.inf, v)
    bias = bias_t.T.astype(BF16)
    baug = jnp.dot(bias, pbig_ref[...], preferred_element_type=F32)
    for kt in range(NT):
        baug_ref[kt] = baug[:, kt * LANES:(kt + 1) * LANES].astype(BF16)

    m_ref[...] = jnp.full((1, NQ), NEG, F32)
    acc_ref[...] = jnp.zeros((VROWS, NQ), F32)

    def key_tile(kt, causal):
        k0 = pl.multiple_of(kt * TK, TK)
        qa = (q4 + baug_ref[kt][None, :, :]).reshape(NQ, LANES)
        s = lax.dot_general(ks_ref[pl.ds(k0, TK), :], qa, NT_DIMS, preferred_element_type=F32)
        if causal:
            kpos = k0 + lax.broadcasted_iota(jnp.int32, (TK, NQ), 0)
            s = jnp.where(kpos <= t_row, s, NEG)
        m_old = m_ref[...]
        m_new = jnp.maximum(m_old, jnp.max(s, axis=0, keepdims=True))
        p = jnp.exp(s - m_new).astype(BF16)
        acc_ref[...] = (jnp.exp(m_old - m_new) * acc_ref[...]
                        + jnp.dot(vs_ref[:, pl.ds(k0, TK)], p, preferred_element_type=F32))
        m_ref[...] = m_new

    n_full = s0 // TK

    def body(kt, carry):
        key_tile(kt, False)
        return carry

    lax.fori_loop(0, n_full, body, 0)
    key_tile(n_full, True)
    acc = acc_ref[...]
    osel = acc[0:DH, :] * (1.0 / acc[DH:DH + 1, :])

    w0 = pl.multiple_of(jnp.maximum(s0 - WINDOW, 0), QB)
    sw = lax.dot_general(kw_ref[pl.ds(w0, WIN_KEYS), :], qcat, NT_DIMS, preferred_element_type=F32)
    kpos = w0 + lax.broadcasted_iota(jnp.int32, (WIN_KEYS, NQ), 0)
    sw = jnp.where((kpos <= t_row) & (kpos > t_row - WINDOW), sw, NEG)
    pw = jnp.exp(sw - jnp.max(sw, axis=0, keepdims=True)).astype(BF16)
    ow = jnp.dot(vw_ref[:, pl.ds(w0, WIN_KEYS)], pw, preferred_element_type=F32)
    owin = ow[0:DH, :] * (1.0 / ow[DH:DH + 1, :])

    heads = []
    for h in range(HPG):
        sl = slice(h * QB, (h + 1) * QB)
        heads.append(gt_ref[3 * h:3 * h + 1, :] * oc[0:DH, sl]
                     + gt_ref[3 * h + 1:3 * h + 2, :] * osel[:, sl]
                     + gt_ref[3 * h + 2:3 * h + 3, :] * owin[:, sl])
    out_ref[...] = jnp.concatenate(heads, axis=0).T.astype(out_ref.dtype)


def _nsa(q, kcmp, vcmp_t, ks, vs_t, kw, vw_t, gt, ovl_t, pbig):
    S = ks.shape[1]
    NC, NB, NT = S // CMP_STRIDE, S // SLC_BLOCK, S // TK
    grp = lambda g, b: (g, 0, 0)
    full2 = lambda g, b: (0, 0)
    return pl.pallas_call(
        functools.partial(_nsa_kernel, S=S),
        grid=(NSA_GROUPS, S // QB),
        in_specs=[pl.BlockSpec((HPG, QB, LANES), lambda g, b: (g, b, 0)),
                  pl.BlockSpec((None, NC, LANES), grp),
                  pl.BlockSpec((None, VROWS, NC), grp),
                  pl.BlockSpec((None, S, LANES), grp),
                  pl.BlockSpec((None, VROWS, S), grp),
                  pl.BlockSpec((None, S, LANES), grp),
                  pl.BlockSpec((None, VROWS, S), grp),
                  pl.BlockSpec((None, 16, QB), lambda g, b: (g, 0, b)),
                  pl.BlockSpec((NB, NC), full2),
                  pl.BlockSpec((NB, NT * LANES), full2)],
        out_specs=pl.BlockSpec((QB, HPG * DH), lambda g, b: (b, g)),
        out_shape=jax.ShapeDtypeStruct((S, NSA_HEADS * DH), BF16),
        scratch_shapes=[pltpu.VMEM((NT, QB, LANES), BF16),
                        pltpu.VMEM((1, HPG * QB), F32),
                        pltpu.VMEM((VROWS, HPG * QB), F32)],
        compiler_params=_params(("arbitrary", "arbitrary")), name="nsa",
    )(q, kcmp, vcmp_t, ks, vs_t, kw, vw_t, gt, ovl_t, pbig)


def _ret_kernel(q_ref, k_ref, v_ref, g_ref, dmat_ref, xi_ref, zeta_ref, dec_ref, o_ref, state_ref, *, tm):
    @pl.when(pl.program_id(0) == 0)
    def _():
        state_ref[...] = jnp.zeros_like(state_ref)

    for j in range(tm // RET_C):
        rows = slice(j * RET_C, (j + 1) * RET_C)
        for h in range(RET_HEADS):
            cols = slice(h * RET_D, (h + 1) * RET_D)
            qc, kc, vc = q_ref[rows, cols], k_ref[rows, cols], v_ref[rows, cols]
            st = state_ref[h]
            st_hi = st.astype(BF16)
            st_lo = (st - st_hi.astype(F32)).astype(BF16)
            sc = lax.dot_general(qc, kc, NT_DIMS, preferred_element_type=F32) * dmat_ref[h]
            inner = jnp.dot(sc.astype(BF16), vc, preferred_element_type=F32)
            cross = (jnp.dot(qc, st_hi, preferred_element_type=F32)
                     + jnp.dot(qc, st_lo, preferred_element_type=F32)) * xi_ref[h]
            kz = (kc.astype(F32) * zeta_ref[h]).astype(BF16)
            state_ref[h] = dec_ref[h] * st + lax.dot_general(kz, vc, TN_DIMS, preferred_element_type=F32)
            o = inner + cross
            o = o * lax.rsqrt(jnp.mean(o * o, axis=-1, keepdims=True) + EPS)
            gate = g_ref[rows, cols].astype(F32)
            o_ref[rows, cols] = (gate * _sigmoid(gate) * o).astype(o_ref.dtype)


def _retention(rq, rk, rv, rg, dmat, xi, zeta, dec, tm=512):
    S = rq.shape[0]
    W = RET_HEADS * RET_D
    row = lambda i: (i, 0)
    cst = lambda i: (0, 0, 0)
    blk = pl.BlockSpec((tm, W), row)
    tbl = pl.BlockSpec((RET_HEADS, RET_C, RET_D), cst)
    return pl.pallas_call(
        functools.partial(_ret_kernel, tm=tm),
        grid=(S // tm,),
        in_specs=[blk, blk, blk, blk, tbl, tbl, tbl, tbl],
        out_specs=blk,
        out_shape=jax.ShapeDtypeStruct((S, W), BF16),
        scratch_shapes=[pltpu.VMEM((RET_HEADS, RET_D, RET_D), F32)],
        compiler_params=_params(("arbitrary",)), name="retention",
    )(rq, rk, rv, rg, dmat, xi, zeta, dec)


def _mix_kernel(x_ref, g_ref, a_ref, r_ref, wg_ref, wpa_ref, wpb_ref, wo_ref, o_ref):
    x = x_ref[...]
    h = _rms(x, g_ref[...]).astype(BF16)
    ga = _sigmoid(jnp.dot(h, wg_ref[:, 0:D_MODEL], preferred_element_type=F32))
    gb = _sigmoid(jnp.dot(h, wg_ref[:, D_MODEL:2 * D_MODEL], preferred_element_type=F32))
    mix = (ga * jnp.dot(a_ref[...], wpa_ref[...], preferred_element_type=F32)
           + gb * jnp.dot(r_ref[...], wpb_ref[...], preferred_element_type=F32))
    o_ref[...] = x + jnp.dot(mix.astype(BF16), wo_ref[...], preferred_element_type=F32)


def _mix(x, g, nsa_out, ret_out, wg, wpa, wpb, wo, tm=512):
    S = x.shape[0]
    row = lambda i: (i, 0)
    cst = lambda i: (0, 0)
    return pl.pallas_call(
        _mix_kernel,
        grid=(S // tm,),
        in_specs=[pl.BlockSpec((tm, D_MODEL), row), pl.BlockSpec((1, D_MODEL), cst),
                  pl.BlockSpec((tm, 512), row), pl.BlockSpec((tm, 512), row),
                  pl.BlockSpec((D_MODEL, 2 * D_MODEL), cst), pl.BlockSpec((512, D_MODEL), cst),
                  pl.BlockSpec((512, D_MODEL), cst), pl.BlockSpec((D_MODEL, D_MODEL), cst)],
        out_specs=pl.BlockSpec((tm, D_MODEL), row),
        out_shape=jax.ShapeDtypeStruct((S, D_MODEL), F32),
        compiler_params=_params(("arbitrary",)), name="mix",
    )(x, g, nsa_out, ret_out, wg, wpa, wpb, wo)


def _mlp_kernel(x_ref, g_ref, wu_ref, wd_ref, gf_ref, o_ref, *, hc):
    x = x_ref[...]
    h = _rms(x, g_ref[...]).astype(BF16)
    acc = x
    for c in range(MLP_HIDDEN // hc):
        u = jnp.maximum(jnp.dot(h, wu_ref[:, c * hc:(c + 1) * hc], preferred_element_type=F32), 0.0)
        acc = acc + jnp.dot((u * u).astype(BF16), wd_ref[c * hc:(c + 1) * hc, :], preferred_element_type=F32)
    o_ref[...] = _rms(acc, gf_ref[...])


def _mlp(x, g, wu, wd, gf, tm=512, hc=1024):
    S = x.shape[0]
    row = lambda i: (i, 0)
    cst = lambda i: (0, 0)
    return pl.pallas_call(
        functools.partial(_mlp_kernel, hc=hc),
        grid=(S // tm,),
        in_specs=[pl.BlockSpec((tm, D_MODEL), row), pl.BlockSpec((1, D_MODEL), cst),
                  pl.BlockSpec((D_MODEL, MLP_HIDDEN), cst), pl.BlockSpec((MLP_HIDDEN, D_MODEL), cst),
                  pl.BlockSpec((1, D_MODEL), cst)],
        out_specs=pl.BlockSpec((tm, D_MODEL), row),
        out_shape=jax.ShapeDtypeStruct((S, D_MODEL), F32),
        compiler_params=_params(("arbitrary",)), name="mlp",
    )(x, g, wu, wd, gf)


def _rope_tables(pos):
    S = pos.shape[0]
    p = pos.astype(F32)[:, None]
    half = ROPE_DIM // 2
    inv = jnp.exp(-math.log(ROPE_THETA) * jnp.arange(half, dtype=F32) * 2.0 / ROPE_DIM)
    ang = p * inv
    cos, sin = jnp.cos(ang), jnp.sin(ang)
    one = jnp.ones((S, DH - ROPE_DIM), F32)
    zero = jnp.zeros((S, DH - ROPE_DIM), F32)
    z8 = jnp.zeros((S, half), F32)
    cn = jnp.tile(jnp.concatenate([cos, cos, one], axis=1), (1, 2))
    sa = jnp.tile(jnp.concatenate([-sin, z8, zero], axis=1), (1, 2))
    sb = jnp.tile(jnp.concatenate([z8, sin, zero], axis=1), (1, 2))
    hr = RET_D // 2
    inv_r = jnp.exp(-math.log(RET_BASE) * jnp.arange(hr, dtype=F32) * 2.0 / RET_D)
    ang_r = p * inv_r
    cr = jnp.concatenate([jnp.cos(ang_r), jnp.cos(ang_r)], axis=1)
    sr = jnp.concatenate([-jnp.sin(ang_r), jnp.sin(ang_r)], axis=1)
    return cn, sa, sb, cr, sr


def _decay_tables():
    H, C = RET_HEADS, RET_C
    log_g = jnp.log(1.0 - jnp.exp2(-5.0 - jnp.arange(H, dtype=F32)))
    n = jnp.arange(C, dtype=F32)
    rel = n[:, None] - n[None, :]
    dmat = jnp.where(rel >= 0, jnp.exp(log_g[:, None, None] * jnp.maximum(rel, 0.0)), 0.0)
    xi = jnp.exp(log_g[:, None] * (n + 1.0))
    zeta = jnp.exp(log_g[:, None] * (C - 1.0 - n))
    dec = jnp.exp(log_g * C)
    bc = lambda a: jnp.broadcast_to(a, (H, C, RET_D))
    return dmat, bc(xi[:, :, None]), bc(zeta[:, :, None]), bc(dec[:, None, None])


def _overlap_t(S):
    n_cmp = (S - CMP_BLOCK) // CMP_STRIDE + 1
    n_slc = S // SLC_BLOCK
    cs = np.arange(n_cmp)[:, None] * CMP_STRIDE
    ss = np.arange(n_slc)[None, :] * SLC_BLOCK
    ov = np.clip(np.minimum(cs + CMP_BLOCK, ss + SLC_BLOCK) - np.maximum(cs, ss), 0, None) / CMP_BLOCK
    out = np.zeros((n_slc, S // CMP_STRIDE), np.float32)
    out[:, :n_cmp] = ov.T
    return jnp.asarray(out, BF16)


def _bias_placement(S):
    n_slc = S // SLC_BLOCK
    out = np.zeros((n_slc, (S // TK) * LANES), np.float32)
    j = np.arange(n_slc)
    out[j, (j // BLK_PER_TILE) * LANES + DH + j % BLK_PER_TILE] = 1.0
    return jnp.asarray(out, BF16)


def _v_rows(v_t):
    S = v_t.shape[1]
    return jnp.concatenate([v_t, jnp.ones((1, S), v_t.dtype), jnp.zeros((VROWS - DH - 1, S), v_t.dtype)], axis=0)


_IN_OFFSETS = np.cumsum((0, 512, 128, 128, 128, 128, 128, 128, 24, 512, 512, 512, 512, 1024, 1024))


def _in_col(wi, i):
    return wi[:, _IN_OFFSETS[i]:_IN_OFFSETS[i + 1]]


def _branches(x, positions, norm_mix, w_in, cmp_pos_k, cmp_pos_v, cmp_k_w1, cmp_k_w2, cmp_v_w1, cmp_v_w2):
    B, S, _ = x.shape
    assert B == 1 and norm_mix.shape[0] == 1 and S % TK == 0 and S >= max(WIN_KEYS, SLC_COUNT * SLC_BLOCK)
    xs = x[0]
    col = functools.partial(_in_col, w_in[0])
    w_a = jnp.concatenate([col(0), col(1), col(3), col(5), col(2), col(4), col(6),
                           col(7), jnp.zeros((D_MODEL, LANES - 24), F32),
                           col(8), col(9), col(10), col(11)], axis=1).astype(BF16)

    tables = _rope_tables(positions[0])
    q, kcvc, ks, kw, vsvw, gates, rq, rk, rv, rg = _proj(xs, norm_mix, w_a, tables)

    n_half = S // CMP_STRIDE
    xh = kcvc.reshape(n_half, CMP_STRIDE, 2 * NSA_GROUPS, DH).transpose(2, 0, 1, 3).reshape(
        2 * NSA_GROUPS, n_half, CMP_STRIDE * DH)
    w1 = jnp.stack([cmp_k_w1[0], cmp_v_w1[0]]).astype(BF16)
    pe = jnp.stack([cmp_pos_k[0], cmp_pos_v[0]]).reshape(2, 1, CMP_BLOCK * DH)
    pe = jnp.broadcast_to(pe, (2, 16, CMP_BLOCK * DH)).astype(BF16)
    w2 = jnp.pad(jnp.stack([cmp_k_w2[0], cmp_v_w2[0]]), ((0, 0), (0, 0), (0, LANES - DH))).astype(BF16)
    cmp = _compress(xh, w1, pe, w2)
    kcmp = cmp[0:NSA_GROUPS].astype(BF16)
    vcmp_t = jax.vmap(_v_rows)(cmp[NSA_GROUPS:, :, 0:DH].transpose(0, 2, 1).astype(BF16))

    vt = vsvw.T.reshape(2 * NSA_GROUPS, DH, S)
    vs_t = jax.vmap(_v_rows)(vt[0:NSA_GROUPS])
    vw_t = jax.vmap(_v_rows)(vt[NSA_GROUPS:])
    gt = jnp.pad(gates[:, 0:3 * NSA_HEADS].T.reshape(NSA_GROUPS, 3 * HPG, S), ((0, 0), (0, 16 - 3 * HPG), (0, 0)))
    nsa_out = _nsa(q, kcmp, vcmp_t, ks, vs_t, kw, vw_t, gt, _overlap_t(S), _bias_placement(S))

    ret_out = _retention(rq, rk, rv, rg, *_decay_tables())
    return nsa_out, ret_out


def kernel(x, positions, norm_mix, w_in, cmp_pos_k, cmp_pos_v, cmp_k_w1, cmp_k_w2, cmp_v_w1, cmp_v_w2,
           w_proj_a, w_proj_b, w_out, norm_mlp, w_up, w_down, norm_final):
    nsa_out, ret_out = _branches(x, positions, norm_mix, w_in, cmp_pos_k, cmp_pos_v,
                                 cmp_k_w1, cmp_k_w2, cmp_v_w1, cmp_v_w2)
    xs = x[0]
    w_gate = jnp.concatenate([_in_col(w_in[0], 12), _in_col(w_in[0], 13)], axis=1).astype(BF16)
    x1 = _mix(xs, norm_mix, nsa_out, ret_out, w_gate, w_proj_a[0].astype(BF16), w_proj_b[0].astype(BF16),
              w_out[0].astype(BF16))
    y = _mlp(x1, norm_mlp, w_up[0].astype(BF16), w_down[0].astype(BF16), norm_final[None, :])
    return y[None]
```

```python
import functools
import math

import numpy as np
import jax
import jax.numpy as jnp
from jax import lax
from jax.experimental import pallas as pl
from jax.experimental.pallas import tpu as pltpu

F32 = jnp.float32
BF16 = jnp.bfloat16

D_MODEL = 1024
NSA_HEADS = 8
NSA_GROUPS = 2
HPG = NSA_HEADS // NSA_GROUPS
DH = 64
CMP_BLOCK = 32
CMP_STRIDE = 16
CMP_HIDDEN = 256
SLC_BLOCK = 64
SLC_COUNT = 16
WINDOW = 512
QB = 128
ROPE_THETA = 500000.0
ROPE_DIM = DH // 4
RET_HEADS = 4
RET_D = 128
RET_C = 128
RET_BASE = 10000.0
MLP_HIDDEN = 4 * D_MODEL
EPS = 1e-6
NEG = -1e30
BIG = 1e30

LANES = 128
TK = 1024
BLK_PER_TILE = TK // SLC_BLOCK
VROWS = 80
WIN_KEYS = WINDOW + QB
VMEM_LIMIT = 56 * 1024 * 1024

NT_DIMS = (((1,), (1,)), ((), ()))
TN_DIMS = (((0,), (0,)), ((), ()))


def _sigmoid(v):
    return 1.0 / (1.0 + jnp.exp(-v))


def _rms(x, g):
    return x * lax.rsqrt(jnp.mean(x * x, axis=-1, keepdims=True) + EPS) * g


def _params(sem):
    return pltpu.CompilerParams(dimension_semantics=sem, vmem_limit_bytes=VMEM_LIMIT)


_C_ROT = (0, 896)
_C_V = (896, 1280)
_C_G = (1280, 1408)
_C_RQ = (1408, 1920)
_C_RK = (1920, 2432)
_C_RV = (2432, 2944)
_C_RG = (2944, 3456)
PROJ_W = 3456


def _proj_kernel(x_ref, g_ref, w_ref, cn_ref, sa_ref, sb_ref, cr_ref, sr_ref,
                 q_ref, kcvc_ref, ks_ref, kw_ref, vsvw_ref, gate_ref, rq_ref, rk_ref, rv_ref, rg_ref,
                 *, tm):
    h = _rms(x_ref[...], g_ref[...]).astype(BF16)
    cn, sa, sb = cn_ref[...], sa_ref[...], sb_ref[...]
    cr, sr = cr_ref[...], sr_ref[...]

    def dot(c):
        return jnp.dot(h, w_ref[:, c[0]:c[1]], preferred_element_type=F32)

    def tile(y, i):
        return y[:, i * LANES:(i + 1) * LANES]

    def rope_n(t):
        return t * cn + pltpu.roll(t, LANES - ROPE_DIM // 2, 1) * sa + pltpu.roll(t, ROPE_DIM // 2, 1) * sb

    def rope_r(t):
        return t * cr + pltpu.roll(t, RET_D // 2, 1) * sr

    lane = lax.broadcasted_iota(jnp.int32, (tm, LANES), 1)
    lo = lane < DH

    def split(t):
        return jnp.where(lo, t, 0.0), jnp.where(lo, pltpu.roll(t, DH, 1), 0.0)

    yr = dot(_C_ROT)
    for i in range(4):
        q_ref[:, i * LANES:(i + 1) * LANES] = (rope_n(tile(yr, i)) * (DH ** -0.5)).astype(BF16)
    kcvc_ref[:, 0:LANES] = rope_n(tile(yr, 4)).astype(BF16)
    row = pl.program_id(0) * tm + lax.broadcasted_iota(jnp.int32, (tm, LANES), 0)
    onehot = jnp.where((lane - DH) == ((row >> 6) & (BLK_PER_TILE - 1)), 1.0, 0.0)
    a, b = split(rope_n(tile(yr, 5)))
    ks_ref[0] = jnp.where(lo, a, onehot).astype(BF16)
    ks_ref[1] = jnp.where(lo, b, onehot).astype(BF16)
    a, b = split(rope_n(tile(yr, 6)))
    kw_ref[0] = a.astype(BF16)
    kw_ref[1] = b.astype(BF16)

    yv = dot(_C_V)
    kcvc_ref[:, LANES:2 * LANES] = tile(yv, 0).astype(BF16)
    vsvw_ref[...] = yv[:, LANES:3 * LANES].astype(BF16)
    gate_ref[...] = _sigmoid(dot(_C_G))

    yq = dot(_C_RQ)
    yk = dot(_C_RK)
    for i in range(RET_HEADS):
        sl = slice(i * LANES, (i + 1) * LANES)
        rq_ref[:, sl] = rope_r(tile(yq, i)).astype(BF16)
        rk_ref[:, sl] = (rope_r(tile(yk, i)) * (RET_D ** -0.5)).astype(BF16)
    rv_ref[...] = dot(_C_RV).astype(BF16)
    rg_ref[...] = dot(_C_RG).astype(BF16)


def _proj(x, g, w, tables, tm=512):
    S = x.shape[0]
    row = lambda i: (i, 0)
    const = lambda i: (0, 0)
    tab = pl.BlockSpec((tm, LANES), row)
    out_shape = (
        jax.ShapeDtypeStruct((S, NSA_HEADS * DH), BF16),
        jax.ShapeDtypeStruct((S, 2 * LANES), BF16),
        jax.ShapeDtypeStruct((NSA_GROUPS, S, LANES), BF16),
        jax.ShapeDtypeStruct((NSA_GROUPS, S, LANES), BF16),
        jax.ShapeDtypeStruct((S, 2 * LANES), BF16),
        jax.ShapeDtypeStruct((S, LANES), F32),
        jax.ShapeDtypeStruct((S, RET_HEADS * RET_D), BF16),
        jax.ShapeDtypeStruct((S, RET_HEADS * RET_D), BF16),
        jax.ShapeDtypeStruct((S, RET_HEADS * RET_D), BF16),
        jax.ShapeDtypeStruct((S, RET_HEADS * RET_D), BF16),
    )
    out_specs = (
        pl.BlockSpec((tm, NSA_HEADS * DH), row),
        pl.BlockSpec((tm, 2 * LANES), row),
        pl.BlockSpec((NSA_GROUPS, tm, LANES), lambda i: (0, i, 0)),
        pl.BlockSpec((NSA_GROUPS, tm, LANES), lambda i: (0, i, 0)),
        pl.BlockSpec((tm, 2 * LANES), row),
        pl.BlockSpec((tm, LANES), row),
        pl.BlockSpec((tm, 512), row),
        pl.BlockSpec((tm, 512), row),
        pl.BlockSpec((tm, 512), row),
        pl.BlockSpec((tm, 512), row),
    )
    return pl.pallas_call(
        functools.partial(_proj_kernel, tm=tm),
        grid=(S // tm,),
        in_specs=[pl.BlockSpec((tm, D_MODEL), row), pl.BlockSpec((1, D_MODEL), const),
                  pl.BlockSpec((D_MODEL, PROJ_W), const), tab, tab, tab, tab, tab],
        out_specs=out_specs, out_shape=out_shape,
        compiler_params=_params(("arbitrary",)), name="proj",
    )(x, g, w, *tables)


def _compress_kernel(x_ref, w1_ref, pe_ref, w2_ref, o_ref):
    x = x_ref[...]
    half = CMP_STRIDE * DH
    a = jnp.dot(x, w1_ref[0:half, :], preferred_element_type=F32)
    b = jnp.dot(x, w1_ref[half:2 * half, :], preferred_element_type=F32)
    peb = jnp.dot(pe_ref[...], w1_ref[...], preferred_element_type=F32)[0:1, :]
    hid = a + pltpu.roll(b, x.shape[0] - 1, 0) + peb
    hid = hid * _sigmoid(hid)
    o_ref[...] = jnp.dot(hid.astype(BF16), w2_ref[...], preferred_element_type=F32)


def _compress(xh, w1, pe, w2):
    n_half = xh.shape[1]
    return pl.pallas_call(
        _compress_kernel,
        grid=(2 * NSA_GROUPS,),
        in_specs=[pl.BlockSpec((None, n_half, CMP_STRIDE * DH), lambda i: (i, 0, 0)),
                  pl.BlockSpec((None, CMP_BLOCK * DH, CMP_HIDDEN), lambda i: (i // NSA_GROUPS, 0, 0)),
                  pl.BlockSpec((None, 16, CMP_BLOCK * DH), lambda i: (i // NSA_GROUPS, 0, 0)),
                  pl.BlockSpec((None, CMP_HIDDEN, LANES), lambda i: (i // NSA_GROUPS, 0, 0))],
        out_specs=pl.BlockSpec((None, n_half, LANES), lambda i: (i, 0, 0)),
        out_shape=jax.ShapeDtypeStruct((2 * NSA_GROUPS, n_half, LANES), F32),
        compiler_params=_params(("arbitrary",)), name="compress",
    )(xh, w1, pe, w2)


def _nsa_kernel(q_ref, kcmp_ref, vcmp_ref, ks_ref, vs_ref, kw_ref, vw_ref, gt_ref, ovl_ref,
                out_ref, qa_ref, bias_ref, m_ref, acc_ref, sa_ref, sb_ref, *, S):
    NC = S // CMP_STRIDE
    NB = S // SLC_BLOCK
    NQ = HPG * QB
    b = pl.program_id(1)
    s0 = b * QB
    t_row = s0 + (lax.broadcasted_iota(jnp.int32, (1, NQ), 1) & (QB - 1))

    qa_ref[0:DH, :] = jnp.concatenate([q_ref[h] for h in range(HPG)], axis=1)
    qa_ref[DH:LANES, :] = jnp.zeros((LANES - DH, NQ), BF16)

    sc = jnp.dot(kcmp_ref[...], qa_ref[...], preferred_element_type=F32)
    c_end = lax.broadcasted_iota(jnp.int32, (NC, NQ), 0) * CMP_STRIDE + (CMP_BLOCK - 1)
    sc = jnp.where(c_end <= t_row, sc, NEG)
    pc = jnp.exp(sc - jnp.max(sc, axis=0, keepdims=True))
    any_valid = jnp.where(t_row >= CMP_BLOCK - 1, 1.0, 0.0)
    pc = pc * (any_valid / jnp.sum(pc, axis=0, keepdims=True))
    oc = jnp.dot(vcmp_ref[...], pc.astype(BF16), preferred_element_type=F32)

    w0 = pl.multiple_of(jnp.maximum(s0 - WINDOW, 0), QB)
    sw = jnp.dot(kw_ref[pl.ds(w0, WIN_KEYS), :], qa_ref[...], preferred_element_type=F32)
    kpos = w0 + lax.broadcasted_iota(jnp.int32, (WIN_KEYS, NQ), 0)
    sw = jnp.where((kpos <= t_row) & (kpos > t_row - WINDOW), sw, NEG)
    pw = jnp.exp(sw - jnp.max(sw, axis=0, keepdims=True)).astype(BF16)
    ow = jnp.dot(vw_ref[:, pl.ds(w0, WIN_KEYS)], pw, preferred_element_type=F32)
    owin = ow[0:DH, :] * (1.0 / ow[DH:DH + 1, :])

    p4 = pc[:, 0:QB] + pc[:, QB:2 * QB] + pc[:, 2 * QB:3 * QB] + pc[:, 3 * QB:4 * QB]
    p_hi = p4.astype(BF16)
    p_lo = (p4 - p_hi.astype(F32)).astype(BF16)
    ovl = ovl_ref[...]
    imp = (jnp.dot(ovl, p_hi, preferred_element_type=F32)
           + jnp.dot(ovl, p_lo, preferred_element_type=F32))

    jidx = lax.broadcasted_iota(jnp.int32, (NB, QB), 0)
    cur = (s0 + lax.broadcasted_iota(jnp.int32, (1, QB), 1)) >> 6
    forced = (jidx == 0) | (jidx == cur) | (jidx == cur - 1)
    cand = jnp.where(forced | (jidx > cur), -jnp.inf, imp)

    def select(break_ties):
        v = cand
        bias_t = jnp.where(forced, 0.0, NEG)
        for _ in range(SLC_COUNT - 3):
            mx = jnp.max(v, axis=0, keepdims=True)
            hit = v == mx
            if break_ties:
                hit = jidx == jnp.min(jnp.where(hit, jidx, NB), axis=0, keepdims=True)
            bias_t = jnp.where(hit, 0.0, bias_t)
            v = jnp.where(hit, -jnp.inf, v)
        return bias_t

    bias_t = select(False)
    n_sel = jnp.sum(jnp.where((bias_t == 0.0) & (jidx <= cur), 1.0, 0.0), axis=0, keepdims=True)
    bias_ref[...] = bias_t.astype(BF16)

    @pl.when(jnp.max(n_sel) > SLC_COUNT)
    def _():
        bias_ref[...] = select(True).astype(BF16)

    m_ref[...] = jnp.full((1, NQ), NEG, F32)
    acc_ref[...] = jnp.zeros((VROWS, NQ), F32)

    def scores(kt, s_ref):
        k0 = pl.multiple_of(kt * TK, TK)
        bias_rows = bias_ref[pl.ds(pl.multiple_of(kt * BLK_PER_TILE, BLK_PER_TILE), BLK_PER_TILE), :]
        qa_ref[DH:DH + BLK_PER_TILE, :] = jnp.concatenate([bias_rows] * HPG, axis=1)
        s_ref[...] = jnp.dot(ks_ref[pl.ds(k0, TK), :], qa_ref[...], preferred_element_type=F32)

    def accumulate(kt, s_ref, causal):
        k0 = pl.multiple_of(kt * TK, TK)
        s = s_ref[...]
        if causal:
            kp = k0 + lax.broadcasted_iota(jnp.int32, (TK, NQ), 0)
            s = jnp.where(kp <= t_row, s, NEG)
        m_old = m_ref[...]
        m_new = jnp.maximum(m_old, jnp.max(s, axis=0, keepdims=True))
        p = jnp.exp(s - m_new).astype(BF16)
        acc_ref[...] = (jnp.exp(m_old - m_new) * acc_ref[...]
                        + jnp.dot(vs_ref[:, pl.ds(k0, TK)], p, preferred_element_type=F32))
        m_ref[...] = m_new

    n_full = s0 // TK
    scores(0, sa_ref)

    def body(kt, carry):
        @pl.when(kt % 2 == 0)
        def _():
            scores(kt + 1, sb_ref)
            accumulate(kt, sa_ref, False)

        @pl.when(kt % 2 == 1)
        def _():
            scores(kt + 1, sa_ref)
            accumulate(kt, sb_ref, False)
        return carry

    lax.fori_loop(0, n_full, body, 0)

    @pl.when(n_full % 2 == 0)
    def _():
        accumulate(n_full, sa_ref, True)

    @pl.when(n_full % 2 == 1)
    def _():
        accumulate(n_full, sb_ref, True)

    acc = acc_ref[...]
    osel = acc[0:DH, :] * (1.0 / acc[DH:DH + 1, :])

    heads = []
    for h in range(HPG):
        sl = slice(h * QB, (h + 1) * QB)
        heads.append(gt_ref[3 * h:3 * h + 1, :] * oc[0:DH, sl]
                     + gt_ref[3 * h + 1:3 * h + 2, :] * osel[:, sl]
                     + gt_ref[3 * h + 2:3 * h + 3, :] * owin[:, sl])
    out_ref[...] = jnp.concatenate(heads, axis=0).T.astype(out_ref.dtype)


def _nsa(q_t, kcmp, vcmp_t, ks, vs_t, kw, vw_t, gt, ovl_t):
    S = ks.shape[1]
    NC, NB = S // CMP_STRIDE, S // SLC_BLOCK
    grp = lambda g, b: (g, 0, 0)
    return pl.pallas_call(
        functools.partial(_nsa_kernel, S=S),
        grid=(NSA_GROUPS, S // QB),
        in_specs=[pl.BlockSpec((HPG, DH, QB), lambda g, b: (g, 0, b)),
                  pl.BlockSpec((None, NC, LANES), grp),
                  pl.BlockSpec((None, VROWS, NC), grp),
                  pl.BlockSpec((None, S, LANES), grp),
                  pl.BlockSpec((None, VROWS, S), grp),
                  pl.BlockSpec((None, S, LANES), grp),
                  pl.BlockSpec((None, VROWS, S), grp),
                  pl.BlockSpec((None, 16, QB), lambda g, b: (g, 0, b)),
                  pl.BlockSpec((NB, NC), lambda g, b: (0, 0))],
        out_specs=pl.BlockSpec((QB, HPG * DH), lambda g, b: (b, g)),
        out_shape=jax.ShapeDtypeStruct((S, NSA_HEADS * DH), BF16),
        scratch_shapes=[pltpu.VMEM((LANES, HPG * QB), BF16),
                        pltpu.VMEM((NB, QB), BF16),
                        pltpu.VMEM((1, HPG * QB), F32),
                        pltpu.VMEM((VROWS, HPG * QB), F32),
                        pltpu.VMEM((TK, HPG * QB), F32),
                        pltpu.VMEM((TK, HPG * QB), F32)],
        compiler_params=_params(("arbitrary", "arbitrary")), name="nsa",
    )(q_t, kcmp, vcmp_t, ks, vs_t, kw, vw_t, gt, ovl_t)


def _ret_kernel(q_ref, k_ref, v_ref, g_ref, dmat_ref, xi_ref, zeta_ref, dec_ref, o_ref, state_ref, *, tm):
    @pl.when(pl.program_id(0) == 0)
    def _():
        state_ref[...] = jnp.zeros_like(state_ref)

    for j in range(tm // RET_C):
        rows = slice(j * RET_C, (j + 1) * RET_C)
        for h in range(RET_HEADS):
            cols = slice(h * RET_D, (h + 1) * RET_D)
            qc, kc, vc = q_ref[rows, cols], k_ref[rows, cols], v_ref[rows, cols]
            st = state_ref[h]
            st_hi = st.astype(BF16)
            st_lo = (st - st_hi.astype(F32)).astype(BF16)
            sc = lax.dot_general(qc, kc, NT_DIMS, preferred_element_type=F32) * dmat_ref[h]
            inner = jnp.dot(sc.astype(BF16), vc, preferred_element_type=F32)
            cross = (jnp.dot(qc, st_hi, preferred_element_type=F32)
                     + jnp.dot(qc, st_lo, preferred_element_type=F32)) * xi_ref[h]
            kz = (kc.astype(F32) * zeta_ref[h]).astype(BF16)
            state_ref[h] = dec_ref[h] * st + lax.dot_general(kz, vc, TN_DIMS, preferred_element_type=F32)
            o = inner + cross
            o = o * lax.rsqrt(jnp.mean(o * o, axis=-1, keepdims=True) + EPS)
            gate = g_ref[rows, cols].astype(F32)
            o_ref[rows, cols] = (gate * _sigmoid(gate) * o).astype(o_ref.dtype)


def _retention(rq, rk, rv, rg, dmat, xi, zeta, dec, tm=512):
    S = rq.shape[0]
    W = RET_HEADS * RET_D
    row = lambda i: (i, 0)
    cst = lambda i: (0, 0, 0)
    blk = pl.BlockSpec((tm, W), row)
    tbl = pl.BlockSpec((RET_HEADS, RET_C, RET_D), cst)
    return pl.pallas_call(
        functools.partial(_ret_kernel, tm=tm),
        grid=(S // tm,),
        in_specs=[blk, blk, blk, blk, tbl, tbl, tbl, tbl],
        out_specs=blk,
        out_shape=jax.ShapeDtypeStruct((S, W), BF16),
        scratch_shapes=[pltpu.VMEM((RET_HEADS, RET_D, RET_D), F32)],
        compiler_params=_params(("arbitrary",)), name="retention",
    )(rq, rk, rv, rg, dmat, xi, zeta, dec)


def _mix_kernel(x_ref, g_ref, a_ref, r_ref, wg_ref, wpa_ref, wpb_ref, wo_ref, o_ref):
    x = x_ref[...]
    h = _rms(x, g_ref[...]).astype(BF16)
    ga = _sigmoid(jnp.dot(h, wg_ref[:, 0:D_MODEL], preferred_element_type=F32))
    gb = _sigmoid(jnp.dot(h, wg_ref[:, D_MODEL:2 * D_MODEL], preferred_element_type=F32))
    mix = (ga * jnp.dot(a_ref[...], wpa_ref[...], preferred_element_type=F32)
           + gb * jnp.dot(r_ref[...], wpb_ref[...], preferred_element_type=F32))
    o_ref[...] = x + jnp.dot(mix.astype(BF16), wo_ref[...], preferred_element_type=F32)


def _mix(x, g, nsa_out, ret_out, wg, wpa, wpb, wo, tm=512):
    S = x.shape[0]
    row = lambda i: (i, 0)
    cst = lambda i: (0, 0)
    return pl.pallas_call(
        _mix_kernel,
        grid=(S // tm,),
        in_specs=[pl.BlockSpec((tm, D_MODEL), row), pl.BlockSpec((1, D_MODEL), cst),
                  pl.BlockSpec((tm, 512), row), pl.BlockSpec((tm, 512), row),
                  pl.BlockSpec((D_MODEL, 2 * D_MODEL), cst), pl.BlockSpec((512, D_MODEL), cst),
                  pl.BlockSpec((512, D_MODEL), cst), pl.BlockSpec((D_MODEL, D_MODEL), cst)],
        out_specs=pl.BlockSpec((tm, D_MODEL), row),
        out_shape=jax.ShapeDtypeStruct((S, D_MODEL), F32),
        compiler_params=_params(("arbitrary",)), name="mix",
    )(x, g, nsa_out, ret_out, wg, wpa, wpb, wo)


def _mlp_kernel(x_ref, g_ref, wu_ref, wd_ref, gf_ref, o_ref, *, hc):
    x = x_ref[...]
    h = _rms(x, g_ref[...]).astype(BF16)
    acc = x
    for c in range(MLP_HIDDEN // hc):
        u = jnp.maximum(jnp.dot(h, wu_ref[:, c * hc:(c + 1) * hc], preferred_element_type=F32), 0.0)
        acc = acc + jnp.dot((u * u).astype(BF16), wd_ref[c * hc:(c + 1) * hc, :], preferred_element_type=F32)
    o_ref[...] = _rms(acc, gf_ref[...])


def _mlp(x, g, wu, wd, gf, tm=512, hc=1024):
    S = x.shape[0]
    row = lambda i: (i, 0)
    cst = lambda i: (0, 0)
    return pl.pallas_call(
        functools.partial(_mlp_kernel, hc=hc),
        grid=(S // tm,),
        in_specs=[pl.BlockSpec((tm, D_MODEL), row), pl.BlockSpec((1, D_MODEL), cst),
                  pl.BlockSpec((D_MODEL, MLP_HIDDEN), cst), pl.BlockSpec((MLP_HIDDEN, D_MODEL), cst),
                  pl.BlockSpec((1, D_MODEL), cst)],
        out_specs=pl.BlockSpec((tm, D_MODEL), row),
        out_shape=jax.ShapeDtypeStruct((S, D_MODEL), F32),
        compiler_params=_params(("arbitrary",)), name="mlp",
    )(x, g, wu, wd, gf)


def _rope_tables(pos):
    S = pos.shape[0]
    p = pos.astype(F32)[:, None]
    half = ROPE_DIM // 2
    inv = jnp.exp(-math.log(ROPE_THETA) * jnp.arange(half, dtype=F32) * 2.0 / ROPE_DIM)
    ang = p * inv
    cos, sin = jnp.cos(ang), jnp.sin(ang)
    one = jnp.ones((S, DH - ROPE_DIM), F32)
    zero = jnp.zeros((S, DH - ROPE_DIM), F32)
    z8 = jnp.zeros((S, half), F32)
    cn = jnp.tile(jnp.concatenate([cos, cos, one], axis=1), (1, 2))
    sa = jnp.tile(jnp.concatenate([-sin, z8, zero], axis=1), (1, 2))
    sb = jnp.tile(jnp.concatenate([z8, sin, zero], axis=1), (1, 2))
    hr = RET_D // 2
    inv_r = jnp.exp(-math.log(RET_BASE) * jnp.arange(hr, dtype=F32) * 2.0 / RET_D)
    ang_r = p * inv_r
    cr = jnp.concatenate([jnp.cos(ang_r), jnp.cos(ang_r)], axis=1)
    sr = jnp.concatenate([-jnp.sin(ang_r), jnp.sin(ang_r)], axis=1)
    return cn, sa, sb, cr, sr


def _decay_tables():
    H, C = RET_HEADS, RET_C
    log_g = jnp.log(1.0 - jnp.exp2(-5.0 - jnp.arange(H, dtype=F32)))
    n = jnp.arange(C, dtype=F32)
    rel = n[:, None] - n[None, :]
    dmat = jnp.where(rel >= 0, jnp.exp(log_g[:, None, None] * jnp.maximum(rel, 0.0)), 0.0)
    xi = jnp.exp(log_g[:, None] * (n + 1.0))
    zeta = jnp.exp(log_g[:, None] * (C - 1.0 - n))
    dec = jnp.exp(log_g * C)
    bc = lambda a: jnp.broadcast_to(a, (H, C, RET_D))
    return dmat, bc(xi[:, :, None]), bc(zeta[:, :, None]), bc(dec[:, None, None])


def _overlap_t(S):
    n_cmp = (S - CMP_BLOCK) // CMP_STRIDE + 1
    n_slc = S // SLC_BLOCK
    cs = np.arange(n_cmp)[:, None] * CMP_STRIDE
    ss = np.arange(n_slc)[None, :] * SLC_BLOCK
    ov = np.clip(np.minimum(cs + CMP_BLOCK, ss + SLC_BLOCK) - np.maximum(cs, ss), 0, None) / CMP_BLOCK
    out = np.zeros((n_slc, S // CMP_STRIDE), np.float32)
    out[:, :n_cmp] = ov.T
    return jnp.asarray(out, BF16)


def _v_rows(v_t):
    S = v_t.shape[1]
    return jnp.concatenate([v_t, jnp.ones((1, S), v_t.dtype), jnp.zeros((VROWS - DH - 1, S), v_t.dtype)], axis=0)


_IN_OFFSETS = np.cumsum((0, 512, 128, 128, 128, 128, 128, 128, 24, 512, 512, 512, 512, 1024, 1024))


def _in_col(wi, i):
    return wi[:, _IN_OFFSETS[i]:_IN_OFFSETS[i + 1]]


def _branches(x, positions, norm_mix, w_in, cmp_pos_k, cmp_pos_v, cmp_k_w1, cmp_k_w2, cmp_v_w1, cmp_v_w2):
    B, S, _ = x.shape
    assert B == 1 and norm_mix.shape[0] == 1 and S % TK == 0 and S >= max(WIN_KEYS, SLC_COUNT * SLC_BLOCK)
    xs = x[0]
    col = functools.partial(_in_col, w_in[0])
    w_a = jnp.concatenate([col(0), col(1), col(3), col(5), col(2), col(4), col(6),
                           col(7), jnp.zeros((D_MODEL, LANES - 24), F32),
                           col(8), col(9), col(10), col(11)], axis=1).astype(BF16)

    tables = _rope_tables(positions[0])
    q, kcvc, ks, kw, vsvw, gates, rq, rk, rv, rg = _proj(xs, norm_mix, w_a, tables)

    n_half = S // CMP_STRIDE
    xh = kcvc.reshape(n_half, CMP_STRIDE, 2 * NSA_GROUPS, DH).transpose(2, 0, 1, 3).reshape(
        2 * NSA_GROUPS, n_half, CMP_STRIDE * DH)
    w1 = jnp.stack([cmp_k_w1[0], cmp_v_w1[0]]).astype(BF16)
    pe = jnp.stack([cmp_pos_k[0], cmp_pos_v[0]]).reshape(2, 1, CMP_BLOCK * DH)
    pe = jnp.broadcast_to(pe, (2, 16, CMP_BLOCK * DH)).astype(BF16)
    w2 = jnp.pad(jnp.stack([cmp_k_w2[0], cmp_v_w2[0]]), ((0, 0), (0, 0), (0, LANES - DH))).astype(BF16)
    cmp = _compress(xh, w1, pe, w2)
    kcmp = cmp[0:NSA_GROUPS].astype(BF16)
    vcmp_t = jax.vmap(_v_rows)(cmp[NSA_GROUPS:, :, 0:DH].transpose(0, 2, 1).astype(BF16))

    vt = vsvw.T.reshape(2 * NSA_GROUPS, DH, S)
    vs_t = jax.vmap(_v_rows)(vt[0:NSA_GROUPS])
    vw_t = jax.vmap(_v_rows)(vt[NSA_GROUPS:])
    gt = jnp.pad(gates[:, 0:3 * NSA_HEADS].T.reshape(NSA_GROUPS, 3 * HPG, S), ((0, 0), (0, 16 - 3 * HPG), (0, 0)))
    q_t = q.T.reshape(NSA_HEADS, DH, S)
    nsa_out = _nsa(q_t, kcmp, vcmp_t, ks, vs_t, kw, vw_t, gt, _overlap_t(S))

    ret_out = _retention(rq, rk, rv, rg, *_decay_tables())
    return nsa_out, ret_out


def kernel(x, positions, norm_mix, w_in, cmp_pos_k, cmp_pos_v, cmp_k_w1, cmp_k_w2, cmp_v_w1, cmp_v_w2,
           w_proj_a, w_proj_b, w_out, norm_mlp, w_up, w_down, norm_final):
    nsa_out, ret_out = _branches(x, positions, norm_mix, w_in, cmp_pos_k, cmp_pos_v,
                                 cmp_k_w1, cmp_k_w2, cmp_v_w1, cmp_v_w2)
    xs = x[0]
    w_gate = jnp.concatenate([_in_col(w_in[0], 12), _in_col(w_in[0], 13)], axis=1).astype(BF16)
    x1 = _mix(xs, norm_mix, nsa_out, ret_out, w_gate, w_proj_a[0].astype(BF16), w_proj_b[0].astype(BF16),
              w_out[0].astype(BF16))
    y = _mlp(x1, norm_mlp, w_up[0].astype(BF16), w_down[0].astype(BF16), norm_final[None, :])
    return y[None]
```

```python
import functools
import math

import numpy as np
import jax
import jax.numpy as jnp
from jax import lax
from jax.experimental import pallas as pl
from jax.experimental.pallas import tpu as pltpu

F32 = jnp.float32
BF16 = jnp.bfloat16

D_MODEL = 1024
NSA_HEADS = 8
NSA_GROUPS = 2
HPG = NSA_HEADS // NSA_GROUPS
DH = 64
CMP_BLOCK = 32
CMP_STRIDE = 16
CMP_HIDDEN = 256
SLC_BLOCK = 64
SLC_COUNT = 16
WINDOW = 512
QB = 128
ROPE_THETA = 500000.0
ROPE_DIM = DH // 4
RET_HEADS = 4
RET_D = 128
RET_C = 128
RET_BASE = 10000.0
MLP_HIDDEN = 4 * D_MODEL
EPS = 1e-6
NEG = -1e30
BIG = 1e30

LANES = 128
TK = 1024
BLK_PER_TILE = TK // SLC_BLOCK
KILL_LANE = DH + BLK_PER_TILE
LOG2E = math.log2(math.e)
VROWS = 80
WIN_KEYS = WINDOW + QB
VMEM_LIMIT = 56 * 1024 * 1024

NT_DIMS = (((1,), (1,)), ((), ()))
TN_DIMS = (((0,), (0,)), ((), ()))


def _sigmoid(v):
    return 1.0 / (1.0 + jnp.exp(-v))


def _rms(x, g):
    return x * lax.rsqrt(jnp.mean(x * x, axis=-1, keepdims=True) + EPS) * g


def _params(sem):
    return pltpu.CompilerParams(dimension_semantics=sem, vmem_limit_bytes=VMEM_LIMIT)


_C_ROT = (0, 896)
_C_V = (896, 1280)
_C_G = (1280, 1408)
_C_RQ = (1408, 1920)
_C_RK = (1920, 2432)
_C_RV = (2432, 2944)
_C_RG = (2944, 3456)
PROJ_W = 3456


def _proj_kernel(x_ref, g_ref, w_ref, cn_ref, sa_ref, sb_ref, cr_ref, sr_ref,
                 q_ref, kcvc_ref, ks_ref, kw_ref, vsvw_ref, gate_ref, rq_ref, rk_ref, rv_ref, rg_ref,
                 *, tm):
    h = _rms(x_ref[...], g_ref[...]).astype(BF16)
    cn, sa, sb = cn_ref[...], sa_ref[...], sb_ref[...]
    cr, sr = cr_ref[...], sr_ref[...]

    def dot(c):
        return jnp.dot(h, w_ref[:, c[0]:c[1]], preferred_element_type=F32)

    def tile(y, i):
        return y[:, i * LANES:(i + 1) * LANES]

    def rope_n(t):
        return t * cn + pltpu.roll(t, LANES - ROPE_DIM // 2, 1) * sa + pltpu.roll(t, ROPE_DIM // 2, 1) * sb

    def rope_r(t):
        return t * cr + pltpu.roll(t, RET_D // 2, 1) * sr

    lane = lax.broadcasted_iota(jnp.int32, (tm, LANES), 1)
    lo = lane < DH

    def split(t):
        return jnp.where(lo, t, 0.0), jnp.where(lo, pltpu.roll(t, DH, 1), 0.0)

    yr = dot(_C_ROT)
    for i in range(4):
        q_ref[:, i * LANES:(i + 1) * LANES] = (rope_n(tile(yr, i)) * (DH ** -0.5 * LOG2E)).astype(BF16)
    kcvc_ref[:, 0:LANES] = rope_n(tile(yr, 4)).astype(BF16)
    row = pl.program_id(0) * tm + lax.broadcasted_iota(jnp.int32, (tm, LANES), 0)
    onehot = jnp.where((lane - DH) == ((row >> 6) & (BLK_PER_TILE - 1)), 1.0, 0.0)
    a, b = split(rope_n(tile(yr, 5)))
    ks_ref[0] = jnp.where(lo, a, onehot).astype(BF16)
    ks_ref[1] = jnp.where(lo, b, onehot).astype(BF16)
    a, b = split(rope_n(tile(yr, 6)))
    kw_ref[0] = a.astype(BF16)
    kw_ref[1] = b.astype(BF16)

    yv = dot(_C_V)
    kcvc_ref[:, LANES:2 * LANES] = tile(yv, 0).astype(BF16)
    vsvw_ref[...] = yv[:, LANES:3 * LANES].astype(BF16)
    gate_ref[...] = _sigmoid(dot(_C_G))

    yq = dot(_C_RQ)
    yk = dot(_C_RK)
    for i in range(RET_HEADS):
        sl = slice(i * LANES, (i + 1) * LANES)
        rq_ref[:, sl] = rope_r(tile(yq, i)).astype(BF16)
        rk_ref[:, sl] = (rope_r(tile(yk, i)) * (RET_D ** -0.5)).astype(BF16)
    rv_ref[...] = dot(_C_RV).astype(BF16)
    rg_ref[...] = dot(_C_RG).astype(BF16)


def _proj(x, g, w, tables, tm=512):
    S = x.shape[0]
    row = lambda i: (i, 0)
    const = lambda i: (0, 0)
    tab = pl.BlockSpec((tm, LANES), row)
    out_shape = (
        jax.ShapeDtypeStruct((S, NSA_HEADS * DH), BF16),
        jax.ShapeDtypeStruct((S, 2 * LANES), BF16),
        jax.ShapeDtypeStruct((NSA_GROUPS, S, LANES), BF16),
        jax.ShapeDtypeStruct((NSA_GROUPS, S, LANES), BF16),
        jax.ShapeDtypeStruct((S, 2 * LANES), BF16),
        jax.ShapeDtypeStruct((S, LANES), F32),
        jax.ShapeDtypeStruct((S, RET_HEADS * RET_D), BF16),
        jax.ShapeDtypeStruct((S, RET_HEADS * RET_D), BF16),
        jax.ShapeDtypeStruct((S, RET_HEADS * RET_D), BF16),
        jax.ShapeDtypeStruct((S, RET_HEADS * RET_D), BF16),
    )
    out_specs = (
        pl.BlockSpec((tm, NSA_HEADS * DH), row),
        pl.BlockSpec((tm, 2 * LANES), row),
        pl.BlockSpec((NSA_GROUPS, tm, LANES), lambda i: (0, i, 0)),
        pl.BlockSpec((NSA_GROUPS, tm, LANES), lambda i: (0, i, 0)),
        pl.BlockSpec((tm, 2 * LANES), row),
        pl.BlockSpec((tm, LANES), row),
        pl.BlockSpec((tm, 512), row),
        pl.BlockSpec((tm, 512), row),
        pl.BlockSpec((tm, 512), row),
        pl.BlockSpec((tm, 512), row),
    )
    return pl.pallas_call(
        functools.partial(_proj_kernel, tm=tm),
        grid=(S // tm,),
        in_specs=[pl.BlockSpec((tm, D_MODEL), row), pl.BlockSpec((1, D_MODEL), const),
                  pl.BlockSpec((D_MODEL, PROJ_W), const), tab, tab, tab, tab, tab],
        out_specs=out_specs, out_shape=out_shape,
        compiler_params=_params(("arbitrary",)), name="proj",
    )(x, g, w, *tables)


def _compress_kernel(x_ref, w1_ref, pe_ref, w2_ref, o_ref):
    x = x_ref[...]
    half = CMP_STRIDE * DH
    a = jnp.dot(x, w1_ref[0:half, :], preferred_element_type=F32)
    b = jnp.dot(x, w1_ref[half:2 * half, :], preferred_element_type=F32)
    peb = jnp.dot(pe_ref[...], w1_ref[...], preferred_element_type=F32)[0:1, :]
    hid = a + pltpu.roll(b, x.shape[0] - 1, 0) + peb
    hid = hid * _sigmoid(hid)
    o_ref[...] = jnp.dot(hid.astype(BF16), w2_ref[...], preferred_element_type=F32)


def _compress(xh, w1, pe, w2):
    n_half = xh.shape[1]
    return pl.pallas_call(
        _compress_kernel,
        grid=(2 * NSA_GROUPS,),
        in_specs=[pl.BlockSpec((None, n_half, CMP_STRIDE * DH), lambda i: (i, 0, 0)),
                  pl.BlockSpec((None, CMP_BLOCK * DH, CMP_HIDDEN), lambda i: (i // NSA_GROUPS, 0, 0)),
                  pl.BlockSpec((None, 16, CMP_BLOCK * DH), lambda i: (i // NSA_GROUPS, 0, 0)),
                  pl.BlockSpec((None, CMP_HIDDEN, LANES), lambda i: (i // NSA_GROUPS, 0, 0))],
        out_specs=pl.BlockSpec((None, n_half, LANES), lambda i: (i, 0, 0)),
        out_shape=jax.ShapeDtypeStruct((2 * NSA_GROUPS, n_half, LANES), F32),
        compiler_params=_params(("arbitrary",)), name="compress",
    )(xh, w1, pe, w2)


def _nsa_kernel(q_ref, kcmp_ref, vcmp_ref, ks_ref, vs_ref, kw_ref, vw_ref, gt_ref, ovl_ref,
                out_ref, qa_ref, bias_ref, m_ref, acc_ref, sa_ref, sb_ref, mta_ref, mtb_ref, *, S):
    NC = S // CMP_STRIDE
    NB = S // SLC_BLOCK
    NQ = HPG * QB
    b = pl.program_id(1)
    s0 = b * QB
    t_row = s0 + (lax.broadcasted_iota(jnp.int32, (1, NQ), 1) & (QB - 1))

    qa_ref[0:DH, :] = jnp.concatenate([q_ref[h] for h in range(HPG)], axis=1)
    qa_ref[DH:KILL_LANE, :] = jnp.zeros((KILL_LANE - DH, NQ), BF16)
    kill_rows = lax.broadcasted_iota(jnp.int32, (LANES - KILL_LANE, NQ), 0) == 0
    qa_ref[KILL_LANE:LANES, :] = jnp.where(kill_rows, NEG, 0.0).astype(BF16)
    band_row = lax.broadcasted_iota(jnp.int32, (QB, NQ), 0)
    band_tq = lax.broadcasted_iota(jnp.int32, (QB, NQ), 1) & (QB - 1)

    sc = jnp.dot(kcmp_ref[...], qa_ref[...], preferred_element_type=F32)
    sw = jnp.dot(kw_ref[pl.ds(pl.multiple_of(s0, QB), WIN_KEYS), :], qa_ref[...], preferred_element_type=F32)
    c_end = lax.broadcasted_iota(jnp.int32, (NC, NQ), 0) * CMP_STRIDE + (CMP_BLOCK - 1)
    sc = jnp.where(c_end <= t_row, sc, NEG)
    pc = jnp.exp2(sc - jnp.max(sc, axis=0, keepdims=True)).astype(BF16)
    ocl = jnp.dot(vcmp_ref[...], pc, preferred_element_type=F32)
    any_valid = jnp.where(t_row >= CMP_BLOCK - 1, 1.0, 0.0)
    rc = any_valid / ocl[DH:DH + 1, :]
    oc = ocl[0:DH, :] * rc

    imp4 = jnp.dot(ovl_ref[...], pc, preferred_element_type=F32) * rc
    imp = imp4[:, 0:QB] + imp4[:, QB:2 * QB] + imp4[:, 2 * QB:3 * QB] + imp4[:, 3 * QB:4 * QB]

    jidx = lax.broadcasted_iota(jnp.int32, (NB, QB), 0)
    cur = (s0 + lax.broadcasted_iota(jnp.int32, (1, QB), 1)) >> 6
    forced = (jidx == 0) | (jidx == cur) | (jidx == cur - 1)
    future = jidx > cur
    cand = jnp.where(forced | future, -jnp.inf, imp)

    def select(break_ties):
        v = cand
        bias_t = jnp.where(forced, 0.0, NEG)
        for _ in range(SLC_COUNT - 3):
            mx = jnp.max(v, axis=0, keepdims=True)
            hit = v == mx
            if break_ties:
                hit = jidx == jnp.min(jnp.where(hit, jidx, NB), axis=0, keepdims=True)
            bias_t = jnp.where(hit, 0.0, bias_t)
            v = jnp.where(hit, -jnp.inf, v)
        return jnp.where(future, NEG, bias_t)

    bias_t = select(False)
    n_sel = jnp.sum(jnp.where((bias_t == 0.0) & (jidx <= cur), 1.0, 0.0), axis=0, keepdims=True)
    bias_ref[...] = bias_t.astype(BF16)

    sw = jnp.concatenate([jnp.where(band_row > band_tq, sw[0:QB], NEG),
                          sw[QB:WINDOW],
                          jnp.where(band_row <= band_tq, sw[WINDOW:WIN_KEYS], NEG)], axis=0)
    pw = jnp.exp2(sw - jnp.max(sw, axis=0, keepdims=True)).astype(BF16)
    ow = jnp.dot(vw_ref[:, pl.ds(pl.multiple_of(s0, QB), WIN_KEYS)], pw, preferred_element_type=F32)
    owin = ow[0:DH, :] * (1.0 / ow[DH:DH + 1, :])

    def scores(kt, s_ref, mt_ref, tile_max=True):
        k0 = pl.multiple_of(kt * TK, TK)
        bias_rows = bias_ref[pl.ds(pl.multiple_of(kt * BLK_PER_TILE, BLK_PER_TILE), BLK_PER_TILE), :]
        qa_ref[DH:DH + BLK_PER_TILE, :] = jnp.concatenate([bias_rows] * HPG, axis=1)
        s = jnp.dot(ks_ref[pl.ds(k0, TK), :], qa_ref[...], preferred_element_type=F32)
        s_ref[...] = s
        if tile_max:
            mt_ref[...] = jnp.max(s, axis=0, keepdims=True)

    def accumulate(kt, s_ref, mt_ref):
        k0 = pl.multiple_of(kt * TK, TK)
        m_old = m_ref[...]
        m_new = jnp.maximum(m_old, mt_ref[...])
        p = jnp.exp2(s_ref[...] - m_new).astype(BF16)
        acc_ref[...] = (jnp.exp2(m_old - m_new) * acc_ref[...]
                        + jnp.dot(vs_ref[:, pl.ds(k0, TK)], p, preferred_element_type=F32))
        m_ref[...] = m_new

    n_full = s0 // TK

    def first_tiles():
        m_ref[...] = jnp.full((1, NQ), NEG, F32)
        acc_ref[...] = jnp.zeros((VROWS, NQ), F32)
        scores(n_full, sa_ref, mta_ref, tile_max=False)
        band = pl.ds(pl.multiple_of(s0 - n_full * TK, QB), QB)
        sa_ref[band, :] = jnp.where(band_row <= band_tq, sa_ref[band, :], NEG)
        mta_ref[...] = jnp.max(sa_ref[...], axis=0, keepdims=True)
        scores(0, sb_ref, mtb_ref)
        accumulate(n_full, sa_ref, mta_ref)

    first_tiles()

    @pl.when(jnp.max(n_sel) > SLC_COUNT)
    def _():
        bias_ref[...] = select(True).astype(BF16)
        first_tiles()

    def older_tile(j, last):
        def stages(s_cur, mt_cur, s_oth, mt_oth):
            if not last:
                scores(j + 1, s_oth, mt_oth)
            accumulate(j, s_cur, mt_cur)

        @pl.when(j % 2 == 0)
        def _():
            stages(sb_ref, mtb_ref, sa_ref, mta_ref)

        @pl.when(j % 2 == 1)
        def _():
            stages(sa_ref, mta_ref, sb_ref, mtb_ref)

    def body(j, carry):
        older_tile(j, False)
        return carry

    lax.fori_loop(0, n_full - 1, body, 0)

    @pl.when(n_full > 0)
    def _():
        older_tile(n_full - 1, True)

    acc = acc_ref[...]
    osel = acc[0:DH, :] * (1.0 / acc[DH:DH + 1, :])

    heads = []
    for h in range(HPG):
        sl = slice(h * QB, (h + 1) * QB)
        heads.append(gt_ref[3 * h:3 * h + 1, :] * oc[0:DH, sl]
                     + gt_ref[3 * h + 1:3 * h + 2, :] * osel[:, sl]
                     + gt_ref[3 * h + 2:3 * h + 3, :] * owin[:, sl])
    out_ref[...] = jnp.concatenate(heads, axis=0).T.astype(out_ref.dtype)


def _nsa(q_t, kcmp, vcmp_t, ks, vs_t, kw, vw_t, gt, ovl_t):
    S = ks.shape[1]
    NC, NB = S // CMP_STRIDE, S // SLC_BLOCK
    grp = lambda g, b: (g, 0, 0)
    return pl.pallas_call(
        functools.partial(_nsa_kernel, S=S),
        grid=(NSA_GROUPS, S // QB),
        in_specs=[pl.BlockSpec((HPG, DH, QB), lambda g, b: (g, 0, b)),
                  pl.BlockSpec((None, NC, LANES), grp),
                  pl.BlockSpec((None, VROWS, NC), grp),
                  pl.BlockSpec((None, S, LANES), grp),
                  pl.BlockSpec((None, VROWS, S), grp),
                  pl.BlockSpec((None, WINDOW + S, LANES), grp),
                  pl.BlockSpec((None, VROWS, WINDOW + S), grp),
                  pl.BlockSpec((None, 16, QB), lambda g, b: (g, 0, b)),
                  pl.BlockSpec((NB, NC), lambda g, b: (0, 0))],
        out_specs=pl.BlockSpec((QB, HPG * DH), lambda g, b: (b, g)),
        out_shape=jax.ShapeDtypeStruct((S, NSA_HEADS * DH), BF16),
        scratch_shapes=[pltpu.VMEM((LANES, HPG * QB), BF16),
                        pltpu.VMEM((NB, QB), BF16),
                        pltpu.VMEM((1, HPG * QB), F32),
                        pltpu.VMEM((VROWS, HPG * QB), F32),
                        pltpu.VMEM((TK, HPG * QB), F32),
                        pltpu.VMEM((TK, HPG * QB), F32),
                        pltpu.VMEM((1, HPG * QB), F32),
                        pltpu.VMEM((1, HPG * QB), F32)],
        compiler_params=_params(("arbitrary", "arbitrary")), name="nsa",
    )(q_t, kcmp, vcmp_t, ks, vs_t, kw, vw_t, gt, ovl_t)


def _ret_kernel(q_ref, k_ref, v_ref, g_ref, dmat_ref, xi_ref, zeta_ref, dec_ref, o_ref, state_ref, *, tm):
    @pl.when(pl.program_id(0) == 0)
    def _():
        state_ref[...] = jnp.zeros_like(state_ref)

    for j in range(tm // RET_C):
        rows = slice(j * RET_C, (j + 1) * RET_C)
        for h in range(RET_HEADS):
            cols = slice(h * RET_D, (h + 1) * RET_D)
            qc, kc, vc = q_ref[rows, cols], k_ref[rows, cols], v_ref[rows, cols]
            st = state_ref[h]
            st_hi = st.astype(BF16)
            st_lo = (st - st_hi.astype(F32)).astype(BF16)
            sc = lax.dot_general(qc, kc, NT_DIMS, preferred_element_type=F32) * dmat_ref[h]
            inner = jnp.dot(sc.astype(BF16), vc, preferred_element_type=F32)
            cross = (jnp.dot(qc, st_hi, preferred_element_type=F32)
                     + jnp.dot(qc, st_lo, preferred_element_type=F32)) * xi_ref[h]
            kz = (kc.astype(F32) * zeta_ref[h]).astype(BF16)
            state_ref[h] = dec_ref[h] * st + lax.dot_general(kz, vc, TN_DIMS, preferred_element_type=F32)
            o = inner + cross
            o = o * lax.rsqrt(jnp.mean(o * o, axis=-1, keepdims=True) + EPS)
            gate = g_ref[rows, cols].astype(F32)
            o_ref[rows, cols] = (gate * _sigmoid(gate) * o).astype(o_ref.dtype)


def _retention(rq, rk, rv, rg, dmat, xi, zeta, dec, tm=512):
    S = rq.shape[0]
    W = RET_HEADS * RET_D
    row = lambda i: (i, 0)
    cst = lambda i: (0, 0, 0)
    blk = pl.BlockSpec((tm, W), row)
    tbl = pl.BlockSpec((RET_HEADS, RET_C, RET_D), cst)
    return pl.pallas_call(
        functools.partial(_ret_kernel, tm=tm),
        grid=(S // tm,),
        in_specs=[blk, blk, blk, blk, tbl, tbl, tbl, tbl],
        out_specs=blk,
        out_shape=jax.ShapeDtypeStruct((S, W), BF16),
        scratch_shapes=[pltpu.VMEM((RET_HEADS, RET_D, RET_D), F32)],
        compiler_params=_params(("arbitrary",)), name="retention",
    )(rq, rk, rv, rg, dmat, xi, zeta, dec)


def _mix_kernel(x_ref, g_ref, a_ref, r_ref, wg_ref, wpa_ref, wpb_ref, wo_ref, o_ref):
    x = x_ref[...]
    h = _rms(x, g_ref[...]).astype(BF16)
    ga = _sigmoid(jnp.dot(h, wg_ref[:, 0:D_MODEL], preferred_element_type=F32))
    gb = _sigmoid(jnp.dot(h, wg_ref[:, D_MODEL:2 * D_MODEL], preferred_element_type=F32))
    mix = (ga * jnp.dot(a_ref[...], wpa_ref[...], preferred_element_type=F32)
           + gb * jnp.dot(r_ref[...], wpb_ref[...], preferred_element_type=F32))
    o_ref[...] = x + jnp.dot(mix.astype(BF16), wo_ref[...], preferred_element_type=F32)


def _mix(x, g, nsa_out, ret_out, wg, wpa, wpb, wo, tm=512):
    S = x.shape[0]
    row = lambda i: (i, 0)
    cst = lambda i: (0, 0)
    return pl.pallas_call(
        _mix_kernel,
        grid=(S // tm,),
        in_specs=[pl.BlockSpec((tm, D_MODEL), row), pl.BlockSpec((1, D_MODEL), cst),
                  pl.BlockSpec((tm, 512), row), pl.BlockSpec((tm, 512), row),
                  pl.BlockSpec((D_MODEL, 2 * D_MODEL), cst), pl.BlockSpec((512, D_MODEL), cst),
                  pl.BlockSpec((512, D_MODEL), cst), pl.BlockSpec((D_MODEL, D_MODEL), cst)],
        out_specs=pl.BlockSpec((tm, D_MODEL), row),
        out_shape=jax.ShapeDtypeStruct((S, D_MODEL), F32),
        compiler_params=_params(("arbitrary",)), name="mix",
    )(x, g, nsa_out, ret_out, wg, wpa, wpb, wo)


def _mlp_kernel(x_ref, g_ref, wu_ref, wd_ref, gf_ref, o_ref, *, hc):
    x = x_ref[...]
    h = _rms(x, g_ref[...]).astype(BF16)
    acc = x
    for c in range(MLP_HIDDEN // hc):
        u = jnp.maximum(jnp.dot(h, wu_ref[:, c * hc:(c + 1) * hc], preferred_element_type=F32), 0.0)
        acc = acc + jnp.dot((u * u).astype(BF16), wd_ref[c * hc:(c + 1) * hc, :], preferred_element_type=F32)
    o_ref[...] = _rms(acc, gf_ref[...])


def _mlp(x, g, wu, wd, gf, tm=512, hc=1024):
    S = x.shape[0]
    row = lambda i: (i, 0)
    cst = lambda i: (0, 0)
    return pl.pallas_call(
        functools.partial(_mlp_kernel, hc=hc),
        grid=(S // tm,),
        in_specs=[pl.BlockSpec((tm, D_MODEL), row), pl.BlockSpec((1, D_MODEL), cst),
                  pl.BlockSpec((D_MODEL, MLP_HIDDEN), cst), pl.BlockSpec((MLP_HIDDEN, D_MODEL), cst),
                  pl.BlockSpec((1, D_MODEL), cst)],
        out_specs=pl.BlockSpec((tm, D_MODEL), row),
        out_shape=jax.ShapeDtypeStruct((S, D_MODEL), F32),
        compiler_params=_params(("arbitrary",)), name="mlp",
    )(x, g, wu, wd, gf)


def _rope_tables(pos):
    S = pos.shape[0]
    p = pos.astype(F32)[:, None]
    half = ROPE_DIM // 2
    inv = jnp.exp(-math.log(ROPE_THETA) * jnp.arange(half, dtype=F32) * 2.0 / ROPE_DIM)
    ang = p * inv
    cos, sin = jnp.cos(ang), jnp.sin(ang)
    one = jnp.ones((S, DH - ROPE_DIM), F32)
    zero = jnp.zeros((S, DH - ROPE_DIM), F32)
    z8 = jnp.zeros((S, half), F32)
    cn = jnp.tile(jnp.concatenate([cos, cos, one], axis=1), (1, 2))
    sa = jnp.tile(jnp.concatenate([-sin, z8, zero], axis=1), (1, 2))
    sb = jnp.tile(jnp.concatenate([z8, sin, zero], axis=1), (1, 2))
    hr = RET_D // 2
    inv_r = jnp.exp(-math.log(RET_BASE) * jnp.arange(hr, dtype=F32) * 2.0 / RET_D)
    ang_r = p * inv_r
    cr = jnp.concatenate([jnp.cos(ang_r), jnp.cos(ang_r)], axis=1)
    sr = jnp.concatenate([-jnp.sin(ang_r), jnp.sin(ang_r)], axis=1)
    return cn, sa, sb, cr, sr


def _decay_tables():
    H, C = RET_HEADS, RET_C
    log_g = jnp.log(1.0 - jnp.exp2(-5.0 - jnp.arange(H, dtype=F32)))
    n = jnp.arange(C, dtype=F32)
    rel = n[:, None] - n[None, :]
    dmat = jnp.where(rel >= 0, jnp.exp(log_g[:, None, None] * jnp.maximum(rel, 0.0)), 0.0)
    xi = jnp.exp(log_g[:, None] * (n + 1.0))
    zeta = jnp.exp(log_g[:, None] * (C - 1.0 - n))
    dec = jnp.exp(log_g * C)
    bc = lambda a: jnp.broadcast_to(a, (H, C, RET_D))
    return dmat, bc(xi[:, :, None]), bc(zeta[:, :, None]), bc(dec[:, None, None])


def _overlap_t(S):
    n_cmp = (S - CMP_BLOCK) // CMP_STRIDE + 1
    n_slc = S // SLC_BLOCK
    cs = np.arange(n_cmp)[:, None] * CMP_STRIDE
    ss = np.arange(n_slc)[None, :] * SLC_BLOCK
    ov = np.clip(np.minimum(cs + CMP_BLOCK, ss + SLC_BLOCK) - np.maximum(cs, ss), 0, None) / CMP_BLOCK
    out = np.zeros((n_slc, S // CMP_STRIDE), np.float32)
    out[:, :n_cmp] = ov.T
    return jnp.asarray(out, BF16)


def _v_rows(v_t):
    S = v_t.shape[1]
    return jnp.concatenate([v_t, jnp.ones((1, S), v_t.dtype), jnp.zeros((VROWS - DH - 1, S), v_t.dtype)], axis=0)


_IN_OFFSETS = np.cumsum((0, 512, 128, 128, 128, 128, 128, 128, 24, 512, 512, 512, 512, 1024, 1024))


def _in_col(wi, i):
    return wi[:, _IN_OFFSETS[i]:_IN_OFFSETS[i + 1]]


def _branches(x, positions, norm_mix, w_in, cmp_pos_k, cmp_pos_v, cmp_k_w1, cmp_k_w2, cmp_v_w1, cmp_v_w2):
    B, S, _ = x.shape
    assert B == 1 and norm_mix.shape[0] == 1 and S % TK == 0 and S >= max(WIN_KEYS, SLC_COUNT * SLC_BLOCK)
    xs = x[0]
    col = functools.partial(_in_col, w_in[0])
    w_a = jnp.concatenate([col(0), col(1), col(3), col(5), col(2), col(4), col(6),
                           col(7), jnp.zeros((D_MODEL, LANES - 24), F32),
                           col(8), col(9), col(10), col(11)], axis=1).astype(BF16)

    tables = _rope_tables(positions[0])
    q, kcvc, ks, kw, vsvw, gates, rq, rk, rv, rg = _proj(xs, norm_mix, w_a, tables)

    n_half = S // CMP_STRIDE
    xh = kcvc.reshape(n_half, CMP_STRIDE, 2 * NSA_GROUPS, DH).transpose(2, 0, 1, 3).reshape(
        2 * NSA_GROUPS, n_half, CMP_STRIDE * DH)
    w1 = jnp.stack([cmp_k_w1[0], cmp_v_w1[0]]).astype(BF16)
    pe = jnp.stack([cmp_pos_k[0], cmp_pos_v[0]]).reshape(2, 1, CMP_BLOCK * DH)
    pe = jnp.broadcast_to(pe, (2, 16, CMP_BLOCK * DH)).astype(BF16)
    w2 = jnp.pad(jnp.stack([cmp_k_w2[0], cmp_v_w2[0]]), ((0, 0), (0, 0), (0, LANES - DH))).astype(BF16)
    cmp = _compress(xh, w1, pe, w2)
    kcmp = cmp[0:NSA_GROUPS].astype(BF16)
    vcmp_t = jax.vmap(_v_rows)(cmp[NSA_GROUPS:, :, 0:DH].transpose(0, 2, 1).astype(BF16))

    vt = vsvw.T.reshape(2 * NSA_GROUPS, DH, S)
    vs_t = jax.vmap(_v_rows)(vt[0:NSA_GROUPS])
    vw_t = jnp.pad(jax.vmap(_v_rows)(vt[NSA_GROUPS:]), ((0, 0), (0, 0), (WINDOW, 0)))
    pad_keys = jnp.zeros((NSA_GROUPS, WINDOW, LANES), BF16).at[:, :, KILL_LANE].set(1.0)
    kw = jnp.concatenate([pad_keys, kw], axis=1)
    gt = jnp.pad(gates[:, 0:3 * NSA_HEADS].T.reshape(NSA_GROUPS, 3 * HPG, S), ((0, 0), (0, 16 - 3 * HPG), (0, 0)))
    q_t = q.T.reshape(NSA_HEADS, DH, S)
    nsa_out = _nsa(q_t, kcmp, vcmp_t, ks, vs_t, kw, vw_t, gt, _overlap_t(S))

    ret_out = _retention(rq, rk, rv, rg, *_decay_tables())
    return nsa_out, ret_out


def kernel(x, positions, norm_mix, w_in, cmp_pos_k, cmp_pos_v, cmp_k_w1, cmp_k_w2, cmp_v_w1, cmp_v_w2,
           w_proj_a, w_proj_b, w_out, norm_mlp, w_up, w_down, norm_final):
    nsa_out, ret_out = _branches(x, positions, norm_mix, w_in, cmp_pos_k, cmp_pos_v,
                                 cmp_k_w1, cmp_k_w2, cmp_v_w1, cmp_v_w2)
    xs = x[0]
    w_gate = jnp.concatenate([_in_col(w_in[0], 12), _in_col(w_in[0], 13)], axis=1).astype(BF16)
    x1 = _mix(xs, norm_mix, nsa_out, ret_out, w_gate, w_proj_a[0].astype(BF16), w_proj_b[0].astype(BF16),
              w_out[0].astype(BF16))
    y = _mlp(x1, norm_mlp, w_up[0].astype(BF16), w_down[0].astype(BF16), norm_final[None, :])
    return y[None]
```

```python
import functools
import math

import numpy as np
import jax
import jax.numpy as jnp
from jax import lax
from jax.experimental import pallas as pl
from jax.experimental.pallas import tpu as pltpu

F32 = jnp.float32
BF16 = jnp.bfloat16

D_MODEL = 1024
NSA_HEADS = 8
NSA_GROUPS = 2
HPG = NSA_HEADS // NSA_GROUPS
DH = 64
CMP_BLOCK = 32
CMP_STRIDE = 16
CMP_HIDDEN = 256
SLC_BLOCK = 64
SLC_COUNT = 16
WINDOW = 512
QB = 128
ROPE_THETA = 500000.0
ROPE_DIM = DH // 4
RET_HEADS = 4
RET_D = 128
RET_C = 128
RET_BASE = 10000.0
MLP_HIDDEN = 4 * D_MODEL
EPS = 1e-6
NEG = -1e30
BIG = 1e30

LANES = 128
TK = 1024
BLK_PER_TILE = TK // SLC_BLOCK
KILL_LANE = DH + BLK_PER_TILE
LOG2E = math.log2(math.e)
VROWS = 80
WIN_KEYS = WINDOW + QB
VMEM_LIMIT = 56 * 1024 * 1024

NT_DIMS = (((1,), (1,)), ((), ()))
TN_DIMS = (((0,), (0,)), ((), ()))


def _sigmoid(v):
    return 1.0 / (1.0 + jnp.exp(-v))


def _rms(x, g):
    return x * lax.rsqrt(jnp.mean(x * x, axis=-1, keepdims=True) + EPS) * g


def _params(sem, flags=None):
    return pltpu.CompilerParams(dimension_semantics=sem, vmem_limit_bytes=VMEM_LIMIT, flags=flags)


_C_ROT = (0, 896)
_C_V = (896, 1280)
_C_G = (1280, 1408)
_C_RQ = (1408, 1920)
_C_RK = (1920, 2432)
_C_RV = (2432, 2944)
_C_RG = (2944, 3456)
PROJ_W = 3456


def _proj_kernel(x_ref, g_ref, w_ref, cn_ref, sa_ref, sb_ref, cr_ref, sr_ref,
                 q_ref, kcvc_ref, ks_ref, kw_ref, vsvw_ref, gate_ref, rq_ref, rk_ref, rv_ref, rg_ref,
                 *, tm):
    h = _rms(x_ref[...], g_ref[...]).astype(BF16)
    cn, sa, sb = cn_ref[...], sa_ref[...], sb_ref[...]
    cr, sr = cr_ref[...], sr_ref[...]

    def dot(c):
        return jnp.dot(h, w_ref[:, c[0]:c[1]], preferred_element_type=F32)

    def tile(y, i):
        return y[:, i * LANES:(i + 1) * LANES]

    def rope_n(t):
        return t * cn + pltpu.roll(t, LANES - ROPE_DIM // 2, 1) * sa + pltpu.roll(t, ROPE_DIM // 2, 1) * sb

    def rope_r(t):
        return t * cr + pltpu.roll(t, RET_D // 2, 1) * sr

    lane = lax.broadcasted_iota(jnp.int32, (tm, LANES), 1)
    lo = lane < DH

    def split(t):
        return jnp.where(lo, t, 0.0), jnp.where(lo, pltpu.roll(t, DH, 1), 0.0)

    yr = dot(_C_ROT)
    for i in range(4):
        q_ref[:, i * LANES:(i + 1) * LANES] = (rope_n(tile(yr, i)) * (DH ** -0.5 * LOG2E)).astype(BF16)
    kcvc_ref[:, 0:LANES] = rope_n(tile(yr, 4)).astype(BF16)
    row = pl.program_id(0) * tm + lax.broadcasted_iota(jnp.int32, (tm, LANES), 0)
    onehot = jnp.where((lane - DH) == ((row >> 6) & (BLK_PER_TILE - 1)), 1.0, 0.0)
    a, b = split(rope_n(tile(yr, 5)))
    ks_ref[0] = jnp.where(lo, a, onehot).astype(BF16)
    ks_ref[1] = jnp.where(lo, b, onehot).astype(BF16)
    a, b = split(rope_n(tile(yr, 6)))
    kw_ref[0] = a.astype(BF16)
    kw_ref[1] = b.astype(BF16)

    yv = dot(_C_V)
    kcvc_ref[:, LANES:2 * LANES] = tile(yv, 0).astype(BF16)
    vsvw_ref[...] = yv[:, LANES:3 * LANES].astype(BF16)
    gate_ref[...] = _sigmoid(dot(_C_G))

    yq = dot(_C_RQ)
    yk = dot(_C_RK)
    for i in range(RET_HEADS):
        sl = slice(i * LANES, (i + 1) * LANES)
        rq_ref[:, sl] = rope_r(tile(yq, i)).astype(BF16)
        rk_ref[:, sl] = (rope_r(tile(yk, i)) * (RET_D ** -0.5)).astype(BF16)
    rv_ref[...] = dot(_C_RV).astype(BF16)
    rg_ref[...] = dot(_C_RG).astype(BF16)


def _proj(x, g, w, tables, tm=512):
    S = x.shape[0]
    row = lambda i: (i, 0)
    const = lambda i: (0, 0)
    tab = pl.BlockSpec((tm, LANES), row)
    out_shape = (
        jax.ShapeDtypeStruct((S, NSA_HEADS * DH), BF16),
        jax.ShapeDtypeStruct((S, 2 * LANES), BF16),
        jax.ShapeDtypeStruct((NSA_GROUPS, S, LANES), BF16),
        jax.ShapeDtypeStruct((NSA_GROUPS, S, LANES), BF16),
        jax.ShapeDtypeStruct((S, 2 * LANES), BF16),
        jax.ShapeDtypeStruct((S, LANES), F32),
        jax.ShapeDtypeStruct((S, RET_HEADS * RET_D), BF16),
        jax.ShapeDtypeStruct((S, RET_HEADS * RET_D), BF16),
        jax.ShapeDtypeStruct((S, RET_HEADS * RET_D), BF16),
        jax.ShapeDtypeStruct((S, RET_HEADS * RET_D), BF16),
    )
    out_specs = (
        pl.BlockSpec((tm, NSA_HEADS * DH), row),
        pl.BlockSpec((tm, 2 * LANES), row),
        pl.BlockSpec((NSA_GROUPS, tm, LANES), lambda i: (0, i, 0)),
        pl.BlockSpec((NSA_GROUPS, tm, LANES), lambda i: (0, i, 0)),
        pl.BlockSpec((tm, 2 * LANES), row),
        pl.BlockSpec((tm, LANES), row),
        pl.BlockSpec((tm, 512), row),
        pl.BlockSpec((tm, 512), row),
        pl.BlockSpec((tm, 512), row),
        pl.BlockSpec((tm, 512), row),
    )
    return pl.pallas_call(
        functools.partial(_proj_kernel, tm=tm),
        grid=(S // tm,),
        in_specs=[pl.BlockSpec((tm, D_MODEL), row), pl.BlockSpec((1, D_MODEL), const),
                  pl.BlockSpec((D_MODEL, PROJ_W), const), tab, tab, tab, tab, tab],
        out_specs=out_specs, out_shape=out_shape,
        compiler_params=_params(("arbitrary",)), name="proj",
    )(x, g, w, *tables)


def _compress_kernel(x_ref, w1_ref, pe_ref, w2_ref, o_ref):
    x = x_ref[...]
    half = CMP_STRIDE * DH
    a = jnp.dot(x, w1_ref[0:half, :], preferred_element_type=F32)
    b = jnp.dot(x, w1_ref[half:2 * half, :], preferred_element_type=F32)
    peb = jnp.dot(pe_ref[...], w1_ref[...], preferred_element_type=F32)[0:1, :]
    hid = a + pltpu.roll(b, x.shape[0] - 1, 0) + peb
    hid = hid * _sigmoid(hid)
    o_ref[...] = jnp.dot(hid.astype(BF16), w2_ref[...], preferred_element_type=F32)


def _compress(xh, w1, pe, w2):
    n_half = xh.shape[1]
    return pl.pallas_call(
        _compress_kernel,
        grid=(2 * NSA_GROUPS,),
        in_specs=[pl.BlockSpec((None, n_half, CMP_STRIDE * DH), lambda i: (i, 0, 0)),
                  pl.BlockSpec((None, CMP_BLOCK * DH, CMP_HIDDEN), lambda i: (i // NSA_GROUPS, 0, 0)),
                  pl.BlockSpec((None, 16, CMP_BLOCK * DH), lambda i: (i // NSA_GROUPS, 0, 0)),
                  pl.BlockSpec((None, CMP_HIDDEN, LANES), lambda i: (i // NSA_GROUPS, 0, 0))],
        out_specs=pl.BlockSpec((None, n_half, LANES), lambda i: (i, 0, 0)),
        out_shape=jax.ShapeDtypeStruct((2 * NSA_GROUPS, n_half, LANES), F32),
        compiler_params=_params(("arbitrary",)), name="compress",
    )(xh, w1, pe, w2)


def _nsa_kernel(q_ref, qn_ref, kcmp_ref, vcmp_ref, ks_ref, vs_ref, kw_ref, vw_ref, gt_ref, ovl_ref,
                out_ref, qa_ref, qs_ref, bias_ref, biasn_ref, ocn_ref, m_ref, acc_ref,
                sa_ref, sb_ref, mta_ref, mtb_ref, p_ref, *, S):
    NC = S // CMP_STRIDE
    NB = S // SLC_BLOCK
    NQ = HPG * QB
    b = pl.program_id(1)
    s0 = b * QB
    lane_q = lax.broadcasted_iota(jnp.int32, (1, NQ), 1) & (QB - 1)
    band_row = lax.broadcasted_iota(jnp.int32, (QB, NQ), 0)
    band_tq = lax.broadcasted_iota(jnp.int32, (QB, NQ), 1) & (QB - 1)

    def sel_scores(qsrc_ref):
        qs_ref[0:DH, :] = jnp.concatenate([qsrc_ref[h] for h in range(HPG)], axis=1)
        qs_ref[DH:LANES, :] = jnp.zeros((LANES - DH, NQ), BF16)
        return jnp.dot(kcmp_ref[...], qs_ref[...], preferred_element_type=F32)

    def sel_probs(sc, blk):
        c_end = lax.broadcasted_iota(jnp.int32, (NC, NQ), 0) * CMP_STRIDE + (CMP_BLOCK - 1)
        sc = jnp.where(c_end <= blk * QB + lane_q, sc, NEG)
        return jnp.exp2(sc - jnp.max(sc, axis=0, keepdims=True)).astype(BF16)

    def sel_matmuls(pc):
        return (jnp.dot(vcmp_ref[...], pc, preferred_element_type=F32),
                jnp.dot(ovl_ref[...], pc, preferred_element_type=F32))

    def sel_choose(ocl, imp4, blk):
        t_row = blk * QB + lane_q
        any_valid = jnp.where(t_row >= CMP_BLOCK - 1, 1.0, 0.0)
        rc = any_valid / ocl[DH:DH + 1, :]
        ocn_ref[...] = ocl[0:DH, :] * rc

        imp4 = imp4 * rc
        imp = imp4[:, 0:QB] + imp4[:, QB:2 * QB] + imp4[:, 2 * QB:3 * QB] + imp4[:, 3 * QB:4 * QB]

        jidx = lax.broadcasted_iota(jnp.int32, (NB, QB), 0)
        cur = (blk * QB + lax.broadcasted_iota(jnp.int32, (1, QB), 1)) >> 6
        forced = (jidx == 0) | (jidx == cur) | (jidx == cur - 1)
        future = jidx > cur
        cand = jnp.where(forced | future, -jnp.inf, imp)

        def select(break_ties):
            v = cand
            bias_t = jnp.where(forced, 0.0, NEG)
            for _ in range(SLC_COUNT - 3):
                mx = jnp.max(v, axis=0, keepdims=True)
                hit = v == mx
                if break_ties:
                    hit = jidx == jnp.min(jnp.where(hit, jidx, NB), axis=0, keepdims=True)
                bias_t = jnp.where(hit, 0.0, bias_t)
                v = jnp.where(hit, -jnp.inf, v)
            return jnp.where(future, NEG, bias_t)

        bias_t = select(False)
        n_sel = jnp.sum(jnp.where((bias_t == 0.0) & (jidx <= cur), 1.0, 0.0), axis=0, keepdims=True)
        biasn_ref[...] = bias_t.astype(BF16)

        def redo_if_tied():
            @pl.when(jnp.max(n_sel) > SLC_COUNT)
            def _():
                biasn_ref[...] = select(True).astype(BF16)
        return redo_if_tied

    @pl.when(b == 0)
    def _():
        ocl, imp4 = sel_matmuls(sel_probs(sel_scores(q_ref), 0))
        sel_choose(ocl, imp4, 0)()

    bias_ref[...] = biasn_ref[...]
    oc = ocn_ref[...]

    qa_ref[0:DH, :] = jnp.concatenate([q_ref[h] for h in range(HPG)], axis=1)
    qa_ref[DH:KILL_LANE, :] = jnp.zeros((KILL_LANE - DH, NQ), BF16)
    kill_rows = lax.broadcasted_iota(jnp.int32, (LANES - KILL_LANE, NQ), 0) == 0
    qa_ref[KILL_LANE:LANES, :] = jnp.where(kill_rows, NEG, 0.0).astype(BF16)

    def scores(kt, s_ref, mt_ref, tile_max=True):
        k0 = pl.multiple_of(kt * TK, TK)
        bias_rows = bias_ref[pl.ds(pl.multiple_of(kt * BLK_PER_TILE, BLK_PER_TILE), BLK_PER_TILE), :]
        qa_ref[DH:DH + BLK_PER_TILE, :] = jnp.concatenate([bias_rows] * HPG, axis=1)
        s = jnp.dot(ks_ref[pl.ds(k0, TK), :], qa_ref[...], preferred_element_type=F32)
        s_ref[...] = s
        if tile_max:
            mt_ref[...] = jnp.max(s, axis=0, keepdims=True)

    def accumulate(kt, s_ref, mt_ref):
        k0 = pl.multiple_of(kt * TK, TK)
        m_old = m_ref[...]
        m_new = jnp.maximum(m_old, mt_ref[...])
        p_ref[...] = jnp.exp2(s_ref[...] - m_new).astype(BF16)
        acc_ref[...] = (jnp.exp2(m_old - m_new) * acc_ref[...]
                        + jnp.dot(vs_ref[:, pl.ds(k0, TK)], p_ref[...], preferred_element_type=F32))
        m_ref[...] = m_new

    n_full = s0 // TK

    m_ref[...] = jnp.full((1, NQ), NEG, F32)
    acc_ref[...] = jnp.zeros((VROWS, NQ), F32)

    sc = sel_scores(qn_ref)
    sw = jnp.dot(kw_ref[pl.ds(pl.multiple_of(s0, QB), WIN_KEYS), :], qa_ref[...], preferred_element_type=F32)
    scores(n_full, sa_ref, mta_ref, tile_max=False)

    pc = sel_probs(sc, b + 1)
    ocl, imp4 = sel_matmuls(pc)
    scores(0, sb_ref, mtb_ref)
    redo_if_tied = sel_choose(ocl, imp4, b + 1)

    sw = jnp.concatenate([jnp.where(band_row > band_tq, sw[0:QB], NEG),
                          sw[QB:WINDOW],
                          jnp.where(band_row <= band_tq, sw[WINDOW:WIN_KEYS], NEG)], axis=0)
    pw = jnp.exp2(sw - jnp.max(sw, axis=0, keepdims=True)).astype(BF16)
    band = pl.ds(pl.multiple_of(s0 - n_full * TK, QB), QB)
    sa_ref[band, :] = jnp.where(band_row <= band_tq, sa_ref[band, :], NEG)
    mta_ref[...] = jnp.max(sa_ref[...], axis=0, keepdims=True)

    ow = jnp.dot(vw_ref[:, pl.ds(pl.multiple_of(s0, QB), WIN_KEYS)], pw, preferred_element_type=F32)
    owin = ow[0:DH, :] * (1.0 / ow[DH:DH + 1, :])
    accumulate(n_full, sa_ref, mta_ref)
    redo_if_tied()

    def older_tile(j, last):
        def stages(s_cur, mt_cur, s_oth, mt_oth):
            if not last:
                scores(j + 1, s_oth, mt_oth)
            accumulate(j, s_cur, mt_cur)

        @pl.when(j % 2 == 0)
        def _():
            stages(sb_ref, mtb_ref, sa_ref, mta_ref)

        @pl.when(j % 2 == 1)
        def _():
            stages(sa_ref, mta_ref, sb_ref, mtb_ref)

    def body(j, carry):
        older_tile(j, False)
        return carry

    lax.fori_loop(0, n_full - 1, body, 0)

    @pl.when(n_full > 0)
    def _():
        older_tile(n_full - 1, True)

    acc = acc_ref[...]
    osel = acc[0:DH, :] * (1.0 / acc[DH:DH + 1, :])

    heads = []
    for h in range(HPG):
        sl = slice(h * QB, (h + 1) * QB)
        heads.append(gt_ref[3 * h:3 * h + 1, :] * oc[:, sl]
                     + gt_ref[3 * h + 1:3 * h + 2, :] * osel[:, sl]
                     + gt_ref[3 * h + 2:3 * h + 3, :] * owin[:, sl])
    out_ref[...] = jnp.concatenate(heads, axis=0).T.astype(out_ref.dtype)


def _nsa(q_t, kcmp, vcmp_t, ks, vs_t, kw, vw_t, gt, ovl_t):
    S = ks.shape[1]
    NC, NB = S // CMP_STRIDE, S // SLC_BLOCK
    n_qb = S // QB
    grp = lambda g, b: (g, 0, 0)
    return pl.pallas_call(
        functools.partial(_nsa_kernel, S=S),
        grid=(NSA_GROUPS, n_qb),
        in_specs=[pl.BlockSpec((HPG, DH, QB), lambda g, b: (g, 0, b)),
                  pl.BlockSpec((HPG, DH, QB), lambda g, b: (g, 0, jnp.minimum(b + 1, n_qb - 1))),
                  pl.BlockSpec((None, NC, LANES), grp),
                  pl.BlockSpec((None, VROWS, NC), grp),
                  pl.BlockSpec((None, S, LANES), grp),
                  pl.BlockSpec((None, VROWS, S), grp),
                  pl.BlockSpec((None, WINDOW + S, LANES), grp),
                  pl.BlockSpec((None, VROWS, WINDOW + S), grp),
                  pl.BlockSpec((None, 16, QB), lambda g, b: (g, 0, b)),
                  pl.BlockSpec((NB, NC), lambda g, b: (0, 0))],
        out_specs=pl.BlockSpec((QB, HPG * DH), lambda g, b: (b, g)),
        out_shape=jax.ShapeDtypeStruct((S, NSA_HEADS * DH), BF16),
        scratch_shapes=[pltpu.VMEM((LANES, HPG * QB), BF16),
                        pltpu.VMEM((LANES, HPG * QB), BF16),
                        pltpu.VMEM((NB, QB), BF16),
                        pltpu.VMEM((NB, QB), BF16),
                        pltpu.VMEM((DH, HPG * QB), F32),
                        pltpu.VMEM((1, HPG * QB), F32),
                        pltpu.VMEM((VROWS, HPG * QB), F32),
                        pltpu.VMEM((TK, HPG * QB), F32),
                        pltpu.VMEM((TK, HPG * QB), F32),
                        pltpu.VMEM((1, HPG * QB), F32),
                        pltpu.VMEM((1, HPG * QB), F32),
                        pltpu.VMEM((TK, HPG * QB), BF16)],
        compiler_params=_params(("arbitrary", "arbitrary")), name="nsa",
    )(q_t, q_t, kcmp, vcmp_t, ks, vs_t, kw, vw_t, gt, ovl_t)


def _ret_kernel(q_ref, k_ref, v_ref, g_ref, dmat_ref, xi_ref, zeta_ref, dec_ref, o_ref, state_ref, *, tm):
    @pl.when(pl.program_id(0) == 0)
    def _():
        state_ref[...] = jnp.zeros_like(state_ref)

    for j in range(tm // RET_C):
        rows = slice(j * RET_C, (j + 1) * RET_C)
        for h in range(RET_HEADS):
            cols = slice(h * RET_D, (h + 1) * RET_D)
            qc, kc, vc = q_ref[rows, cols], k_ref[rows, cols], v_ref[rows, cols]
            st = state_ref[h]
            st_hi = st.astype(BF16)
            st_lo = (st - st_hi.astype(F32)).astype(BF16)
            sc = lax.dot_general(qc, kc, NT_DIMS, preferred_element_type=F32) * dmat_ref[h]
            inner = jnp.dot(sc.astype(BF16), vc, preferred_element_type=F32)
            cross = (jnp.dot(qc, st_hi, preferred_element_type=F32)
                     + jnp.dot(qc, st_lo, preferred_element_type=F32)) * xi_ref[h]
            kz = (kc.astype(F32) * zeta_ref[h]).astype(BF16)
            state_ref[h] = dec_ref[h] * st + lax.dot_general(kz, vc, TN_DIMS, preferred_element_type=F32)
            o = inner + cross
            o = o * lax.rsqrt(jnp.mean(o * o, axis=-1, keepdims=True) + EPS)
            gate = g_ref[rows, cols].astype(F32)
            o_ref[rows, cols] = (gate * _sigmoid(gate) * o).astype(o_ref.dtype)


def _retention(rq, rk, rv, rg, dmat, xi, zeta, dec, tm=512):
    S = rq.shape[0]
    W = RET_HEADS * RET_D
    row = lambda i: (i, 0)
    cst = lambda i: (0, 0, 0)
    blk = pl.BlockSpec((tm, W), row)
    tbl = pl.BlockSpec((RET_HEADS, RET_C, RET_D), cst)
    return pl.pallas_call(
        functools.partial(_ret_kernel, tm=tm),
        grid=(S // tm,),
        in_specs=[blk, blk, blk, blk, tbl, tbl, tbl, tbl],
        out_specs=blk,
        out_shape=jax.ShapeDtypeStruct((S, W), BF16),
        scratch_shapes=[pltpu.VMEM((RET_HEADS, RET_D, RET_D), F32)],
        compiler_params=_params(("arbitrary",)), name="retention",
    )(rq, rk, rv, rg, dmat, xi, zeta, dec)


def _mix_kernel(x_ref, g_ref, a_ref, r_ref, wg_ref, wpa_ref, wpb_ref, wo_ref, o_ref):
    x = x_ref[...]
    h = _rms(x, g_ref[...]).astype(BF16)
    ga = _sigmoid(jnp.dot(h, wg_ref[:, 0:D_MODEL], preferred_element_type=F32))
    gb = _sigmoid(jnp.dot(h, wg_ref[:, D_MODEL:2 * D_MODEL], preferred_element_type=F32))
    mix = (ga * jnp.dot(a_ref[...], wpa_ref[...], preferred_element_type=F32)
           + gb * jnp.dot(r_ref[...], wpb_ref[...], preferred_element_type=F32))
    o_ref[...] = x + jnp.dot(mix.astype(BF16), wo_ref[...], preferred_element_type=F32)


def _mix(x, g, nsa_out, ret_out, wg, wpa, wpb, wo, tm=512):
    S = x.shape[0]
    row = lambda i: (i, 0)
    cst = lambda i: (0, 0)
    return pl.pallas_call(
        _mix_kernel,
        grid=(S // tm,),
        in_specs=[pl.BlockSpec((tm, D_MODEL), row), pl.BlockSpec((1, D_MODEL), cst),
                  pl.BlockSpec((tm, 512), row), pl.BlockSpec((tm, 512), row),
                  pl.BlockSpec((D_MODEL, 2 * D_MODEL), cst), pl.BlockSpec((512, D_MODEL), cst),
                  pl.BlockSpec((512, D_MODEL), cst), pl.BlockSpec((D_MODEL, D_MODEL), cst)],
        out_specs=pl.BlockSpec((tm, D_MODEL), row),
        out_shape=jax.ShapeDtypeStruct((S, D_MODEL), F32),
        compiler_params=_params(("arbitrary",)), name="mix",
    )(x, g, nsa_out, ret_out, wg, wpa, wpb, wo)


def _mlp_kernel(x_ref, g_ref, wu_ref, wd_ref, gf_ref, o_ref, *, hc):
    x = x_ref[...]
    h = _rms(x, g_ref[...]).astype(BF16)
    acc = x
    for c in range(MLP_HIDDEN // hc):
        u = jnp.maximum(jnp.dot(h, wu_ref[:, c * hc:(c + 1) * hc], preferred_element_type=F32), 0.0)
        acc = acc + jnp.dot((u * u).astype(BF16), wd_ref[c * hc:(c + 1) * hc, :], preferred_element_type=F32)
    o_ref[...] = _rms(acc, gf_ref[...])


def _mlp(x, g, wu, wd, gf, tm=512, hc=1024):
    S = x.shape[0]
    row = lambda i: (i, 0)
    cst = lambda i: (0, 0)
    return pl.pallas_call(
        functools.partial(_mlp_kernel, hc=hc),
        grid=(S // tm,),
        in_specs=[pl.BlockSpec((tm, D_MODEL), row), pl.BlockSpec((1, D_MODEL), cst),
                  pl.BlockSpec((D_MODEL, MLP_HIDDEN), cst), pl.BlockSpec((MLP_HIDDEN, D_MODEL), cst),
                  pl.BlockSpec((1, D_MODEL), cst)],
        out_specs=pl.BlockSpec((tm, D_MODEL), row),
        out_shape=jax.ShapeDtypeStruct((S, D_MODEL), F32),
        compiler_params=_params(("arbitrary",)), name="mlp",
    )(x, g, wu, wd, gf)


def _rope_tables(pos):
    S = pos.shape[0]
    p = pos.astype(F32)[:, None]
    half = ROPE_DIM // 2
    inv = jnp.exp(-math.log(ROPE_THETA) * jnp.arange(half, dtype=F32) * 2.0 / ROPE_DIM)
    ang = p * inv
    cos, sin = jnp.cos(ang), jnp.sin(ang)
    one = jnp.ones((S, DH - ROPE_DIM), F32)
    zero = jnp.zeros((S, DH - ROPE_DIM), F32)
    z8 = jnp.zeros((S, half), F32)
    cn = jnp.tile(jnp.concatenate([cos, cos, one], axis=1), (1, 2))
    sa = jnp.tile(jnp.concatenate([-sin, z8, zero], axis=1), (1, 2))
    sb = jnp.tile(jnp.concatenate([z8, sin, zero], axis=1), (1, 2))
    hr = RET_D // 2
    inv_r = jnp.exp(-math.log(RET_BASE) * jnp.arange(hr, dtype=F32) * 2.0 / RET_D)
    ang_r = p * inv_r
    cr = jnp.concatenate([jnp.cos(ang_r), jnp.cos(ang_r)], axis=1)
    sr = jnp.concatenate([-jnp.sin(ang_r), jnp.sin(ang_r)], axis=1)
    return cn, sa, sb, cr, sr


def _decay_tables():
    H, C = RET_HEADS, RET_C
    log_g = jnp.log(1.0 - jnp.exp2(-5.0 - jnp.arange(H, dtype=F32)))
    n = jnp.arange(C, dtype=F32)
    rel = n[:, None] - n[None, :]
    dmat = jnp.where(rel >= 0, jnp.exp(log_g[:, None, None] * jnp.maximum(rel, 0.0)), 0.0)
    xi = jnp.exp(log_g[:, None] * (n + 1.0))
    zeta = jnp.exp(log_g[:, None] * (C - 1.0 - n))
    dec = jnp.exp(log_g * C)
    bc = lambda a: jnp.broadcast_to(a, (H, C, RET_D))
    return dmat, bc(xi[:, :, None]), bc(zeta[:, :, None]), bc(dec[:, None, None])


def _overlap_t(S):
    n_cmp = (S - CMP_BLOCK) // CMP_STRIDE + 1
    n_slc = S // SLC_BLOCK
    cs = np.arange(n_cmp)[:, None] * CMP_STRIDE
    ss = np.arange(n_slc)[None, :] * SLC_BLOCK
    ov = np.clip(np.minimum(cs + CMP_BLOCK, ss + SLC_BLOCK) - np.maximum(cs, ss), 0, None) / CMP_BLOCK
    out = np.zeros((n_slc, S // CMP_STRIDE), np.float32)
    out[:, :n_cmp] = ov.T
    return jnp.asarray(out, BF16)


def _v_rows(v_t):
    S = v_t.shape[1]
    return jnp.concatenate([v_t, jnp.ones((1, S), v_t.dtype), jnp.zeros((VROWS - DH - 1, S), v_t.dtype)], axis=0)


_IN_OFFSETS = np.cumsum((0, 512, 128, 128, 128, 128, 128, 128, 24, 512, 512, 512, 512, 1024, 1024))


def _in_col(wi, i):
    return wi[:, _IN_OFFSETS[i]:_IN_OFFSETS[i + 1]]


def _branches(x, positions, norm_mix, w_in, cmp_pos_k, cmp_pos_v, cmp_k_w1, cmp_k_w2, cmp_v_w1, cmp_v_w2):
    B, S, _ = x.shape
    assert B == 1 and norm_mix.shape[0] == 1 and S % TK == 0 and S >= max(WIN_KEYS, SLC_COUNT * SLC_BLOCK)
    xs = x[0]
    col = functools.partial(_in_col, w_in[0])
    w_a = jnp.concatenate([col(0), col(1), col(3), col(5), col(2), col(4), col(6),
                           col(7), jnp.zeros((D_MODEL, LANES - 24), F32),
                           col(8), col(9), col(10), col(11)], axis=1).astype(BF16)

    tables = _rope_tables(positions[0])
    q, kcvc, ks, kw, vsvw, gates, rq, rk, rv, rg = _proj(xs, norm_mix, w_a, tables)

    n_half = S // CMP_STRIDE
    xh = kcvc.reshape(n_half, CMP_STRIDE, 2 * NSA_GROUPS, DH).transpose(2, 0, 1, 3).reshape(
        2 * NSA_GROUPS, n_half, CMP_STRIDE * DH)
    w1 = jnp.stack([cmp_k_w1[0], cmp_v_w1[0]]).astype(BF16)
    pe = jnp.stack([cmp_pos_k[0], cmp_pos_v[0]]).reshape(2, 1, CMP_BLOCK * DH)
    pe = jnp.broadcast_to(pe, (2, 16, CMP_BLOCK * DH)).astype(BF16)
    w2 = jnp.pad(jnp.stack([cmp_k_w2[0], cmp_v_w2[0]]), ((0, 0), (0, 0), (0, LANES - DH))).astype(BF16)
    cmp = _compress(xh, w1, pe, w2)
    kcmp = cmp[0:NSA_GROUPS].astype(BF16)
    vcmp_t = jax.vmap(_v_rows)(cmp[NSA_GROUPS:, :, 0:DH].transpose(0, 2, 1).astype(BF16))

    vt = vsvw.T.reshape(2 * NSA_GROUPS, DH, S)
    vs_t = jax.vmap(_v_rows)(vt[0:NSA_GROUPS])
    vw_t = jnp.pad(jax.vmap(_v_rows)(vt[NSA_GROUPS:]), ((0, 0), (0, 0), (WINDOW, 0)))
    pad_keys = jnp.zeros((NSA_GROUPS, WINDOW, LANES), BF16).at[:, :, KILL_LANE].set(1.0)
    kw = jnp.concatenate([pad_keys, kw], axis=1)
    gt = jnp.pad(gates[:, 0:3 * NSA_HEADS].T.reshape(NSA_GROUPS, 3 * HPG, S), ((0, 0), (0, 16 - 3 * HPG), (0, 0)))
    q_t = q.T.reshape(NSA_HEADS, DH, S)
    nsa_out = _nsa(q_t, kcmp, vcmp_t, ks, vs_t, kw, vw_t, gt, _overlap_t(S))

    ret_out = _retention(rq, rk, rv, rg, *_decay_tables())
    return nsa_out, ret_out


def kernel(x, positions, norm_mix, w_in, cmp_pos_k, cmp_pos_v, cmp_k_w1, cmp_k_w2, cmp_v_w1, cmp_v_w2,
           w_proj_a, w_proj_b, w_out, norm_mlp, w_up, w_down, norm_final):
    nsa_out, ret_out = _branches(x, positions, norm_mix, w_in, cmp_pos_k, cmp_pos_v,
                                 cmp_k_w1, cmp_k_w2, cmp_v_w1, cmp_v_w2)
    xs = x[0]
    w_gate = jnp.concatenate([_in_col(w_in[0], 12), _in_col(w_in[0], 13)], axis=1).astype(BF16)
    x1 = _mix(xs, norm_mix, nsa_out, ret_out, w_gate, w_proj_a[0].astype(BF16), w_proj_b[0].astype(BF16),
              w_out[0].astype(BF16))
    y = _mlp(x1, norm_mlp, w_up[0].astype(BF16), w_down[0].astype(BF16), norm_final[None, :])
    return y[None]
```

```python
import functools
import math

import numpy as np
import jax
import jax.numpy as jnp
from jax import lax
from jax.experimental import pallas as pl
from jax.experimental.pallas import tpu as pltpu

F32 = jnp.float32
BF16 = jnp.bfloat16

D_MODEL = 1024
NSA_HEADS = 8
NSA_GROUPS = 2
HPG = NSA_HEADS // NSA_GROUPS
DH = 64
CMP_BLOCK = 32
CMP_STRIDE = 16
CMP_HIDDEN = 256
SLC_BLOCK = 64
SLC_COUNT = 16
WINDOW = 512
QB = 128
ROPE_THETA = 500000.0
ROPE_DIM = DH // 4
RET_HEADS = 4
RET_D = 128
RET_C = 128
RET_BASE = 10000.0
MLP_HIDDEN = 4 * D_MODEL
EPS = 1e-6
NEG = -1e30
BIG = 1e30

LANES = 128
TK = 1024
BLK_PER_TILE = TK // SLC_BLOCK
KILL_LANE = DH + BLK_PER_TILE
LOG2E = math.log2(math.e)
VROWS = 80
WIN_KEYS = WINDOW + QB
VMEM_LIMIT = 56 * 1024 * 1024

NT_DIMS = (((1,), (1,)), ((), ()))
TN_DIMS = (((0,), (0,)), ((), ()))


def _sigmoid(v):
    return 1.0 / (1.0 + jnp.exp(-v))


def _rms(x, g):
    return x * lax.rsqrt(jnp.mean(x * x, axis=-1, keepdims=True) + EPS) * g


def _params(sem, flags=None):
    return pltpu.CompilerParams(dimension_semantics=sem, vmem_limit_bytes=VMEM_LIMIT, flags=flags)


_C_ROT = (0, 896)
_C_V = (896, 1280)
_C_G = (1280, 1408)
_C_RQ = (1408, 1920)
_C_RK = (1920, 2432)
_C_RV = (2432, 2944)
_C_RG = (2944, 3456)
PROJ_W = 3456


def _proj_kernel(x_ref, g_ref, w_ref, pos_ref, freq_ref,
                 q_ref, xh_ref, ks_ref, kw_ref, vs_ref, vw_ref, gt_ref, rq_ref, rk_ref, rv_ref, rg_ref,
                 kv_ref, *, tm):
    h = _rms(x_ref[...], g_ref[...]).astype(BF16)
    lane = lax.broadcasted_iota(jnp.int32, (tm, LANES), 1)
    lo = lane < DH

    ang = pos_ref[...] * freq_ref[...]
    c, s = jnp.cos(ang), jnp.sin(ang)
    half_n = ROPE_DIM // 2
    cr = jnp.where(lo, c, pltpu.roll(c, DH, 1))
    sr = jnp.where(lo, -s, pltpu.roll(s, DH, 1))
    dim = lane & (DH - 1)
    first, second = dim < half_n, (dim >= half_n) & (dim < ROPE_DIM)
    c1 = jnp.where(lo, pltpu.roll(c, DH, 1), c)
    s1 = jnp.where(lo, pltpu.roll(s, DH, 1), s)
    c2 = jnp.where(lo, pltpu.roll(c, DH + half_n, 1), pltpu.roll(c, half_n, 1))
    s2 = jnp.where(lo, pltpu.roll(s, DH + half_n, 1), pltpu.roll(s, half_n, 1))
    cn = jnp.where(first, c1, jnp.where(second, c2, 1.0))
    sa = jnp.where(first, -s1, 0.0)
    sb = jnp.where(second, s2, 0.0)

    def dot(c):
        return jnp.dot(h, w_ref[:, c[0]:c[1]], preferred_element_type=F32)

    def tile(y, i):
        return y[:, i * LANES:(i + 1) * LANES]

    def rope_n(t):
        return t * cn + pltpu.roll(t, LANES - half_n, 1) * sa + pltpu.roll(t, half_n, 1) * sb

    def rope_r(t):
        return t * cr + pltpu.roll(t, RET_D // 2, 1) * sr

    def split(t):
        return jnp.where(lo, t, 0.0), jnp.where(lo, pltpu.roll(t, DH, 1), 0.0)

    def v_rows(dst_ref, t):
        tt = t.T
        tail = jnp.where(lax.broadcasted_iota(jnp.int32, (VROWS - DH, tm), 0) == 0, 1.0, 0.0).astype(BF16)
        for g in range(NSA_GROUPS):
            dst_ref[g, 0:DH, :] = tt[g * DH:(g + 1) * DH, :].astype(BF16)
            dst_ref[g, DH:VROWS, :] = tail

    yr = dot(_C_ROT)
    for i in range(4):
        tt = (rope_n(tile(yr, i)) * (DH ** -0.5 * LOG2E)).T
        q_ref[2 * i] = tt[0:DH, :].astype(BF16)
        q_ref[2 * i + 1] = tt[DH:2 * DH, :].astype(BF16)
    row = pl.program_id(0) * tm + lax.broadcasted_iota(jnp.int32, (tm, LANES), 0)
    onehot = jnp.where((lane - DH) == ((row >> 6) & (BLK_PER_TILE - 1)), 1.0, 0.0)
    a, b = split(rope_n(tile(yr, 5)))
    ks_ref[0] = jnp.where(lo, a, onehot).astype(BF16)
    ks_ref[1] = jnp.where(lo, b, onehot).astype(BF16)
    a, b = split(rope_n(tile(yr, 6)))
    kw_ref[0] = a.astype(BF16)
    kw_ref[1] = b.astype(BF16)

    yv = dot(_C_V)
    v_rows(vs_ref, tile(yv, 1))
    v_rows(vw_ref, tile(yv, 2))
    gt = _sigmoid(dot(_C_G)).T
    gt_ref[0] = gt[0:16, :]
    gt_ref[1] = gt[16:32, :]

    kv_ref[0] = rope_n(tile(yr, 4))
    kv_ref[1] = tile(yv, 0)
    lo_r = lax.broadcasted_iota(jnp.int32, (tm // CMP_STRIDE, LANES), 1) < DH
    for typ in range(2):
        for u in range(CMP_STRIDE // 2):
            pa = kv_ref[typ, pl.ds(2 * u, tm // CMP_STRIDE, stride=CMP_STRIDE), :]
            pb = kv_ref[typ, pl.ds(2 * u + 1, tm // CMP_STRIDE, stride=CMP_STRIDE), :]
            cols = slice(u * LANES, (u + 1) * LANES)
            xh_ref[2 * typ, :, cols] = jnp.where(lo_r, pa, pltpu.roll(pb, DH, 1)).astype(BF16)
            xh_ref[2 * typ + 1, :, cols] = jnp.where(lo_r, pltpu.roll(pa, DH, 1), pb).astype(BF16)

    yq = dot(_C_RQ)
    yk = dot(_C_RK)
    for i in range(RET_HEADS):
        sl = slice(i * LANES, (i + 1) * LANES)
        rq_ref[:, sl] = rope_r(tile(yq, i)).astype(BF16)
        rk_ref[:, sl] = (rope_r(tile(yk, i)) * (RET_D ** -0.5)).astype(BF16)
    rv_ref[...] = dot(_C_RV).astype(BF16)
    rg_ref[...] = dot(_C_RG).astype(BF16)


def _proj(x, g, w, pos, freq, tm=512):
    S = x.shape[0]
    row = lambda i: (i, 0)
    const = lambda i: (0, 0)
    mid = lambda i: (0, i, 0)
    last = lambda i: (0, 0, i)
    W = RET_HEADS * RET_D
    out_shape = (
        jax.ShapeDtypeStruct((NSA_HEADS, DH, S), BF16),
        jax.ShapeDtypeStruct((2 * NSA_GROUPS, S // CMP_STRIDE, CMP_STRIDE * DH), BF16),
        jax.ShapeDtypeStruct((NSA_GROUPS, S, LANES), BF16),
        jax.ShapeDtypeStruct((NSA_GROUPS, S, LANES), BF16),
        jax.ShapeDtypeStruct((NSA_GROUPS, VROWS, S), BF16),
        jax.ShapeDtypeStruct((NSA_GROUPS, VROWS, S), BF16),
        jax.ShapeDtypeStruct((NSA_GROUPS, 16, S), F32),
        jax.ShapeDtypeStruct((S, W), BF16),
        jax.ShapeDtypeStruct((S, W), BF16),
        jax.ShapeDtypeStruct((S, W), BF16),
        jax.ShapeDtypeStruct((S, W), BF16),
    )
    out_specs = (
        pl.BlockSpec((NSA_HEADS, DH, tm), last),
        pl.BlockSpec((2 * NSA_GROUPS, tm // CMP_STRIDE, CMP_STRIDE * DH), mid),
        pl.BlockSpec((NSA_GROUPS, tm, LANES), mid),
        pl.BlockSpec((NSA_GROUPS, tm, LANES), mid),
        pl.BlockSpec((NSA_GROUPS, VROWS, tm), last),
        pl.BlockSpec((NSA_GROUPS, VROWS, tm), last),
        pl.BlockSpec((NSA_GROUPS, 16, tm), last),
        pl.BlockSpec((tm, W), row),
        pl.BlockSpec((tm, W), row),
        pl.BlockSpec((tm, W), row),
        pl.BlockSpec((tm, W), row),
    )
    return pl.pallas_call(
        functools.partial(_proj_kernel, tm=tm),
        grid=(S // tm,),
        in_specs=[pl.BlockSpec((tm, D_MODEL), row), pl.BlockSpec((1, D_MODEL), const),
                  pl.BlockSpec((D_MODEL, PROJ_W), const), pl.BlockSpec((tm, LANES), row),
                  pl.BlockSpec((1, LANES), const)],
        out_specs=out_specs, out_shape=out_shape,
        scratch_shapes=[pltpu.VMEM((2, tm, LANES), F32)],
        compiler_params=_params(("arbitrary",)), name="proj",
    )(x, g, w, pos, freq)


def _compress_kernel(x_ref, w1_ref, pe_ref, w2_ref, o_ref):
    x = x_ref[...]
    half = CMP_STRIDE * DH
    a = jnp.dot(x, w1_ref[0:half, :], preferred_element_type=F32)
    b = jnp.dot(x, w1_ref[half:2 * half, :], preferred_element_type=F32)
    peb = jnp.dot(pe_ref[...], w1_ref[...], preferred_element_type=F32)[0:1, :]
    hid = a + pltpu.roll(b, x.shape[0] - 1, 0) + peb
    hid = hid * _sigmoid(hid)
    o_ref[...] = jnp.dot(hid.astype(BF16), w2_ref[...], preferred_element_type=F32)


def _compress(xh, w1, pe, w2):
    n_half = xh.shape[1]
    return pl.pallas_call(
        _compress_kernel,
        grid=(2 * NSA_GROUPS,),
        in_specs=[pl.BlockSpec((None, n_half, CMP_STRIDE * DH), lambda i: (i, 0, 0)),
                  pl.BlockSpec((None, CMP_BLOCK * DH, CMP_HIDDEN), lambda i: (i // NSA_GROUPS, 0, 0)),
                  pl.BlockSpec((None, 16, CMP_BLOCK * DH), lambda i: (i // NSA_GROUPS, 0, 0)),
                  pl.BlockSpec((None, CMP_HIDDEN, LANES), lambda i: (i // NSA_GROUPS, 0, 0))],
        out_specs=pl.BlockSpec((None, n_half, LANES), lambda i: (i, 0, 0)),
        out_shape=jax.ShapeDtypeStruct((2 * NSA_GROUPS, n_half, LANES), F32),
        compiler_params=_params(("arbitrary",)), name="compress",
    )(xh, w1, pe, w2)


def _nsa_kernel(q_ref, qn_ref, kcmp_ref, vcmp_ref, ks_ref, vs_ref, kw_ref, vw_ref, gt_ref, ovl_ref,
                out_ref, qa_ref, qs_ref, bias_ref, biasn_ref, ocn_ref, m_ref, acc_ref,
                sa_ref, sb_ref, mta_ref, mtb_ref, p_ref, *, S):
    NC = S // CMP_STRIDE
    NB = S // SLC_BLOCK
    NQ = HPG * QB
    b = pl.program_id(1)
    s0 = b * QB
    lane_q = lax.broadcasted_iota(jnp.int32, (1, NQ), 1) & (QB - 1)
    band_row = lax.broadcasted_iota(jnp.int32, (QB, NQ), 0)
    band_tq = lax.broadcasted_iota(jnp.int32, (QB, NQ), 1) & (QB - 1)

    def sel_scores(qsrc_ref):
        qs_ref[0:DH, :] = jnp.concatenate([qsrc_ref[h] for h in range(HPG)], axis=1)
        qs_ref[DH:LANES, :] = jnp.zeros((LANES - DH, NQ), BF16)
        return jnp.dot(kcmp_ref[...], qs_ref[...], preferred_element_type=F32)

    def sel_probs(sc, blk):
        c_end = lax.broadcasted_iota(jnp.int32, (NC, NQ), 0) * CMP_STRIDE + (CMP_BLOCK - 1)
        sc = jnp.where(c_end <= blk * QB + lane_q, sc, NEG)
        return jnp.exp2(sc - jnp.max(sc, axis=0, keepdims=True)).astype(BF16)

    def sel_matmuls(pc):
        return (jnp.dot(vcmp_ref[...], pc, preferred_element_type=F32),
                jnp.dot(ovl_ref[...], pc, preferred_element_type=F32))

    def sel_choose(ocl, imp4, blk):
        t_row = blk * QB + lane_q
        any_valid = jnp.where(t_row >= CMP_BLOCK - 1, 1.0, 0.0)
        rc = any_valid / ocl[DH:DH + 1, :]
        ocn_ref[...] = ocl[0:DH, :] * rc

        imp4 = imp4 * rc
        imp = imp4[:, 0:QB] + imp4[:, QB:2 * QB] + imp4[:, 2 * QB:3 * QB] + imp4[:, 3 * QB:4 * QB]

        jidx = lax.broadcasted_iota(jnp.int32, (NB, QB), 0)
        cur = (blk * QB + lax.broadcasted_iota(jnp.int32, (1, QB), 1)) >> 6
        forced = (jidx == 0) | (jidx == cur) | (jidx == cur - 1)
        future = jidx > cur
        cand = jnp.where(forced | future, -jnp.inf, imp)

        def select(break_ties):
            v = cand
            bias_t = jnp.where(forced, 0.0, NEG)
            for _ in range(SLC_COUNT - 3):
                mx = jnp.max(v, axis=0, keepdims=True)
                hit = v == mx
                if break_ties:
                    hit = jidx == jnp.min(jnp.where(hit, jidx, NB), axis=0, keepdims=True)
                bias_t = jnp.where(hit, 0.0, bias_t)
                v = jnp.where(hit, -jnp.inf, v)
            return jnp.where(future, NEG, bias_t)

        bias_t = select(False)
        n_sel = jnp.sum(jnp.where((bias_t == 0.0) & (jidx <= cur), 1.0, 0.0), axis=0, keepdims=True)
        biasn_ref[...] = bias_t.astype(BF16)

        def redo_if_tied():
            @pl.when(jnp.max(n_sel) > SLC_COUNT)
            def _():
                biasn_ref[...] = select(True).astype(BF16)
        return redo_if_tied

    @pl.when(b == 0)
    def _():
        ocl, imp4 = sel_matmuls(sel_probs(sel_scores(q_ref), 0))
        sel_choose(ocl, imp4, 0)()

    bias_ref[...] = biasn_ref[...]
    oc = ocn_ref[...]

    qa_ref[0:DH, :] = jnp.concatenate([q_ref[h] for h in range(HPG)], axis=1)
    qa_ref[DH:KILL_LANE, :] = jnp.zeros((KILL_LANE - DH, NQ), BF16)
    kill_rows = lax.broadcasted_iota(jnp.int32, (LANES - KILL_LANE, NQ), 0) == 0
    qa_ref[KILL_LANE:LANES, :] = jnp.where(kill_rows, NEG, 0.0).astype(BF16)

    def scores(kt, s_ref, mt_ref, tile_max=True):
        k0 = pl.multiple_of(kt * TK, TK)
        bias_rows = bias_ref[pl.ds(pl.multiple_of(kt * BLK_PER_TILE, BLK_PER_TILE), BLK_PER_TILE), :]
        qa_ref[DH:DH + BLK_PER_TILE, :] = jnp.concatenate([bias_rows] * HPG, axis=1)
        s = jnp.dot(ks_ref[pl.ds(k0, TK), :], qa_ref[...], preferred_element_type=F32)
        s_ref[...] = s
        if tile_max:
            mt_ref[...] = jnp.max(s, axis=0, keepdims=True)

    def accumulate(kt, s_ref, mt_ref):
        k0 = pl.multiple_of(kt * TK, TK)
        m_old = m_ref[...]
        m_new = jnp.maximum(m_old, mt_ref[...])
        p_ref[...] = jnp.exp2(s_ref[...] - m_new).astype(BF16)
        acc_ref[...] = (jnp.exp2(m_old - m_new) * acc_ref[...]
                        + jnp.dot(vs_ref[:, pl.ds(k0, TK)], p_ref[...], preferred_element_type=F32))
        m_ref[...] = m_new

    n_full = s0 // TK

    m_ref[...] = jnp.full((1, NQ), NEG, F32)
    acc_ref[...] = jnp.zeros((VROWS, NQ), F32)

    sc = sel_scores(qn_ref)
    sw = jnp.dot(kw_ref[pl.ds(pl.multiple_of(s0, QB), WIN_KEYS), :], qa_ref[...], preferred_element_type=F32)
    scores(n_full, sa_ref, mta_ref, tile_max=False)

    pc = sel_probs(sc, b + 1)
    ocl, imp4 = sel_matmuls(pc)
    scores(0, sb_ref, mtb_ref)
    redo_if_tied = sel_choose(ocl, imp4, b + 1)

    sw = jnp.concatenate([jnp.where(band_row > band_tq, sw[0:QB], NEG),
                          sw[QB:WINDOW],
                          jnp.where(band_row <= band_tq, sw[WINDOW:WIN_KEYS], NEG)], axis=0)
    pw = jnp.exp2(sw - jnp.max(sw, axis=0, keepdims=True)).astype(BF16)
    band = pl.ds(pl.multiple_of(s0 - n_full * TK, QB), QB)
    sa_ref[band, :] = jnp.where(band_row <= band_tq, sa_ref[band, :], NEG)
    mta_ref[...] = jnp.max(sa_ref[...], axis=0, keepdims=True)

    ow = jnp.dot(vw_ref[:, pl.ds(pl.multiple_of(s0, QB), WIN_KEYS)], pw, preferred_element_type=F32)
    owin = ow[0:DH, :] * (1.0 / ow[DH:DH + 1, :])
    accumulate(n_full, sa_ref, mta_ref)
    redo_if_tied()

    def older_tile(j, last):
        def stages(s_cur, mt_cur, s_oth, mt_oth):
            if not last:
                scores(j + 1, s_oth, mt_oth)
            accumulate(j, s_cur, mt_cur)

        @pl.when(j % 2 == 0)
        def _():
            stages(sb_ref, mtb_ref, sa_ref, mta_ref)

        @pl.when(j % 2 == 1)
        def _():
            stages(sa_ref, mta_ref, sb_ref, mtb_ref)

    def body(j, carry):
        older_tile(j, False)
        return carry

    lax.fori_loop(0, n_full - 1, body, 0)

    @pl.when(n_full > 0)
    def _():
        older_tile(n_full - 1, True)

    acc = acc_ref[...]
    osel = acc[0:DH, :] * (1.0 / acc[DH:DH + 1, :])

    heads = []
    for h in range(HPG):
        sl = slice(h * QB, (h + 1) * QB)
        heads.append(gt_ref[3 * h:3 * h + 1, :] * oc[:, sl]
                     + gt_ref[3 * h + 1:3 * h + 2, :] * osel[:, sl]
                     + gt_ref[3 * h + 2:3 * h + 3, :] * owin[:, sl])
    out_ref[...] = jnp.concatenate(heads, axis=0).T.astype(out_ref.dtype)


def _nsa(q_t, kcmp, vcmp_t, ks, vs_t, kw, vw_t, gt, ovl_t):
    S = ks.shape[1]
    NC, NB = S // CMP_STRIDE, S // SLC_BLOCK
    n_qb = S // QB
    grp = lambda g, b: (g, 0, 0)
    return pl.pallas_call(
        functools.partial(_nsa_kernel, S=S),
        grid=(NSA_GROUPS, n_qb),
        in_specs=[pl.BlockSpec((HPG, DH, QB), lambda g, b: (g, 0, b)),
                  pl.BlockSpec((HPG, DH, QB), lambda g, b: (g, 0, jnp.minimum(b + 1, n_qb - 1))),
                  pl.BlockSpec((None, NC, LANES), grp),
                  pl.BlockSpec((None, VROWS, NC), grp),
                  pl.BlockSpec((None, S, LANES), grp),
                  pl.BlockSpec((None, VROWS, S), grp),
                  pl.BlockSpec((None, WINDOW + S, LANES), grp),
                  pl.BlockSpec((None, VROWS, WINDOW + S), grp),
                  pl.BlockSpec((None, 16, QB), lambda g, b: (g, 0, b)),
                  pl.BlockSpec((NB, NC), lambda g, b: (0, 0))],
        out_specs=pl.BlockSpec((QB, HPG * DH), lambda g, b: (b, g)),
        out_shape=jax.ShapeDtypeStruct((S, NSA_HEADS * DH), BF16),
        scratch_shapes=[pltpu.VMEM((LANES, HPG * QB), BF16),
                        pltpu.VMEM((LANES, HPG * QB), BF16),
                        pltpu.VMEM((NB, QB), BF16),
                        pltpu.VMEM((NB, QB), BF16),
                        pltpu.VMEM((DH, HPG * QB), F32),
                        pltpu.VMEM((1, HPG * QB), F32),
                        pltpu.VMEM((VROWS, HPG * QB), F32),
                        pltpu.VMEM((TK, HPG * QB), F32),
                        pltpu.VMEM((TK, HPG * QB), F32),
                        pltpu.VMEM((1, HPG * QB), F32),
                        pltpu.VMEM((1, HPG * QB), F32),
                        pltpu.VMEM((TK, HPG * QB), BF16)],
        compiler_params=_params(("arbitrary", "arbitrary")), name="nsa",
    )(q_t, q_t, kcmp, vcmp_t, ks, vs_t, kw, vw_t, gt, ovl_t)


def _ret_kernel(q_ref, k_ref, v_ref, g_ref, dmat_ref, xi_ref, zeta_ref, dec_ref, o_ref, state_ref, *, tm):
    @pl.when(pl.program_id(0) == 0)
    def _():
        state_ref[...] = jnp.zeros_like(state_ref)

    for j in range(tm // RET_C):
        rows = slice(j * RET_C, (j + 1) * RET_C)
        for h in range(RET_HEADS):
            cols = slice(h * RET_D, (h + 1) * RET_D)
            qc, kc, vc = q_ref[rows, cols], k_ref[rows, cols], v_ref[rows, cols]
            st = state_ref[h]
            st_hi = st.astype(BF16)
            st_lo = (st - st_hi.astype(F32)).astype(BF16)
            sc = lax.dot_general(qc, kc, NT_DIMS, preferred_element_type=F32) * dmat_ref[h]
            inner = jnp.dot(sc.astype(BF16), vc, preferred_element_type=F32)
            cross = (jnp.dot(qc, st_hi, preferred_element_type=F32)
                     + jnp.dot(qc, st_lo, preferred_element_type=F32)) * xi_ref[h]
            kz = (kc.astype(F32) * zeta_ref[h]).astype(BF16)
            state_ref[h] = dec_ref[h] * st + lax.dot_general(kz, vc, TN_DIMS, preferred_element_type=F32)
            o = inner + cross
            o = o * lax.rsqrt(jnp.mean(o * o, axis=-1, keepdims=True) + EPS)
            gate = g_ref[rows, cols].astype(F32)
            o_ref[rows, cols] = (gate * _sigmoid(gate) * o).astype(o_ref.dtype)


def _retention(rq, rk, rv, rg, dmat, xi, zeta, dec, tm=512):
    S = rq.shape[0]
    W = RET_HEADS * RET_D
    row = lambda i: (i, 0)
    cst = lambda i: (0, 0, 0)
    blk = pl.BlockSpec((tm, W), row)
    tbl = pl.BlockSpec((RET_HEADS, RET_C, RET_D), cst)
    return pl.pallas_call(
        functools.partial(_ret_kernel, tm=tm),
        grid=(S // tm,),
        in_specs=[blk, blk, blk, blk, tbl, tbl, tbl, tbl],
        out_specs=blk,
        out_shape=jax.ShapeDtypeStruct((S, W), BF16),
        scratch_shapes=[pltpu.VMEM((RET_HEADS, RET_D, RET_D), F32)],
        compiler_params=_params(("arbitrary",)), name="retention",
    )(rq, rk, rv, rg, dmat, xi, zeta, dec)


def _mix_kernel(x_ref, g_ref, a_ref, r_ref, wg_ref, wpa_ref, wpb_ref, wo_ref, o_ref):
    x = x_ref[...]
    h = _rms(x, g_ref[...]).astype(BF16)
    ga = _sigmoid(jnp.dot(h, wg_ref[:, 0:D_MODEL], preferred_element_type=F32))
    gb = _sigmoid(jnp.dot(h, wg_ref[:, D_MODEL:2 * D_MODEL], preferred_element_type=F32))
    mix = (ga * jnp.dot(a_ref[...], wpa_ref[...], preferred_element_type=F32)
           + gb * jnp.dot(r_ref[...], wpb_ref[...], preferred_element_type=F32))
    o_ref[...] = x + jnp.dot(mix.astype(BF16), wo_ref[...], preferred_element_type=F32)


def _mix(x, g, nsa_out, ret_out, wg, wpa, wpb, wo, tm=512):
    S = x.shape[0]
    row = lambda i: (i, 0)
    cst = lambda i: (0, 0)
    return pl.pallas_call(
        _mix_kernel,
        grid=(S // tm,),
        in_specs=[pl.BlockSpec((tm, D_MODEL), row), pl.BlockSpec((1, D_MODEL), cst),
                  pl.BlockSpec((tm, 512), row), pl.BlockSpec((tm, 512), row),
                  pl.BlockSpec((D_MODEL, 2 * D_MODEL), cst), pl.BlockSpec((512, D_MODEL), cst),
                  pl.BlockSpec((512, D_MODEL), cst), pl.BlockSpec((D_MODEL, D_MODEL), cst)],
        out_specs=pl.BlockSpec((tm, D_MODEL), row),
        out_shape=jax.ShapeDtypeStruct((S, D_MODEL), F32),
        compiler_params=_params(("arbitrary",)), name="mix",
    )(x, g, nsa_out, ret_out, wg, wpa, wpb, wo)


def _mlp_kernel(x_ref, g_ref, wu_ref, wd_ref, gf_ref, o_ref, *, hc):
    x = x_ref[...]
    h = _rms(x, g_ref[...]).astype(BF16)
    acc = x
    for c in range(MLP_HIDDEN // hc):
        u = jnp.maximum(jnp.dot(h, wu_ref[:, c * hc:(c + 1) * hc], preferred_element_type=F32), 0.0)
        acc = acc + jnp.dot((u * u).astype(BF16), wd_ref[c * hc:(c + 1) * hc, :], preferred_element_type=F32)
    o_ref[...] = _rms(acc, gf_ref[...])


def _mlp(x, g, wu, wd, gf, tm=512, hc=1024):
    S = x.shape[0]
    row = lambda i: (i, 0)
    cst = lambda i: (0, 0)
    return pl.pallas_call(
        functools.partial(_mlp_kernel, hc=hc),
        grid=(S // tm,),
        in_specs=[pl.BlockSpec((tm, D_MODEL), row), pl.BlockSpec((1, D_MODEL), cst),
                  pl.BlockSpec((D_MODEL, MLP_HIDDEN), cst), pl.BlockSpec((MLP_HIDDEN, D_MODEL), cst),
                  pl.BlockSpec((1, D_MODEL), cst)],
        out_specs=pl.BlockSpec((tm, D_MODEL), row),
        out_shape=jax.ShapeDtypeStruct((S, D_MODEL), F32),
        compiler_params=_params(("arbitrary",)), name="mlp",
    )(x, g, wu, wd, gf)


def _rope_freqs():
    half = ROPE_DIM // 2
    inv = jnp.exp(-math.log(ROPE_THETA) * jnp.arange(half, dtype=F32) * 2.0 / ROPE_DIM)
    hr = RET_D // 2
    inv_r = jnp.exp(-math.log(RET_BASE) * jnp.arange(hr, dtype=F32) * 2.0 / RET_D)
    return jnp.concatenate([inv_r, inv, jnp.zeros((LANES - hr - half,), F32)])[None, :]


def _decay_tables():
    H, C = RET_HEADS, RET_C
    log_g = jnp.log(1.0 - jnp.exp2(-5.0 - jnp.arange(H, dtype=F32)))
    n = jnp.arange(C, dtype=F32)
    rel = n[:, None] - n[None, :]
    dmat = jnp.where(rel >= 0, jnp.exp(log_g[:, None, None] * jnp.maximum(rel, 0.0)), 0.0)
    xi = jnp.exp(log_g[:, None] * (n + 1.0))
    zeta = jnp.exp(log_g[:, None] * (C - 1.0 - n))
    dec = jnp.exp(log_g * C)
    bc = lambda a: jnp.broadcast_to(a, (H, C, RET_D))
    return dmat, bc(xi[:, :, None]), bc(zeta[:, :, None]), bc(dec[:, None, None])


def _overlap_t(S):
    n_cmp = (S - CMP_BLOCK) // CMP_STRIDE + 1
    n_slc = S // SLC_BLOCK
    cs = np.arange(n_cmp)[:, None] * CMP_STRIDE
    ss = np.arange(n_slc)[None, :] * SLC_BLOCK
    ov = np.clip(np.minimum(cs + CMP_BLOCK, ss + SLC_BLOCK) - np.maximum(cs, ss), 0, None) / CMP_BLOCK
    out = np.zeros((n_slc, S // CMP_STRIDE), np.float32)
    out[:, :n_cmp] = ov.T
    return jnp.asarray(out, BF16)


def _v_rows(v_t):
    S = v_t.shape[1]
    return jnp.concatenate([v_t, jnp.ones((1, S), v_t.dtype), jnp.zeros((VROWS - DH - 1, S), v_t.dtype)], axis=0)


_IN_OFFSETS = np.cumsum((0, 512, 128, 128, 128, 128, 128, 128, 24, 512, 512, 512, 512, 1024, 1024))


def _in_col(wi, i):
    return wi[:, _IN_OFFSETS[i]:_IN_OFFSETS[i + 1]]


def _branches(x, positions, norm_mix, w_in, cmp_pos_k, cmp_pos_v, cmp_k_w1, cmp_k_w2, cmp_v_w1, cmp_v_w2):
    B, S, _ = x.shape
    assert B == 1 and norm_mix.shape[0] == 1 and S % TK == 0 and S >= max(WIN_KEYS, SLC_COUNT * SLC_BLOCK)
    xs = x[0]
    col = functools.partial(_in_col, w_in[0])
    n_gate = 3 * HPG
    z = lambda n: jnp.zeros((D_MODEL, n), F32)
    w_a = jnp.concatenate([col(0), col(1), col(3), col(5), col(2), col(4), col(6),
                           col(7)[:, 0:n_gate], z(16 - n_gate), col(7)[:, n_gate:], z(LANES - 16 - n_gate),
                           col(8), col(9), col(10), col(11)], axis=1).astype(BF16)

    q_t, xh, ks, kw, vs_t, vw_t, gt, rq, rk, rv, rg = _proj(
        xs, norm_mix, w_a, jnp.broadcast_to(positions[0].astype(F32)[:, None], (S, LANES)), _rope_freqs())

    w1 = jnp.stack([cmp_k_w1[0], cmp_v_w1[0]]).astype(BF16)
    pe = jnp.stack([cmp_pos_k[0], cmp_pos_v[0]]).reshape(2, 1, CMP_BLOCK * DH)
    pe = jnp.broadcast_to(pe, (2, 16, CMP_BLOCK * DH)).astype(BF16)
    w2 = jnp.pad(jnp.stack([cmp_k_w2[0], cmp_v_w2[0]]), ((0, 0), (0, 0), (0, LANES - DH))).astype(BF16)
    cmp = _compress(xh, w1, pe, w2)
    kcmp = cmp[0:NSA_GROUPS].astype(BF16)
    vcmp_t = jax.vmap(_v_rows)(cmp[NSA_GROUPS:, :, 0:DH].transpose(0, 2, 1).astype(BF16))

    vw_t = jnp.pad(vw_t, ((0, 0), (0, 0), (WINDOW, 0)))
    pad_keys = jnp.zeros((NSA_GROUPS, WINDOW, LANES), BF16).at[:, :, KILL_LANE].set(1.0)
    kw = jnp.concatenate([pad_keys, kw], axis=1)
    nsa_out = _nsa(q_t, kcmp, vcmp_t, ks, vs_t, kw, vw_t, gt, _overlap_t(S))

    ret_out = _retention(rq, rk, rv, rg, *_decay_tables())
    return nsa_out, ret_out


def kernel(x, positions, norm_mix, w_in, cmp_pos_k, cmp_pos_v, cmp_k_w1, cmp_k_w2, cmp_v_w1, cmp_v_w2,
           w_proj_a, w_proj_b, w_out, norm_mlp, w_up, w_down, norm_final):
    nsa_out, ret_out = _branches(x, positions, norm_mix, w_in, cmp_pos_k, cmp_pos_v,
                                 cmp_k_w1, cmp_k_w2, cmp_v_w1, cmp_v_w2)
    xs = x[0]
    w_gate = jnp.concatenate([_in_col(w_in[0], 12), _in_col(w_in[0], 13)], axis=1).astype(BF16)
    x1 = _mix(xs, norm_mix, nsa_out, ret_out, w_gate, w_proj_a[0].astype(BF16), w_proj_b[0].astype(BF16),
              w_out[0].astype(BF16))
    y = _mlp(x1, norm_mlp, w_up[0].astype(BF16), w_down[0].astype(BF16), norm_final[None, :])
    return y[None]
```

```python
import functools
import math

import numpy as np
import jax
import jax.numpy as jnp
from jax import lax
from jax.experimental import pallas as pl
from jax.experimental.pallas import tpu as pltpu

F32 = jnp.float32
BF16 = jnp.bfloat16

D_MODEL = 1024
NSA_HEADS = 8
NSA_GROUPS = 2
HPG = NSA_HEADS // NSA_GROUPS
DH = 64
CMP_BLOCK = 32
CMP_STRIDE = 16
CMP_HIDDEN = 256
SLC_BLOCK = 64
SLC_COUNT = 16
WINDOW = 512
QB = 256
ROPE_THETA = 500000.0
ROPE_DIM = DH // 4
RET_HEADS = 4
RET_D = 128
RET_C = 128
RET_BASE = 10000.0
MLP_HIDDEN = 4 * D_MODEL
EPS = 1e-6
NEG = -1e30
BIG = 1e30

LANES = 128
TK = 1024
BLK_PER_TILE = TK // SLC_BLOCK
IMP_ROWS = 64
KILL_LANE = DH + BLK_PER_TILE
LOG2E = math.log2(math.e)
VROWS = 80
WIN_KEYS = WINDOW + QB
VMEM_LIMIT = 56 * 1024 * 1024

NT_DIMS = (((1,), (1,)), ((), ()))
TN_DIMS = (((0,), (0,)), ((), ()))


def _sigmoid(v):
    return 1.0 / (1.0 + jnp.exp(-v))


def _rms(x, g):
    return x * lax.rsqrt(jnp.mean(x * x, axis=-1, keepdims=True) + EPS) * g


def _params(sem, flags=None):
    return pltpu.CompilerParams(dimension_semantics=sem, vmem_limit_bytes=VMEM_LIMIT, flags=flags)


_C_ROT = (0, 896)
_C_V = (896, 1280)
_C_G = (1280, 1408)
_C_RQ = (1408, 1920)
_C_RK = (1920, 2432)
_C_RV = (2432, 2944)
_C_RG = (2944, 3456)
PROJ_W = 3456


def _proj_kernel(x_ref, g_ref, w_ref, pos_ref, freq_ref,
                 q_ref, xh_ref, ks_ref, kw_ref, vs_ref, vw_ref, gt_ref, rq_ref, rk_ref, rv_ref, rg_ref,
                 kv_ref, *, tm):
    h = _rms(x_ref[...], g_ref[...]).astype(BF16)
    lane = lax.broadcasted_iota(jnp.int32, (tm, LANES), 1)
    lo = lane < DH

    ang = pos_ref[...] * freq_ref[...]
    c, s = jnp.cos(ang), jnp.sin(ang)
    half_n = ROPE_DIM // 2
    cr = jnp.where(lo, c, pltpu.roll(c, DH, 1))
    sr = jnp.where(lo, -s, pltpu.roll(s, DH, 1))
    dim = lane & (DH - 1)
    first, second = dim < half_n, (dim >= half_n) & (dim < ROPE_DIM)
    c1 = jnp.where(lo, pltpu.roll(c, DH, 1), c)
    s1 = jnp.where(lo, pltpu.roll(s, DH, 1), s)
    c2 = jnp.where(lo, pltpu.roll(c, DH + half_n, 1), pltpu.roll(c, half_n, 1))
    s2 = jnp.where(lo, pltpu.roll(s, DH + half_n, 1), pltpu.roll(s, half_n, 1))
    cn = jnp.where(first, c1, jnp.where(second, c2, 1.0))
    sa = jnp.where(first, -s1, 0.0)
    sb = jnp.where(second, s2, 0.0)

    def dot(c):
        return jnp.dot(h, w_ref[:, c[0]:c[1]], preferred_element_type=F32)

    def tile(y, i):
        return y[:, i * LANES:(i + 1) * LANES]

    def rope_n(t):
        return t * cn + pltpu.roll(t, LANES - half_n, 1) * sa + pltpu.roll(t, half_n, 1) * sb

    def rope_r(t):
        return t * cr + pltpu.roll(t, RET_D // 2, 1) * sr

    def split(t):
        return jnp.where(lo, t, 0.0), jnp.where(lo, pltpu.roll(t, DH, 1), 0.0)

    def v_rows(dst_ref, t):
        tt = t.T
        tail = jnp.where(lax.broadcasted_iota(jnp.int32, (VROWS - DH, tm), 0) == 0, 1.0, 0.0).astype(BF16)
        for g in range(NSA_GROUPS):
            dst_ref[g, 0:DH, :] = tt[g * DH:(g + 1) * DH, :].astype(BF16)
            dst_ref[g, DH:VROWS, :] = tail

    yr = dot(_C_ROT)
    for i in range(4):
        tt = (rope_n(tile(yr, i)) * (DH ** -0.5 * LOG2E)).T
        q_ref[2 * i] = tt[0:DH, :].astype(BF16)
        q_ref[2 * i + 1] = tt[DH:2 * DH, :].astype(BF16)
    row = pl.program_id(0) * tm + lax.broadcasted_iota(jnp.int32, (tm, LANES), 0)
    onehot = jnp.where((lane - DH) == ((row >> 6) & (BLK_PER_TILE - 1)), 1.0, 0.0)
    a, b = split(rope_n(tile(yr, 5)))
    ks_ref[0] = jnp.where(lo, a, onehot).astype(BF16)
    ks_ref[1] = jnp.where(lo, b, onehot).astype(BF16)
    a, b = split(rope_n(tile(yr, 6)))
    kw_ref[0] = a.astype(BF16)
    kw_ref[1] = b.astype(BF16)

    yv = dot(_C_V)
    v_rows(vs_ref, tile(yv, 1))
    v_rows(vw_ref, tile(yv, 2))
    gt = _sigmoid(dot(_C_G)).T
    gt_ref[0] = gt[0:16, :]
    gt_ref[1] = gt[16:32, :]

    kv_ref[0] = rope_n(tile(yr, 4))
    kv_ref[1] = tile(yv, 0)
    lo_r = lax.broadcasted_iota(jnp.int32, (tm // CMP_STRIDE, LANES), 1) < DH
    for typ in range(2):
        for u in range(CMP_STRIDE // 2):
            pa = kv_ref[typ, pl.ds(2 * u, tm // CMP_STRIDE, stride=CMP_STRIDE), :]
            pb = kv_ref[typ, pl.ds(2 * u + 1, tm // CMP_STRIDE, stride=CMP_STRIDE), :]
            cols = slice(u * LANES, (u + 1) * LANES)
            xh_ref[2 * typ, :, cols] = jnp.where(lo_r, pa, pltpu.roll(pb, DH, 1)).astype(BF16)
            xh_ref[2 * typ + 1, :, cols] = jnp.where(lo_r, pltpu.roll(pa, DH, 1), pb).astype(BF16)

    yq = dot(_C_RQ)
    yk = dot(_C_RK)
    for i in range(RET_HEADS):
        sl = slice(i * LANES, (i + 1) * LANES)
        rq_ref[:, sl] = rope_r(tile(yq, i)).astype(BF16)
        rk_ref[:, sl] = (rope_r(tile(yk, i)) * (RET_D ** -0.5)).astype(BF16)
    rv_ref[...] = dot(_C_RV).astype(BF16)
    rg_ref[...] = dot(_C_RG).astype(BF16)


def _proj(x, g, w, pos, freq, tm=512):
    S = x.shape[0]
    row = lambda i: (i, 0)
    const = lambda i: (0, 0)
    mid = lambda i: (0, i, 0)
    last = lambda i: (0, 0, i)
    W = RET_HEADS * RET_D
    out_shape = (
        jax.ShapeDtypeStruct((NSA_HEADS, DH, S), BF16),
        jax.ShapeDtypeStruct((2 * NSA_GROUPS, S // CMP_STRIDE, CMP_STRIDE * DH), BF16),
        jax.ShapeDtypeStruct((NSA_GROUPS, S, LANES), BF16),
        jax.ShapeDtypeStruct((NSA_GROUPS, S, LANES), BF16),
        jax.ShapeDtypeStruct((NSA_GROUPS, VROWS, S), BF16),
        jax.ShapeDtypeStruct((NSA_GROUPS, VROWS, S), BF16),
        jax.ShapeDtypeStruct((NSA_GROUPS, 16, S), F32),
        jax.ShapeDtypeStruct((S, W), BF16),
        jax.ShapeDtypeStruct((S, W), BF16),
        jax.ShapeDtypeStruct((S, W), BF16),
        jax.ShapeDtypeStruct((S, W), BF16),
    )
    out_specs = (
        pl.BlockSpec((NSA_HEADS, DH, tm), last),
        pl.BlockSpec((2 * NSA_GROUPS, tm // CMP_STRIDE, CMP_STRIDE * DH), mid),
        pl.BlockSpec((NSA_GROUPS, tm, LANES), mid),
        pl.BlockSpec((NSA_GROUPS, tm, LANES), mid),
        pl.BlockSpec((NSA_GROUPS, VROWS, tm), last),
        pl.BlockSpec((NSA_GROUPS, VROWS, tm), last),
        pl.BlockSpec((NSA_GROUPS, 16, tm), last),
        pl.BlockSpec((tm, W), row),
        pl.BlockSpec((tm, W), row),
        pl.BlockSpec((tm, W), row),
        pl.BlockSpec((tm, W), row),
    )
    return pl.pallas_call(
        functools.partial(_proj_kernel, tm=tm),
        grid=(S // tm,),
        in_specs=[pl.BlockSpec((tm, D_MODEL), row), pl.BlockSpec((1, D_MODEL), const),
                  pl.BlockSpec((D_MODEL, PROJ_W), const), pl.BlockSpec((tm, LANES), row),
                  pl.BlockSpec((1, LANES), const)],
        out_specs=out_specs, out_shape=out_shape,
        scratch_shapes=[pltpu.VMEM((2, tm, LANES), F32)],
        compiler_params=_params(("arbitrary",)), name="proj",
    )(x, g, w, pos, freq)


def _compress_kernel(x_ref, w1_ref, pe_ref, w2_ref, o_ref):
    x = x_ref[...]
    half = CMP_STRIDE * DH
    a = jnp.dot(x, w1_ref[0:half, :], preferred_element_type=F32)
    b = jnp.dot(x, w1_ref[half:2 * half, :], preferred_element_type=F32)
    peb = jnp.dot(pe_ref[...], w1_ref[...], preferred_element_type=F32)[0:1, :]
    hid = a + pltpu.roll(b, x.shape[0] - 1, 0) + peb
    hid = hid * _sigmoid(hid)
    o_ref[...] = jnp.dot(hid.astype(BF16), w2_ref[...], preferred_element_type=F32)


def _compress(xh, w1, pe, w2):
    n_half = xh.shape[1]
    return pl.pallas_call(
        _compress_kernel,
        grid=(2 * NSA_GROUPS,),
        in_specs=[pl.BlockSpec((None, n_half, CMP_STRIDE * DH), lambda i: (i, 0, 0)),
                  pl.BlockSpec((None, CMP_BLOCK * DH, CMP_HIDDEN), lambda i: (i // NSA_GROUPS, 0, 0)),
                  pl.BlockSpec((None, 16, CMP_BLOCK * DH), lambda i: (i // NSA_GROUPS, 0, 0)),
                  pl.BlockSpec((None, CMP_HIDDEN, LANES), lambda i: (i // NSA_GROUPS, 0, 0))],
        out_specs=pl.BlockSpec((None, n_half, LANES), lambda i: (i, 0, 0)),
        out_shape=jax.ShapeDtypeStruct((2 * NSA_GROUPS, n_half, LANES), F32),
        compiler_params=_params(("arbitrary",)), name="compress",
    )(xh, w1, pe, w2)


def _nsa_kernel(q_ref, qn_ref, kcmp_ref, vcmp_ref, ks_ref, vs_ref, kw_ref, vw_ref, gt_ref, ovl_ref,
                out_ref, qa_ref, qs_ref, bias_ref, biasn_ref, ocn_ref, m_ref, acc_ref,
                sa_ref, sb_ref, mta_ref, mtb_ref, p_ref, *, S):
    NC = S // CMP_STRIDE
    NB = S // SLC_BLOCK
    NQ = HPG * QB
    b = pl.program_id(1)
    s0 = b * QB
    lane_q = lax.broadcasted_iota(jnp.int32, (1, NQ), 1) & (QB - 1)
    band_row = lax.broadcasted_iota(jnp.int32, (QB, NQ), 0)
    band_tq = lax.broadcasted_iota(jnp.int32, (QB, NQ), 1) & (QB - 1)

    def sel_scores(qsrc_ref):
        qs_ref[0:DH, :] = jnp.concatenate([qsrc_ref[h] for h in range(HPG)], axis=1)
        qs_ref[DH:LANES, :] = jnp.zeros((LANES - DH, NQ), BF16)
        return jnp.dot(kcmp_ref[...], qs_ref[...], preferred_element_type=F32)

    def sel_probs(sc, blk):
        c_end = lax.broadcasted_iota(jnp.int32, (NC, NQ), 0) * CMP_STRIDE + (CMP_BLOCK - 1)
        sc = jnp.where(c_end <= blk * QB + lane_q, sc, NEG)
        return jnp.exp2(sc - jnp.max(sc, axis=0, keepdims=True)).astype(BF16)

    def sel_matmuls(pc):
        ocl = jnp.dot(vcmp_ref[...], pc, preferred_element_type=F32)
        ir, ic = IMP_ROWS, 4 * IMP_ROWS
        if NB % ir:
            return ocl, jnp.dot(ovl_ref[...], pc, preferred_element_type=F32)
        parts = []
        for i in range(NB // ir):
            c0 = max(i - 1, 0) * ic
            parts.append(jnp.dot(ovl_ref[i * ir:(i + 1) * ir, c0:(i + 1) * ic], pc[c0:(i + 1) * ic, :],
                                 preferred_element_type=F32))
        return ocl, jnp.concatenate(parts, axis=0)

    def sel_choose(ocl, imp4, blk):
        t_row = blk * QB + lane_q
        any_valid = jnp.where(t_row >= CMP_BLOCK - 1, 1.0, 0.0)
        rc = any_valid / ocl[DH:DH + 1, :]
        ocn_ref[...] = ocl[0:DH, :] * rc

        imp4 = imp4 * rc
        imp = imp4[:, 0:QB] + imp4[:, QB:2 * QB] + imp4[:, 2 * QB:3 * QB] + imp4[:, 3 * QB:4 * QB]

        jidx = lax.broadcasted_iota(jnp.int32, (NB, QB), 0)
        cur = (blk * QB + lax.broadcasted_iota(jnp.int32, (1, QB), 1)) >> 6
        forced = (jidx == 0) | (jidx == cur) | (jidx == cur - 1)
        future = jidx > cur
        cand = jnp.where(forced | future, -jnp.inf, imp)

        def select(break_ties):
            v = cand
            for _ in range(SLC_COUNT - 3):
                mx = jnp.max(v, axis=0, keepdims=True)
                hit = v == mx
                if break_ties:
                    hit = jidx == jnp.min(jnp.where(hit, jidx, NB), axis=0, keepdims=True)
                v = jnp.where(hit, -jnp.inf, v)
            return jnp.where((v == -jnp.inf) & (jidx <= cur), 0.0, NEG)

        bias_t = select(False)
        n_sel = jnp.sum(jnp.where(bias_t == 0.0, 1.0, 0.0), axis=0, keepdims=True)
        biasn_ref[...] = bias_t.astype(BF16)

        def redo_if_tied():
            @pl.when(jnp.max(n_sel) > SLC_COUNT)
            def _():
                biasn_ref[...] = select(True).astype(BF16)
        return redo_if_tied

    @pl.when(b == 0)
    def _():
        ocl, imp4 = sel_matmuls(sel_probs(sel_scores(q_ref), 0))
        sel_choose(ocl, imp4, 0)()

    bias_ref[...] = biasn_ref[...]
    oc = ocn_ref[...]

    qa_ref[0:DH, :] = jnp.concatenate([q_ref[h] for h in range(HPG)], axis=1)
    qa_ref[DH:KILL_LANE, :] = jnp.zeros((KILL_LANE - DH, NQ), BF16)
    kill_rows = lax.broadcasted_iota(jnp.int32, (LANES - KILL_LANE, NQ), 0) == 0
    qa_ref[KILL_LANE:LANES, :] = jnp.where(kill_rows, NEG, 0.0).astype(BF16)

    def scores(kt, s_ref, mt_ref, tile_max=True):
        k0 = pl.multiple_of(kt * TK, TK)
        bias_rows = bias_ref[pl.ds(pl.multiple_of(kt * BLK_PER_TILE, BLK_PER_TILE), BLK_PER_TILE), :]
        qa_ref[DH:DH + BLK_PER_TILE, :] = jnp.concatenate([bias_rows] * HPG, axis=1)
        s = jnp.dot(ks_ref[pl.ds(k0, TK), :], qa_ref[...], preferred_element_type=F32)
        s_ref[...] = s
        if tile_max:
            mt_ref[...] = jnp.max(s, axis=0, keepdims=True)

    def accumulate(kt, s_ref, mt_ref):
        k0 = pl.multiple_of(kt * TK, TK)
        m_old = m_ref[...]
        m_new = jnp.maximum(m_old, mt_ref[...])
        p_ref[...] = jnp.exp2(s_ref[...] - m_new).astype(BF16)
        acc_ref[...] = (jnp.exp2(m_old - m_new) * acc_ref[...]
                        + jnp.dot(vs_ref[:, pl.ds(k0, TK)], p_ref[...], preferred_element_type=F32))
        m_ref[...] = m_new

    n_full = s0 // TK

    m_ref[...] = jnp.full((1, NQ), NEG, F32)
    acc_ref[...] = jnp.zeros((VROWS, NQ), F32)

    sc = sel_scores(qn_ref)
    sw = jnp.dot(kw_ref[pl.ds(pl.multiple_of(s0, QB), WIN_KEYS), :], qa_ref[...], preferred_element_type=F32)
    scores(n_full, sa_ref, mta_ref, tile_max=False)

    pc = sel_probs(sc, b + 1)
    ocl, imp4 = sel_matmuls(pc)
    scores(0, sb_ref, mtb_ref)
    redo_if_tied = sel_choose(ocl, imp4, b + 1)

    sw = jnp.concatenate([jnp.where(band_row > band_tq, sw[0:QB], NEG),
                          sw[QB:WINDOW],
                          jnp.where(band_row <= band_tq, sw[WINDOW:WIN_KEYS], NEG)], axis=0)
    pw = jnp.exp2(sw - jnp.max(sw, axis=0, keepdims=True)).astype(BF16)
    band = pl.ds(pl.multiple_of(s0 - n_full * TK, QB), QB)
    sa_ref[band, :] = jnp.where(band_row <= band_tq, sa_ref[band, :], NEG)
    mta_ref[...] = jnp.max(sa_ref[...], axis=0, keepdims=True)

    ow = jnp.dot(vw_ref[:, pl.ds(pl.multiple_of(s0, QB), WIN_KEYS)], pw, preferred_element_type=F32)
    owin = ow[0:DH, :] * (1.0 / ow[DH:DH + 1, :])
    accumulate(n_full, sa_ref, mta_ref)
    redo_if_tied()

    def older_tile(j, last):
        def stages(s_cur, mt_cur, s_oth, mt_oth):
            if not last:
                scores(j + 1, s_oth, mt_oth)
            accumulate(j, s_cur, mt_cur)

        @pl.when(j % 2 == 0)
        def _():
            stages(sb_ref, mtb_ref, sa_ref, mta_ref)

        @pl.when(j % 2 == 1)
        def _():
            stages(sa_ref, mta_ref, sb_ref, mtb_ref)

    def body(j, carry):
        older_tile(j, False)
        return carry

    lax.fori_loop(0, n_full - 1, body, 0)

    @pl.when(n_full > 0)
    def _():
        older_tile(n_full - 1, True)

    acc = acc_ref[...]
    osel = acc[0:DH, :] * (1.0 / acc[DH:DH + 1, :])

    heads = []
    for h in range(HPG):
        sl = slice(h * QB, (h + 1) * QB)
        heads.append(gt_ref[3 * h:3 * h + 1, :] * oc[:, sl]
                     + gt_ref[3 * h + 1:3 * h + 2, :] * osel[:, sl]
                     + gt_ref[3 * h + 2:3 * h + 3, :] * owin[:, sl])
    out_ref[...] = jnp.concatenate(heads, axis=0).T.astype(out_ref.dtype)


def _nsa(q_t, kcmp, vcmp_t, ks, vs_t, kw, vw_t, gt, ovl_t):
    S = ks.shape[1]
    NC, NB = S // CMP_STRIDE, S // SLC_BLOCK
    n_qb = S // QB
    grp = lambda g, b: (g, 0, 0)
    return pl.pallas_call(
        functools.partial(_nsa_kernel, S=S),
        grid=(NSA_GROUPS, n_qb),
        in_specs=[pl.BlockSpec((HPG, DH, QB), lambda g, b: (g, 0, b)),
                  pl.BlockSpec((HPG, DH, QB), lambda g, b: (g, 0, jnp.minimum(b + 1, n_qb - 1))),
                  pl.BlockSpec((None, NC, LANES), grp),
                  pl.BlockSpec((None, VROWS, NC), grp),
                  pl.BlockSpec((None, S, LANES), grp, pipeline_mode=pl.Buffered(1)),
                  pl.BlockSpec((None, VROWS, S), grp, pipeline_mode=pl.Buffered(1)),
                  pl.BlockSpec((None, WINDOW + S, LANES), grp, pipeline_mode=pl.Buffered(1)),
                  pl.BlockSpec((None, VROWS, WINDOW + S), grp, pipeline_mode=pl.Buffered(1)),
                  pl.BlockSpec((None, 16, QB), lambda g, b: (g, 0, b)),
                  pl.BlockSpec((NB, NC), lambda g, b: (0, 0))],
        out_specs=pl.BlockSpec((QB, HPG * DH), lambda g, b: (b, g)),
        out_shape=jax.ShapeDtypeStruct((S, NSA_HEADS * DH), BF16),
        scratch_shapes=[pltpu.VMEM((LANES, HPG * QB), BF16),
                        pltpu.VMEM((LANES, HPG * QB), BF16),
                        pltpu.VMEM((NB, QB), BF16),
                        pltpu.VMEM((NB, QB), BF16),
                        pltpu.VMEM((DH, HPG * QB), F32),
                        pltpu.VMEM((1, HPG * QB), F32),
                        pltpu.VMEM((VROWS, HPG * QB), F32),
                        pltpu.VMEM((TK, HPG * QB), F32),
                        pltpu.VMEM((TK, HPG * QB), F32),
                        pltpu.VMEM((1, HPG * QB), F32),
                        pltpu.VMEM((1, HPG * QB), F32),
                        pltpu.VMEM((TK, HPG * QB), BF16)],
        compiler_params=_params(("arbitrary", "arbitrary")), name="nsa",
    )(q_t, q_t, kcmp, vcmp_t, ks, vs_t, kw, vw_t, gt, ovl_t)


def _ret_kernel(q_ref, k_ref, v_ref, g_ref, dmat_ref, xi_ref, zeta_ref, dec_ref, o_ref, state_ref, *, tm):
    @pl.when(pl.program_id(0) == 0)
    def _():
        state_ref[...] = jnp.zeros_like(state_ref)

    for j in range(tm // RET_C):
        rows = slice(j * RET_C, (j + 1) * RET_C)
        for h in range(RET_HEADS):
            cols = slice(h * RET_D, (h + 1) * RET_D)
            qc, kc, vc = q_ref[rows, cols], k_ref[rows, cols], v_ref[rows, cols]
            st = state_ref[h]
            st_hi = st.astype(BF16)
            st_lo = (st - st_hi.astype(F32)).astype(BF16)
            sc = lax.dot_general(qc, kc, NT_DIMS, preferred_element_type=F32) * dmat_ref[h]
            inner = jnp.dot(sc.astype(BF16), vc, preferred_element_type=F32)
            cross = (jnp.dot(qc, st_hi, preferred_element_type=F32)
                     + jnp.dot(qc, st_lo, preferred_element_type=F32)) * xi_ref[h]
            kz = (kc.astype(F32) * zeta_ref[h]).astype(BF16)
            state_ref[h] = dec_ref[h] * st + lax.dot_general(kz, vc, TN_DIMS, preferred_element_type=F32)
            o = inner + cross
            o = o * lax.rsqrt(jnp.mean(o * o, axis=-1, keepdims=True) + EPS)
            gate = g_ref[rows, cols].astype(F32)
            o_ref[rows, cols] = (gate * _sigmoid(gate) * o).astype(o_ref.dtype)


def _retention(rq, rk, rv, rg, dmat, xi, zeta, dec, tm=512):
    S = rq.shape[0]
    W = RET_HEADS * RET_D
    row = lambda i: (i, 0)
    cst = lambda i: (0, 0, 0)
    blk = pl.BlockSpec((tm, W), row)
    tbl = pl.BlockSpec((RET_HEADS, RET_C, RET_D), cst)
    return pl.pallas_call(
        functools.partial(_ret_kernel, tm=tm),
        grid=(S // tm,),
        in_specs=[blk, blk, blk, blk, tbl, tbl, tbl, tbl],
        out_specs=blk,
        out_shape=jax.ShapeDtypeStruct((S, W), BF16),
        scratch_shapes=[pltpu.VMEM((RET_HEADS, RET_D, RET_D), F32)],
        compiler_params=_params(("arbitrary",)), name="retention",
    )(rq, rk, rv, rg, dmat, xi, zeta, dec)


def _mix_kernel(x_ref, g_ref, a_ref, r_ref, wg_ref, wpa_ref, wpb_ref, wo_ref, o_ref):
    x = x_ref[...]
    h = _rms(x, g_ref[...]).astype(BF16)
    ga = _sigmoid(jnp.dot(h, wg_ref[:, 0:D_MODEL], preferred_element_type=F32))
    gb = _sigmoid(jnp.dot(h, wg_ref[:, D_MODEL:2 * D_MODEL], preferred_element_type=F32))
    mix = (ga * jnp.dot(a_ref[...], wpa_ref[...], preferred_element_type=F32)
           + gb * jnp.dot(r_ref[...], wpb_ref[...], preferred_element_type=F32))
    o_ref[...] = x + jnp.dot(mix.astype(BF16), wo_ref[...], preferred_element_type=F32)


def _mix(x, g, nsa_out, ret_out, wg, wpa, wpb, wo, tm=512):
    S = x.shape[0]
    row = lambda i: (i, 0)
    cst = lambda i: (0, 0)
    return pl.pallas_call(
        _mix_kernel,
        grid=(S // tm,),
        in_specs=[pl.BlockSpec((tm, D_MODEL), row), pl.BlockSpec((1, D_MODEL), cst),
                  pl.BlockSpec((tm, 512), row), pl.BlockSpec((tm, 512), row),
                  pl.BlockSpec((D_MODEL, 2 * D_MODEL), cst), pl.BlockSpec((512, D_MODEL), cst),
                  pl.BlockSpec((512, D_MODEL), cst), pl.BlockSpec((D_MODEL, D_MODEL), cst)],
        out_specs=pl.BlockSpec((tm, D_MODEL), row),
        out_shape=jax.ShapeDtypeStruct((S, D_MODEL), F32),
        compiler_params=_params(("arbitrary",)), name="mix",
    )(x, g, nsa_out, ret_out, wg, wpa, wpb, wo)


def _mlp_kernel(x_ref, g_ref, wu_ref, wd_ref, gf_ref, o_ref, *, hc):
    x = x_ref[...]
    h = _rms(x, g_ref[...]).astype(BF16)
    acc = x
    for c in range(MLP_HIDDEN // hc):
        u = jnp.maximum(jnp.dot(h, wu_ref[:, c * hc:(c + 1) * hc], preferred_element_type=F32), 0.0)
        acc = acc + jnp.dot((u * u).astype(BF16), wd_ref[c * hc:(c + 1) * hc, :], preferred_element_type=F32)
    o_ref[...] = _rms(acc, gf_ref[...])


def _mlp(x, g, wu, wd, gf, tm=512, hc=1024):
    S = x.shape[0]
    row = lambda i: (i, 0)
    cst = lambda i: (0, 0)
    return pl.pallas_call(
        functools.partial(_mlp_kernel, hc=hc),
        grid=(S // tm,),
        in_specs=[pl.BlockSpec((tm, D_MODEL), row), pl.BlockSpec((1, D_MODEL), cst),
                  pl.BlockSpec((D_MODEL, MLP_HIDDEN), cst), pl.BlockSpec((MLP_HIDDEN, D_MODEL), cst),
                  pl.BlockSpec((1, D_MODEL), cst)],
        out_specs=pl.BlockSpec((tm, D_MODEL), row),
        out_shape=jax.ShapeDtypeStruct((S, D_MODEL), F32),
        compiler_params=_params(("arbitrary",)), name="mlp",
    )(x, g, wu, wd, gf)


def _rope_freqs():
    half = ROPE_DIM // 2
    inv = jnp.exp(-math.log(ROPE_THETA) * jnp.arange(half, dtype=F32) * 2.0 / ROPE_DIM)
    hr = RET_D // 2
    inv_r = jnp.exp(-math.log(RET_BASE) * jnp.arange(hr, dtype=F32) * 2.0 / RET_D)
    return jnp.concatenate([inv_r, inv, jnp.zeros((LANES - hr - half,), F32)])[None, :]


def _decay_tables():
    H, C = RET_HEADS, RET_C
    log_g = jnp.log(1.0 - jnp.exp2(-5.0 - jnp.arange(H, dtype=F32)))
    n = jnp.arange(C, dtype=F32)
    rel = n[:, None] - n[None, :]
    dmat = jnp.where(rel >= 0, jnp.exp(log_g[:, None, None] * jnp.maximum(rel, 0.0)), 0.0)
    xi = jnp.exp(log_g[:, None] * (n + 1.0))
    zeta = jnp.exp(log_g[:, None] * (C - 1.0 - n))
    dec = jnp.exp(log_g * C)
    bc = lambda a: jnp.broadcast_to(a, (H, C, RET_D))
    return dmat, bc(xi[:, :, None]), bc(zeta[:, :, None]), bc(dec[:, None, None])


def _overlap_t(S):
    n_cmp = (S - CMP_BLOCK) // CMP_STRIDE + 1
    n_slc = S // SLC_BLOCK
    cs = np.arange(n_cmp)[:, None] * CMP_STRIDE
    ss = np.arange(n_slc)[None, :] * SLC_BLOCK
    ov = np.clip(np.minimum(cs + CMP_BLOCK, ss + SLC_BLOCK) - np.maximum(cs, ss), 0, None) / CMP_BLOCK
    out = np.zeros((n_slc, S // CMP_STRIDE), np.float32)
    out[:, :n_cmp] = ov.T
    return jnp.asarray(out, BF16)


def _v_rows(v_t):
    S = v_t.shape[1]
    return jnp.concatenate([v_t, jnp.ones((1, S), v_t.dtype), jnp.zeros((VROWS - DH - 1, S), v_t.dtype)], axis=0)


_IN_OFFSETS = np.cumsum((0, 512, 128, 128, 128, 128, 128, 128, 24, 512, 512, 512, 512, 1024, 1024))


def _in_col(wi, i):
    return wi[:, _IN_OFFSETS[i]:_IN_OFFSETS[i + 1]]


def _branches(x, positions, norm_mix, w_in, cmp_pos_k, cmp_pos_v, cmp_k_w1, cmp_k_w2, cmp_v_w1, cmp_v_w2):
    B, S, _ = x.shape
    assert B == 1 and norm_mix.shape[0] == 1 and S % TK == 0 and S >= max(WIN_KEYS, SLC_COUNT * SLC_BLOCK)
    xs = x[0]
    col = functools.partial(_in_col, w_in[0])
    n_gate = 3 * HPG
    z = lambda n: jnp.zeros((D_MODEL, n), F32)
    w_a = jnp.concatenate([col(0), col(1), col(3), col(5), col(2), col(4), col(6),
                           col(7)[:, 0:n_gate], z(16 - n_gate), col(7)[:, n_gate:], z(LANES - 16 - n_gate),
                           col(8), col(9), col(10), col(11)], axis=1).astype(BF16)

    q_t, xh, ks, kw, vs_t, vw_t, gt, rq, rk, rv, rg = _proj(
        xs, norm_mix, w_a, jnp.broadcast_to(positions[0].astype(F32)[:, None], (S, LANES)), _rope_freqs())

    w1 = jnp.stack([cmp_k_w1[0], cmp_v_w1[0]]).astype(BF16)
    pe = jnp.stack([cmp_pos_k[0], cmp_pos_v[0]]).reshape(2, 1, CMP_BLOCK * DH)
    pe = jnp.broadcast_to(pe, (2, 16, CMP_BLOCK * DH)).astype(BF16)
    w2 = jnp.pad(jnp.stack([cmp_k_w2[0], cmp_v_w2[0]]), ((0, 0), (0, 0), (0, LANES - DH))).astype(BF16)
    cmp = _compress(xh, w1, pe, w2)
    kcmp = cmp[0:NSA_GROUPS].astype(BF16)
    vcmp_t = jax.vmap(_v_rows)(cmp[NSA_GROUPS:, :, 0:DH].transpose(0, 2, 1).astype(BF16))

    vw_t = jnp.pad(vw_t, ((0, 0), (0, 0), (WINDOW, 0)))
    pad_keys = jnp.zeros((NSA_GROUPS, WINDOW, LANES), BF16).at[:, :, KILL_LANE].set(1.0)
    kw = jnp.concatenate([pad_keys, kw], axis=1)
    nsa_out = _nsa(q_t, kcmp, vcmp_t, ks, vs_t, kw, vw_t, gt, _overlap_t(S))

    ret_out = _retention(rq, rk, rv, rg, *_decay_tables())
    return nsa_out, ret_out


def kernel(x, positions, norm_mix, w_in, cmp_pos_k, cmp_pos_v, cmp_k_w1, cmp_k_w2, cmp_v_w1, cmp_v_w2,
           w_proj_a, w_proj_b, w_out, norm_mlp, w_up, w_down, norm_final):
    nsa_out, ret_out = _branches(x, positions, norm_mix, w_in, cmp_pos_k, cmp_pos_v,
                                 cmp_k_w1, cmp_k_w2, cmp_v_w1, cmp_v_w2)
    xs = x[0]
    w_gate = jnp.concatenate([_in_col(w_in[0], 12), _in_col(w_in[0], 13)], axis=1).astype(BF16)
    x1 = _mix(xs, norm_mix, nsa_out, ret_out, w_gate, w_proj_a[0].astype(BF16), w_proj_b[0].astype(BF16),
              w_out[0].astype(BF16))
    y = _mlp(x1, norm_mlp, w_up[0].astype(BF16), w_down[0].astype(BF16), norm_final[None, :])
    return y[None]
```

```python
import functools
import math

import numpy as np
import jax
import jax.numpy as jnp
from jax import lax
from jax.experimental import pallas as pl
from jax.experimental.pallas import tpu as pltpu

F32 = jnp.float32
BF16 = jnp.bfloat16

D_MODEL = 1024
NSA_HEADS = 8
NSA_GROUPS = 2
HPG = NSA_HEADS // NSA_GROUPS
DH = 64
CMP_BLOCK = 32
CMP_STRIDE = 16
CMP_HIDDEN = 256
SLC_BLOCK = 64
SLC_COUNT = 16
WINDOW = 512
QB = 256
ROPE_THETA = 500000.0
ROPE_DIM = DH // 4
RET_HEADS = 4
RET_D = 128
RET_C = 128
RET_BASE = 10000.0
MLP_HIDDEN = 4 * D_MODEL
EPS = 1e-6
NEG = -1e30
BIG = 1e30

LANES = 128
TK = 1024
BLK_PER_TILE = TK // SLC_BLOCK
IMP_ROWS = 64
KILL_LANE = DH + BLK_PER_TILE
REF_LANE = KILL_LANE + 1
REF_SLACK = 64.0
LOG2E = math.log2(math.e)
VROWS = 80
WIN_KEYS = WINDOW + QB
VMEM_LIMIT = 56 * 1024 * 1024

NT_DIMS = (((1,), (1,)), ((), ()))
TN_DIMS = (((0,), (0,)), ((), ()))


def _sigmoid(v):
    return 1.0 / (1.0 + jnp.exp(-v))


def _rms(x, g):
    return x * lax.rsqrt(jnp.mean(x * x, axis=-1, keepdims=True) + EPS) * g


def _params(sem, flags=None):
    return pltpu.CompilerParams(dimension_semantics=sem, vmem_limit_bytes=VMEM_LIMIT, flags=flags)


_C_ROT = (0, 896)
_C_V = (896, 1280)
_C_G = (1280, 1408)
_C_RQ = (1408, 1920)
_C_RK = (1920, 2432)
_C_RV = (2432, 2944)
_C_RG = (2944, 3456)
PROJ_W = 3456


def _proj_kernel(x_ref, g_ref, w_ref, pos_ref, freq_ref,
                 q_ref, xh_ref, ks_ref, kw_ref, vs_ref, vw_ref, gt_ref, rq_ref, rk_ref, rv_ref, rg_ref,
                 kv_ref, *, tm):
    h = _rms(x_ref[...], g_ref[...]).astype(BF16)
    lane = lax.broadcasted_iota(jnp.int32, (tm, LANES), 1)
    lo = lane < DH

    ang = pos_ref[...] * freq_ref[...]
    c, s = jnp.cos(ang), jnp.sin(ang)
    half_n = ROPE_DIM // 2
    cr = jnp.where(lo, c, pltpu.roll(c, DH, 1))
    sr = jnp.where(lo, -s, pltpu.roll(s, DH, 1))
    dim = lane & (DH - 1)
    first, second = dim < half_n, (dim >= half_n) & (dim < ROPE_DIM)
    c1 = jnp.where(lo, pltpu.roll(c, DH, 1), c)
    s1 = jnp.where(lo, pltpu.roll(s, DH, 1), s)
    c2 = jnp.where(lo, pltpu.roll(c, DH + half_n, 1), pltpu.roll(c, half_n, 1))
    s2 = jnp.where(lo, pltpu.roll(s, DH + half_n, 1), pltpu.roll(s, half_n, 1))
    cn = jnp.where(first, c1, jnp.where(second, c2, 1.0))
    sa = jnp.where(first, -s1, 0.0)
    sb = jnp.where(second, s2, 0.0)

    def dot(c):
        return jnp.dot(h, w_ref[:, c[0]:c[1]], preferred_element_type=F32)

    def tile(y, i):
        return y[:, i * LANES:(i + 1) * LANES]

    def rope_n(t):
        return t * cn + pltpu.roll(t, LANES - half_n, 1) * sa + pltpu.roll(t, half_n, 1) * sb

    def rope_r(t):
        return t * cr + pltpu.roll(t, RET_D // 2, 1) * sr

    def split(t):
        return jnp.where(lo, t, 0.0), jnp.where(lo, pltpu.roll(t, DH, 1), 0.0)

    def v_rows(dst_ref, t):
        tt = t.T
        tail = jnp.where(lax.broadcasted_iota(jnp.int32, (VROWS - DH, tm), 0) == 0, 1.0, 0.0).astype(BF16)
        for g in range(NSA_GROUPS):
            dst_ref[g, 0:DH, :] = tt[g * DH:(g + 1) * DH, :].astype(BF16)
            dst_ref[g, DH:VROWS, :] = tail

    yr = dot(_C_ROT)
    for i in range(4):
        tt = (rope_n(tile(yr, i)) * (DH ** -0.5 * LOG2E)).T
        q_ref[2 * i] = tt[0:DH, :].astype(BF16)
        q_ref[2 * i + 1] = tt[DH:2 * DH, :].astype(BF16)
    row = pl.program_id(0) * tm + lax.broadcasted_iota(jnp.int32, (tm, LANES), 0)
    blk = (row >> 6) & (BLK_PER_TILE - 1)
    onehot = jnp.where(((lane - DH) == blk) | (lane == REF_LANE), 1.0, 0.0)
    a, b = split(rope_n(tile(yr, 5)))
    ks_ref[0] = jnp.where(lo, a, onehot).astype(BF16)
    ks_ref[1] = jnp.where(lo, b, onehot).astype(BF16)
    a, b = split(rope_n(tile(yr, 6)))
    kw_ref[0] = a.astype(BF16)
    kw_ref[1] = b.astype(BF16)

    yv = dot(_C_V)
    v_rows(vs_ref, tile(yv, 1))
    v_rows(vw_ref, tile(yv, 2))
    gt = _sigmoid(dot(_C_G)).T
    gt_ref[0] = gt[0:16, :]
    gt_ref[1] = gt[16:32, :]

    kv_ref[0] = rope_n(tile(yr, 4))
    kv_ref[1] = tile(yv, 0)
    lo_r = lax.broadcasted_iota(jnp.int32, (tm // CMP_STRIDE, LANES), 1) < DH
    for typ in range(2):
        for u in range(CMP_STRIDE // 2):
            pa = kv_ref[typ, pl.ds(2 * u, tm // CMP_STRIDE, stride=CMP_STRIDE), :]
            pb = kv_ref[typ, pl.ds(2 * u + 1, tm // CMP_STRIDE, stride=CMP_STRIDE), :]
            cols = slice(u * LANES, (u + 1) * LANES)
            xh_ref[2 * typ, :, cols] = jnp.where(lo_r, pa, pltpu.roll(pb, DH, 1)).astype(BF16)
            xh_ref[2 * typ + 1, :, cols] = jnp.where(lo_r, pltpu.roll(pa, DH, 1), pb).astype(BF16)

    yq = dot(_C_RQ)
    yk = dot(_C_RK)
    for i in range(RET_HEADS):
        sl = slice(i * LANES, (i + 1) * LANES)
        rq_ref[:, sl] = rope_r(tile(yq, i)).astype(BF16)
        rk_ref[:, sl] = (rope_r(tile(yk, i)) * (RET_D ** -0.5)).astype(BF16)
    rv_ref[...] = dot(_C_RV).astype(BF16)
    rg_ref[...] = dot(_C_RG).astype(BF16)


def _proj(x, g, w, pos, freq, tm=512):
    S = x.shape[0]
    row = lambda i: (i, 0)
    const = lambda i: (0, 0)
    mid = lambda i: (0, i, 0)
    last = lambda i: (0, 0, i)
    W = RET_HEADS * RET_D
    out_shape = (
        jax.ShapeDtypeStruct((NSA_HEADS, DH, S), BF16),
        jax.ShapeDtypeStruct((2 * NSA_GROUPS, S // CMP_STRIDE, CMP_STRIDE * DH), BF16),
        jax.ShapeDtypeStruct((NSA_GROUPS, S, LANES), BF16),
        jax.ShapeDtypeStruct((NSA_GROUPS, S, LANES), BF16),
        jax.ShapeDtypeStruct((NSA_GROUPS, VROWS, S), BF16),
        jax.ShapeDtypeStruct((NSA_GROUPS, VROWS, S), BF16),
        jax.ShapeDtypeStruct((NSA_GROUPS, 16, S), F32),
        jax.ShapeDtypeStruct((S, W), BF16),
        jax.ShapeDtypeStruct((S, W), BF16),
        jax.ShapeDtypeStruct((S, W), BF16),
        jax.ShapeDtypeStruct((S, W), BF16),
    )
    out_specs = (
        pl.BlockSpec((NSA_HEADS, DH, tm), last),
        pl.BlockSpec((2 * NSA_GROUPS, tm // CMP_STRIDE, CMP_STRIDE * DH), mid),
        pl.BlockSpec((NSA_GROUPS, tm, LANES), mid),
        pl.BlockSpec((NSA_GROUPS, tm, LANES), mid),
        pl.BlockSpec((NSA_GROUPS, VROWS, tm), last),
        pl.BlockSpec((NSA_GROUPS, VROWS, tm), last),
        pl.BlockSpec((NSA_GROUPS, 16, tm), last),
        pl.BlockSpec((tm, W), row),
        pl.BlockSpec((tm, W), row),
        pl.BlockSpec((tm, W), row),
        pl.BlockSpec((tm, W), row),
    )
    return pl.pallas_call(
        functools.partial(_proj_kernel, tm=tm),
        grid=(S // tm,),
        in_specs=[pl.BlockSpec((tm, D_MODEL), row), pl.BlockSpec((1, D_MODEL), const),
                  pl.BlockSpec((D_MODEL, PROJ_W), const), pl.BlockSpec((tm, LANES), row),
                  pl.BlockSpec((1, LANES), const)],
        out_specs=out_specs, out_shape=out_shape,
        scratch_shapes=[pltpu.VMEM((2, tm, LANES), F32)],
        compiler_params=_params(("arbitrary",)), name="proj",
    )(x, g, w, pos, freq)


def _compress_kernel(x_ref, w1_ref, pe_ref, w2_ref, o_ref):
    x = x_ref[...]
    half = CMP_STRIDE * DH
    a = jnp.dot(x, w1_ref[0:half, :], preferred_element_type=F32)
    b = jnp.dot(x, w1_ref[half:2 * half, :], preferred_element_type=F32)
    peb = jnp.dot(pe_ref[...], w1_ref[...], preferred_element_type=F32)[0:1, :]
    hid = a + pltpu.roll(b, x.shape[0] - 1, 0) + peb
    hid = hid * _sigmoid(hid)
    o_ref[...] = jnp.dot(hid.astype(BF16), w2_ref[...], preferred_element_type=F32)


def _compress(xh, w1, pe, w2):
    n_half = xh.shape[1]
    return pl.pallas_call(
        _compress_kernel,
        grid=(2 * NSA_GROUPS,),
        in_specs=[pl.BlockSpec((None, n_half, CMP_STRIDE * DH), lambda i: (i, 0, 0)),
                  pl.BlockSpec((None, CMP_BLOCK * DH, CMP_HIDDEN), lambda i: (i // NSA_GROUPS, 0, 0)),
                  pl.BlockSpec((None, 16, CMP_BLOCK * DH), lambda i: (i // NSA_GROUPS, 0, 0)),
                  pl.BlockSpec((None, CMP_HIDDEN, LANES), lambda i: (i // NSA_GROUPS, 0, 0))],
        out_specs=pl.BlockSpec((None, n_half, LANES), lambda i: (i, 0, 0)),
        out_shape=jax.ShapeDtypeStruct((2 * NSA_GROUPS, n_half, LANES), F32),
        compiler_params=_params(("arbitrary",)), name="compress",
    )(xh, w1, pe, w2)


def _nsa_kernel(q_ref, qn_ref, kcmp_ref, vcmp_ref, ks_ref, vs_ref, kw_ref, vw_ref, gt_ref, ovl_ref,
                out_ref, qa_ref, qs_ref, bias_ref, biasn_ref, ocn_ref, m_ref, acc_ref,
                sa_ref, p_ref, c_ref, rise_ref, *, S):
    NC = S // CMP_STRIDE
    NB = S // SLC_BLOCK
    NQ = HPG * QB
    b = pl.program_id(1)
    s0 = b * QB
    lane_q = lax.broadcasted_iota(jnp.int32, (1, NQ), 1) & (QB - 1)
    band_row = lax.broadcasted_iota(jnp.int32, (QB, NQ), 0)
    band_tq = lax.broadcasted_iota(jnp.int32, (QB, NQ), 1) & (QB - 1)

    def sel_scores(qsrc_ref):
        qs_ref[0:DH, :] = jnp.concatenate([qsrc_ref[h] for h in range(HPG)], axis=1)
        qs_ref[DH:LANES, :] = jnp.zeros((LANES - DH, NQ), BF16)
        return jnp.dot(kcmp_ref[...], qs_ref[...], preferred_element_type=F32)

    def sel_probs(sc, blk):
        c_end = lax.broadcasted_iota(jnp.int32, (NC, NQ), 0) * CMP_STRIDE + (CMP_BLOCK - 1)
        sc = jnp.where(c_end <= blk * QB + lane_q, sc, NEG)
        return jnp.exp2(sc - jnp.max(sc, axis=0, keepdims=True)).astype(BF16)

    def sel_matmuls(pc):
        ocl = jnp.dot(vcmp_ref[...], pc, preferred_element_type=F32)
        ir, ic = IMP_ROWS, 4 * IMP_ROWS
        if NB % ir:
            return ocl, jnp.dot(ovl_ref[...], pc, preferred_element_type=F32)
        parts = []
        for i in range(NB // ir):
            c0 = max(i - 1, 0) * ic
            parts.append(jnp.dot(ovl_ref[i * ir:(i + 1) * ir, c0:(i + 1) * ic], pc[c0:(i + 1) * ic, :],
                                 preferred_element_type=F32))
        return ocl, jnp.concatenate(parts, axis=0)

    def sel_choose(ocl, imp4, blk):
        t_row = blk * QB + lane_q
        any_valid = jnp.where(t_row >= CMP_BLOCK - 1, 1.0, 0.0)
        rc = any_valid / ocl[DH:DH + 1, :]
        ocn_ref[...] = ocl[0:DH, :] * rc

        imp4 = imp4 * rc
        imp = imp4[:, 0:QB] + imp4[:, QB:2 * QB] + imp4[:, 2 * QB:3 * QB] + imp4[:, 3 * QB:4 * QB]

        jidx = lax.broadcasted_iota(jnp.int32, (NB, QB), 0)
        cur = (blk * QB + lax.broadcasted_iota(jnp.int32, (1, QB), 1)) >> 6
        forced = (jidx == 0) | (jidx == cur) | (jidx == cur - 1)
        future = jidx > cur
        cand = jnp.where(forced | future, -jnp.inf, imp)

        def select(break_ties):
            v = cand
            for _ in range(SLC_COUNT - 3):
                mx = jnp.max(v, axis=0, keepdims=True)
                hit = v == mx
                if break_ties:
                    hit = jidx == jnp.min(jnp.where(hit, jidx, NB), axis=0, keepdims=True)
                v = jnp.where(hit, -jnp.inf, v)
            return jnp.where((v == -jnp.inf) & (jidx <= cur), 0.0, NEG)

        bias_t = select(False)
        n_sel = jnp.sum(jnp.where(bias_t == 0.0, 1.0, 0.0), axis=0, keepdims=True)
        biasn_ref[...] = bias_t.astype(BF16)

        def redo_if_tied():
            @pl.when(jnp.max(n_sel) > SLC_COUNT)
            def _():
                biasn_ref[...] = select(True).astype(BF16)
        return redo_if_tied

    @pl.when(b == 0)
    def _():
        ocl, imp4 = sel_matmuls(sel_probs(sel_scores(q_ref), 0))
        sel_choose(ocl, imp4, 0)()

    bias_ref[...] = biasn_ref[...]
    oc = ocn_ref[...]

    qa_ref[0:DH, :] = jnp.concatenate([q_ref[h] for h in range(HPG)], axis=1)
    qa_ref[DH:KILL_LANE, :] = jnp.zeros((KILL_LANE - DH, NQ), BF16)

    def set_bias_rows(kt):
        bias_rows = bias_ref[pl.ds(pl.multiple_of(kt * BLK_PER_TILE, BLK_PER_TILE), BLK_PER_TILE), :]
        qa_ref[DH:DH + BLK_PER_TILE, :] = jnp.concatenate([bias_rows] * HPG, axis=1)

    def set_flag_rows(ref):
        r = lax.broadcasted_iota(jnp.int32, (LANES - KILL_LANE, NQ), 0)
        rows = jnp.where(r == 0, NEG, 0.0) if ref is None else jnp.where(r == 0, NEG, jnp.where(r == 1, -ref, 0.0))
        qa_ref[KILL_LANE:LANES, :] = rows.astype(BF16)

    def scores(kt, s_ref):
        k0 = pl.multiple_of(kt * TK, TK)
        set_bias_rows(kt)
        s_ref[...] = jnp.dot(ks_ref[pl.ds(k0, TK), :], qa_ref[...], preferred_element_type=F32)

    def accumulate(kt, s_ref):
        k0 = pl.multiple_of(kt * TK, TK)
        m_old = m_ref[...]
        m_new = jnp.maximum(m_old, jnp.max(s_ref[...], axis=0, keepdims=True))
        p_ref[...] = jnp.exp2(s_ref[...] - m_new).astype(BF16)
        acc_ref[...] = (jnp.exp2(m_old - m_new) * acc_ref[...]
                        + jnp.dot(vs_ref[:, pl.ds(k0, TK)], p_ref[...], preferred_element_type=F32))
        m_ref[...] = m_new

    def mask_own_keys(s_ref):
        band = pl.ds(pl.multiple_of(s0 - n_full * TK, QB), QB)
        s_ref[band, :] = jnp.where(band_row <= band_tq, s_ref[band, :], NEG)

    n_full = s0 // TK

    m_ref[...] = jnp.full((1, NQ), NEG, F32)
    acc_ref[...] = jnp.zeros((VROWS, NQ), F32)
    set_flag_rows(None)

    sc = sel_scores(qn_ref)
    sw = jnp.dot(kw_ref[pl.ds(pl.multiple_of(s0, QB), WIN_KEYS), :], qa_ref[...], preferred_element_type=F32)
    scores(n_full, sa_ref)

    pc = sel_probs(sc, b + 1)
    ocl, imp4 = sel_matmuls(pc)
    redo_if_tied = sel_choose(ocl, imp4, b + 1)

    sw = jnp.concatenate([jnp.where(band_row > band_tq, sw[0:QB], NEG),
                          sw[QB:WINDOW],
                          jnp.where(band_row <= band_tq, sw[WINDOW:WIN_KEYS], NEG)], axis=0)
    pw = jnp.exp2(sw - jnp.max(sw, axis=0, keepdims=True)).astype(BF16)
    mask_own_keys(sa_ref)

    ow = jnp.dot(vw_ref[:, pl.ds(pl.multiple_of(s0, QB), WIN_KEYS)], pw, preferred_element_type=F32)
    owin = ow[0:DH, :] * (1.0 / ow[DH:DH + 1, :])
    accumulate(n_full, sa_ref)

    m = m_ref[...]
    c0 = m.astype(BF16).astype(F32)
    c_ref[...] = c0
    acc_ref[...] = acc_ref[...] * jnp.exp2(m - c0)
    rise_ref[...] = jnp.zeros((1, NQ), F32)
    redo_if_tied()

    def older_tile(j, carry):
        k0 = pl.multiple_of(j * TK, TK)
        c = c_ref[...]
        set_bias_rows(j)
        set_flag_rows(c)
        s = jnp.dot(ks_ref[pl.ds(k0, TK), :], qa_ref[...], preferred_element_type=F32)
        mt = jnp.max(s, axis=0, keepdims=True)
        p_ref[...] = jnp.exp2(s).astype(BF16)
        pv = jnp.dot(vs_ref[:, pl.ds(k0, TK)], p_ref[...], preferred_element_type=F32)
        c_new = (c + jnp.maximum(mt, 0.0)).astype(BF16).astype(F32)
        acc_ref[...] = (acc_ref[...] + pv) * jnp.exp2(c - c_new)
        c_ref[...] = c_new
        rise_ref[...] = jnp.maximum(rise_ref[...], mt)
        return carry

    lax.fori_loop(0, n_full, older_tile, 0)

    @pl.when(jnp.max(rise_ref[...]) > REF_SLACK)
    def _():
        m_ref[...] = jnp.full((1, NQ), NEG, F32)
        acc_ref[...] = jnp.zeros((VROWS, NQ), F32)
        set_flag_rows(None)

        def two_pass_tile(kt, carry):
            scores(kt, sa_ref)

            @pl.when(kt == n_full)
            def _():
                mask_own_keys(sa_ref)
            accumulate(kt, sa_ref)
            return carry

        lax.fori_loop(0, n_full + 1, two_pass_tile, 0)

    acc = acc_ref[...]
    osel = acc[0:DH, :] * (1.0 / acc[DH:DH + 1, :])

    heads = []
    for h in range(HPG):
        sl = slice(h * QB, (h + 1) * QB)
        heads.append(gt_ref[3 * h:3 * h + 1, :] * oc[:, sl]
                     + gt_ref[3 * h + 1:3 * h + 2, :] * osel[:, sl]
                     + gt_ref[3 * h + 2:3 * h + 3, :] * owin[:, sl])
    out_ref[...] = jnp.concatenate(heads, axis=0).T.astype(out_ref.dtype)


def _nsa(q_t, kcmp, vcmp_t, ks, vs_t, kw, vw_t, gt, ovl_t):
    S = ks.shape[1]
    NC, NB = S // CMP_STRIDE, S // SLC_BLOCK
    n_qb = S // QB
    grp = lambda g, b: (g, 0, 0)
    return pl.pallas_call(
        functools.partial(_nsa_kernel, S=S),
        grid=(NSA_GROUPS, n_qb),
        in_specs=[pl.BlockSpec((HPG, DH, QB), lambda g, b: (g, 0, b)),
                  pl.BlockSpec((HPG, DH, QB), lambda g, b: (g, 0, jnp.minimum(b + 1, n_qb - 1))),
                  pl.BlockSpec((None, NC, LANES), grp),
                  pl.BlockSpec((None, VROWS, NC), grp),
                  pl.BlockSpec((None, S, LANES), grp, pipeline_mode=pl.Buffered(1)),
                  pl.BlockSpec((None, VROWS, S), grp, pipeline_mode=pl.Buffered(1)),
                  pl.BlockSpec((None, WINDOW + S, LANES), grp, pipeline_mode=pl.Buffered(1)),
                  pl.BlockSpec((None, VROWS, WINDOW + S), grp, pipeline_mode=pl.Buffered(1)),
                  pl.BlockSpec((None, 16, QB), lambda g, b: (g, 0, b)),
                  pl.BlockSpec((NB, NC), lambda g, b: (0, 0))],
        out_specs=pl.BlockSpec((QB, HPG * DH), lambda g, b: (b, g)),
        out_shape=jax.ShapeDtypeStruct((S, NSA_HEADS * DH), BF16),
        scratch_shapes=[pltpu.VMEM((LANES, HPG * QB), BF16),
                        pltpu.VMEM((LANES, HPG * QB), BF16),
                        pltpu.VMEM((NB, QB), BF16),
                        pltpu.VMEM((NB, QB), BF16),
                        pltpu.VMEM((DH, HPG * QB), F32),
                        pltpu.VMEM((1, HPG * QB), F32),
                        pltpu.VMEM((VROWS, HPG * QB), F32),
                        pltpu.VMEM((TK, HPG * QB), F32),
                        pltpu.VMEM((TK, HPG * QB), BF16),
                        pltpu.VMEM((1, HPG * QB), F32),
                        pltpu.VMEM((1, HPG * QB), F32)],
        compiler_params=_params(("arbitrary", "arbitrary")), name="nsa",
    )(q_t, q_t, kcmp, vcmp_t, ks, vs_t, kw, vw_t, gt, ovl_t)


def _ret_kernel(q_ref, k_ref, v_ref, g_ref, dmat_ref, xi_ref, zeta_ref, dec_ref, o_ref, state_ref, *, tm):
    @pl.when(pl.program_id(0) == 0)
    def _():
        state_ref[...] = jnp.zeros_like(state_ref)

    for j in range(tm // RET_C):
        rows = slice(j * RET_C, (j + 1) * RET_C)
        for h in range(RET_HEADS):
            cols = slice(h * RET_D, (h + 1) * RET_D)
            qc, kc, vc = q_ref[rows, cols], k_ref[rows, cols], v_ref[rows, cols]
            st = state_ref[h]
            st_hi = st.astype(BF16)
            st_lo = (st - st_hi.astype(F32)).astype(BF16)
            sc = lax.dot_general(qc, kc, NT_DIMS, preferred_element_type=F32) * dmat_ref[h]
            inner = jnp.dot(sc.astype(BF16), vc, preferred_element_type=F32)
            cross = (jnp.dot(qc, st_hi, preferred_element_type=F32)
                     + jnp.dot(qc, st_lo, preferred_element_type=F32)) * xi_ref[h]
            kz = (kc.astype(F32) * zeta_ref[h]).astype(BF16)
            state_ref[h] = dec_ref[h] * st + lax.dot_general(kz, vc, TN_DIMS, preferred_element_type=F32)
            o = inner + cross
            o = o * lax.rsqrt(jnp.mean(o * o, axis=-1, keepdims=True) + EPS)
            gate = g_ref[rows, cols].astype(F32)
            o_ref[rows, cols] = (gate * _sigmoid(gate) * o).astype(o_ref.dtype)


def _retention(rq, rk, rv, rg, dmat, xi, zeta, dec, tm=512):
    S = rq.shape[0]
    W = RET_HEADS * RET_D
    row = lambda i: (i, 0)
    cst = lambda i: (0, 0, 0)
    blk = pl.BlockSpec((tm, W), row)
    tbl = pl.BlockSpec((RET_HEADS, RET_C, RET_D), cst)
    return pl.pallas_call(
        functools.partial(_ret_kernel, tm=tm),
        grid=(S // tm,),
        in_specs=[blk, blk, blk, blk, tbl, tbl, tbl, tbl],
        out_specs=blk,
        out_shape=jax.ShapeDtypeStruct((S, W), BF16),
        scratch_shapes=[pltpu.VMEM((RET_HEADS, RET_D, RET_D), F32)],
        compiler_params=_params(("arbitrary",)), name="retention",
    )(rq, rk, rv, rg, dmat, xi, zeta, dec)


def _mix_kernel(x_ref, g_ref, a_ref, r_ref, wg_ref, wpa_ref, wpb_ref, wo_ref, o_ref):
    x = x_ref[...]
    h = _rms(x, g_ref[...]).astype(BF16)
    ga = _sigmoid(jnp.dot(h, wg_ref[:, 0:D_MODEL], preferred_element_type=F32))
    gb = _sigmoid(jnp.dot(h, wg_ref[:, D_MODEL:2 * D_MODEL], preferred_element_type=F32))
    mix = (ga * jnp.dot(a_ref[...], wpa_ref[...], preferred_element_type=F32)
           + gb * jnp.dot(r_ref[...], wpb_ref[...], preferred_element_type=F32))
    o_ref[...] = x + jnp.dot(mix.astype(BF16), wo_ref[...], preferred_element_type=F32)


def _mix(x, g, nsa_out, ret_out, wg, wpa, wpb, wo, tm=512):
    S = x.shape[0]
    row = lambda i: (i, 0)
    cst = lambda i: (0, 0)
    return pl.pallas_call(
        _mix_kernel,
        grid=(S // tm,),
        in_specs=[pl.BlockSpec((tm, D_MODEL), row), pl.BlockSpec((1, D_MODEL), cst),
                  pl.BlockSpec((tm, 512), row), pl.BlockSpec((tm, 512), row),
                  pl.BlockSpec((D_MODEL, 2 * D_MODEL), cst), pl.BlockSpec((512, D_MODEL), cst),
                  pl.BlockSpec((512, D_MODEL), cst), pl.BlockSpec((D_MODEL, D_MODEL), cst)],
        out_specs=pl.BlockSpec((tm, D_MODEL), row),
        out_shape=jax.ShapeDtypeStruct((S, D_MODEL), F32),
        compiler_params=_params(("arbitrary",)), name="mix",
    )(x, g, nsa_out, ret_out, wg, wpa, wpb, wo)


def _mlp_kernel(x_ref, g_ref, wu_ref, wd_ref, gf_ref, o_ref, *, hc):
    x = x_ref[...]
    h = _rms(x, g_ref[...]).astype(BF16)
    acc = x
    for c in range(MLP_HIDDEN // hc):
        u = jnp.maximum(jnp.dot(h, wu_ref[:, c * hc:(c + 1) * hc], preferred_element_type=F32), 0.0)
        acc = acc + jnp.dot((u * u).astype(BF16), wd_ref[c * hc:(c + 1) * hc, :], preferred_element_type=F32)
    o_ref[...] = _rms(acc, gf_ref[...])


def _mlp(x, g, wu, wd, gf, tm=512, hc=1024):
    S = x.shape[0]
    row = lambda i: (i, 0)
    cst = lambda i: (0, 0)
    return pl.pallas_call(
        functools.partial(_mlp_kernel, hc=hc),
        grid=(S // tm,),
        in_specs=[pl.BlockSpec((tm, D_MODEL), row), pl.BlockSpec((1, D_MODEL), cst),
                  pl.BlockSpec((D_MODEL, MLP_HIDDEN), cst), pl.BlockSpec((MLP_HIDDEN, D_MODEL), cst),
                  pl.BlockSpec((1, D_MODEL), cst)],
        out_specs=pl.BlockSpec((tm, D_MODEL), row),
        out_shape=jax.ShapeDtypeStruct((S, D_MODEL), F32),
        compiler_params=_params(("arbitrary",)), name="mlp",
    )(x, g, wu, wd, gf)


def _rope_freqs():
    half = ROPE_DIM // 2
    inv = jnp.exp(-math.log(ROPE_THETA) * jnp.arange(half, dtype=F32) * 2.0 / ROPE_DIM)
    hr = RET_D // 2
    inv_r = jnp.exp(-math.log(RET_BASE) * jnp.arange(hr, dtype=F32) * 2.0 / RET_D)
    return jnp.concatenate([inv_r, inv, jnp.zeros((LANES - hr - half,), F32)])[None, :]


def _decay_tables():
    H, C = RET_HEADS, RET_C
    log_g = jnp.log(1.0 - jnp.exp2(-5.0 - jnp.arange(H, dtype=F32)))
    n = jnp.arange(C, dtype=F32)
    rel = n[:, None] - n[None, :]
    dmat = jnp.where(rel >= 0, jnp.exp(log_g[:, None, None] * jnp.maximum(rel, 0.0)), 0.0)
    xi = jnp.exp(log_g[:, None] * (n + 1.0))
    zeta = jnp.exp(log_g[:, None] * (C - 1.0 - n))
    dec = jnp.exp(log_g * C)
    bc = lambda a: jnp.broadcast_to(a, (H, C, RET_D))
    return dmat, bc(xi[:, :, None]), bc(zeta[:, :, None]), bc(dec[:, None, None])


def _overlap_t(S):
    n_cmp = (S - CMP_BLOCK) // CMP_STRIDE + 1
    n_slc = S // SLC_BLOCK
    cs = np.arange(n_cmp)[:, None] * CMP_STRIDE
    ss = np.arange(n_slc)[None, :] * SLC_BLOCK
    ov = np.clip(np.minimum(cs + CMP_BLOCK, ss + SLC_BLOCK) - np.maximum(cs, ss), 0, None) / CMP_BLOCK
    out = np.zeros((n_slc, S // CMP_STRIDE), np.float32)
    out[:, :n_cmp] = ov.T
    return jnp.asarray(out, BF16)


def _v_rows(v_t):
    S = v_t.shape[1]
    return jnp.concatenate([v_t, jnp.ones((1, S), v_t.dtype), jnp.zeros((VROWS - DH - 1, S), v_t.dtype)], axis=0)


_IN_OFFSETS = np.cumsum((0, 512, 128, 128, 128, 128, 128, 128, 24, 512, 512, 512, 512, 1024, 1024))


def _in_col(wi, i):
    return wi[:, _IN_OFFSETS[i]:_IN_OFFSETS[i + 1]]


def _branches(x, positions, norm_mix, w_in, cmp_pos_k, cmp_pos_v, cmp_k_w1, cmp_k_w2, cmp_v_w1, cmp_v_w2):
    B, S, _ = x.shape
    assert B == 1 and norm_mix.shape[0] == 1 and S % TK == 0 and S >= max(WIN_KEYS, SLC_COUNT * SLC_BLOCK)
    xs = x[0]
    col = functools.partial(_in_col, w_in[0])
    n_gate = 3 * HPG
    z = lambda n: jnp.zeros((D_MODEL, n), F32)
    w_a = jnp.concatenate([col(0), col(1), col(3), col(5), col(2), col(4), col(6),
                           col(7)[:, 0:n_gate], z(16 - n_gate), col(7)[:, n_gate:], z(LANES - 16 - n_gate),
                           col(8), col(9), col(10), col(11)], axis=1).astype(BF16)

    q_t, xh, ks, kw, vs_t, vw_t, gt, rq, rk, rv, rg = _proj(
        xs, norm_mix, w_a, jnp.broadcast_to(positions[0].astype(F32)[:, None], (S, LANES)), _rope_freqs())

    w1 = jnp.stack([cmp_k_w1[0], cmp_v_w1[0]]).astype(BF16)
    pe = jnp.stack([cmp_pos_k[0], cmp_pos_v[0]]).reshape(2, 1, CMP_BLOCK * DH)
    pe = jnp.broadcast_to(pe, (2, 16, CMP_BLOCK * DH)).astype(BF16)
    w2 = jnp.pad(jnp.stack([cmp_k_w2[0], cmp_v_w2[0]]), ((0, 0), (0, 0), (0, LANES - DH))).astype(BF16)
    cmp = _compress(xh, w1, pe, w2)
    kcmp = cmp[0:NSA_GROUPS].astype(BF16)
    vcmp_t = jax.vmap(_v_rows)(cmp[NSA_GROUPS:, :, 0:DH].transpose(0, 2, 1).astype(BF16))

    vw_t = jnp.pad(vw_t, ((0, 0), (0, 0), (WINDOW, 0)))
    pad_keys = jnp.zeros((NSA_GROUPS, WINDOW, LANES), BF16).at[:, :, KILL_LANE].set(1.0)
    kw = jnp.concatenate([pad_keys, kw], axis=1)
    nsa_out = _nsa(q_t, kcmp, vcmp_t, ks, vs_t, kw, vw_t, gt, _overlap_t(S))

    ret_out = _retention(rq, rk, rv, rg, *_decay_tables())
    return nsa_out, ret_out


def kernel(x, positions, norm_mix, w_in, cmp_pos_k, cmp_pos_v, cmp_k_w1, cmp_k_w2, cmp_v_w1, cmp_v_w2,
           w_proj_a, w_proj_b, w_out, norm_mlp, w_up, w_down, norm_final):
    nsa_out, ret_out = _branches(x, positions, norm_mix, w_in, cmp_pos_k, cmp_pos_v,
                                 cmp_k_w1, cmp_k_w2, cmp_v_w1, cmp_v_w2)
    xs = x[0]
    w_gate = jnp.concatenate([_in_col(w_in[0], 12), _in_col(w_in[0], 13)], axis=1).astype(BF16)
    x1 = _mix(xs, norm_mix, nsa_out, ret_out, w_gate, w_proj_a[0].astype(BF16), w_proj_b[0].astype(BF16),
              w_out[0].astype(BF16))
    y = _mlp(x1, norm_mlp, w_up[0].astype(BF16), w_down[0].astype(BF16), norm_final[None, :])
    return y[None]
```

```python
import functools
import math

import numpy as np
import jax
import jax.numpy as jnp
from jax import lax
from jax.experimental import pallas as pl
from jax.experimental.pallas import tpu as pltpu

F32 = jnp.float32
BF16 = jnp.bfloat16

D_MODEL = 1024
NSA_HEADS = 8
NSA_GROUPS = 2
HPG = NSA_HEADS // NSA_GROUPS
DH = 64
CMP_BLOCK = 32
CMP_STRIDE = 16
CMP_HIDDEN = 256
SLC_BLOCK = 64
SLC_COUNT = 16
WINDOW = 512
QB = 256
ROPE_THETA = 500000.0
ROPE_DIM = DH // 4
RET_HEADS = 4
RET_D = 128
RET_C = 128
RET_BASE = 10000.0
MLP_HIDDEN = 4 * D_MODEL
EPS = 1e-6
NEG = -1e30
BIG = 1e30

LANES = 128
TK = 1024
BLK_PER_TILE = TK // SLC_BLOCK
IMP_ROWS = 64
KILL_LANE = DH + BLK_PER_TILE
REF_LANE = KILL_LANE + 1
REF_SLACK = 64.0
LOG2E = math.log2(math.e)
VROWS = 80
WIN_KEYS = WINDOW + QB
VMEM_LIMIT = 56 * 1024 * 1024

NT_DIMS = (((1,), (1,)), ((), ()))
TN_DIMS = (((0,), (0,)), ((), ()))


def _sigmoid(v):
    return 1.0 / (1.0 + jnp.exp(-v))


def _rms(x, g):
    return x * lax.rsqrt(jnp.mean(x * x, axis=-1, keepdims=True) + EPS) * g


def _params(sem, flags=None):
    return pltpu.CompilerParams(dimension_semantics=sem, vmem_limit_bytes=VMEM_LIMIT, flags=flags)


_C_ROT = (0, 896)
_C_V = (896, 1280)
_C_G = (1280, 1408)
_C_RQ = (1408, 1920)
_C_RK = (1920, 2432)
_C_RV = (2432, 2944)
_C_RG = (2944, 3456)
PROJ_W = 3456


def _proj_kernel(x_ref, g_ref, w_ref, pos_ref, freq_ref,
                 q_ref, xh_ref, ks_ref, kw_ref, vs_ref, vw_ref, gt_ref, rq_ref, rk_ref, rv_ref, rg_ref,
                 kv_ref, *, tm):
    h = _rms(x_ref[...], g_ref[...]).astype(BF16)
    lane = lax.broadcasted_iota(jnp.int32, (tm, LANES), 1)
    lo = lane < DH

    ang = pos_ref[...] * freq_ref[...]
    c, s = jnp.cos(ang), jnp.sin(ang)
    half_n = ROPE_DIM // 2
    cr = jnp.where(lo, c, pltpu.roll(c, DH, 1))
    sr = jnp.where(lo, -s, pltpu.roll(s, DH, 1))
    dim = lane & (DH - 1)
    first, second = dim < half_n, (dim >= half_n) & (dim < ROPE_DIM)
    c1 = jnp.where(lo, pltpu.roll(c, DH, 1), c)
    s1 = jnp.where(lo, pltpu.roll(s, DH, 1), s)
    c2 = jnp.where(lo, pltpu.roll(c, DH + half_n, 1), pltpu.roll(c, half_n, 1))
    s2 = jnp.where(lo, pltpu.roll(s, DH + half_n, 1), pltpu.roll(s, half_n, 1))
    cn = jnp.where(first, c1, jnp.where(second, c2, 1.0))
    sa = jnp.where(first, -s1, 0.0)
    sb = jnp.where(second, s2, 0.0)

    def dot(c):
        return jnp.dot(h, w_ref[:, c[0]:c[1]], preferred_element_type=F32)

    def tile(y, i):
        return y[:, i * LANES:(i + 1) * LANES]

    def rope_n(t):
        return t * cn + pltpu.roll(t, LANES - half_n, 1) * sa + pltpu.roll(t, half_n, 1) * sb

    def rope_r(t):
        return t * cr + pltpu.roll(t, RET_D // 2, 1) * sr

    def split(t):
        return jnp.where(lo, t, 0.0), jnp.where(lo, pltpu.roll(t, DH, 1), 0.0)

    def v_rows(dst_ref, t):
        tt = t.T
        tail = jnp.where(lax.broadcasted_iota(jnp.int32, (VROWS - DH, tm), 0) == 0, 1.0, 0.0).astype(BF16)
        for g in range(NSA_GROUPS):
            dst_ref[g, 0:DH, :] = tt[g * DH:(g + 1) * DH, :].astype(BF16)
            dst_ref[g, DH:VROWS, :] = tail

    yr = dot(_C_ROT)
    for i in range(4):
        tt = (rope_n(tile(yr, i)) * (DH ** -0.5 * LOG2E)).T
        q_ref[2 * i] = tt[0:DH, :].astype(BF16)
        q_ref[2 * i + 1] = tt[DH:2 * DH, :].astype(BF16)
    row = pl.program_id(0) * tm + lax.broadcasted_iota(jnp.int32, (tm, LANES), 0)
    blk = (row >> 6) & (BLK_PER_TILE - 1)
    onehot = jnp.where(((lane - DH) == blk) | (lane == REF_LANE), 1.0, 0.0)
    a, b = split(rope_n(tile(yr, 5)))
    ks_ref[0] = jnp.where(lo, a, onehot).astype(BF16)
    ks_ref[1] = jnp.where(lo, b, onehot).astype(BF16)
    a, b = split(rope_n(tile(yr, 6)))
    kw_ref[0] = a.astype(BF16)
    kw_ref[1] = b.astype(BF16)

    yv = dot(_C_V)
    v_rows(vs_ref, tile(yv, 1))
    v_rows(vw_ref, tile(yv, 2))
    gt = _sigmoid(dot(_C_G)).T
    gt_ref[0] = gt[0:16, :]
    gt_ref[1] = gt[16:32, :]

    kv_ref[0] = rope_n(tile(yr, 4))
    kv_ref[1] = tile(yv, 0)
    lo_r = lax.broadcasted_iota(jnp.int32, (tm // CMP_STRIDE, LANES), 1) < DH
    for typ in range(2):
        for u in range(CMP_STRIDE // 2):
            pa = kv_ref[typ, pl.ds(2 * u, tm // CMP_STRIDE, stride=CMP_STRIDE), :]
            pb = kv_ref[typ, pl.ds(2 * u + 1, tm // CMP_STRIDE, stride=CMP_STRIDE), :]
            cols = slice(u * LANES, (u + 1) * LANES)
            xh_ref[2 * typ, :, cols] = jnp.where(lo_r, pa, pltpu.roll(pb, DH, 1)).astype(BF16)
            xh_ref[2 * typ + 1, :, cols] = jnp.where(lo_r, pltpu.roll(pa, DH, 1), pb).astype(BF16)

    yq = dot(_C_RQ)
    yk = dot(_C_RK)
    for i in range(RET_HEADS):
        sl = slice(i * LANES, (i + 1) * LANES)
        rq_ref[:, sl] = rope_r(tile(yq, i)).astype(BF16)
        rk_ref[:, sl] = (rope_r(tile(yk, i)) * (RET_D ** -0.5)).astype(BF16)
    rv_ref[...] = dot(_C_RV).astype(BF16)
    rg_ref[...] = dot(_C_RG).astype(BF16)


def _proj(x, g, w, pos, freq, tm=512):
    S = x.shape[0]
    row = lambda i: (i, 0)
    const = lambda i: (0, 0)
    mid = lambda i: (0, i, 0)
    last = lambda i: (0, 0, i)
    W = RET_HEADS * RET_D
    out_shape = (
        jax.ShapeDtypeStruct((NSA_HEADS, DH, S), BF16),
        jax.ShapeDtypeStruct((2 * NSA_GROUPS, S // CMP_STRIDE, CMP_STRIDE * DH), BF16),
        jax.ShapeDtypeStruct((NSA_GROUPS, S, LANES), BF16),
        jax.ShapeDtypeStruct((NSA_GROUPS, S, LANES), BF16),
        jax.ShapeDtypeStruct((NSA_GROUPS, VROWS, S), BF16),
        jax.ShapeDtypeStruct((NSA_GROUPS, VROWS, S), BF16),
        jax.ShapeDtypeStruct((NSA_GROUPS, 16, S), F32),
        jax.ShapeDtypeStruct((S, W), BF16),
        jax.ShapeDtypeStruct((S, W), BF16),
        jax.ShapeDtypeStruct((S, W), BF16),
        jax.ShapeDtypeStruct((S, W), BF16),
    )
    out_specs = (
        pl.BlockSpec((NSA_HEADS, DH, tm), last),
        pl.BlockSpec((2 * NSA_GROUPS, tm // CMP_STRIDE, CMP_STRIDE * DH), mid),
        pl.BlockSpec((NSA_GROUPS, tm, LANES), mid),
        pl.BlockSpec((NSA_GROUPS, tm, LANES), mid),
        pl.BlockSpec((NSA_GROUPS, VROWS, tm), last),
        pl.BlockSpec((NSA_GROUPS, VROWS, tm), last),
        pl.BlockSpec((NSA_GROUPS, 16, tm), last),
        pl.BlockSpec((tm, W), row),
        pl.BlockSpec((tm, W), row),
        pl.BlockSpec((tm, W), row),
        pl.BlockSpec((tm, W), row),
    )
    return pl.pallas_call(
        functools.partial(_proj_kernel, tm=tm),
        grid=(S // tm,),
        in_specs=[pl.BlockSpec((tm, D_MODEL), row), pl.BlockSpec((1, D_MODEL), const),
                  pl.BlockSpec((D_MODEL, PROJ_W), const), pl.BlockSpec((tm, LANES), row),
                  pl.BlockSpec((1, LANES), const)],
        out_specs=out_specs, out_shape=out_shape,
        scratch_shapes=[pltpu.VMEM((2, tm, LANES), F32)],
        compiler_params=_params(("arbitrary",)), name="proj",
    )(x, g, w, pos, freq)


def _compress_kernel(x_ref, w1_ref, pe_ref, w2_ref, o_ref):
    x = x_ref[...]
    half = CMP_STRIDE * DH
    a = jnp.dot(x, w1_ref[0:half, :], preferred_element_type=F32)
    b = jnp.dot(x, w1_ref[half:2 * half, :], preferred_element_type=F32)
    peb = jnp.dot(pe_ref[...], w1_ref[...], preferred_element_type=F32)[0:1, :]
    hid = a + pltpu.roll(b, x.shape[0] - 1, 0) + peb
    hid = hid * _sigmoid(hid)
    o_ref[...] = jnp.dot(hid.astype(BF16), w2_ref[...], preferred_element_type=F32)


def _compress(xh, w1, pe, w2):
    n_half = xh.shape[1]
    return pl.pallas_call(
        _compress_kernel,
        grid=(2 * NSA_GROUPS,),
        in_specs=[pl.BlockSpec((None, n_half, CMP_STRIDE * DH), lambda i: (i, 0, 0)),
                  pl.BlockSpec((None, CMP_BLOCK * DH, CMP_HIDDEN), lambda i: (i // NSA_GROUPS, 0, 0)),
                  pl.BlockSpec((None, 16, CMP_BLOCK * DH), lambda i: (i // NSA_GROUPS, 0, 0)),
                  pl.BlockSpec((None, CMP_HIDDEN, LANES), lambda i: (i // NSA_GROUPS, 0, 0))],
        out_specs=pl.BlockSpec((None, n_half, LANES), lambda i: (i, 0, 0)),
        out_shape=jax.ShapeDtypeStruct((2 * NSA_GROUPS, n_half, LANES), F32),
        compiler_params=_params(("arbitrary",)), name="compress",
    )(xh, w1, pe, w2)


def _nsa_kernel(q_ref, qn_ref, kcmp_ref, vcmp_ref, ks_ref, vs_ref, kw_ref, vw_ref, gt_ref, ovl_ref,
                out_ref, qa_ref, qs_ref, bias_ref, biasn_ref, ocn_ref, m_ref, acc_ref,
                sa_ref, p_ref, c_ref, rise_ref, owin_ref, *, S):
    NC = S // CMP_STRIDE
    NB = S // SLC_BLOCK
    NQ = HPG * QB
    b = pl.program_id(1)
    s0 = b * QB
    lane_q = lax.broadcasted_iota(jnp.int32, (1, NQ), 1) & (QB - 1)
    band_row = lax.broadcasted_iota(jnp.int32, (QB, NQ), 0)
    band_tq = lax.broadcasted_iota(jnp.int32, (QB, NQ), 1) & (QB - 1)

    def sel_scores(qsrc_ref, cr):
        qs_ref[0:DH, :] = jnp.concatenate([qsrc_ref[h] for h in range(HPG)], axis=1)
        qs_ref[DH:LANES, :] = jnp.zeros((LANES - DH, NQ), BF16)
        return jnp.dot(kcmp_ref[0:cr, :], qs_ref[...], preferred_element_type=F32)

    def sel_probs(sc, blk):
        c_end = lax.broadcasted_iota(jnp.int32, sc.shape, 0) * CMP_STRIDE + (CMP_BLOCK - 1)
        sc = jnp.where(c_end <= blk * QB + lane_q, sc, NEG)
        return jnp.exp2(sc - jnp.max(sc, axis=0, keepdims=True)).astype(BF16)

    def sel_matmuls(pc):
        cr = pc.shape[0]
        ocl = jnp.dot(vcmp_ref[:, 0:cr], pc, preferred_element_type=F32)
        ir, ic = IMP_ROWS, 4 * IMP_ROWS
        if cr % ic:
            return ocl, jnp.dot(ovl_ref[0:cr // 4, 0:cr], pc, preferred_element_type=F32)
        parts = []
        for i in range(cr // ic):
            c0 = max(i - 1, 0) * ic
            parts.append(jnp.dot(ovl_ref[i * ir:(i + 1) * ir, c0:(i + 1) * ic], pc[c0:(i + 1) * ic, :],
                                 preferred_element_type=F32))
        return ocl, jnp.concatenate(parts, axis=0)

    def sel_choose(ocl, imp4, blk):
        t_row = blk * QB + lane_q
        any_valid = jnp.where(t_row >= CMP_BLOCK - 1, 1.0, 0.0)
        rc = any_valid / ocl[DH:DH + 1, :]
        ocn_ref[...] = ocl[0:DH, :] * rc

        imp4 = imp4 * rc
        imp = imp4[:, 0:QB] + imp4[:, QB:2 * QB] + imp4[:, 2 * QB:3 * QB] + imp4[:, 3 * QB:4 * QB]
        if imp.shape[0] < NB:
            imp = jnp.concatenate([imp, jnp.zeros((NB - imp.shape[0], QB), F32)], axis=0)

        jidx = lax.broadcasted_iota(jnp.int32, (NB, QB), 0)
        cur = (blk * QB + lax.broadcasted_iota(jnp.int32, (1, QB), 1)) >> 6
        forced = (jidx == 0) | (jidx == cur) | (jidx == cur - 1)
        future = jidx > cur
        cand = jnp.where(forced | future, -jnp.inf, imp)

        def select(break_ties):
            v = cand
            for _ in range(SLC_COUNT - 3):
                mx = jnp.max(v, axis=0, keepdims=True)
                hit = v == mx
                if break_ties:
                    hit = jidx == jnp.min(jnp.where(hit, jidx, NB), axis=0, keepdims=True)
                v = jnp.where(hit, -jnp.inf, v)
            return jnp.where((v == -jnp.inf) & (jidx <= cur), 0.0, NEG)

        bias_t = select(False)
        n_sel = jnp.sum(jnp.where(bias_t == 0.0, 1.0, 0.0), axis=0, keepdims=True)
        biasn_ref[...] = bias_t.astype(BF16)

        def redo_if_tied():
            @pl.when(jnp.max(n_sel) > SLC_COUNT)
            def _():
                biasn_ref[...] = select(True).astype(BF16)
        return redo_if_tied

    cr_need = jnp.minimum((b + 2) * (QB // CMP_STRIDE) - 1, NC)
    cr_sizes = (NC // 2, NC) if (NC // 2) % (4 * IMP_ROWS) == 0 else (NC,)

    @pl.when(b == 0)
    def _():
        ocl, imp4 = sel_matmuls(sel_probs(sel_scores(q_ref, cr_sizes[0]), 0))
        sel_choose(ocl, imp4, 0)()

    bias_ref[...] = biasn_ref[...]
    oc = ocn_ref[...]

    qa_ref[0:DH, :] = jnp.concatenate([q_ref[h] for h in range(HPG)], axis=1)
    qa_ref[DH:KILL_LANE, :] = jnp.zeros((KILL_LANE - DH, NQ), BF16)

    def set_bias_rows(kt):
        bias_rows = bias_ref[pl.ds(pl.multiple_of(kt * BLK_PER_TILE, BLK_PER_TILE), BLK_PER_TILE), :]
        qa_ref[DH:DH + BLK_PER_TILE, :] = jnp.concatenate([bias_rows] * HPG, axis=1)

    def set_flag_rows(ref):
        r = lax.broadcasted_iota(jnp.int32, (LANES - KILL_LANE, NQ), 0)
        rows = jnp.where(r == 0, NEG, 0.0) if ref is None else jnp.where(r == 0, NEG, jnp.where(r == 1, -ref, 0.0))
        qa_ref[KILL_LANE:LANES, :] = rows.astype(BF16)

    def scores(kt, s_ref):
        k0 = pl.multiple_of(kt * TK, TK)
        set_bias_rows(kt)
        s_ref[...] = jnp.dot(ks_ref[pl.ds(k0, TK), :], qa_ref[...], preferred_element_type=F32)

    def accumulate(kt, s_ref):
        k0 = pl.multiple_of(kt * TK, TK)
        m_old = m_ref[...]
        m_new = jnp.maximum(m_old, jnp.max(s_ref[...], axis=0, keepdims=True))
        p_ref[...] = jnp.exp2(s_ref[...] - m_new).astype(BF16)
        acc_ref[...] = (jnp.exp2(m_old - m_new) * acc_ref[...]
                        + jnp.dot(vs_ref[:, pl.ds(k0, TK)], p_ref[...], preferred_element_type=F32))
        m_ref[...] = m_new

    def mask_own_keys(s_ref):
        band = pl.ds(pl.multiple_of(s0 - n_full * TK, QB), QB)
        s_ref[band, :] = jnp.where(band_row <= band_tq, s_ref[band, :], NEG)

    n_full = s0 // TK

    m_ref[...] = jnp.full((1, NQ), NEG, F32)
    acc_ref[...] = jnp.zeros((VROWS, NQ), F32)
    set_flag_rows(None)

    def first_region(cr):
        sc = sel_scores(qn_ref, cr)
        sw = jnp.dot(kw_ref[pl.ds(pl.multiple_of(s0, QB), WIN_KEYS), :], qa_ref[...],
                     preferred_element_type=F32)
        scores(n_full, sa_ref)

        pc = sel_probs(sc, b + 1)
        ocl, imp4 = sel_matmuls(pc)
        redo_if_tied = sel_choose(ocl, imp4, b + 1)

        sw = jnp.concatenate([jnp.where(band_row > band_tq, sw[0:QB], NEG),
                              sw[QB:WINDOW],
                              jnp.where(band_row <= band_tq, sw[WINDOW:WIN_KEYS], NEG)], axis=0)
        pw = jnp.exp2(sw - jnp.max(sw, axis=0, keepdims=True)).astype(BF16)
        mask_own_keys(sa_ref)

        ow = jnp.dot(vw_ref[:, pl.ds(pl.multiple_of(s0, QB), WIN_KEYS)], pw, preferred_element_type=F32)
        owin_ref[...] = ow[0:DH, :] * (1.0 / ow[DH:DH + 1, :])
        accumulate(n_full, sa_ref)

        m = m_ref[...]
        c0 = m.astype(BF16).astype(F32)
        c_ref[...] = c0
        acc_ref[...] = acc_ref[...] * jnp.exp2(m - c0)
        rise_ref[...] = jnp.zeros((1, NQ), F32)
        redo_if_tied()

    for i, cr in enumerate(cr_sizes):
        fits = cr_need <= cr
        if i > 0:
            fits = fits & (cr_need > cr_sizes[i - 1])
        pl.when(fits)(functools.partial(first_region, cr))

    def older_tile(j, carry):
        k0 = pl.multiple_of(j * TK, TK)
        c = c_ref[...]
        set_bias_rows(j)
        set_flag_rows(c)
        s = jnp.dot(ks_ref[pl.ds(k0, TK), :], qa_ref[...], preferred_element_type=F32)
        mt = jnp.max(s, axis=0, keepdims=True)
        p_ref[...] = jnp.exp2(s).astype(BF16)
        pv = jnp.dot(vs_ref[:, pl.ds(k0, TK)], p_ref[...], preferred_element_type=F32)
        c_new = (c + jnp.maximum(mt, 0.0)).astype(BF16).astype(F32)
        acc_ref[...] = (acc_ref[...] + pv) * jnp.exp2(c - c_new)
        c_ref[...] = c_new
        rise_ref[...] = jnp.maximum(rise_ref[...], mt)
        return carry

    lax.fori_loop(0, n_full, older_tile, 0)

    @pl.when(jnp.max(rise_ref[...]) > REF_SLACK)
    def _():
        m_ref[...] = jnp.full((1, NQ), NEG, F32)
        acc_ref[...] = jnp.zeros((VROWS, NQ), F32)
        set_flag_rows(None)

        def two_pass_tile(kt, carry):
            scores(kt, sa_ref)

            @pl.when(kt == n_full)
            def _():
                mask_own_keys(sa_ref)
            accumulate(kt, sa_ref)
            return carry

        lax.fori_loop(0, n_full + 1, two_pass_tile, 0)

    acc = acc_ref[...]
    osel = acc[0:DH, :] * (1.0 / acc[DH:DH + 1, :])
    owin = owin_ref[...]

    heads = []
    for h in range(HPG):
        sl = slice(h * QB, (h + 1) * QB)
        heads.append(gt_ref[3 * h:3 * h + 1, :] * oc[:, sl]
                     + gt_ref[3 * h + 1:3 * h + 2, :] * osel[:, sl]
                     + gt_ref[3 * h + 2:3 * h + 3, :] * owin[:, sl])
    out_ref[...] = jnp.concatenate(heads, axis=0).T.astype(out_ref.dtype)


def _nsa(q_t, kcmp, vcmp_t, ks, vs_t, kw, vw_t, gt, ovl_t):
    S = ks.shape[1]
    NC, NB = S // CMP_STRIDE, S // SLC_BLOCK
    n_qb = S // QB
    grp = lambda g, b: (g, 0, 0)
    return pl.pallas_call(
        functools.partial(_nsa_kernel, S=S),
        grid=(NSA_GROUPS, n_qb),
        in_specs=[pl.BlockSpec((HPG, DH, QB), lambda g, b: (g, 0, b)),
                  pl.BlockSpec((HPG, DH, QB), lambda g, b: (g, 0, jnp.minimum(b + 1, n_qb - 1))),
                  pl.BlockSpec((None, NC, LANES), grp),
                  pl.BlockSpec((None, VROWS, NC), grp),
                  pl.BlockSpec((None, S, LANES), grp, pipeline_mode=pl.Buffered(1)),
                  pl.BlockSpec((None, VROWS, S), grp, pipeline_mode=pl.Buffered(1)),
                  pl.BlockSpec((None, WINDOW + S, LANES), grp, pipeline_mode=pl.Buffered(1)),
                  pl.BlockSpec((None, VROWS, WINDOW + S), grp, pipeline_mode=pl.Buffered(1)),
                  pl.BlockSpec((None, 16, QB), lambda g, b: (g, 0, b)),
                  pl.BlockSpec((NB, NC), lambda g, b: (0, 0))],
        out_specs=pl.BlockSpec((QB, HPG * DH), lambda g, b: (b, g)),
        out_shape=jax.ShapeDtypeStruct((S, NSA_HEADS * DH), BF16),
        scratch_shapes=[pltpu.VMEM((LANES, HPG * QB), BF16),
                        pltpu.VMEM((LANES, HPG * QB), BF16),
                        pltpu.VMEM((NB, QB), BF16),
                        pltpu.VMEM((NB, QB), BF16),
                        pltpu.VMEM((DH, HPG * QB), F32),
                        pltpu.VMEM((1, HPG * QB), F32),
                        pltpu.VMEM((VROWS, HPG * QB), F32),
                        pltpu.VMEM((TK, HPG * QB), F32),
                        pltpu.VMEM((TK, HPG * QB), BF16),
                        pltpu.VMEM((1, HPG * QB), F32),
                        pltpu.VMEM((1, HPG * QB), F32),
                        pltpu.VMEM((DH, HPG * QB), F32)],
        compiler_params=_params(("arbitrary", "arbitrary")), name="nsa",
    )(q_t, q_t, kcmp, vcmp_t, ks, vs_t, kw, vw_t, gt, ovl_t)


def _ret_kernel(q_ref, k_ref, v_ref, g_ref, dmat_ref, xi_ref, zeta_ref, dec_ref, o_ref, state_ref, *, tm):
    @pl.when(pl.program_id(0) == 0)
    def _():
        state_ref[...] = jnp.zeros_like(state_ref)

    for j in range(tm // RET_C):
        rows = slice(j * RET_C, (j + 1) * RET_C)
        for h in range(RET_HEADS):
            cols = slice(h * RET_D, (h + 1) * RET_D)
            qc, kc, vc = q_ref[rows, cols], k_ref[rows, cols], v_ref[rows, cols]
            st = state_ref[h]
            st_hi = st.astype(BF16)
            st_lo = (st - st_hi.astype(F32)).astype(BF16)
            sc = lax.dot_general(qc, kc, NT_DIMS, preferred_element_type=F32) * dmat_ref[h]
            inner = jnp.dot(sc.astype(BF16), vc, preferred_element_type=F32)
            cross = (jnp.dot(qc, st_hi, preferred_element_type=F32)
                     + jnp.dot(qc, st_lo, preferred_element_type=F32)) * xi_ref[h]
            kz = (kc.astype(F32) * zeta_ref[h]).astype(BF16)
            state_ref[h] = dec_ref[h] * st + lax.dot_general(kz, vc, TN_DIMS, preferred_element_type=F32)
            o = inner + cross
            o = o * lax.rsqrt(jnp.mean(o * o, axis=-1, keepdims=True) + EPS)
            gate = g_ref[rows, cols].astype(F32)
            o_ref[rows, cols] = (gate * _sigmoid(gate) * o).astype(o_ref.dtype)


def _retention(rq, rk, rv, rg, dmat, xi, zeta, dec, tm=512):
    S = rq.shape[0]
    W = RET_HEADS * RET_D
    row = lambda i: (i, 0)
    cst = lambda i: (0, 0, 0)
    blk = pl.BlockSpec((tm, W), row)
    tbl = pl.BlockSpec((RET_HEADS, RET_C, RET_D), cst)
    return pl.pallas_call(
        functools.partial(_ret_kernel, tm=tm),
        grid=(S // tm,),
        in_specs=[blk, blk, blk, blk, tbl, tbl, tbl, tbl],
        out_specs=blk,
        out_shape=jax.ShapeDtypeStruct((S, W), BF16),
        scratch_shapes=[pltpu.VMEM((RET_HEADS, RET_D, RET_D), F32)],
        compiler_params=_params(("arbitrary",)), name="retention",
    )(rq, rk, rv, rg, dmat, xi, zeta, dec)


def _mix_kernel(x_ref, g_ref, a_ref, r_ref, wg_ref, wpa_ref, wpb_ref, wo_ref, o_ref):
    x = x_ref[...]
    h = _rms(x, g_ref[...]).astype(BF16)
    ga = _sigmoid(jnp.dot(h, wg_ref[:, 0:D_MODEL], preferred_element_type=F32))
    gb = _sigmoid(jnp.dot(h, wg_ref[:, D_MODEL:2 * D_MODEL], preferred_element_type=F32))
    mix = (ga * jnp.dot(a_ref[...], wpa_ref[...], preferred_element_type=F32)
           + gb * jnp.dot(r_ref[...], wpb_ref[...], preferred_element_type=F32))
    o_ref[...] = x + jnp.dot(mix.astype(BF16), wo_ref[...], preferred_element_type=F32)


def _mix(x, g, nsa_out, ret_out, wg, wpa, wpb, wo, tm=512):
    S = x.shape[0]
    row = lambda i: (i, 0)
    cst = lambda i: (0, 0)
    return pl.pallas_call(
        _mix_kernel,
        grid=(S // tm,),
        in_specs=[pl.BlockSpec((tm, D_MODEL), row), pl.BlockSpec((1, D_MODEL), cst),
                  pl.BlockSpec((tm, 512), row), pl.BlockSpec((tm, 512), row),
                  pl.BlockSpec((D_MODEL, 2 * D_MODEL), cst), pl.BlockSpec((512, D_MODEL), cst),
                  pl.BlockSpec((512, D_MODEL), cst), pl.BlockSpec((D_MODEL, D_MODEL), cst)],
        out_specs=pl.BlockSpec((tm, D_MODEL), row),
        out_shape=jax.ShapeDtypeStruct((S, D_MODEL), F32),
        compiler_params=_params(("arbitrary",)), name="mix",
    )(x, g, nsa_out, ret_out, wg, wpa, wpb, wo)


def _mlp_kernel(x_ref, g_ref, wu_ref, wd_ref, gf_ref, o_ref, *, hc):
    x = x_ref[...]
    h = _rms(x, g_ref[...]).astype(BF16)
    acc = x
    for c in range(MLP_HIDDEN // hc):
        u = jnp.maximum(jnp.dot(h, wu_ref[:, c * hc:(c + 1) * hc], preferred_element_type=F32), 0.0)
        acc = acc + jnp.dot((u * u).astype(BF16), wd_ref[c * hc:(c + 1) * hc, :], preferred_element_type=F32)
    o_ref[...] = _rms(acc, gf_ref[...])


def _mlp(x, g, wu, wd, gf, tm=512, hc=1024):
    S = x.shape[0]
    row = lambda i: (i, 0)
    cst = lambda i: (0, 0)
    return pl.pallas_call(
        functools.partial(_mlp_kernel, hc=hc),
        grid=(S // tm,),
        in_specs=[pl.BlockSpec((tm, D_MODEL), row), pl.BlockSpec((1, D_MODEL), cst),
                  pl.BlockSpec((D_MODEL, MLP_HIDDEN), cst), pl.BlockSpec((MLP_HIDDEN, D_MODEL), cst),
                  pl.BlockSpec((1, D_MODEL), cst)],
        out_specs=pl.BlockSpec((tm, D_MODEL), row),
        out_shape=jax.ShapeDtypeStruct((S, D_MODEL), F32),
        compiler_params=_params(("arbitrary",)), name="mlp",
    )(x, g, wu, wd, gf)


def _rope_freqs():
    half = ROPE_DIM // 2
    inv = jnp.exp(-math.log(ROPE_THETA) * jnp.arange(half, dtype=F32) * 2.0 / ROPE_DIM)
    hr = RET_D // 2
    inv_r = jnp.exp(-math.log(RET_BASE) * jnp.arange(hr, dtype=F32) * 2.0 / RET_D)
    return jnp.concatenate([inv_r, inv, jnp.zeros((LANES - hr - half,), F32)])[None, :]


def _decay_tables():
    H, C = RET_HEADS, RET_C
    log_g = jnp.log(1.0 - jnp.exp2(-5.0 - jnp.arange(H, dtype=F32)))
    n = jnp.arange(C, dtype=F32)
    rel = n[:, None] - n[None, :]
    dmat = jnp.where(rel >= 0, jnp.exp(log_g[:, None, None] * jnp.maximum(rel, 0.0)), 0.0)
    xi = jnp.exp(log_g[:, None] * (n + 1.0))
    zeta = jnp.exp(log_g[:, None] * (C - 1.0 - n))
    dec = jnp.exp(log_g * C)
    bc = lambda a: jnp.broadcast_to(a, (H, C, RET_D))
    return dmat, bc(xi[:, :, None]), bc(zeta[:, :, None]), bc(dec[:, None, None])


def _overlap_t(S):
    n_cmp = (S - CMP_BLOCK) // CMP_STRIDE + 1
    n_slc = S // SLC_BLOCK
    cs = np.arange(n_cmp)[:, None] * CMP_STRIDE
    ss = np.arange(n_slc)[None, :] * SLC_BLOCK
    ov = np.clip(np.minimum(cs + CMP_BLOCK, ss + SLC_BLOCK) - np.maximum(cs, ss), 0, None) / CMP_BLOCK
    out = np.zeros((n_slc, S // CMP_STRIDE), np.float32)
    out[:, :n_cmp] = ov.T
    return jnp.asarray(out, BF16)


def _v_rows(v_t):
    S = v_t.shape[1]
    return jnp.concatenate([v_t, jnp.ones((1, S), v_t.dtype), jnp.zeros((VROWS - DH - 1, S), v_t.dtype)], axis=0)


_IN_OFFSETS = np.cumsum((0, 512, 128, 128, 128, 128, 128, 128, 24, 512, 512, 512, 512, 1024, 1024))


def _in_col(wi, i):
    return wi[:, _IN_OFFSETS[i]:_IN_OFFSETS[i + 1]]


def _branches(x, positions, norm_mix, w_in, cmp_pos_k, cmp_pos_v, cmp_k_w1, cmp_k_w2, cmp_v_w1, cmp_v_w2):
    B, S, _ = x.shape
    assert B == 1 and norm_mix.shape[0] == 1 and S % TK == 0 and S >= max(WIN_KEYS, SLC_COUNT * SLC_BLOCK)
    xs = x[0]
    col = functools.partial(_in_col, w_in[0])
    n_gate = 3 * HPG
    z = lambda n: jnp.zeros((D_MODEL, n), F32)
    w_a = jnp.concatenate([col(0), col(1), col(3), col(5), col(2), col(4), col(6),
                           col(7)[:, 0:n_gate], z(16 - n_gate), col(7)[:, n_gate:], z(LANES - 16 - n_gate),
                           col(8), col(9), col(10), col(11)], axis=1).astype(BF16)

    q_t, xh, ks, kw, vs_t, vw_t, gt, rq, rk, rv, rg = _proj(
        xs, norm_mix, w_a, jnp.broadcast_to(positions[0].astype(F32)[:, None], (S, LANES)), _rope_freqs())

    w1 = jnp.stack([cmp_k_w1[0], cmp_v_w1[0]]).astype(BF16)
    pe = jnp.stack([cmp_pos_k[0], cmp_pos_v[0]]).reshape(2, 1, CMP_BLOCK * DH)
    pe = jnp.broadcast_to(pe, (2, 16, CMP_BLOCK * DH)).astype(BF16)
    w2 = jnp.pad(jnp.stack([cmp_k_w2[0], cmp_v_w2[0]]), ((0, 0), (0, 0), (0, LANES - DH))).astype(BF16)
    cmp = _compress(xh, w1, pe, w2)
    kcmp = cmp[0:NSA_GROUPS].astype(BF16)
    vcmp_t = jax.vmap(_v_rows)(cmp[NSA_GROUPS:, :, 0:DH].transpose(0, 2, 1).astype(BF16))

    vw_t = jnp.pad(vw_t, ((0, 0), (0, 0), (WINDOW, 0)))
    pad_keys = jnp.zeros((NSA_GROUPS, WINDOW, LANES), BF16).at[:, :, KILL_LANE].set(1.0)
    kw = jnp.concatenate([pad_keys, kw], axis=1)
    nsa_out = _nsa(q_t, kcmp, vcmp_t, ks, vs_t, kw, vw_t, gt, _overlap_t(S))

    ret_out = _retention(rq, rk, rv, rg, *_decay_tables())
    return nsa_out, ret_out


def kernel(x, positions, norm_mix, w_in, cmp_pos_k, cmp_pos_v, cmp_k_w1, cmp_k_w2, cmp_v_w1, cmp_v_w2,
           w_proj_a, w_proj_b, w_out, norm_mlp, w_up, w_down, norm_final):
    nsa_out, ret_out = _branches(x, positions, norm_mix, w_in, cmp_pos_k, cmp_pos_v,
                                 cmp_k_w1, cmp_k_w2, cmp_v_w1, cmp_v_w2)
    xs = x[0]
    w_gate = jnp.concatenate([_in_col(w_in[0], 12), _in_col(w_in[0], 13)], axis=1).astype(BF16)
    x1 = _mix(xs, norm_mix, nsa_out, ret_out, w_gate, w_proj_a[0].astype(BF16), w_proj_b[0].astype(BF16),
              w_out[0].astype(BF16))
    y = _mlp(x1, norm_mlp, w_up[0].astype(BF16), w_down[0].astype(BF16), norm_final[None, :])
    return y[None]
```

```python
import functools
import math

import numpy as np
import jax
import jax.numpy as jnp
from jax import lax
from jax.experimental import pallas as pl
from jax.experimental.pallas import tpu as pltpu

F32 = jnp.float32
BF16 = jnp.bfloat16

D_MODEL = 1024
NSA_HEADS = 8
NSA_GROUPS = 2
HPG = NSA_HEADS // NSA_GROUPS
DH = 64
CMP_BLOCK = 32
CMP_STRIDE = 16
CMP_HIDDEN = 256
SLC_BLOCK = 64
SLC_COUNT = 16
WINDOW = 512
QB = 256
ROPE_THETA = 500000.0
ROPE_DIM = DH // 4
RET_HEADS = 4
RET_D = 128
RET_C = 128
RET_BASE = 10000.0
MLP_HIDDEN = 4 * D_MODEL
EPS = 1e-6
NEG = -1e30
BIG = 1e30

LANES = 128
TK = 1024
TK_OLD = 2 * TK
BLK_PER_TILE = TK_OLD // SLC_BLOCK
IMP_ROWS = 64
KILL_LANE = DH + BLK_PER_TILE
REF_LANE = KILL_LANE + 1
REF_SLACK = 64.0
LOG2E = math.log2(math.e)
VROWS = 80
WIN_KEYS = WINDOW + QB
VMEM_LIMIT = 56 * 1024 * 1024

NT_DIMS = (((1,), (1,)), ((), ()))
TN_DIMS = (((0,), (0,)), ((), ()))


def _sigmoid(v):
    return 1.0 / (1.0 + jnp.exp(-v))


def _rms(x, g):
    return x * lax.rsqrt(jnp.mean(x * x, axis=-1, keepdims=True) + EPS) * g


def _params(sem, flags=None):
    return pltpu.CompilerParams(dimension_semantics=sem, vmem_limit_bytes=VMEM_LIMIT, flags=flags)


_C_ROT = (0, 896)
_C_V = (896, 1280)
_C_G = (1280, 1408)
_C_RQ = (1408, 1920)
_C_RK = (1920, 2432)
_C_RV = (2432, 2944)
_C_RG = (2944, 3456)
PROJ_W = 3456


def _proj_kernel(x_ref, g_ref, w_ref, pos_ref, freq_ref,
                 q_ref, xh_ref, ks_ref, kw_ref, vs_ref, vw_ref, gt_ref, rq_ref, rk_ref, rv_ref, rg_ref,
                 kv_ref, *, tm):
    h = _rms(x_ref[...], g_ref[...]).astype(BF16)
    lane = lax.broadcasted_iota(jnp.int32, (tm, LANES), 1)
    lo = lane < DH

    ang = pos_ref[...] * freq_ref[...]
    c, s = jnp.cos(ang), jnp.sin(ang)
    half_n = ROPE_DIM // 2
    cr = jnp.where(lo, c, pltpu.roll(c, DH, 1))
    sr = jnp.where(lo, -s, pltpu.roll(s, DH, 1))
    dim = lane & (DH - 1)
    first, second = dim < half_n, (dim >= half_n) & (dim < ROPE_DIM)
    c1 = jnp.where(lo, pltpu.roll(c, DH, 1), c)
    s1 = jnp.where(lo, pltpu.roll(s, DH, 1), s)
    c2 = jnp.where(lo, pltpu.roll(c, DH + half_n, 1), pltpu.roll(c, half_n, 1))
    s2 = jnp.where(lo, pltpu.roll(s, DH + half_n, 1), pltpu.roll(s, half_n, 1))
    cn = jnp.where(first, c1, jnp.where(second, c2, 1.0))
    sa = jnp.where(first, -s1, 0.0)
    sb = jnp.where(second, s2, 0.0)

    def dot(c):
        return jnp.dot(h, w_ref[:, c[0]:c[1]], preferred_element_type=F32)

    def tile(y, i):
        return y[:, i * LANES:(i + 1) * LANES]

    def rope_n(t):
        return t * cn + pltpu.roll(t, LANES - half_n, 1) * sa + pltpu.roll(t, half_n, 1) * sb

    def rope_r(t):
        return t * cr + pltpu.roll(t, RET_D // 2, 1) * sr

    def split(t):
        return jnp.where(lo, t, 0.0), jnp.where(lo, pltpu.roll(t, DH, 1), 0.0)

    def v_rows(dst_ref, t):
        tt = t.T
        tail = jnp.where(lax.broadcasted_iota(jnp.int32, (VROWS - DH, tm), 0) == 0, 1.0, 0.0).astype(BF16)
        for g in range(NSA_GROUPS):
            dst_ref[g, 0:DH, :] = tt[g * DH:(g + 1) * DH, :].astype(BF16)
            dst_ref[g, DH:VROWS, :] = tail

    yr = dot(_C_ROT)
    for i in range(4):
        tt = (rope_n(tile(yr, i)) * (DH ** -0.5 * LOG2E)).T
        q_ref[2 * i] = tt[0:DH, :].astype(BF16)
        q_ref[2 * i + 1] = tt[DH:2 * DH, :].astype(BF16)
    row = pl.program_id(0) * tm + lax.broadcasted_iota(jnp.int32, (tm, LANES), 0)
    blk = (row >> 6) & (BLK_PER_TILE - 1)
    onehot = jnp.where(((lane - DH) == blk) | (lane == REF_LANE), 1.0, 0.0)
    a, b = split(rope_n(tile(yr, 5)))
    ks_ref[0] = jnp.where(lo, a, onehot).astype(BF16)
    ks_ref[1] = jnp.where(lo, b, onehot).astype(BF16)
    a, b = split(rope_n(tile(yr, 6)))
    kw_ref[0] = a.astype(BF16)
    kw_ref[1] = b.astype(BF16)

    yv = dot(_C_V)
    v_rows(vs_ref, tile(yv, 1))
    v_rows(vw_ref, tile(yv, 2))
    gt = _sigmoid(dot(_C_G)).T
    gt_ref[0] = gt[0:16, :]
    gt_ref[1] = gt[16:32, :]

    kv_ref[0] = rope_n(tile(yr, 4))
    kv_ref[1] = tile(yv, 0)
    lo_r = lax.broadcasted_iota(jnp.int32, (tm // CMP_STRIDE, LANES), 1) < DH
    for typ in range(2):
        for u in range(CMP_STRIDE // 2):
            pa = kv_ref[typ, pl.ds(2 * u, tm // CMP_STRIDE, stride=CMP_STRIDE), :]
            pb = kv_ref[typ, pl.ds(2 * u + 1, tm // CMP_STRIDE, stride=CMP_STRIDE), :]
            cols = slice(u * LANES, (u + 1) * LANES)
            xh_ref[2 * typ, :, cols] = jnp.where(lo_r, pa, pltpu.roll(pb, DH, 1)).astype(BF16)
            xh_ref[2 * typ + 1, :, cols] = jnp.where(lo_r, pltpu.roll(pa, DH, 1), pb).astype(BF16)

    yq = dot(_C_RQ)
    yk = dot(_C_RK)
    for i in range(RET_HEADS):
        sl = slice(i * LANES, (i + 1) * LANES)
        rq_ref[:, sl] = rope_r(tile(yq, i)).astype(BF16)
        rk_ref[:, sl] = (rope_r(tile(yk, i)) * (RET_D ** -0.5)).astype(BF16)
    rv_ref[...] = dot(_C_RV).astype(BF16)
    rg_ref[...] = dot(_C_RG).astype(BF16)


def _proj(x, g, w, pos, freq, tm=512):
    S = x.shape[0]
    row = lambda i: (i, 0)
    const = lambda i: (0, 0)
    mid = lambda i: (0, i, 0)
    last = lambda i: (0, 0, i)
    W = RET_HEADS * RET_D
    out_shape = (
        jax.ShapeDtypeStruct((NSA_HEADS, DH, S), BF16),
        jax.ShapeDtypeStruct((2 * NSA_GROUPS, S // CMP_STRIDE, CMP_STRIDE * DH), BF16),
        jax.ShapeDtypeStruct((NSA_GROUPS, S, LANES), BF16),
        jax.ShapeDtypeStruct((NSA_GROUPS, S, LANES), BF16),
        jax.ShapeDtypeStruct((NSA_GROUPS, VROWS, S), BF16),
        jax.ShapeDtypeStruct((NSA_GROUPS, VROWS, S), BF16),
        jax.ShapeDtypeStruct((NSA_GROUPS, 16, S), F32),
        jax.ShapeDtypeStruct((S, W), BF16),
        jax.ShapeDtypeStruct((S, W), BF16),
        jax.ShapeDtypeStruct((S, W), BF16),
        jax.ShapeDtypeStruct((S, W), BF16),
    )
    out_specs = (
        pl.BlockSpec((NSA_HEADS, DH, tm), last),
        pl.BlockSpec((2 * NSA_GROUPS, tm // CMP_STRIDE, CMP_STRIDE * DH), mid),
        pl.BlockSpec((NSA_GROUPS, tm, LANES), mid),
        pl.BlockSpec((NSA_GROUPS, tm, LANES), mid),
        pl.BlockSpec((NSA_GROUPS, VROWS, tm), last),
        pl.BlockSpec((NSA_GROUPS, VROWS, tm), last),
        pl.BlockSpec((NSA_GROUPS, 16, tm), last),
        pl.BlockSpec((tm, W), row),
        pl.BlockSpec((tm, W), row),
        pl.BlockSpec((tm, W), row),
        pl.BlockSpec((tm, W), row),
    )
    return pl.pallas_call(
        functools.partial(_proj_kernel, tm=tm),
        grid=(S // tm,),
        in_specs=[pl.BlockSpec((tm, D_MODEL), row), pl.BlockSpec((1, D_MODEL), const),
                  pl.BlockSpec((D_MODEL, PROJ_W), const), pl.BlockSpec((tm, LANES), row),
                  pl.BlockSpec((1, LANES), const)],
        out_specs=out_specs, out_shape=out_shape,
        scratch_shapes=[pltpu.VMEM((2, tm, LANES), F32)],
        compiler_params=_params(("arbitrary",)), name="proj",
    )(x, g, w, pos, freq)


def _compress_kernel(x_ref, w1_ref, pe_ref, w2_ref, o_ref):
    x = x_ref[...]
    half = CMP_STRIDE * DH
    a = jnp.dot(x, w1_ref[0:half, :], preferred_element_type=F32)
    b = jnp.dot(x, w1_ref[half:2 * half, :], preferred_element_type=F32)
    peb = jnp.dot(pe_ref[...], w1_ref[...], preferred_element_type=F32)[0:1, :]
    hid = a + pltpu.roll(b, x.shape[0] - 1, 0) + peb
    hid = hid * _sigmoid(hid)
    o_ref[...] = jnp.dot(hid.astype(BF16), w2_ref[...], preferred_element_type=F32)


def _compress(xh, w1, pe, w2):
    n_half = xh.shape[1]
    return pl.pallas_call(
        _compress_kernel,
        grid=(2 * NSA_GROUPS,),
        in_specs=[pl.BlockSpec((None, n_half, CMP_STRIDE * DH), lambda i: (i, 0, 0)),
                  pl.BlockSpec((None, CMP_BLOCK * DH, CMP_HIDDEN), lambda i: (i // NSA_GROUPS, 0, 0)),
                  pl.BlockSpec((None, 16, CMP_BLOCK * DH), lambda i: (i // NSA_GROUPS, 0, 0)),
                  pl.BlockSpec((None, CMP_HIDDEN, LANES), lambda i: (i // NSA_GROUPS, 0, 0))],
        out_specs=pl.BlockSpec((None, n_half, LANES), lambda i: (i, 0, 0)),
        out_shape=jax.ShapeDtypeStruct((2 * NSA_GROUPS, n_half, LANES), F32),
        compiler_params=_params(("arbitrary",)), name="compress",
    )(xh, w1, pe, w2)


def _nsa_kernel(q_ref, qn_ref, kcmp_ref, vcmp_ref, ks_ref, vs_ref, kw_ref, vw_ref, gt_ref, ovl_ref,
                out_ref, qa_ref, qs_ref, bias_ref, biasn_ref, ocn_ref, m_ref, acc_ref,
                sa_ref, p_ref, c_ref, rise_ref, owin_ref, *, S):
    NC = S // CMP_STRIDE
    NB = S // SLC_BLOCK
    NQ = HPG * QB
    b = pl.program_id(1)
    s0 = b * QB
    lane_q = lax.broadcasted_iota(jnp.int32, (1, NQ), 1) & (QB - 1)
    band_row = lax.broadcasted_iota(jnp.int32, (QB, NQ), 0)
    band_tq = lax.broadcasted_iota(jnp.int32, (QB, NQ), 1) & (QB - 1)

    def sel_scores(qsrc_ref, cr):
        qs_ref[0:DH, :] = jnp.concatenate([qsrc_ref[h] for h in range(HPG)], axis=1)
        qs_ref[DH:LANES, :] = jnp.zeros((LANES - DH, NQ), BF16)
        return jnp.dot(kcmp_ref[0:cr, :], qs_ref[...], preferred_element_type=F32)

    def sel_probs(sc, blk):
        c_end = lax.broadcasted_iota(jnp.int32, sc.shape, 0) * CMP_STRIDE + (CMP_BLOCK - 1)
        sc = jnp.where(c_end <= blk * QB + lane_q, sc, NEG)
        return jnp.exp2(sc - jnp.max(sc, axis=0, keepdims=True)).astype(BF16)

    def sel_matmuls(pc):
        cr = pc.shape[0]
        ocl = jnp.dot(vcmp_ref[:, 0:cr], pc, preferred_element_type=F32)
        ir, ic = IMP_ROWS, 4 * IMP_ROWS
        if cr % ic:
            return ocl, jnp.dot(ovl_ref[0:cr // 4, 0:cr], pc, preferred_element_type=F32)
        parts = []
        for i in range(cr // ic):
            c0 = max(i - 1, 0) * ic
            parts.append(jnp.dot(ovl_ref[i * ir:(i + 1) * ir, c0:(i + 1) * ic], pc[c0:(i + 1) * ic, :],
                                 preferred_element_type=F32))
        return ocl, jnp.concatenate(parts, axis=0)

    def sel_choose(ocl, imp4, blk):
        t_row = blk * QB + lane_q
        any_valid = jnp.where(t_row >= CMP_BLOCK - 1, 1.0, 0.0)
        rc = any_valid / ocl[DH:DH + 1, :]
        ocn_ref[...] = ocl[0:DH, :] * rc

        imp4 = imp4 * rc
        imp = imp4[:, 0:QB] + imp4[:, QB:2 * QB] + imp4[:, 2 * QB:3 * QB] + imp4[:, 3 * QB:4 * QB]
        if imp.shape[0] < NB:
            imp = jnp.concatenate([imp, jnp.zeros((NB - imp.shape[0], QB), F32)], axis=0)

        jidx = lax.broadcasted_iota(jnp.int32, (NB, QB), 0)
        cur = (blk * QB + lax.broadcasted_iota(jnp.int32, (1, QB), 1)) >> 6
        forced = (jidx == 0) | (jidx == cur) | (jidx == cur - 1)
        future = jidx > cur
        cand = jnp.where(forced | future, -jnp.inf, imp)

        def select(break_ties):
            v = cand
            for _ in range(SLC_COUNT - 3):
                mx = jnp.max(v, axis=0, keepdims=True)
                hit = v == mx
                if break_ties:
                    hit = jidx == jnp.min(jnp.where(hit, jidx, NB), axis=0, keepdims=True)
                v = jnp.where(hit, -jnp.inf, v)
            return jnp.where((v == -jnp.inf) & (jidx <= cur), 0.0, NEG)

        bias_t = select(False)
        n_sel = jnp.sum(jnp.where(bias_t == 0.0, 1.0, 0.0), axis=0, keepdims=True)
        biasn_ref[...] = bias_t.astype(BF16)

        def redo_if_tied():
            @pl.when(jnp.max(n_sel) > SLC_COUNT)
            def _():
                biasn_ref[...] = select(True).astype(BF16)
        return redo_if_tied

    cr_need = jnp.minimum((b + 2) * (QB // CMP_STRIDE) - 1, NC)
    cr_sizes = (NC // 2, NC) if (NC // 2) % (4 * IMP_ROWS) == 0 else (NC,)

    @pl.when(b == 0)
    def _():
        ocl, imp4 = sel_matmuls(sel_probs(sel_scores(q_ref, cr_sizes[0]), 0))
        sel_choose(ocl, imp4, 0)()

    bias_ref[...] = biasn_ref[...]
    oc = ocn_ref[...]

    qa_ref[0:DH, :] = jnp.concatenate([q_ref[h] for h in range(HPG)], axis=1)
    qa_ref[DH:KILL_LANE, :] = jnp.zeros((KILL_LANE - DH, NQ), BF16)

    def set_bias_rows(grp):
        bias_rows = bias_ref[pl.ds(pl.multiple_of(grp * BLK_PER_TILE, BLK_PER_TILE), BLK_PER_TILE), :]
        qa_ref[DH:DH + BLK_PER_TILE, :] = jnp.concatenate([bias_rows] * HPG, axis=1)

    def set_flag_rows(ref):
        r = lax.broadcasted_iota(jnp.int32, (LANES - KILL_LANE, NQ), 0)
        rows = jnp.where(r == 0, NEG, 0.0) if ref is None else jnp.where(r == 0, NEG, jnp.where(r == 1, -ref, 0.0))
        qa_ref[KILL_LANE:LANES, :] = rows.astype(BF16)

    def scores(kt, s_ref):
        k0 = pl.multiple_of(kt * TK, TK)
        set_bias_rows(kt // 2)
        s_ref[...] = jnp.dot(ks_ref[pl.ds(k0, TK), :], qa_ref[...], preferred_element_type=F32)

    def accumulate(kt, s_ref):
        k0 = pl.multiple_of(kt * TK, TK)
        m_old = m_ref[...]
        m_new = jnp.maximum(m_old, jnp.max(s_ref[...], axis=0, keepdims=True))
        p_ref[0:TK, :] = jnp.exp2(s_ref[...] - m_new).astype(BF16)
        acc_ref[...] = (jnp.exp2(m_old - m_new) * acc_ref[...]
                        + jnp.dot(vs_ref[:, pl.ds(k0, TK)], p_ref[0:TK, :], preferred_element_type=F32))
        m_ref[...] = m_new

    def mask_own_keys(s_ref):
        band = pl.ds(pl.multiple_of(s0 - n_full * TK, QB), QB)
        s_ref[band, :] = jnp.where(band_row <= band_tq, s_ref[band, :], NEG)

    n_full = s0 // TK

    m_ref[...] = jnp.full((1, NQ), NEG, F32)
    acc_ref[...] = jnp.zeros((VROWS, NQ), F32)
    set_flag_rows(None)

    def first_region(cr):
        sc = sel_scores(qn_ref, cr)
        sw = jnp.dot(kw_ref[pl.ds(pl.multiple_of(s0, QB), WIN_KEYS), :], qa_ref[...],
                     preferred_element_type=F32)
        scores(n_full, sa_ref)

        pc = sel_probs(sc, b + 1)
        ocl, imp4 = sel_matmuls(pc)
        redo_if_tied = sel_choose(ocl, imp4, b + 1)

        sw = jnp.concatenate([jnp.where(band_row > band_tq, sw[0:QB], NEG),
                              sw[QB:WINDOW],
                              jnp.where(band_row <= band_tq, sw[WINDOW:WIN_KEYS], NEG)], axis=0)
        pw = jnp.exp2(sw - jnp.max(sw, axis=0, keepdims=True)).astype(BF16)
        mask_own_keys(sa_ref)

        ow = jnp.dot(vw_ref[:, pl.ds(pl.multiple_of(s0, QB), WIN_KEYS)], pw, preferred_element_type=F32)
        owin_ref[...] = ow[0:DH, :] * (1.0 / ow[DH:DH + 1, :])
        accumulate(n_full, sa_ref)

        m = m_ref[...]
        c0 = m.astype(BF16).astype(F32)
        c_ref[...] = c0
        acc_ref[...] = acc_ref[...] * jnp.exp2(m - c0)
        rise_ref[...] = jnp.zeros((1, NQ), F32)
        redo_if_tied()

    for i, cr in enumerate(cr_sizes):
        fits = cr_need <= cr
        if i > 0:
            fits = fits & (cr_need > cr_sizes[i - 1])
        pl.when(fits)(functools.partial(first_region, cr))

    def older_keys(grp, k0, rows):
        c = c_ref[...]
        set_bias_rows(grp)
        set_flag_rows(c)
        s = jnp.dot(ks_ref[pl.ds(k0, rows), :], qa_ref[...], preferred_element_type=F32)
        mt = jnp.max(s, axis=0, keepdims=True)
        p_ref[0:rows, :] = jnp.exp2(s).astype(BF16)
        pv = jnp.dot(vs_ref[:, pl.ds(k0, rows)], p_ref[0:rows, :], preferred_element_type=F32)
        c_new = (c + jnp.maximum(mt, 0.0)).astype(BF16).astype(F32)
        acc_ref[...] = (acc_ref[...] + pv) * jnp.exp2(c - c_new)
        c_ref[...] = c_new
        rise_ref[...] = jnp.maximum(rise_ref[...], mt)

    def older_pair(jj, carry):
        older_keys(jj, pl.multiple_of(jj * TK_OLD, TK_OLD), TK_OLD)
        return carry

    lax.fori_loop(0, n_full // 2, older_pair, 0)

    @pl.when(n_full % 2 == 1)
    def _():
        older_keys(n_full // 2, pl.multiple_of((n_full - 1) * TK, TK), TK)

    @pl.when(jnp.max(rise_ref[...]) > REF_SLACK)
    def _():
        m_ref[...] = jnp.full((1, NQ), NEG, F32)
        acc_ref[...] = jnp.zeros((VROWS, NQ), F32)
        set_flag_rows(None)

        def two_pass_tile(kt, carry):
            scores(kt, sa_ref)

            @pl.when(kt == n_full)
            def _():
                mask_own_keys(sa_ref)
            accumulate(kt, sa_ref)
            return carry

        lax.fori_loop(0, n_full + 1, two_pass_tile, 0)

    acc = acc_ref[...]
    osel = acc[0:DH, :] * (1.0 / acc[DH:DH + 1, :])
    owin = owin_ref[...]

    heads = []
    for h in range(HPG):
        sl = slice(h * QB, (h + 1) * QB)
        heads.append(gt_ref[3 * h:3 * h + 1, :] * oc[:, sl]
                     + gt_ref[3 * h + 1:3 * h + 2, :] * osel[:, sl]
                     + gt_ref[3 * h + 2:3 * h + 3, :] * owin[:, sl])
    out_ref[...] = jnp.concatenate(heads, axis=0).T.astype(out_ref.dtype)


def _nsa(q_t, kcmp, vcmp_t, ks, vs_t, kw, vw_t, gt, ovl_t):
    S = ks.shape[1]
    NC, NB = S // CMP_STRIDE, S // SLC_BLOCK
    n_qb = S // QB
    grp = lambda g, b: (g, 0, 0)
    return pl.pallas_call(
        functools.partial(_nsa_kernel, S=S),
        grid=(NSA_GROUPS, n_qb),
        in_specs=[pl.BlockSpec((HPG, DH, QB), lambda g, b: (g, 0, b)),
                  pl.BlockSpec((HPG, DH, QB), lambda g, b: (g, 0, jnp.minimum(b + 1, n_qb - 1))),
                  pl.BlockSpec((None, NC, LANES), grp),
                  pl.BlockSpec((None, VROWS, NC), grp),
                  pl.BlockSpec((None, S, LANES), grp, pipeline_mode=pl.Buffered(1)),
                  pl.BlockSpec((None, VROWS, S), grp, pipeline_mode=pl.Buffered(1)),
                  pl.BlockSpec((None, WINDOW + S, LANES), grp, pipeline_mode=pl.Buffered(1)),
                  pl.BlockSpec((None, VROWS, WINDOW + S), grp, pipeline_mode=pl.Buffered(1)),
                  pl.BlockSpec((None, 16, QB), lambda g, b: (g, 0, b)),
                  pl.BlockSpec((NB, NC), lambda g, b: (0, 0))],
        out_specs=pl.BlockSpec((QB, HPG * DH), lambda g, b: (b, g)),
        out_shape=jax.ShapeDtypeStruct((S, NSA_HEADS * DH), BF16),
        scratch_shapes=[pltpu.VMEM((LANES, HPG * QB), BF16),
                        pltpu.VMEM((LANES, HPG * QB), BF16),
                        pltpu.VMEM((NB, QB), BF16),
                        pltpu.VMEM((NB, QB), BF16),
                        pltpu.VMEM((DH, HPG * QB), F32),
                        pltpu.VMEM((1, HPG * QB), F32),
                        pltpu.VMEM((VROWS, HPG * QB), F32),
                        pltpu.VMEM((TK, HPG * QB), F32),
                        pltpu.VMEM((TK_OLD, HPG * QB), BF16),
                        pltpu.VMEM((1, HPG * QB), F32),
                        pltpu.VMEM((1, HPG * QB), F32),
                        pltpu.VMEM((DH, HPG * QB), F32)],
        compiler_params=_params(("arbitrary", "arbitrary")), name="nsa",
    )(q_t, q_t, kcmp, vcmp_t, ks, vs_t, kw, vw_t, gt, ovl_t)


def _ret_kernel(q_ref, k_ref, v_ref, g_ref, dmat_ref, xi_ref, zeta_ref, dec_ref, o_ref, state_ref, *, tm):
    @pl.when(pl.program_id(0) == 0)
    def _():
        state_ref[...] = jnp.zeros_like(state_ref)

    for j in range(tm // RET_C):
        rows = slice(j * RET_C, (j + 1) * RET_C)
        for h in range(RET_HEADS):
            cols = slice(h * RET_D, (h + 1) * RET_D)
            qc, kc, vc = q_ref[rows, cols], k_ref[rows, cols], v_ref[rows, cols]
            st = state_ref[h]
            st_hi = st.astype(BF16)
            st_lo = (st - st_hi.astype(F32)).astype(BF16)
            sc = lax.dot_general(qc, kc, NT_DIMS, preferred_element_type=F32) * dmat_ref[h]
            inner = jnp.dot(sc.astype(BF16), vc, preferred_element_type=F32)
            cross = (jnp.dot(qc, st_hi, preferred_element_type=F32)
                     + jnp.dot(qc, st_lo, preferred_element_type=F32)) * xi_ref[h]
            kz = (kc.astype(F32) * zeta_ref[h]).astype(BF16)
            state_ref[h] = dec_ref[h] * st + lax.dot_general(kz, vc, TN_DIMS, preferred_element_type=F32)
            o = inner + cross
            o = o * lax.rsqrt(jnp.mean(o * o, axis=-1, keepdims=True) + EPS)
            gate = g_ref[rows, cols].astype(F32)
            o_ref[rows, cols] = (gate * _sigmoid(gate) * o).astype(o_ref.dtype)


def _retention(rq, rk, rv, rg, dmat, xi, zeta, dec, tm=512):
    S = rq.shape[0]
    W = RET_HEADS * RET_D
    row = lambda i: (i, 0)
    cst = lambda i: (0, 0, 0)
    blk = pl.BlockSpec((tm, W), row)
    tbl = pl.BlockSpec((RET_HEADS, RET_C, RET_D), cst)
    return pl.pallas_call(
        functools.partial(_ret_kernel, tm=tm),
        grid=(S // tm,),
        in_specs=[blk, blk, blk, blk, tbl, tbl, tbl, tbl],
        out_specs=blk,
        out_shape=jax.ShapeDtypeStruct((S, W), BF16),
        scratch_shapes=[pltpu.VMEM((RET_HEADS, RET_D, RET_D), F32)],
        compiler_params=_params(("arbitrary",)), name="retention",
    )(rq, rk, rv, rg, dmat, xi, zeta, dec)


def _mix_kernel(x_ref, g_ref, a_ref, r_ref, wg_ref, wpa_ref, wpb_ref, wo_ref, o_ref):
    x = x_ref[...]
    h = _rms(x, g_ref[...]).astype(BF16)
    ga = _sigmoid(jnp.dot(h, wg_ref[:, 0:D_MODEL], preferred_element_type=F32))
    gb = _sigmoid(jnp.dot(h, wg_ref[:, D_MODEL:2 * D_MODEL], preferred_element_type=F32))
    mix = (ga * jnp.dot(a_ref[...], wpa_ref[...], preferred_element_type=F32)
           + gb * jnp.dot(r_ref[...], wpb_ref[...], preferred_element_type=F32))
    o_ref[...] = x + jnp.dot(mix.astype(BF16), wo_ref[...], preferred_element_type=F32)


def _mix(x, g, nsa_out, ret_out, wg, wpa, wpb, wo, tm=512):
    S = x.shape[0]
    row = lambda i: (i, 0)
    cst = lambda i: (0, 0)
    return pl.pallas_call(
        _mix_kernel,
        grid=(S // tm,),
        in_specs=[pl.BlockSpec((tm, D_MODEL), row), pl.BlockSpec((1, D_MODEL), cst),
                  pl.BlockSpec((tm, 512), row), pl.BlockSpec((tm, 512), row),
                  pl.BlockSpec((D_MODEL, 2 * D_MODEL), cst), pl.BlockSpec((512, D_MODEL), cst),
                  pl.BlockSpec((512, D_MODEL), cst), pl.BlockSpec((D_MODEL, D_MODEL), cst)],
        out_specs=pl.BlockSpec((tm, D_MODEL), row),
        out_shape=jax.ShapeDtypeStruct((S, D_MODEL), F32),
        compiler_params=_params(("arbitrary",)), name="mix",
    )(x, g, nsa_out, ret_out, wg, wpa, wpb, wo)


def _mlp_kernel(x_ref, g_ref, wu_ref, wd_ref, gf_ref, o_ref, *, hc):
    x = x_ref[...]
    h = _rms(x, g_ref[...]).astype(BF16)
    acc = x
    for c in range(MLP_HIDDEN // hc):
        u = jnp.maximum(jnp.dot(h, wu_ref[:, c * hc:(c + 1) * hc], preferred_element_type=F32), 0.0)
        acc = acc + jnp.dot((u * u).astype(BF16), wd_ref[c * hc:(c + 1) * hc, :], preferred_element_type=F32)
    o_ref[...] = _rms(acc, gf_ref[...])


def _mlp(x, g, wu, wd, gf, tm=512, hc=1024):
    S = x.shape[0]
    row = lambda i: (i, 0)
    cst = lambda i: (0, 0)
    return pl.pallas_call(
        functools.partial(_mlp_kernel, hc=hc),
        grid=(S // tm,),
        in_specs=[pl.BlockSpec((tm, D_MODEL), row), pl.BlockSpec((1, D_MODEL), cst),
                  pl.BlockSpec((D_MODEL, MLP_HIDDEN), cst), pl.BlockSpec((MLP_HIDDEN, D_MODEL), cst),
                  pl.BlockSpec((1, D_MODEL), cst)],
        out_specs=pl.BlockSpec((tm, D_MODEL), row),
        out_shape=jax.ShapeDtypeStruct((S, D_MODEL), F32),
        compiler_params=_params(("arbitrary",)), name="mlp",
    )(x, g, wu, wd, gf)


def _rope_freqs():
    half = ROPE_DIM // 2
    inv = jnp.exp(-math.log(ROPE_THETA) * jnp.arange(half, dtype=F32) * 2.0 / ROPE_DIM)
    hr = RET_D // 2
    inv_r = jnp.exp(-math.log(RET_BASE) * jnp.arange(hr, dtype=F32) * 2.0 / RET_D)
    return jnp.concatenate([inv_r, inv, jnp.zeros((LANES - hr - half,), F32)])[None, :]


def _decay_tables():
    H, C = RET_HEADS, RET_C
    log_g = jnp.log(1.0 - jnp.exp2(-5.0 - jnp.arange(H, dtype=F32)))
    n = jnp.arange(C, dtype=F32)
    rel = n[:, None] - n[None, :]
    dmat = jnp.where(rel >= 0, jnp.exp(log_g[:, None, None] * jnp.maximum(rel, 0.0)), 0.0)
    xi = jnp.exp(log_g[:, None] * (n + 1.0))
    zeta = jnp.exp(log_g[:, None] * (C - 1.0 - n))
    dec = jnp.exp(log_g * C)
    bc = lambda a: jnp.broadcast_to(a, (H, C, RET_D))
    return dmat, bc(xi[:, :, None]), bc(zeta[:, :, None]), bc(dec[:, None, None])


def _overlap_t(S):
    n_cmp = (S - CMP_BLOCK) // CMP_STRIDE + 1
    n_slc = S // SLC_BLOCK
    cs = np.arange(n_cmp)[:, None] * CMP_STRIDE
    ss = np.arange(n_slc)[None, :] * SLC_BLOCK
    ov = np.clip(np.minimum(cs + CMP_BLOCK, ss + SLC_BLOCK) - np.maximum(cs, ss), 0, None) / CMP_BLOCK
    out = np.zeros((n_slc, S // CMP_STRIDE), np.float32)
    out[:, :n_cmp] = ov.T
    return jnp.asarray(out, BF16)


def _v_rows(v_t):
    S = v_t.shape[1]
    return jnp.concatenate([v_t, jnp.ones((1, S), v_t.dtype), jnp.zeros((VROWS - DH - 1, S), v_t.dtype)], axis=0)


_IN_OFFSETS = np.cumsum((0, 512, 128, 128, 128, 128, 128, 128, 24, 512, 512, 512, 512, 1024, 1024))


def _in_col(wi, i):
    return wi[:, _IN_OFFSETS[i]:_IN_OFFSETS[i + 1]]


def _branches(x, positions, norm_mix, w_in, cmp_pos_k, cmp_pos_v, cmp_k_w1, cmp_k_w2, cmp_v_w1, cmp_v_w2):
    B, S, _ = x.shape
    assert B == 1 and norm_mix.shape[0] == 1 and S % TK == 0 and S >= max(WIN_KEYS, SLC_COUNT * SLC_BLOCK)
    xs = x[0]
    col = functools.partial(_in_col, w_in[0])
    n_gate = 3 * HPG
    z = lambda n: jnp.zeros((D_MODEL, n), F32)
    w_a = jnp.concatenate([col(0), col(1), col(3), col(5), col(2), col(4), col(6),
                           col(7)[:, 0:n_gate], z(16 - n_gate), col(7)[:, n_gate:], z(LANES - 16 - n_gate),
                           col(8), col(9), col(10), col(11)], axis=1).astype(BF16)

    q_t, xh, ks, kw, vs_t, vw_t, gt, rq, rk, rv, rg = _proj(
        xs, norm_mix, w_a, jnp.broadcast_to(positions[0].astype(F32)[:, None], (S, LANES)), _rope_freqs())

    w1 = jnp.stack([cmp_k_w1[0], cmp_v_w1[0]]).astype(BF16)
    pe = jnp.stack([cmp_pos_k[0], cmp_pos_v[0]]).reshape(2, 1, CMP_BLOCK * DH)
    pe = jnp.broadcast_to(pe, (2, 16, CMP_BLOCK * DH)).astype(BF16)
    w2 = jnp.pad(jnp.stack([cmp_k_w2[0], cmp_v_w2[0]]), ((0, 0), (0, 0), (0, LANES - DH))).astype(BF16)
    cmp = _compress(xh, w1, pe, w2)
    kcmp = cmp[0:NSA_GROUPS].astype(BF16)
    vcmp_t = jax.vmap(_v_rows)(cmp[NSA_GROUPS:, :, 0:DH].transpose(0, 2, 1).astype(BF16))

    vw_t = jnp.pad(vw_t, ((0, 0), (0, 0), (WINDOW, 0)))
    pad_keys = jnp.zeros((NSA_GROUPS, WINDOW, LANES), BF16).at[:, :, KILL_LANE].set(1.0)
    kw = jnp.concatenate([pad_keys, kw], axis=1)
    nsa_out = _nsa(q_t, kcmp, vcmp_t, ks, vs_t, kw, vw_t, gt, _overlap_t(S))

    ret_out = _retention(rq, rk, rv, rg, *_decay_tables())
    return nsa_out, ret_out


def kernel(x, positions, norm_mix, w_in, cmp_pos_k, cmp_pos_v, cmp_k_w1, cmp_k_w2, cmp_v_w1, cmp_v_w2,
           w_proj_a, w_proj_b, w_out, norm_mlp, w_up, w_down, norm_final):
    nsa_out, ret_out = _branches(x, positions, norm_mix, w_in, cmp_pos_k, cmp_pos_v,
                                 cmp_k_w1, cmp_k_w2, cmp_v_w1, cmp_v_w2)
    xs = x[0]
    w_gate = jnp.concatenate([_in_col(w_in[0], 12), _in_col(w_in[0], 13)], axis=1).astype(BF16)
    x1 = _mix(xs, norm_mix, nsa_out, ret_out, w_gate, w_proj_a[0].astype(BF16), w_proj_b[0].astype(BF16),
              w_out[0].astype(BF16))
    y = _mlp(x1, norm_mlp, w_up[0].astype(BF16), w_down[0].astype(BF16), norm_final[None, :])
    return y[None]
```

```python
import functools
import math

import numpy as np
import jax
import jax.numpy as jnp
from jax import lax
from jax.experimental import pallas as pl
from jax.experimental.pallas import tpu as pltpu

F32 = jnp.float32
BF16 = jnp.bfloat16

D_MODEL = 1024
NSA_HEADS = 8
NSA_GROUPS = 2
HPG = NSA_HEADS // NSA_GROUPS
DH = 64
CMP_BLOCK = 32
CMP_STRIDE = 16
CMP_HIDDEN = 256
SLC_BLOCK = 64
SLC_COUNT = 16
WINDOW = 512
QB = 256
ROPE_THETA = 500000.0
ROPE_DIM = DH // 4
RET_HEADS = 4
RET_D = 128
RET_C = 128
RET_BASE = 10000.0
MLP_HIDDEN = 4 * D_MODEL
EPS = 1e-6
NEG = -1e30

LANES = 128
TK = 1024
TK_OLD = 2 * TK
BLK_PER_TILE = TK_OLD // SLC_BLOCK
IMP_ROWS = 64
KILL_LANE = DH + BLK_PER_TILE
REF_LANE = KILL_LANE + 1
REF_SLACK = 64.0
LOG2E = math.log2(math.e)
VROWS = 80
BF16_ROWS = 16
GATE_ROWS = 16
SLC_SHIFT = SLC_BLOCK.bit_length() - 1
WIN_KEYS = WINDOW + QB
VMEM_LIMIT = 56 * 1024 * 1024

NT_DIMS = (((1,), (1,)), ((), ()))
TN_DIMS = (((0,), (0,)), ((), ()))


def _sigmoid(v):
    return 1.0 / (1.0 + jnp.exp(-v))


def _rms(x, g):
    return x * lax.rsqrt(jnp.mean(x * x, axis=-1, keepdims=True) + EPS) * g


def _params(sem, flags=None):
    return pltpu.CompilerParams(dimension_semantics=sem, vmem_limit_bytes=VMEM_LIMIT, flags=flags)


_C_ROT = (0, 896)
_C_V = (896, 1280)
_C_G = (1280, 1408)
_C_RQ = (1408, 1920)
_C_RK = (1920, 2432)
_C_RV = (2432, 2944)
_C_RG = (2944, 3456)
PROJ_W = 3456


def _proj_kernel(x_ref, g_ref, w_ref, pos_ref, freq_ref,
                 q_ref, xh_ref, ks_ref, kw_ref, vs_ref, vw_ref, gt_ref, rq_ref, rk_ref, rv_ref, rg_ref,
                 kv_ref, *, tm):
    h = _rms(x_ref[...], g_ref[...]).astype(BF16)
    lane = lax.broadcasted_iota(jnp.int32, (tm, LANES), 1)
    lo = lane < DH

    ang = pos_ref[...] * freq_ref[...]
    c, s = jnp.cos(ang), jnp.sin(ang)
    half_n = ROPE_DIM // 2
    cr = jnp.where(lo, c, pltpu.roll(c, DH, 1))
    sr = jnp.where(lo, -s, pltpu.roll(s, DH, 1))
    dim = lane & (DH - 1)
    first, second = dim < half_n, (dim >= half_n) & (dim < ROPE_DIM)
    c1 = jnp.where(lo, pltpu.roll(c, DH, 1), c)
    s1 = jnp.where(lo, pltpu.roll(s, DH, 1), s)
    c2 = jnp.where(lo, pltpu.roll(c, DH + half_n, 1), pltpu.roll(c, half_n, 1))
    s2 = jnp.where(lo, pltpu.roll(s, DH + half_n, 1), pltpu.roll(s, half_n, 1))
    cn = jnp.where(first, c1, jnp.where(second, c2, 1.0))
    sa = jnp.where(first, -s1, 0.0)
    sb = jnp.where(second, s2, 0.0)

    def dot(c):
        return jnp.dot(h, w_ref[:, c[0]:c[1]], preferred_element_type=F32)

    def tile(y, i):
        return y[:, i * LANES:(i + 1) * LANES]

    def rope_n(t):
        return t * cn + pltpu.roll(t, LANES - half_n, 1) * sa + pltpu.roll(t, half_n, 1) * sb

    def rope_r(t):
        return t * cr + pltpu.roll(t, RET_D // 2, 1) * sr

    def split(t):
        return jnp.where(lo, t, 0.0), jnp.where(lo, pltpu.roll(t, DH, 1), 0.0)

    def v_rows(dst_ref, t):
        tt = t.T
        tail = jnp.where(lax.broadcasted_iota(jnp.int32, (VROWS - DH, tm), 0) == 0, 1.0, 0.0).astype(BF16)
        for g in range(NSA_GROUPS):
            dst_ref[g, 0:DH, :] = tt[g * DH:(g + 1) * DH, :].astype(BF16)
            dst_ref[g, DH:VROWS, :] = tail

    yr = dot(_C_ROT)
    for i in range(4):
        tt = (rope_n(tile(yr, i)) * (DH ** -0.5 * LOG2E)).T
        q_ref[2 * i] = tt[0:DH, :].astype(BF16)
        q_ref[2 * i + 1] = tt[DH:2 * DH, :].astype(BF16)
    row = pl.program_id(0) * tm + lax.broadcasted_iota(jnp.int32, (tm, LANES), 0)
    blk = (row >> SLC_SHIFT) & (BLK_PER_TILE - 1)
    onehot = jnp.where(((lane - DH) == blk) | (lane == REF_LANE), 1.0, 0.0)
    a, b = split(rope_n(tile(yr, 5)))
    ks_ref[0] = jnp.where(lo, a, onehot).astype(BF16)
    ks_ref[1] = jnp.where(lo, b, onehot).astype(BF16)
    a, b = split(rope_n(tile(yr, 6)))
    kw_ref[0] = a.astype(BF16)
    kw_ref[1] = b.astype(BF16)

    yv = dot(_C_V)
    v_rows(vs_ref, tile(yv, 1))
    v_rows(vw_ref, tile(yv, 2))
    gt = _sigmoid(dot(_C_G)).T
    for g in range(NSA_GROUPS):
        gt_ref[g] = gt[g * GATE_ROWS:(g + 1) * GATE_ROWS, :]

    kv_ref[0] = rope_n(tile(yr, 4))
    kv_ref[1] = tile(yv, 0)
    lo_r = lax.broadcasted_iota(jnp.int32, (tm // CMP_STRIDE, LANES), 1) < DH
    for typ in range(2):
        for u in range(CMP_STRIDE // 2):
            pa = kv_ref[typ, pl.ds(2 * u, tm // CMP_STRIDE, stride=CMP_STRIDE), :]
            pb = kv_ref[typ, pl.ds(2 * u + 1, tm // CMP_STRIDE, stride=CMP_STRIDE), :]
            cols = slice(u * LANES, (u + 1) * LANES)
            xh_ref[2 * typ, :, cols] = jnp.where(lo_r, pa, pltpu.roll(pb, DH, 1)).astype(BF16)
            xh_ref[2 * typ + 1, :, cols] = jnp.where(lo_r, pltpu.roll(pa, DH, 1), pb).astype(BF16)

    yq = dot(_C_RQ)
    yk = dot(_C_RK)
    for i in range(RET_HEADS):
        sl = slice(i * LANES, (i + 1) * LANES)
        rq_ref[:, sl] = rope_r(tile(yq, i)).astype(BF16)
        rk_ref[:, sl] = (rope_r(tile(yk, i)) * (RET_D ** -0.5)).astype(BF16)
    rv_ref[...] = dot(_C_RV).astype(BF16)
    rg_ref[...] = dot(_C_RG).astype(BF16)


def _proj(x, g, w, pos, freq, tm=512):
    S = x.shape[0]
    row = lambda i: (i, 0)
    const = lambda i: (0, 0)
    mid = lambda i: (0, i, 0)
    last = lambda i: (0, 0, i)
    W = RET_HEADS * RET_D
    out_shape = (
        jax.ShapeDtypeStruct((NSA_HEADS, DH, S), BF16),
        jax.ShapeDtypeStruct((2 * NSA_GROUPS, S // CMP_STRIDE, CMP_STRIDE * DH), BF16),
        jax.ShapeDtypeStruct((NSA_GROUPS, S, LANES), BF16),
        jax.ShapeDtypeStruct((NSA_GROUPS, S, LANES), BF16),
        jax.ShapeDtypeStruct((NSA_GROUPS, VROWS, S), BF16),
        jax.ShapeDtypeStruct((NSA_GROUPS, VROWS, S), BF16),
        jax.ShapeDtypeStruct((NSA_GROUPS, GATE_ROWS, S), F32),
        jax.ShapeDtypeStruct((S, W), BF16),
        jax.ShapeDtypeStruct((S, W), BF16),
        jax.ShapeDtypeStruct((S, W), BF16),
        jax.ShapeDtypeStruct((S, W), BF16),
    )
    out_specs = (
        pl.BlockSpec((NSA_HEADS, DH, tm), last),
        pl.BlockSpec((2 * NSA_GROUPS, tm // CMP_STRIDE, CMP_STRIDE * DH), mid),
        pl.BlockSpec((NSA_GROUPS, tm, LANES), mid),
        pl.BlockSpec((NSA_GROUPS, tm, LANES), mid),
        pl.BlockSpec((NSA_GROUPS, VROWS, tm), last),
        pl.BlockSpec((NSA_GROUPS, VROWS, tm), last),
        pl.BlockSpec((NSA_GROUPS, GATE_ROWS, tm), last),
        pl.BlockSpec((tm, W), row),
        pl.BlockSpec((tm, W), row),
        pl.BlockSpec((tm, W), row),
        pl.BlockSpec((tm, W), row),
    )
    return pl.pallas_call(
        functools.partial(_proj_kernel, tm=tm),
        grid=(S // tm,),
        in_specs=[pl.BlockSpec((tm, D_MODEL), row), pl.BlockSpec((1, D_MODEL), const),
                  pl.BlockSpec((D_MODEL, PROJ_W), const), pl.BlockSpec((tm, LANES), row),
                  pl.BlockSpec((1, LANES), const)],
        out_specs=out_specs, out_shape=out_shape,
        scratch_shapes=[pltpu.VMEM((2, tm, LANES), F32)],
        compiler_params=_params(("arbitrary",)), name="proj",
    )(x, g, w, pos, freq)


def _compress_kernel(x_ref, w1_ref, pe_ref, w2_ref, o_ref):
    x = x_ref[...]
    half = CMP_STRIDE * DH
    a = jnp.dot(x, w1_ref[0:half, :], preferred_element_type=F32)
    b = jnp.dot(x, w1_ref[half:2 * half, :], preferred_element_type=F32)
    peb = jnp.dot(pe_ref[...], w1_ref[...], preferred_element_type=F32)[0:1, :]
    hid = a + pltpu.roll(b, x.shape[0] - 1, 0) + peb
    hid = hid * _sigmoid(hid)
    o_ref[...] = jnp.dot(hid.astype(BF16), w2_ref[...], preferred_element_type=F32)


def _compress(xh, w1, pe, w2):
    n_half = xh.shape[1]
    return pl.pallas_call(
        _compress_kernel,
        grid=(2 * NSA_GROUPS,),
        in_specs=[pl.BlockSpec((None, n_half, CMP_STRIDE * DH), lambda i: (i, 0, 0)),
                  pl.BlockSpec((None, CMP_BLOCK * DH, CMP_HIDDEN), lambda i: (i // NSA_GROUPS, 0, 0)),
                  pl.BlockSpec((None, BF16_ROWS, CMP_BLOCK * DH), lambda i: (i // NSA_GROUPS, 0, 0)),
                  pl.BlockSpec((None, CMP_HIDDEN, LANES), lambda i: (i // NSA_GROUPS, 0, 0))],
        out_specs=pl.BlockSpec((None, n_half, LANES), lambda i: (i, 0, 0)),
        out_shape=jax.ShapeDtypeStruct((2 * NSA_GROUPS, n_half, LANES), F32),
        compiler_params=_params(("arbitrary",)), name="compress",
    )(xh, w1, pe, w2)


def _nsa_kernel(q_ref, qn_ref, kcmp_ref, vcmp_ref, ks_ref, vs_ref, kw_ref, vw_ref, gt_ref, ovl_ref,
                out_ref, qa_ref, qs_ref, bias_ref, biasn_ref, ocn_ref, m_ref, acc_ref,
                sa_ref, p_ref, c_ref, rise_ref, owin_ref, *, S):
    NC = S // CMP_STRIDE
    NB = S // SLC_BLOCK
    NQ = HPG * QB
    b = pl.program_id(1)
    s0 = b * QB
    lane_q = lax.broadcasted_iota(jnp.int32, (1, NQ), 1) & (QB - 1)
    band_row = lax.broadcasted_iota(jnp.int32, (QB, NQ), 0)
    band_tq = lax.broadcasted_iota(jnp.int32, (QB, NQ), 1) & (QB - 1)

    def sel_scores(qsrc_ref, cr):
        qs_ref[0:DH, :] = jnp.concatenate([qsrc_ref[h] for h in range(HPG)], axis=1)
        qs_ref[DH:LANES, :] = jnp.zeros((LANES - DH, NQ), BF16)
        return jnp.dot(kcmp_ref[0:cr, :], qs_ref[...], preferred_element_type=F32)

    def sel_probs(sc, blk):
        c_end = lax.broadcasted_iota(jnp.int32, sc.shape, 0) * CMP_STRIDE + (CMP_BLOCK - 1)
        sc = jnp.where(c_end <= blk * QB + lane_q, sc, NEG)
        return jnp.exp2(sc - jnp.max(sc, axis=0, keepdims=True)).astype(BF16)

    def sel_matmuls(pc):
        cr = pc.shape[0]
        ocl = jnp.dot(vcmp_ref[:, 0:cr], pc, preferred_element_type=F32)
        ir, ic = IMP_ROWS, 4 * IMP_ROWS
        if cr % ic:
            return ocl, jnp.dot(ovl_ref[0:cr // 4, 0:cr], pc, preferred_element_type=F32)
        parts = []
        for i in range(cr // ic):
            c0 = max(i - 1, 0) * ic
            parts.append(jnp.dot(ovl_ref[i * ir:(i + 1) * ir, c0:(i + 1) * ic], pc[c0:(i + 1) * ic, :],
                                 preferred_element_type=F32))
        return ocl, jnp.concatenate(parts, axis=0)

    def sel_choose(ocl, imp4, blk):
        t_row = blk * QB + lane_q
        any_valid = jnp.where(t_row >= CMP_BLOCK - 1, 1.0, 0.0)
        rc = any_valid / ocl[DH:DH + 1, :]
        ocn_ref[...] = ocl[0:DH, :] * rc

        imp4 = imp4 * rc
        imp = imp4[:, 0:QB] + imp4[:, QB:2 * QB] + imp4[:, 2 * QB:3 * QB] + imp4[:, 3 * QB:4 * QB]
        if imp.shape[0] < NB:
            imp = jnp.concatenate([imp, jnp.zeros((NB - imp.shape[0], QB), F32)], axis=0)

        jidx = lax.broadcasted_iota(jnp.int32, (NB, QB), 0)
        cur = (blk * QB + lax.broadcasted_iota(jnp.int32, (1, QB), 1)) >> SLC_SHIFT
        forced = (jidx == 0) | (jidx == cur) | (jidx == cur - 1)
        future = jidx > cur
        cand = jnp.where(forced | future, -jnp.inf, imp)

        def select(break_ties):
            v = cand
            for _ in range(SLC_COUNT - 3):
                mx = jnp.max(v, axis=0, keepdims=True)
                hit = v == mx
                if break_ties:
                    hit = jidx == jnp.min(jnp.where(hit, jidx, NB), axis=0, keepdims=True)
                v = jnp.where(hit, -jnp.inf, v)
            return jnp.where((v == -jnp.inf) & (jidx <= cur), 0.0, NEG)

        bias_t = select(False)
        n_sel = jnp.sum(jnp.where(bias_t == 0.0, 1.0, 0.0), axis=0, keepdims=True)
        biasn_ref[...] = bias_t.astype(BF16)

        def redo_if_tied():
            @pl.when(jnp.max(n_sel) > SLC_COUNT)
            def _():
                biasn_ref[...] = select(True).astype(BF16)
        return redo_if_tied

    cr_need = jnp.minimum((b + 2) * (QB // CMP_STRIDE) - 1, NC)
    cr_sizes = (NC // 2, NC) if (NC // 2) % (4 * IMP_ROWS) == 0 else (NC,)

    @pl.when(b == 0)
    def _():
        ocl, imp4 = sel_matmuls(sel_probs(sel_scores(q_ref, cr_sizes[0]), 0))
        sel_choose(ocl, imp4, 0)()

    bias_ref[...] = biasn_ref[...]
    oc = ocn_ref[...]

    qa_ref[0:DH, :] = jnp.concatenate([q_ref[h] for h in range(HPG)], axis=1)
    qa_ref[DH:KILL_LANE, :] = jnp.zeros((KILL_LANE - DH, NQ), BF16)

    def set_bias_rows(grp):
        bias_rows = bias_ref[pl.ds(pl.multiple_of(grp * BLK_PER_TILE, BLK_PER_TILE), BLK_PER_TILE), :]
        qa_ref[DH:DH + BLK_PER_TILE, :] = jnp.concatenate([bias_rows] * HPG, axis=1)

    def set_flag_rows(ref):
        r = lax.broadcasted_iota(jnp.int32, (LANES - KILL_LANE, NQ), 0)
        rows = jnp.where(r == 0, NEG, 0.0) if ref is None else jnp.where(r == 0, NEG, jnp.where(r == 1, -ref, 0.0))
        qa_ref[KILL_LANE:LANES, :] = rows.astype(BF16)

    def scores(kt, s_ref):
        k0 = pl.multiple_of(kt * TK, TK)
        set_bias_rows(kt // 2)
        s_ref[...] = jnp.dot(ks_ref[pl.ds(k0, TK), :], qa_ref[...], preferred_element_type=F32)

    def accumulate(kt, s_ref):
        k0 = pl.multiple_of(kt * TK, TK)
        m_old = m_ref[...]
        m_new = jnp.maximum(m_old, jnp.max(s_ref[...], axis=0, keepdims=True))
        p_ref[0:TK, :] = jnp.exp2(s_ref[...] - m_new).astype(BF16)
        acc_ref[...] = (jnp.exp2(m_old - m_new) * acc_ref[...]
                        + jnp.dot(vs_ref[:, pl.ds(k0, TK)], p_ref[0:TK, :], preferred_element_type=F32))
        m_ref[...] = m_new

    def mask_own_keys(s_ref):
        band = pl.ds(pl.multiple_of(s0 - n_full * TK, QB), QB)
        s_ref[band, :] = jnp.where(band_row <= band_tq, s_ref[band, :], NEG)

    n_full = s0 // TK

    m_ref[...] = jnp.full((1, NQ), NEG, F32)
    acc_ref[...] = jnp.zeros((VROWS, NQ), F32)
    set_flag_rows(None)

    def first_region(cr):
        sc = sel_scores(qn_ref, cr)
        sw = jnp.dot(kw_ref[pl.ds(pl.multiple_of(s0, QB), WIN_KEYS), :], qa_ref[...],
                     preferred_element_type=F32)
        scores(n_full, sa_ref)

        pc = sel_probs(sc, b + 1)
        ocl, imp4 = sel_matmuls(pc)
        redo_if_tied = sel_choose(ocl, imp4, b + 1)

        sw = jnp.concatenate([jnp.where(band_row > band_tq, sw[0:QB], NEG),
                              sw[QB:WINDOW],
                              jnp.where(band_row <= band_tq, sw[WINDOW:WIN_KEYS], NEG)], axis=0)
        pw = jnp.exp2(sw - jnp.max(sw, axis=0, keepdims=True)).astype(BF16)
        mask_own_keys(sa_ref)

        ow = jnp.dot(vw_ref[:, pl.ds(pl.multiple_of(s0, QB), WIN_KEYS)], pw, preferred_element_type=F32)
        owin_ref[...] = ow[0:DH, :] * (1.0 / ow[DH:DH + 1, :])
        accumulate(n_full, sa_ref)

        m = m_ref[...]
        c0 = m.astype(BF16).astype(F32)
        c_ref[...] = c0
        acc_ref[...] = acc_ref[...] * jnp.exp2(m - c0)
        rise_ref[...] = jnp.zeros((1, NQ), F32)
        redo_if_tied()

    for i, cr in enumerate(cr_sizes):
        fits = cr_need <= cr
        if i > 0:
            fits = fits & (cr_need > cr_sizes[i - 1])
        pl.when(fits)(functools.partial(first_region, cr))

    def older_keys(grp, k0, rows):
        c = c_ref[...]
        set_bias_rows(grp)
        set_flag_rows(c)
        s = jnp.dot(ks_ref[pl.ds(k0, rows), :], qa_ref[...], preferred_element_type=F32)
        mt = jnp.max(s, axis=0, keepdims=True)
        p_ref[0:rows, :] = jnp.exp2(s).astype(BF16)
        pv = jnp.dot(vs_ref[:, pl.ds(k0, rows)], p_ref[0:rows, :], preferred_element_type=F32)
        c_new = (c + jnp.maximum(mt, 0.0)).astype(BF16).astype(F32)
        acc_ref[...] = (acc_ref[...] + pv) * jnp.exp2(c - c_new)
        c_ref[...] = c_new
        rise_ref[...] = jnp.maximum(rise_ref[...], mt)

    def older_pair(jj, carry):
        older_keys(jj, pl.multiple_of(jj * TK_OLD, TK_OLD), TK_OLD)
        return carry

    lax.fori_loop(0, n_full // 2, older_pair, 0)

    @pl.when(n_full % 2 == 1)
    def _():
        older_keys(n_full // 2, pl.multiple_of((n_full - 1) * TK, TK), TK)

    @pl.when(jnp.max(rise_ref[...]) > REF_SLACK)
    def _():
        m_ref[...] = jnp.full((1, NQ), NEG, F32)
        acc_ref[...] = jnp.zeros((VROWS, NQ), F32)
        set_flag_rows(None)

        def two_pass_tile(kt, carry):
            scores(kt, sa_ref)

            @pl.when(kt == n_full)
            def _():
                mask_own_keys(sa_ref)
            accumulate(kt, sa_ref)
            return carry

        lax.fori_loop(0, n_full + 1, two_pass_tile, 0)

    acc = acc_ref[...]
    osel = acc[0:DH, :] * (1.0 / acc[DH:DH + 1, :])
    owin = owin_ref[...]

    heads = []
    for h in range(HPG):
        sl = slice(h * QB, (h + 1) * QB)
        heads.append(gt_ref[3 * h:3 * h + 1, :] * oc[:, sl]
                     + gt_ref[3 * h + 1:3 * h + 2, :] * osel[:, sl]
                     + gt_ref[3 * h + 2:3 * h + 3, :] * owin[:, sl])
    out_ref[...] = jnp.concatenate(heads, axis=0).T.astype(out_ref.dtype)


def _nsa(q_t, kcmp, vcmp_t, ks, vs_t, kw, vw_t, gt, ovl_t):
    S = ks.shape[1]
    NC, NB = S // CMP_STRIDE, S // SLC_BLOCK
    n_qb = S // QB
    grp = lambda g, b: (g, 0, 0)
    return pl.pallas_call(
        functools.partial(_nsa_kernel, S=S),
        grid=(NSA_GROUPS, n_qb),
        in_specs=[pl.BlockSpec((HPG, DH, QB), lambda g, b: (g, 0, b)),
                  pl.BlockSpec((HPG, DH, QB), lambda g, b: (g, 0, jnp.minimum(b + 1, n_qb - 1))),
                  pl.BlockSpec((None, NC, LANES), grp),
                  pl.BlockSpec((None, VROWS, NC), grp),
                  pl.BlockSpec((None, S, LANES), grp, pipeline_mode=pl.Buffered(1)),
                  pl.BlockSpec((None, VROWS, S), grp, pipeline_mode=pl.Buffered(1)),
                  pl.BlockSpec((None, WINDOW + S, LANES), grp, pipeline_mode=pl.Buffered(1)),
                  pl.BlockSpec((None, VROWS, WINDOW + S), grp, pipeline_mode=pl.Buffered(1)),
                  pl.BlockSpec((None, GATE_ROWS, QB), lambda g, b: (g, 0, b)),
                  pl.BlockSpec((NB, NC), lambda g, b: (0, 0))],
        out_specs=pl.BlockSpec((QB, HPG * DH), lambda g, b: (b, g)),
        out_shape=jax.ShapeDtypeStruct((S, NSA_HEADS * DH), BF16),
        scratch_shapes=[pltpu.VMEM((LANES, HPG * QB), BF16),
                        pltpu.VMEM((LANES, HPG * QB), BF16),
                        pltpu.VMEM((NB, QB), BF16),
                        pltpu.VMEM((NB, QB), BF16),
                        pltpu.VMEM((DH, HPG * QB), F32),
                        pltpu.VMEM((1, HPG * QB), F32),
                        pltpu.VMEM((VROWS, HPG * QB), F32),
                        pltpu.VMEM((TK, HPG * QB), F32),
                        pltpu.VMEM((TK_OLD, HPG * QB), BF16),
                        pltpu.VMEM((1, HPG * QB), F32),
                        pltpu.VMEM((1, HPG * QB), F32),
                        pltpu.VMEM((DH, HPG * QB), F32)],
        compiler_params=_params(("arbitrary", "arbitrary")), name="nsa",
    )(q_t, q_t, kcmp, vcmp_t, ks, vs_t, kw, vw_t, gt, ovl_t)


def _ret_kernel(q_ref, k_ref, v_ref, g_ref, dmat_ref, xi_ref, zeta_ref, dec_ref, o_ref, state_ref, *, tm):
    @pl.when(pl.program_id(0) == 0)
    def _():
        state_ref[...] = jnp.zeros_like(state_ref)

    for j in range(tm // RET_C):
        rows = slice(j * RET_C, (j + 1) * RET_C)
        for h in range(RET_HEADS):
            cols = slice(h * RET_D, (h + 1) * RET_D)
            qc, kc, vc = q_ref[rows, cols], k_ref[rows, cols], v_ref[rows, cols]
            st = state_ref[h]
            st_hi = st.astype(BF16)
            st_lo = (st - st_hi.astype(F32)).astype(BF16)
            sc = lax.dot_general(qc, kc, NT_DIMS, preferred_element_type=F32) * dmat_ref[h]
            inner = jnp.dot(sc.astype(BF16), vc, preferred_element_type=F32)
            cross = (jnp.dot(qc, st_hi, preferred_element_type=F32)
                     + jnp.dot(qc, st_lo, preferred_element_type=F32)) * xi_ref[h]
            kz = (kc.astype(F32) * zeta_ref[h]).astype(BF16)
            state_ref[h] = dec_ref[h] * st + lax.dot_general(kz, vc, TN_DIMS, preferred_element_type=F32)
            o = inner + cross
            o = o * lax.rsqrt(jnp.mean(o * o, axis=-1, keepdims=True) + EPS)
            gate = g_ref[rows, cols].astype(F32)
            o_ref[rows, cols] = (gate * _sigmoid(gate) * o).astype(o_ref.dtype)


def _retention(rq, rk, rv, rg, dmat, xi, zeta, dec, tm=512):
    S = rq.shape[0]
    W = RET_HEADS * RET_D
    row = lambda i: (i, 0)
    cst = lambda i: (0, 0, 0)
    blk = pl.BlockSpec((tm, W), row)
    tbl = pl.BlockSpec((RET_HEADS, RET_C, RET_D), cst)
    return pl.pallas_call(
        functools.partial(_ret_kernel, tm=tm),
        grid=(S // tm,),
        in_specs=[blk, blk, blk, blk, tbl, tbl, tbl, tbl],
        out_specs=blk,
        out_shape=jax.ShapeDtypeStruct((S, W), BF16),
        scratch_shapes=[pltpu.VMEM((RET_HEADS, RET_D, RET_D), F32)],
        compiler_params=_params(("arbitrary",)), name="retention",
    )(rq, rk, rv, rg, dmat, xi, zeta, dec)


def _mix_kernel(x_ref, g_ref, a_ref, r_ref, wg_ref, wpa_ref, wpb_ref, wo_ref, o_ref):
    x = x_ref[...]
    h = _rms(x, g_ref[...]).astype(BF16)
    ga = _sigmoid(jnp.dot(h, wg_ref[:, 0:D_MODEL], preferred_element_type=F32))
    gb = _sigmoid(jnp.dot(h, wg_ref[:, D_MODEL:2 * D_MODEL], preferred_element_type=F32))
    mix = (ga * jnp.dot(a_ref[...], wpa_ref[...], preferred_element_type=F32)
           + gb * jnp.dot(r_ref[...], wpb_ref[...], preferred_element_type=F32))
    o_ref[...] = x + jnp.dot(mix.astype(BF16), wo_ref[...], preferred_element_type=F32)


def _mix(x, g, nsa_out, ret_out, wg, wpa, wpb, wo, tm=512):
    S = x.shape[0]
    row = lambda i: (i, 0)
    cst = lambda i: (0, 0)
    return pl.pallas_call(
        _mix_kernel,
        grid=(S // tm,),
        in_specs=[pl.BlockSpec((tm, D_MODEL), row), pl.BlockSpec((1, D_MODEL), cst),
                  pl.BlockSpec((tm, NSA_HEADS * DH), row), pl.BlockSpec((tm, RET_HEADS * RET_D), row),
                  pl.BlockSpec((D_MODEL, 2 * D_MODEL), cst), pl.BlockSpec((512, D_MODEL), cst),
                  pl.BlockSpec((512, D_MODEL), cst), pl.BlockSpec((D_MODEL, D_MODEL), cst)],
        out_specs=pl.BlockSpec((tm, D_MODEL), row),
        out_shape=jax.ShapeDtypeStruct((S, D_MODEL), F32),
        compiler_params=_params(("arbitrary",)), name="mix",
    )(x, g, nsa_out, ret_out, wg, wpa, wpb, wo)


def _mlp_kernel(x_ref, g_ref, wu_ref, wd_ref, gf_ref, o_ref, *, hc):
    x = x_ref[...]
    h = _rms(x, g_ref[...]).astype(BF16)
    acc = x
    for c in range(MLP_HIDDEN // hc):
        u = jnp.maximum(jnp.dot(h, wu_ref[:, c * hc:(c + 1) * hc], preferred_element_type=F32), 0.0)
        acc = acc + jnp.dot((u * u).astype(BF16), wd_ref[c * hc:(c + 1) * hc, :], preferred_element_type=F32)
    o_ref[...] = _rms(acc, gf_ref[...])


def _mlp(x, g, wu, wd, gf, tm=512, hc=1024):
    S = x.shape[0]
    row = lambda i: (i, 0)
    cst = lambda i: (0, 0)
    return pl.pallas_call(
        functools.partial(_mlp_kernel, hc=hc),
        grid=(S // tm,),
        in_specs=[pl.BlockSpec((tm, D_MODEL), row), pl.BlockSpec((1, D_MODEL), cst),
                  pl.BlockSpec((D_MODEL, MLP_HIDDEN), cst), pl.BlockSpec((MLP_HIDDEN, D_MODEL), cst),
                  pl.BlockSpec((1, D_MODEL), cst)],
        out_specs=pl.BlockSpec((tm, D_MODEL), row),
        out_shape=jax.ShapeDtypeStruct((S, D_MODEL), F32),
        compiler_params=_params(("arbitrary",)), name="mlp",
    )(x, g, wu, wd, gf)


def _rope_freqs():
    half = ROPE_DIM // 2
    inv = jnp.exp(-math.log(ROPE_THETA) * jnp.arange(half, dtype=F32) * 2.0 / ROPE_DIM)
    hr = RET_D // 2
    inv_r = jnp.exp(-math.log(RET_BASE) * jnp.arange(hr, dtype=F32) * 2.0 / RET_D)
    return jnp.concatenate([inv_r, inv, jnp.zeros((LANES - hr - half,), F32)])[None, :]


def _decay_tables():
    H, C = RET_HEADS, RET_C
    log_g = jnp.log(1.0 - jnp.exp2(-5.0 - jnp.arange(H, dtype=F32)))
    n = jnp.arange(C, dtype=F32)
    rel = n[:, None] - n[None, :]
    dmat = jnp.where(rel >= 0, jnp.exp(log_g[:, None, None] * jnp.maximum(rel, 0.0)), 0.0)
    xi = jnp.exp(log_g[:, None] * (n + 1.0))
    zeta = jnp.exp(log_g[:, None] * (C - 1.0 - n))
    dec = jnp.exp(log_g * C)
    bc = lambda a: jnp.broadcast_to(a, (H, C, RET_D))
    return dmat, bc(xi[:, :, None]), bc(zeta[:, :, None]), bc(dec[:, None, None])


def _overlap_t(S):
    n_cmp = (S - CMP_BLOCK) // CMP_STRIDE + 1
    n_slc = S // SLC_BLOCK
    cs = np.arange(n_cmp)[:, None] * CMP_STRIDE
    ss = np.arange(n_slc)[None, :] * SLC_BLOCK
    ov = np.clip(np.minimum(cs + CMP_BLOCK, ss + SLC_BLOCK) - np.maximum(cs, ss), 0, None) / CMP_BLOCK
    out = np.zeros((n_slc, S // CMP_STRIDE), np.float32)
    out[:, :n_cmp] = ov.T
    return jnp.asarray(out, BF16)


def _v_rows(v_t):
    S = v_t.shape[1]
    return jnp.concatenate([v_t, jnp.ones((1, S), v_t.dtype), jnp.zeros((VROWS - DH - 1, S), v_t.dtype)], axis=0)


_IN_OFFSETS = np.cumsum((0, 512, 128, 128, 128, 128, 128, 128, 24, 512, 512, 512, 512, 1024, 1024))


def _in_col(wi, i):
    return wi[:, _IN_OFFSETS[i]:_IN_OFFSETS[i + 1]]


def _branches(x, positions, norm_mix, w_in, cmp_pos_k, cmp_pos_v, cmp_k_w1, cmp_k_w2, cmp_v_w1, cmp_v_w2):
    B, S, _ = x.shape
    assert B == 1 and norm_mix.shape[0] == 1 and S % TK == 0 and S >= max(WIN_KEYS, SLC_COUNT * SLC_BLOCK)
    xs = x[0]
    col = functools.partial(_in_col, w_in[0])
    n_gate = 3 * HPG
    z = lambda n: jnp.zeros((D_MODEL, n), F32)
    w_a = jnp.concatenate([col(0), col(1), col(3), col(5), col(2), col(4), col(6),
                           col(7)[:, 0:n_gate], z(GATE_ROWS - n_gate),
                           col(7)[:, n_gate:], z(LANES - GATE_ROWS - n_gate),
                           col(8), col(9), col(10), col(11)], axis=1).astype(BF16)

    q_t, xh, ks, kw, vs_t, vw_t, gt, rq, rk, rv, rg = _proj(
        xs, norm_mix, w_a, jnp.broadcast_to(positions[0].astype(F32)[:, None], (S, LANES)), _rope_freqs())

    w1 = jnp.stack([cmp_k_w1[0], cmp_v_w1[0]]).astype(BF16)
    pe = jnp.stack([cmp_pos_k[0], cmp_pos_v[0]]).reshape(2, 1, CMP_BLOCK * DH)
    pe = jnp.broadcast_to(pe, (2, BF16_ROWS, CMP_BLOCK * DH)).astype(BF16)
    w2 = jnp.pad(jnp.stack([cmp_k_w2[0], cmp_v_w2[0]]), ((0, 0), (0, 0), (0, LANES - DH))).astype(BF16)
    cmp = _compress(xh, w1, pe, w2)
    kcmp = cmp[0:NSA_GROUPS].astype(BF16)
    vcmp_t = jax.vmap(_v_rows)(cmp[NSA_GROUPS:, :, 0:DH].transpose(0, 2, 1).astype(BF16))

    vw_t = jnp.pad(vw_t, ((0, 0), (0, 0), (WINDOW, 0)))
    pad_keys = jnp.zeros((NSA_GROUPS, WINDOW, LANES), BF16).at[:, :, KILL_LANE].set(1.0)
    kw = jnp.concatenate([pad_keys, kw], axis=1)
    nsa_out = _nsa(q_t, kcmp, vcmp_t, ks, vs_t, kw, vw_t, gt, _overlap_t(S))

    ret_out = _retention(rq, rk, rv, rg, *_decay_tables())
    return nsa_out, ret_out


def kernel(x, positions, norm_mix, w_in, cmp_pos_k, cmp_pos_v, cmp_k_w1, cmp_k_w2, cmp_v_w1, cmp_v_w2,
           w_proj_a, w_proj_b, w_out, norm_mlp, w_up, w_down, norm_final):
    nsa_out, ret_out = _branches(x, positions, norm_mix, w_in, cmp_pos_k, cmp_pos_v,
                                 cmp_k_w1, cmp_k_w2, cmp_v_w1, cmp_v_w2)
    xs = x[0]
    w_gate = jnp.concatenate([_in_col(w_in[0], 12), _in_col(w_in[0], 13)], axis=1).astype(BF16)
    x1 = _mix(xs, norm_mix, nsa_out, ret_out, w_gate, w_proj_a[0].astype(BF16), w_proj_b[0].astype(BF16),
              w_out[0].astype(BF16))
    y = _mlp(x1, norm_mlp, w_up[0].astype(BF16), w_down[0].astype(BF16), norm_final[None, :])
    return y[None]
```

```python
import functools
import math

import numpy as np
import jax
import jax.numpy as jnp
from jax import lax
from jax.experimental import pallas as pl
from jax.experimental.pallas import tpu as pltpu

F32 = jnp.float32
BF16 = jnp.bfloat16

D_MODEL = 1024
NSA_HEADS = 8
NSA_GROUPS = 2
HPG = NSA_HEADS // NSA_GROUPS
DH = 64
CMP_BLOCK = 32
CMP_STRIDE = 16
CMP_HIDDEN = 256
SLC_BLOCK = 64
SLC_COUNT = 16
WINDOW = 512
QB = 256
ROPE_THETA = 500000.0
ROPE_DIM = DH // 4
RET_HEADS = 4
RET_D = 128
RET_C = 128
RET_BASE = 10000.0
MLP_HIDDEN = 4 * D_MODEL
EPS = 1e-6
NEG = -1e30

LANES = 128
TK = 1024
TK_OLD = 2 * TK
BLK_PER_TILE = TK_OLD // SLC_BLOCK
IMP_ROWS = 64
KILL_LANE = DH + BLK_PER_TILE
REF_LANE = KILL_LANE + 1
REF_SLACK = 64.0
LOG2E = math.log2(math.e)
VROWS = 80
BF16_ROWS = 16
GATE_ROWS = 16
SLC_SHIFT = SLC_BLOCK.bit_length() - 1
WIN_KEYS = WINDOW + QB
VMEM_LIMIT = 56 * 1024 * 1024

NT_DIMS = (((1,), (1,)), ((), ()))
TN_DIMS = (((0,), (0,)), ((), ()))


def _sigmoid(v):
    return 1.0 / (1.0 + jnp.exp(-v))


def _rms(x, g):
    return x * lax.rsqrt(jnp.mean(x * x, axis=-1, keepdims=True) + EPS) * g


def _params(sem, flags=None):
    return pltpu.CompilerParams(dimension_semantics=sem, vmem_limit_bytes=VMEM_LIMIT, flags=flags)


_C_ROT = (0, 896)
_C_V = (896, 1280)
_C_G = (1280, 1408)
_C_RQ = (1408, 1920)
_C_RK = (1920, 2432)
_C_RV = (2432, 2944)
_C_RG = (2944, 3456)
PROJ_W = 3456


def _proj_kernel(x_ref, g_ref, win_ref, pos_ref, freq_ref,
                 q_ref, xh_ref, ks_ref, kw_ref, vs_ref, vw_ref, gt_ref, rq_ref, rk_ref, rv_ref, rg_ref,
                 kv_ref, w_ref, *, tm):
    @pl.when(pl.program_id(0) == 0)
    def _():
        for src, dst, n in _W_MOVES:
            w_ref[:, dst:dst + n] = win_ref[:, src:src + n].astype(BF16)
        n_gate = 3 * HPG
        g0 = _IN_OFFSETS[7]
        zeros = lambda n: jnp.zeros((D_MODEL, n), F32)
        gates = jnp.concatenate([win_ref[:, g0:g0 + n_gate], zeros(GATE_ROWS - n_gate),
                                 win_ref[:, g0 + n_gate:g0 + 2 * n_gate], zeros(LANES - GATE_ROWS - n_gate)],
                                axis=1)
        w_ref[:, _C_G[0]:_C_G[1]] = gates.astype(BF16)

    h = _rms(x_ref[...], g_ref[...]).astype(BF16)
    lane = lax.broadcasted_iota(jnp.int32, (tm, LANES), 1)
    lo = lane < DH

    ang = pos_ref[...] * freq_ref[...]
    c, s = jnp.cos(ang), jnp.sin(ang)
    half_n = ROPE_DIM // 2
    cr = jnp.where(lo, c, pltpu.roll(c, DH, 1))
    sr = jnp.where(lo, -s, pltpu.roll(s, DH, 1))
    dim = lane & (DH - 1)
    first, second = dim < half_n, (dim >= half_n) & (dim < ROPE_DIM)
    c1 = jnp.where(lo, pltpu.roll(c, DH, 1), c)
    s1 = jnp.where(lo, pltpu.roll(s, DH, 1), s)
    c2 = jnp.where(lo, pltpu.roll(c, DH + half_n, 1), pltpu.roll(c, half_n, 1))
    s2 = jnp.where(lo, pltpu.roll(s, DH + half_n, 1), pltpu.roll(s, half_n, 1))
    cn = jnp.where(first, c1, jnp.where(second, c2, 1.0))
    sa = jnp.where(first, -s1, 0.0)
    sb = jnp.where(second, s2, 0.0)

    def dot(c):
        return jnp.dot(h, w_ref[:, c[0]:c[1]], preferred_element_type=F32)

    def tile(y, i):
        return y[:, i * LANES:(i + 1) * LANES]

    def rope_n(t):
        return t * cn + pltpu.roll(t, LANES - half_n, 1) * sa + pltpu.roll(t, half_n, 1) * sb

    def rope_r(t):
        return t * cr + pltpu.roll(t, RET_D // 2, 1) * sr

    def split(t):
        return jnp.where(lo, t, 0.0), jnp.where(lo, pltpu.roll(t, DH, 1), 0.0)

    def v_rows(dst_ref, t):
        tt = t.T
        tail = jnp.where(lax.broadcasted_iota(jnp.int32, (VROWS - DH, tm), 0) == 0, 1.0, 0.0).astype(BF16)
        for g in range(NSA_GROUPS):
            dst_ref[g, 0:DH, :] = tt[g * DH:(g + 1) * DH, :].astype(BF16)
            dst_ref[g, DH:VROWS, :] = tail

    yr = dot(_C_ROT)
    for i in range(4):
        tt = (rope_n(tile(yr, i)) * (DH ** -0.5 * LOG2E)).T
        q_ref[2 * i] = tt[0:DH, :].astype(BF16)
        q_ref[2 * i + 1] = tt[DH:2 * DH, :].astype(BF16)
    row = pl.program_id(0) * tm + lax.broadcasted_iota(jnp.int32, (tm, LANES), 0)
    blk = (row >> SLC_SHIFT) & (BLK_PER_TILE - 1)
    onehot = jnp.where(((lane - DH) == blk) | (lane == REF_LANE), 1.0, 0.0)
    a, b = split(rope_n(tile(yr, 5)))
    ks_ref[0] = jnp.where(lo, a, onehot).astype(BF16)
    ks_ref[1] = jnp.where(lo, b, onehot).astype(BF16)
    a, b = split(rope_n(tile(yr, 6)))
    kw_ref[0] = a.astype(BF16)
    kw_ref[1] = b.astype(BF16)

    yv = dot(_C_V)
    v_rows(vs_ref, tile(yv, 1))
    v_rows(vw_ref, tile(yv, 2))
    gt = _sigmoid(dot(_C_G)).T
    for g in range(NSA_GROUPS):
        gt_ref[g] = gt[g * GATE_ROWS:(g + 1) * GATE_ROWS, :]

    kv_ref[0] = rope_n(tile(yr, 4))
    kv_ref[1] = tile(yv, 0)
    lo_r = lax.broadcasted_iota(jnp.int32, (tm // CMP_STRIDE, LANES), 1) < DH
    for typ in range(2):
        for u in range(CMP_STRIDE // 2):
            pa = kv_ref[typ, pl.ds(2 * u, tm // CMP_STRIDE, stride=CMP_STRIDE), :]
            pb = kv_ref[typ, pl.ds(2 * u + 1, tm // CMP_STRIDE, stride=CMP_STRIDE), :]
            cols = slice(u * LANES, (u + 1) * LANES)
            xh_ref[2 * typ, :, cols] = jnp.where(lo_r, pa, pltpu.roll(pb, DH, 1)).astype(BF16)
            xh_ref[2 * typ + 1, :, cols] = jnp.where(lo_r, pltpu.roll(pa, DH, 1), pb).astype(BF16)

    yq = dot(_C_RQ)
    yk = dot(_C_RK)
    for i in range(RET_HEADS):
        sl = slice(i * LANES, (i + 1) * LANES)
        rq_ref[:, sl] = rope_r(tile(yq, i)).astype(BF16)
        rk_ref[:, sl] = (rope_r(tile(yk, i)) * (RET_D ** -0.5)).astype(BF16)
    rv_ref[...] = dot(_C_RV).astype(BF16)
    rg_ref[...] = dot(_C_RG).astype(BF16)


def _proj(x, g, w, pos, freq, tm=512):
    S = x.shape[0]
    row = lambda i: (i, 0)
    const = lambda i: (0, 0)
    mid = lambda i: (0, i, 0)
    last = lambda i: (0, 0, i)
    W = RET_HEADS * RET_D
    out_shape = (
        jax.ShapeDtypeStruct((NSA_HEADS, DH, S), BF16),
        jax.ShapeDtypeStruct((2 * NSA_GROUPS, S // CMP_STRIDE, CMP_STRIDE * DH), BF16),
        jax.ShapeDtypeStruct((NSA_GROUPS, S, LANES), BF16),
        jax.ShapeDtypeStruct((NSA_GROUPS, S, LANES), BF16),
        jax.ShapeDtypeStruct((NSA_GROUPS, VROWS, S), BF16),
        jax.ShapeDtypeStruct((NSA_GROUPS, VROWS, S), BF16),
        jax.ShapeDtypeStruct((NSA_GROUPS, GATE_ROWS, S), F32),
        jax.ShapeDtypeStruct((S, W), BF16),
        jax.ShapeDtypeStruct((S, W), BF16),
        jax.ShapeDtypeStruct((S, W), BF16),
        jax.ShapeDtypeStruct((S, W), BF16),
    )
    out_specs = (
        pl.BlockSpec((NSA_HEADS, DH, tm), last),
        pl.BlockSpec((2 * NSA_GROUPS, tm // CMP_STRIDE, CMP_STRIDE * DH), mid),
        pl.BlockSpec((NSA_GROUPS, tm, LANES), mid),
        pl.BlockSpec((NSA_GROUPS, tm, LANES), mid),
        pl.BlockSpec((NSA_GROUPS, VROWS, tm), last),
        pl.BlockSpec((NSA_GROUPS, VROWS, tm), last),
        pl.BlockSpec((NSA_GROUPS, GATE_ROWS, tm), last),
        pl.BlockSpec((tm, W), row),
        pl.BlockSpec((tm, W), row),
        pl.BlockSpec((tm, W), row),
        pl.BlockSpec((tm, W), row),
    )
    return pl.pallas_call(
        functools.partial(_proj_kernel, tm=tm),
        grid=(S // tm,),
        in_specs=[pl.BlockSpec((tm, D_MODEL), row), pl.BlockSpec((1, D_MODEL), const),
                  pl.BlockSpec(w.shape, const, pipeline_mode=pl.Buffered(1)), pl.BlockSpec((tm, LANES), row),
                  pl.BlockSpec((1, LANES), const)],
        out_specs=out_specs, out_shape=out_shape,
        scratch_shapes=[pltpu.VMEM((2, tm, LANES), F32),
                        pltpu.VMEM((D_MODEL, PROJ_W), BF16)],
        compiler_params=_params(("arbitrary",)), name="proj",
    )(x, g, w, pos, freq)


def _compress_kernel(x_ref, w1_ref, pe_ref, w2_ref, o_ref):
    x = x_ref[...]
    half = CMP_STRIDE * DH
    a = jnp.dot(x, w1_ref[0:half, :], preferred_element_type=F32)
    b = jnp.dot(x, w1_ref[half:2 * half, :], preferred_element_type=F32)
    peb = jnp.dot(pe_ref[...], w1_ref[...], preferred_element_type=F32)[0:1, :]
    hid = a + pltpu.roll(b, x.shape[0] - 1, 0) + peb
    hid = hid * _sigmoid(hid)
    o_ref[...] = jnp.dot(hid.astype(BF16), w2_ref[...], preferred_element_type=F32)


def _compress(xh, w1, pe, w2):
    n_half = xh.shape[1]
    return pl.pallas_call(
        _compress_kernel,
        grid=(2 * NSA_GROUPS,),
        in_specs=[pl.BlockSpec((None, n_half, CMP_STRIDE * DH), lambda i: (i, 0, 0)),
                  pl.BlockSpec((None, CMP_BLOCK * DH, CMP_HIDDEN), lambda i: (i // NSA_GROUPS, 0, 0)),
                  pl.BlockSpec((None, BF16_ROWS, CMP_BLOCK * DH), lambda i: (i // NSA_GROUPS, 0, 0)),
                  pl.BlockSpec((None, CMP_HIDDEN, LANES), lambda i: (i // NSA_GROUPS, 0, 0))],
        out_specs=pl.BlockSpec((None, n_half, LANES), lambda i: (i, 0, 0)),
        out_shape=jax.ShapeDtypeStruct((2 * NSA_GROUPS, n_half, LANES), F32),
        compiler_params=_params(("arbitrary",)), name="compress",
    )(xh, w1, pe, w2)


def _nsa_kernel(q_ref, qn_ref, kcmp_ref, vcmp_ref, ks_ref, vs_ref, kw_ref, vw_ref, gt_ref, ovl_ref,
                out_ref, qa_ref, qs_ref, bias_ref, biasn_ref, ocn_ref, m_ref, acc_ref,
                sa_ref, p_ref, c_ref, rise_ref, owin_ref, *, S):
    NC = S // CMP_STRIDE
    NB = S // SLC_BLOCK
    NQ = HPG * QB
    b = pl.program_id(1)
    s0 = b * QB
    lane_q = lax.broadcasted_iota(jnp.int32, (1, NQ), 1) & (QB - 1)
    band_row = lax.broadcasted_iota(jnp.int32, (QB, NQ), 0)
    band_tq = lax.broadcasted_iota(jnp.int32, (QB, NQ), 1) & (QB - 1)

    def sel_scores(qsrc_ref, cr):
        qs_ref[0:DH, :] = jnp.concatenate([qsrc_ref[h] for h in range(HPG)], axis=1)
        qs_ref[DH:LANES, :] = jnp.zeros((LANES - DH, NQ), BF16)
        return jnp.dot(kcmp_ref[0:cr, :], qs_ref[...], preferred_element_type=F32)

    def sel_probs(sc, blk):
        c_end = lax.broadcasted_iota(jnp.int32, sc.shape, 0) * CMP_STRIDE + (CMP_BLOCK - 1)
        sc = jnp.where(c_end <= blk * QB + lane_q, sc, NEG)
        return jnp.exp2(sc - jnp.max(sc, axis=0, keepdims=True)).astype(BF16)

    def sel_matmuls(pc):
        cr = pc.shape[0]
        ocl = jnp.dot(vcmp_ref[:, 0:cr], pc, preferred_element_type=F32)
        ir, ic = IMP_ROWS, 4 * IMP_ROWS
        if cr % ic:
            return ocl, jnp.dot(ovl_ref[0:cr // 4, 0:cr], pc, preferred_element_type=F32)
        parts = []
        for i in range(cr // ic):
            c0 = max(i - 1, 0) * ic
            parts.append(jnp.dot(ovl_ref[i * ir:(i + 1) * ir, c0:(i + 1) * ic], pc[c0:(i + 1) * ic, :],
                                 preferred_element_type=F32))
        return ocl, jnp.concatenate(parts, axis=0)

    def sel_choose(ocl, imp4, blk):
        t_row = blk * QB + lane_q
        any_valid = jnp.where(t_row >= CMP_BLOCK - 1, 1.0, 0.0)
        rc = any_valid / ocl[DH:DH + 1, :]
        ocn_ref[...] = ocl[0:DH, :] * rc

        imp4 = imp4 * rc
        imp = imp4[:, 0:QB] + imp4[:, QB:2 * QB] + imp4[:, 2 * QB:3 * QB] + imp4[:, 3 * QB:4 * QB]
        if imp.shape[0] < NB:
            imp = jnp.concatenate([imp, jnp.zeros((NB - imp.shape[0], QB), F32)], axis=0)

        jidx = lax.broadcasted_iota(jnp.int32, (NB, QB), 0)
        cur = (blk * QB + lax.broadcasted_iota(jnp.int32, (1, QB), 1)) >> SLC_SHIFT
        forced = (jidx == 0) | (jidx == cur) | (jidx == cur - 1)
        future = jidx > cur
        cand = jnp.where(forced | future, -jnp.inf, imp)

        def select(break_ties):
            v = cand
            for _ in range(SLC_COUNT - 3):
                mx = jnp.max(v, axis=0, keepdims=True)
                hit = v == mx
                if break_ties:
                    hit = jidx == jnp.min(jnp.where(hit, jidx, NB), axis=0, keepdims=True)
                v = jnp.where(hit, -jnp.inf, v)
            return jnp.where((v == -jnp.inf) & (jidx <= cur), 0.0, NEG)

        bias_t = select(False)
        n_sel = jnp.sum(jnp.where(bias_t == 0.0, 1.0, 0.0), axis=0, keepdims=True)
        biasn_ref[...] = bias_t.astype(BF16)

        def redo_if_tied():
            @pl.when(jnp.max(n_sel) > SLC_COUNT)
            def _():
                biasn_ref[...] = select(True).astype(BF16)
        return redo_if_tied

    cr_need = jnp.minimum((b + 2) * (QB // CMP_STRIDE) - 1, NC)
    cr_sizes = (NC // 2, NC) if (NC // 2) % (4 * IMP_ROWS) == 0 else (NC,)

    @pl.when(b == 0)
    def _():
        ocl, imp4 = sel_matmuls(sel_probs(sel_scores(q_ref, cr_sizes[0]), 0))
        sel_choose(ocl, imp4, 0)()

    bias_ref[...] = biasn_ref[...]
    oc = ocn_ref[...]

    qa_ref[0:DH, :] = jnp.concatenate([q_ref[h] for h in range(HPG)], axis=1)
    qa_ref[DH:KILL_LANE, :] = jnp.zeros((KILL_LANE - DH, NQ), BF16)

    def set_bias_rows(grp):
        bias_rows = bias_ref[pl.ds(pl.multiple_of(grp * BLK_PER_TILE, BLK_PER_TILE), BLK_PER_TILE), :]
        qa_ref[DH:DH + BLK_PER_TILE, :] = jnp.concatenate([bias_rows] * HPG, axis=1)

    def set_flag_rows(ref):
        r = lax.broadcasted_iota(jnp.int32, (LANES - KILL_LANE, NQ), 0)
        rows = jnp.where(r == 0, NEG, 0.0) if ref is None else jnp.where(r == 0, NEG, jnp.where(r == 1, -ref, 0.0))
        qa_ref[KILL_LANE:LANES, :] = rows.astype(BF16)

    def scores(kt, s_ref):
        k0 = pl.multiple_of(kt * TK, TK)
        set_bias_rows(kt // 2)
        s_ref[...] = jnp.dot(ks_ref[pl.ds(k0, TK), :], qa_ref[...], preferred_element_type=F32)

    def accumulate(kt, s_ref):
        k0 = pl.multiple_of(kt * TK, TK)
        m_old = m_ref[...]
        m_new = jnp.maximum(m_old, jnp.max(s_ref[...], axis=0, keepdims=True))
        p_ref[0:TK, :] = jnp.exp2(s_ref[...] - m_new).astype(BF16)
        acc_ref[...] = (jnp.exp2(m_old - m_new) * acc_ref[...]
                        + jnp.dot(vs_ref[:, pl.ds(k0, TK)], p_ref[0:TK, :], preferred_element_type=F32))
        m_ref[...] = m_new

    def mask_own_keys(s_ref):
        band = pl.ds(pl.multiple_of(s0 - n_full * TK, QB), QB)
        s_ref[band, :] = jnp.where(band_row <= band_tq, s_ref[band, :], NEG)

    n_full = s0 // TK

    m_ref[...] = jnp.full((1, NQ), NEG, F32)
    acc_ref[...] = jnp.zeros((VROWS, NQ), F32)
    set_flag_rows(None)

    def first_region(cr):
        sc = sel_scores(qn_ref, cr)
        sw = jnp.dot(kw_ref[pl.ds(pl.multiple_of(s0, QB), WIN_KEYS), :], qa_ref[...],
                     preferred_element_type=F32)
        scores(n_full, sa_ref)

        pc = sel_probs(sc, b + 1)
        ocl, imp4 = sel_matmuls(pc)
        redo_if_tied = sel_choose(ocl, imp4, b + 1)

        sw = jnp.concatenate([jnp.where(band_row > band_tq, sw[0:QB], NEG),
                              sw[QB:WINDOW],
                              jnp.where(band_row <= band_tq, sw[WINDOW:WIN_KEYS], NEG)], axis=0)
        pw = jnp.exp2(sw - jnp.max(sw, axis=0, keepdims=True)).astype(BF16)
        mask_own_keys(sa_ref)

        ow = jnp.dot(vw_ref[:, pl.ds(pl.multiple_of(s0, QB), WIN_KEYS)], pw, preferred_element_type=F32)
        owin_ref[...] = ow[0:DH, :] * (1.0 / ow[DH:DH + 1, :])
        accumulate(n_full, sa_ref)

        m = m_ref[...]
        c0 = m.astype(BF16).astype(F32)
        c_ref[...] = c0
        acc_ref[...] = acc_ref[...] * jnp.exp2(m - c0)
        rise_ref[...] = jnp.zeros((1, NQ), F32)
        redo_if_tied()

    for i, cr in enumerate(cr_sizes):
        fits = cr_need <= cr
        if i > 0:
            fits = fits & (cr_need > cr_sizes[i - 1])
        pl.when(fits)(functools.partial(first_region, cr))

    def older_keys(grp, k0, rows):
        c = c_ref[...]
        set_bias_rows(grp)
        set_flag_rows(c)
        s = jnp.dot(ks_ref[pl.ds(k0, rows), :], qa_ref[...], preferred_element_type=F32)
        mt = jnp.max(s, axis=0, keepdims=True)
        p_ref[0:rows, :] = jnp.exp2(s).astype(BF16)
        pv = jnp.dot(vs_ref[:, pl.ds(k0, rows)], p_ref[0:rows, :], preferred_element_type=F32)
        c_new = (c + jnp.maximum(mt, 0.0)).astype(BF16).astype(F32)
        acc_ref[...] = (acc_ref[...] + pv) * jnp.exp2(c - c_new)
        c_ref[...] = c_new
        rise_ref[...] = jnp.maximum(rise_ref[...], mt)

    def older_pair(jj, carry):
        older_keys(jj, pl.multiple_of(jj * TK_OLD, TK_OLD), TK_OLD)
        return carry

    lax.fori_loop(0, n_full // 2, older_pair, 0)

    @pl.when(n_full % 2 == 1)
    def _():
        older_keys(n_full // 2, pl.multiple_of((n_full - 1) * TK, TK), TK)

    @pl.when(jnp.max(rise_ref[...]) > REF_SLACK)
    def _():
        m_ref[...] = jnp.full((1, NQ), NEG, F32)
        acc_ref[...] = jnp.zeros((VROWS, NQ), F32)
        set_flag_rows(None)

        def two_pass_tile(kt, carry):
            scores(kt, sa_ref)

            @pl.when(kt == n_full)
            def _():
                mask_own_keys(sa_ref)
            accumulate(kt, sa_ref)
            return carry

        lax.fori_loop(0, n_full + 1, two_pass_tile, 0)

    acc = acc_ref[...]
    osel = acc[0:DH, :] * (1.0 / acc[DH:DH + 1, :])
    owin = owin_ref[...]

    heads = []
    for h in range(HPG):
        sl = slice(h * QB, (h + 1) * QB)
        heads.append(gt_ref[3 * h:3 * h + 1, :] * oc[:, sl]
                     + gt_ref[3 * h + 1:3 * h + 2, :] * osel[:, sl]
                     + gt_ref[3 * h + 2:3 * h + 3, :] * owin[:, sl])
    out_ref[...] = jnp.concatenate(heads, axis=0).T.astype(out_ref.dtype)


def _nsa(q_t, kcmp, vcmp_t, ks, vs_t, kw, vw_t, gt, ovl_t):
    S = ks.shape[1]
    NC, NB = S // CMP_STRIDE, S // SLC_BLOCK
    n_qb = S // QB
    grp = lambda g, b: (g, 0, 0)
    return pl.pallas_call(
        functools.partial(_nsa_kernel, S=S),
        grid=(NSA_GROUPS, n_qb),
        in_specs=[pl.BlockSpec((HPG, DH, QB), lambda g, b: (g, 0, b)),
                  pl.BlockSpec((HPG, DH, QB), lambda g, b: (g, 0, jnp.minimum(b + 1, n_qb - 1))),
                  pl.BlockSpec((None, NC, LANES), grp),
                  pl.BlockSpec((None, VROWS, NC), grp),
                  pl.BlockSpec((None, S, LANES), grp, pipeline_mode=pl.Buffered(1)),
                  pl.BlockSpec((None, VROWS, S), grp, pipeline_mode=pl.Buffered(1)),
                  pl.BlockSpec((None, WINDOW + S, LANES), grp, pipeline_mode=pl.Buffered(1)),
                  pl.BlockSpec((None, VROWS, WINDOW + S), grp, pipeline_mode=pl.Buffered(1)),
                  pl.BlockSpec((None, GATE_ROWS, QB), lambda g, b: (g, 0, b)),
                  pl.BlockSpec((NB, NC), lambda g, b: (0, 0))],
        out_specs=pl.BlockSpec((QB, HPG * DH), lambda g, b: (b, g)),
        out_shape=jax.ShapeDtypeStruct((S, NSA_HEADS * DH), BF16),
        scratch_shapes=[pltpu.VMEM((LANES, HPG * QB), BF16),
                        pltpu.VMEM((LANES, HPG * QB), BF16),
                        pltpu.VMEM((NB, QB), BF16),
                        pltpu.VMEM((NB, QB), BF16),
                        pltpu.VMEM((DH, HPG * QB), F32),
                        pltpu.VMEM((1, HPG * QB), F32),
                        pltpu.VMEM((VROWS, HPG * QB), F32),
                        pltpu.VMEM((TK, HPG * QB), F32),
                        pltpu.VMEM((TK_OLD, HPG * QB), BF16),
                        pltpu.VMEM((1, HPG * QB), F32),
                        pltpu.VMEM((1, HPG * QB), F32),
                        pltpu.VMEM((DH, HPG * QB), F32)],
        compiler_params=_params(("arbitrary", "arbitrary")), name="nsa",
    )(q_t, q_t, kcmp, vcmp_t, ks, vs_t, kw, vw_t, gt, ovl_t)


def _ret_kernel(q_ref, k_ref, v_ref, g_ref, dmat_ref, xi_ref, zeta_ref, dec_ref, o_ref, state_ref, *, tm):
    @pl.when(pl.program_id(0) == 0)
    def _():
        state_ref[...] = jnp.zeros_like(state_ref)

    for j in range(tm // RET_C):
        rows = slice(j * RET_C, (j + 1) * RET_C)
        for h in range(RET_HEADS):
            cols = slice(h * RET_D, (h + 1) * RET_D)
            qc, kc, vc = q_ref[rows, cols], k_ref[rows, cols], v_ref[rows, cols]
            st = state_ref[h]
            st_hi = st.astype(BF16)
            st_lo = (st - st_hi.astype(F32)).astype(BF16)
            sc = lax.dot_general(qc, kc, NT_DIMS, preferred_element_type=F32) * dmat_ref[h]
            inner = jnp.dot(sc.astype(BF16), vc, preferred_element_type=F32)
            cross = (jnp.dot(qc, st_hi, preferred_element_type=F32)
                     + jnp.dot(qc, st_lo, preferred_element_type=F32)) * xi_ref[h]
            kz = (kc.astype(F32) * zeta_ref[h]).astype(BF16)
            state_ref[h] = dec_ref[h] * st + lax.dot_general(kz, vc, TN_DIMS, preferred_element_type=F32)
            o = inner + cross
            o = o * lax.rsqrt(jnp.mean(o * o, axis=-1, keepdims=True) + EPS)
            gate = g_ref[rows, cols].astype(F32)
            o_ref[rows, cols] = (gate * _sigmoid(gate) * o).astype(o_ref.dtype)


def _retention(rq, rk, rv, rg, dmat, xi, zeta, dec, tm=512):
    S = rq.shape[0]
    W = RET_HEADS * RET_D
    row = lambda i: (i, 0)
    cst = lambda i: (0, 0, 0)
    blk = pl.BlockSpec((tm, W), row)
    tbl = pl.BlockSpec((RET_HEADS, RET_C, RET_D), cst)
    return pl.pallas_call(
        functools.partial(_ret_kernel, tm=tm),
        grid=(S // tm,),
        in_specs=[blk, blk, blk, blk, tbl, tbl, tbl, tbl],
        out_specs=blk,
        out_shape=jax.ShapeDtypeStruct((S, W), BF16),
        scratch_shapes=[pltpu.VMEM((RET_HEADS, RET_D, RET_D), F32)],
        compiler_params=_params(("arbitrary",)), name="retention",
    )(rq, rk, rv, rg, dmat, xi, zeta, dec)


def _mix_kernel(x_ref, g_ref, a_ref, r_ref, wg_ref, wpa_ref, wpb_ref, wo_ref, o_ref):
    x = x_ref[...]
    h = _rms(x, g_ref[...]).astype(BF16)
    ga = _sigmoid(jnp.dot(h, wg_ref[:, 0:D_MODEL], preferred_element_type=F32))
    gb = _sigmoid(jnp.dot(h, wg_ref[:, D_MODEL:2 * D_MODEL], preferred_element_type=F32))
    mix = (ga * jnp.dot(a_ref[...], wpa_ref[...], preferred_element_type=F32)
           + gb * jnp.dot(r_ref[...], wpb_ref[...], preferred_element_type=F32))
    o_ref[...] = x + jnp.dot(mix.astype(BF16), wo_ref[...], preferred_element_type=F32)


def _mix(x, g, nsa_out, ret_out, wg, wpa, wpb, wo, tm=512):
    S = x.shape[0]
    row = lambda i: (i, 0)
    cst = lambda i: (0, 0)
    return pl.pallas_call(
        _mix_kernel,
        grid=(S // tm,),
        in_specs=[pl.BlockSpec((tm, D_MODEL), row), pl.BlockSpec((1, D_MODEL), cst),
                  pl.BlockSpec((tm, NSA_HEADS * DH), row), pl.BlockSpec((tm, RET_HEADS * RET_D), row),
                  pl.BlockSpec((D_MODEL, 2 * D_MODEL), cst), pl.BlockSpec((512, D_MODEL), cst),
                  pl.BlockSpec((512, D_MODEL), cst), pl.BlockSpec((D_MODEL, D_MODEL), cst)],
        out_specs=pl.BlockSpec((tm, D_MODEL), row),
        out_shape=jax.ShapeDtypeStruct((S, D_MODEL), F32),
        compiler_params=_params(("arbitrary",)), name="mix",
    )(x, g, nsa_out, ret_out, wg, wpa, wpb, wo)


def _mlp_kernel(x_ref, g_ref, wu_ref, wd_ref, gf_ref, o_ref, *, hc):
    x = x_ref[...]
    h = _rms(x, g_ref[...]).astype(BF16)
    acc = x
    for c in range(MLP_HIDDEN // hc):
        u = jnp.maximum(jnp.dot(h, wu_ref[:, c * hc:(c + 1) * hc], preferred_element_type=F32), 0.0)
        acc = acc + jnp.dot((u * u).astype(BF16), wd_ref[c * hc:(c + 1) * hc, :], preferred_element_type=F32)
    o_ref[...] = _rms(acc, gf_ref[...])


def _mlp(x, g, wu, wd, gf, tm=512, hc=1024):
    S = x.shape[0]
    row = lambda i: (i, 0)
    cst = lambda i: (0, 0)
    return pl.pallas_call(
        functools.partial(_mlp_kernel, hc=hc),
        grid=(S // tm,),
        in_specs=[pl.BlockSpec((tm, D_MODEL), row), pl.BlockSpec((1, D_MODEL), cst),
                  pl.BlockSpec((D_MODEL, MLP_HIDDEN), cst), pl.BlockSpec((MLP_HIDDEN, D_MODEL), cst),
                  pl.BlockSpec((1, D_MODEL), cst)],
        out_specs=pl.BlockSpec((tm, D_MODEL), row),
        out_shape=jax.ShapeDtypeStruct((S, D_MODEL), F32),
        compiler_params=_params(("arbitrary",)), name="mlp",
    )(x, g, wu, wd, gf)


def _rope_freqs():
    half = ROPE_DIM // 2
    inv = jnp.exp(-math.log(ROPE_THETA) * jnp.arange(half, dtype=F32) * 2.0 / ROPE_DIM)
    hr = RET_D // 2
    inv_r = jnp.exp(-math.log(RET_BASE) * jnp.arange(hr, dtype=F32) * 2.0 / RET_D)
    return jnp.concatenate([inv_r, inv, jnp.zeros((LANES - hr - half,), F32)])[None, :]


def _decay_tables():
    H, C = RET_HEADS, RET_C
    log_g = jnp.log(1.0 - jnp.exp2(-5.0 - jnp.arange(H, dtype=F32)))
    n = jnp.arange(C, dtype=F32)
    rel = n[:, None] - n[None, :]
    dmat = jnp.where(rel >= 0, jnp.exp(log_g[:, None, None] * jnp.maximum(rel, 0.0)), 0.0)
    xi = jnp.exp(log_g[:, None] * (n + 1.0))
    zeta = jnp.exp(log_g[:, None] * (C - 1.0 - n))
    dec = jnp.exp(log_g * C)
    bc = lambda a: jnp.broadcast_to(a, (H, C, RET_D))
    return dmat, bc(xi[:, :, None]), bc(zeta[:, :, None]), bc(dec[:, None, None])


def _overlap_t(S):
    n_cmp = (S - CMP_BLOCK) // CMP_STRIDE + 1
    n_slc = S // SLC_BLOCK
    cs = np.arange(n_cmp)[:, None] * CMP_STRIDE
    ss = np.arange(n_slc)[None, :] * SLC_BLOCK
    ov = np.clip(np.minimum(cs + CMP_BLOCK, ss + SLC_BLOCK) - np.maximum(cs, ss), 0, None) / CMP_BLOCK
    out = np.zeros((n_slc, S // CMP_STRIDE), np.float32)
    out[:, :n_cmp] = ov.T
    return jnp.asarray(out, BF16)


def _v_rows(v_t):
    S = v_t.shape[1]
    return jnp.concatenate([v_t, jnp.ones((1, S), v_t.dtype), jnp.zeros((VROWS - DH - 1, S), v_t.dtype)], axis=0)


_IN_OFFSETS = np.cumsum((0, 512, 128, 128, 128, 128, 128, 128, 24, 512, 512, 512, 512, 1024, 1024))


_W_MOVES = tuple((int(_IN_OFFSETS[i]), dst, int(_IN_OFFSETS[i + 1] - _IN_OFFSETS[i])) for i, dst in
                 ((0, 0), (1, 512), (3, 640), (5, 768), (2, 896), (4, 1024), (6, 1152),
                  (8, 1408), (9, 1920), (10, 2432), (11, 2944)))


def _in_col(wi, i):
    return wi[:, _IN_OFFSETS[i]:_IN_OFFSETS[i + 1]]


def _branches(x, positions, norm_mix, w_in, cmp_pos_k, cmp_pos_v, cmp_k_w1, cmp_k_w2, cmp_v_w1, cmp_v_w2):
    B, S, _ = x.shape
    assert B == 1 and norm_mix.shape[0] == 1 and S % TK == 0 and S >= max(WIN_KEYS, SLC_COUNT * SLC_BLOCK)
    xs = x[0]
    q_t, xh, ks, kw, vs_t, vw_t, gt, rq, rk, rv, rg = _proj(
        xs, norm_mix, w_in[0], jnp.broadcast_to(positions[0].astype(F32)[:, None], (S, LANES)), _rope_freqs())

    w1 = jnp.stack([cmp_k_w1[0], cmp_v_w1[0]]).astype(BF16)
    pe = jnp.stack([cmp_pos_k[0], cmp_pos_v[0]]).reshape(2, 1, CMP_BLOCK * DH)
    pe = jnp.broadcast_to(pe, (2, BF16_ROWS, CMP_BLOCK * DH)).astype(BF16)
    w2 = jnp.pad(jnp.stack([cmp_k_w2[0], cmp_v_w2[0]]), ((0, 0), (0, 0), (0, LANES - DH))).astype(BF16)
    cmp = _compress(xh, w1, pe, w2)
    kcmp = cmp[0:NSA_GROUPS].astype(BF16)
    vcmp_t = jax.vmap(_v_rows)(cmp[NSA_GROUPS:, :, 0:DH].transpose(0, 2, 1).astype(BF16))

    vw_t = jnp.pad(vw_t, ((0, 0), (0, 0), (WINDOW, 0)))
    pad_keys = jnp.zeros((NSA_GROUPS, WINDOW, LANES), BF16).at[:, :, KILL_LANE].set(1.0)
    kw = jnp.concatenate([pad_keys, kw], axis=1)
    nsa_out = _nsa(q_t, kcmp, vcmp_t, ks, vs_t, kw, vw_t, gt, _overlap_t(S))

    ret_out = _retention(rq, rk, rv, rg, *_decay_tables())
    return nsa_out, ret_out


def kernel(x, positions, norm_mix, w_in, cmp_pos_k, cmp_pos_v, cmp_k_w1, cmp_k_w2, cmp_v_w1, cmp_v_w2,
           w_proj_a, w_proj_b, w_out, norm_mlp, w_up, w_down, norm_final):
    nsa_out, ret_out = _branches(x, positions, norm_mix, w_in, cmp_pos_k, cmp_pos_v,
                                 cmp_k_w1, cmp_k_w2, cmp_v_w1, cmp_v_w2)
    xs = x[0]
    w_gate = jnp.concatenate([_in_col(w_in[0], 12), _in_col(w_in[0], 13)], axis=1).astype(BF16)
    x1 = _mix(xs, norm_mix, nsa_out, ret_out, w_gate, w_proj_a[0].astype(BF16), w_proj_b[0].astype(BF16),
              w_out[0].astype(BF16))
    y = _mlp(x1, norm_mlp, w_up[0].astype(BF16), w_down[0].astype(BF16), norm_final[None, :])
    return y[None]
```

```python
import functools
import math

import numpy as np
import jax
import jax.numpy as jnp
from jax import lax
from jax.experimental import pallas as pl
from jax.experimental.pallas import tpu as pltpu

F32 = jnp.float32
BF16 = jnp.bfloat16

D_MODEL = 1024
NSA_HEADS = 8
NSA_GROUPS = 2
HPG = NSA_HEADS // NSA_GROUPS
DH = 64
CMP_BLOCK = 32
CMP_STRIDE = 16
CMP_HIDDEN = 256
SLC_BLOCK = 64
SLC_COUNT = 16
WINDOW = 512
QB = 256
ROPE_THETA = 500000.0
ROPE_DIM = DH // 4
RET_HEADS = 4
RET_D = 128
RET_C = 128
RET_BASE = 10000.0
MLP_HIDDEN = 4 * D_MODEL
EPS = 1e-6
NEG = -1e30

LANES = 128
TK = 1024
TK_OLD = 2 * TK
BLK_PER_TILE = TK_OLD // SLC_BLOCK
IMP_ROWS = 64
KILL_LANE = DH + BLK_PER_TILE
REF_LANE = KILL_LANE + 1
REF_SLACK = 64.0
LOG2E = math.log2(math.e)
VROWS = 80
BF16_ROWS = 16
GATE_ROWS = 16
SLC_SHIFT = SLC_BLOCK.bit_length() - 1
WIN_KEYS = WINDOW + QB
VMEM_LIMIT = 56 * 1024 * 1024

NT_DIMS = (((1,), (1,)), ((), ()))
TN_DIMS = (((0,), (0,)), ((), ()))


def _sigmoid(v):
    return 1.0 / (1.0 + jnp.exp(-v))


def _rms(x, g):
    return x * lax.rsqrt(jnp.mean(x * x, axis=-1, keepdims=True) + EPS) * g


def _params(sem, flags=None):
    return pltpu.CompilerParams(dimension_semantics=sem, vmem_limit_bytes=VMEM_LIMIT, flags=flags)


_C_ROT = (0, 896)
_C_V = (896, 1280)
_C_G = (1280, 1408)
_C_RQ = (1408, 1920)
_C_RK = (1920, 2432)
_C_RV = (2432, 2944)
_C_RG = (2944, 3456)
PROJ_W = 3456


def _proj_kernel(x_ref, g_ref, win_ref, pos_ref, freq_ref,
                 q_ref, xh_ref, ks_ref, kw_ref, vs_ref, vw_ref, gt_ref, rq_ref, rk_ref, rv_ref, rg_ref,
                 kv_ref, w_ref, *, tm):
    @pl.when(pl.program_id(0) == 0)
    def _():
        for src, dst, n in _W_MOVES:
            w_ref[:, dst:dst + n] = win_ref[:, src:src + n].astype(BF16)
        n_gate = 3 * HPG
        g0 = _IN_OFFSETS[7]
        zeros = lambda n: jnp.zeros((D_MODEL, n), F32)
        gates = jnp.concatenate([win_ref[:, g0:g0 + n_gate], zeros(GATE_ROWS - n_gate),
                                 win_ref[:, g0 + n_gate:g0 + 2 * n_gate], zeros(LANES - GATE_ROWS - n_gate)],
                                axis=1)
        w_ref[:, _C_G[0]:_C_G[1]] = gates.astype(BF16)

    h = _rms(x_ref[...], g_ref[...]).astype(BF16)
    lane = lax.broadcasted_iota(jnp.int32, (tm, LANES), 1)
    lo = lane < DH

    ang = pos_ref[...] * freq_ref[...]
    c, s = jnp.cos(ang), jnp.sin(ang)
    half_n = ROPE_DIM // 2
    cr = jnp.where(lo, c, pltpu.roll(c, DH, 1))
    sr = jnp.where(lo, -s, pltpu.roll(s, DH, 1))
    dim = lane & (DH - 1)
    first, second = dim < half_n, (dim >= half_n) & (dim < ROPE_DIM)
    c1 = jnp.where(lo, pltpu.roll(c, DH, 1), c)
    s1 = jnp.where(lo, pltpu.roll(s, DH, 1), s)
    c2 = jnp.where(lo, pltpu.roll(c, DH + half_n, 1), pltpu.roll(c, half_n, 1))
    s2 = jnp.where(lo, pltpu.roll(s, DH + half_n, 1), pltpu.roll(s, half_n, 1))
    cn = jnp.where(first, c1, jnp.where(second, c2, 1.0))
    sa = jnp.where(first, -s1, 0.0)
    sb = jnp.where(second, s2, 0.0)

    def dot(c):
        return jnp.dot(h, w_ref[:, c[0]:c[1]], preferred_element_type=F32)

    def tile(y, i):
        return y[:, i * LANES:(i + 1) * LANES]

    def rope_n(t):
        return t * cn + pltpu.roll(t, LANES - half_n, 1) * sa + pltpu.roll(t, half_n, 1) * sb

    def rope_r(t):
        return t * cr + pltpu.roll(t, RET_D // 2, 1) * sr

    def split(t):
        return jnp.where(lo, t, 0.0), jnp.where(lo, pltpu.roll(t, DH, 1), 0.0)

    def v_rows(dst_ref, t):
        tt = t.T
        tail = jnp.where(lax.broadcasted_iota(jnp.int32, (VROWS - DH, tm), 0) == 0, 1.0, 0.0).astype(BF16)
        for g in range(NSA_GROUPS):
            dst_ref[g, 0:DH, :] = tt[g * DH:(g + 1) * DH, :].astype(BF16)
            dst_ref[g, DH:VROWS, :] = tail

    yr = dot(_C_ROT)
    for i in range(4):
        tt = (rope_n(tile(yr, i)) * (DH ** -0.5 * LOG2E)).T
        q_ref[2 * i] = tt[0:DH, :].astype(BF16)
        q_ref[2 * i + 1] = tt[DH:2 * DH, :].astype(BF16)
    row = pl.program_id(0) * tm + lax.broadcasted_iota(jnp.int32, (tm, LANES), 0)
    blk = (row >> SLC_SHIFT) & (BLK_PER_TILE - 1)
    onehot = jnp.where(((lane - DH) == blk) | (lane == REF_LANE), 1.0, 0.0)
    a, b = split(rope_n(tile(yr, 5)))
    ks_ref[0] = jnp.where(lo, a, onehot).astype(BF16)
    ks_ref[1] = jnp.where(lo, b, onehot).astype(BF16)
    a, b = split(rope_n(tile(yr, 6)))
    kw_ref[0] = a.astype(BF16)
    kw_ref[1] = b.astype(BF16)

    yv = dot(_C_V)
    v_rows(vs_ref, tile(yv, 1))
    v_rows(vw_ref, tile(yv, 2))
    gt = _sigmoid(dot(_C_G)).T
    for g in range(NSA_GROUPS):
        gt_ref[g] = gt[g * GATE_ROWS:(g + 1) * GATE_ROWS, :]

    kv_ref[0] = rope_n(tile(yr, 4))
    kv_ref[1] = tile(yv, 0)
    lo_r = lax.broadcasted_iota(jnp.int32, (tm // CMP_STRIDE, LANES), 1) < DH
    for typ in range(2):
        for u in range(CMP_STRIDE // 2):
            pa = kv_ref[typ, pl.ds(2 * u, tm // CMP_STRIDE, stride=CMP_STRIDE), :]
            pb = kv_ref[typ, pl.ds(2 * u + 1, tm // CMP_STRIDE, stride=CMP_STRIDE), :]
            cols = slice(u * LANES, (u + 1) * LANES)
            xh_ref[2 * typ, :, cols] = jnp.where(lo_r, pa, pltpu.roll(pb, DH, 1)).astype(BF16)
            xh_ref[2 * typ + 1, :, cols] = jnp.where(lo_r, pltpu.roll(pa, DH, 1), pb).astype(BF16)

    yq = dot(_C_RQ)
    yk = dot(_C_RK)
    for i in range(RET_HEADS):
        sl = slice(i * LANES, (i + 1) * LANES)
        rq_ref[:, sl] = rope_r(tile(yq, i)).astype(BF16)
        rk_ref[:, sl] = (rope_r(tile(yk, i)) * (RET_D ** -0.5)).astype(BF16)
    rv_ref[...] = dot(_C_RV).astype(BF16)
    rg_ref[...] = dot(_C_RG).astype(BF16)


def _proj(x, g, w, pos, freq, tm=512):
    S = x.shape[0]
    row = lambda i: (i, 0)
    const = lambda i: (0, 0)
    mid = lambda i: (0, i, 0)
    last = lambda i: (0, 0, i)
    W = RET_HEADS * RET_D
    out_shape = (
        jax.ShapeDtypeStruct((NSA_HEADS, DH, S), BF16),
        jax.ShapeDtypeStruct((2 * NSA_GROUPS, S // CMP_STRIDE, CMP_STRIDE * DH), BF16),
        jax.ShapeDtypeStruct((NSA_GROUPS, S, LANES), BF16),
        jax.ShapeDtypeStruct((NSA_GROUPS, S, LANES), BF16),
        jax.ShapeDtypeStruct((NSA_GROUPS, VROWS, S), BF16),
        jax.ShapeDtypeStruct((NSA_GROUPS, VROWS, S), BF16),
        jax.ShapeDtypeStruct((NSA_GROUPS, GATE_ROWS, S), F32),
        jax.ShapeDtypeStruct((S, W), BF16),
        jax.ShapeDtypeStruct((S, W), BF16),
        jax.ShapeDtypeStruct((S, W), BF16),
        jax.ShapeDtypeStruct((S, W), BF16),
    )
    out_specs = (
        pl.BlockSpec((NSA_HEADS, DH, tm), last),
        pl.BlockSpec((2 * NSA_GROUPS, tm // CMP_STRIDE, CMP_STRIDE * DH), mid),
        pl.BlockSpec((NSA_GROUPS, tm, LANES), mid),
        pl.BlockSpec((NSA_GROUPS, tm, LANES), mid),
        pl.BlockSpec((NSA_GROUPS, VROWS, tm), last),
        pl.BlockSpec((NSA_GROUPS, VROWS, tm), last),
        pl.BlockSpec((NSA_GROUPS, GATE_ROWS, tm), last),
        pl.BlockSpec((tm, W), row),
        pl.BlockSpec((tm, W), row),
        pl.BlockSpec((tm, W), row),
        pl.BlockSpec((tm, W), row),
    )
    return pl.pallas_call(
        functools.partial(_proj_kernel, tm=tm),
        grid=(S // tm,),
        in_specs=[pl.BlockSpec((tm, D_MODEL), row), pl.BlockSpec((1, D_MODEL), const),
                  pl.BlockSpec((None,) + w.shape[1:], lambda i: (0, 0, 0), pipeline_mode=pl.Buffered(1)),
                  pl.BlockSpec((tm, LANES), row),
                  pl.BlockSpec((1, LANES), const)],
        out_specs=out_specs, out_shape=out_shape,
        scratch_shapes=[pltpu.VMEM((2, tm, LANES), F32),
                        pltpu.VMEM((D_MODEL, PROJ_W), BF16)],
        compiler_params=_params(("arbitrary",)), name="proj",
    )(x, g, w, pos, freq)


def _compress_kernel(x_ref, w1_ref, pe_ref, w2_ref, o_ref):
    x = x_ref[...]
    half = CMP_STRIDE * DH
    a = jnp.dot(x, w1_ref[0:half, :], preferred_element_type=F32)
    b = jnp.dot(x, w1_ref[half:2 * half, :], preferred_element_type=F32)
    peb = jnp.dot(pe_ref[...], w1_ref[...], preferred_element_type=F32)[0:1, :]
    hid = a + pltpu.roll(b, x.shape[0] - 1, 0) + peb
    hid = hid * _sigmoid(hid)
    o_ref[...] = jnp.dot(hid.astype(BF16), w2_ref[...], preferred_element_type=F32)


def _compress(xh, w1, pe, w2):
    n_half = xh.shape[1]
    return pl.pallas_call(
        _compress_kernel,
        grid=(2 * NSA_GROUPS,),
        in_specs=[pl.BlockSpec((None, n_half, CMP_STRIDE * DH), lambda i: (i, 0, 0)),
                  pl.BlockSpec((None, CMP_BLOCK * DH, CMP_HIDDEN), lambda i: (i // NSA_GROUPS, 0, 0)),
                  pl.BlockSpec((None, BF16_ROWS, CMP_BLOCK * DH), lambda i: (i // NSA_GROUPS, 0, 0)),
                  pl.BlockSpec((None, CMP_HIDDEN, LANES), lambda i: (i // NSA_GROUPS, 0, 0))],
        out_specs=pl.BlockSpec((None, n_half, LANES), lambda i: (i, 0, 0)),
        out_shape=jax.ShapeDtypeStruct((2 * NSA_GROUPS, n_half, LANES), F32),
        compiler_params=_params(("arbitrary",)), name="compress",
    )(xh, w1, pe, w2)


def _nsa_kernel(q_ref, qn_ref, kcmp_ref, vcmp_ref, ks_ref, vs_ref, kw_ref, vw_ref, gt_ref, ovl_ref,
                out_ref, qa_ref, qs_ref, bias_ref, biasn_ref, ocn_ref, m_ref, acc_ref,
                sa_ref, p_ref, c_ref, rise_ref, owin_ref, *, S):
    NC = S // CMP_STRIDE
    NB = S // SLC_BLOCK
    NQ = HPG * QB
    b = pl.program_id(1)
    s0 = b * QB
    lane_q = lax.broadcasted_iota(jnp.int32, (1, NQ), 1) & (QB - 1)
    band_row = lax.broadcasted_iota(jnp.int32, (QB, NQ), 0)
    band_tq = lax.broadcasted_iota(jnp.int32, (QB, NQ), 1) & (QB - 1)

    def sel_scores(qsrc_ref, cr):
        qs_ref[0:DH, :] = jnp.concatenate([qsrc_ref[h] for h in range(HPG)], axis=1)
        qs_ref[DH:LANES, :] = jnp.zeros((LANES - DH, NQ), BF16)
        return jnp.dot(kcmp_ref[0:cr, :], qs_ref[...], preferred_element_type=F32)

    def sel_probs(sc, blk):
        c_end = lax.broadcasted_iota(jnp.int32, sc.shape, 0) * CMP_STRIDE + (CMP_BLOCK - 1)
        sc = jnp.where(c_end <= blk * QB + lane_q, sc, NEG)
        return jnp.exp2(sc - jnp.max(sc, axis=0, keepdims=True)).astype(BF16)

    def sel_matmuls(pc):
        cr = pc.shape[0]
        ocl = jnp.dot(vcmp_ref[:, 0:cr], pc, preferred_element_type=F32)
        ir, ic = IMP_ROWS, 4 * IMP_ROWS
        if cr % ic:
            return ocl, jnp.dot(ovl_ref[0:cr // 4, 0:cr], pc, preferred_element_type=F32)
        parts = []
        for i in range(cr // ic):
            c0 = max(i - 1, 0) * ic
            parts.append(jnp.dot(ovl_ref[i * ir:(i + 1) * ir, c0:(i + 1) * ic], pc[c0:(i + 1) * ic, :],
                                 preferred_element_type=F32))
        return ocl, jnp.concatenate(parts, axis=0)

    def sel_choose(ocl, imp4, blk):
        t_row = blk * QB + lane_q
        any_valid = jnp.where(t_row >= CMP_BLOCK - 1, 1.0, 0.0)
        rc = any_valid / ocl[DH:DH + 1, :]
        ocn_ref[...] = ocl[0:DH, :] * rc

        imp4 = imp4 * rc
        imp = imp4[:, 0:QB] + imp4[:, QB:2 * QB] + imp4[:, 2 * QB:3 * QB] + imp4[:, 3 * QB:4 * QB]
        if imp.shape[0] < NB:
            imp = jnp.concatenate([imp, jnp.zeros((NB - imp.shape[0], QB), F32)], axis=0)

        jidx = lax.broadcasted_iota(jnp.int32, (NB, QB), 0)
        cur = (blk * QB + lax.broadcasted_iota(jnp.int32, (1, QB), 1)) >> SLC_SHIFT
        forced = (jidx == 0) | (jidx == cur) | (jidx == cur - 1)
        future = jidx > cur
        cand = jnp.where(forced | future, -jnp.inf, imp)

        def select(break_ties):
            v = cand
            for _ in range(SLC_COUNT - 3):
                mx = jnp.max(v, axis=0, keepdims=True)
                hit = v == mx
                if break_ties:
                    hit = jidx == jnp.min(jnp.where(hit, jidx, NB), axis=0, keepdims=True)
                v = jnp.where(hit, -jnp.inf, v)
            return jnp.where((v == -jnp.inf) & (jidx <= cur), 0.0, NEG)

        bias_t = select(False)
        n_sel = jnp.sum(jnp.where(bias_t == 0.0, 1.0, 0.0), axis=0, keepdims=True)
        biasn_ref[...] = bias_t.astype(BF16)

        def redo_if_tied():
            @pl.when(jnp.max(n_sel) > SLC_COUNT)
            def _():
                biasn_ref[...] = select(True).astype(BF16)
        return redo_if_tied

    cr_need = jnp.minimum((b + 2) * (QB // CMP_STRIDE) - 1, NC)
    cr_sizes = (NC // 2, NC) if (NC // 2) % (4 * IMP_ROWS) == 0 else (NC,)

    @pl.when(b == 0)
    def _():
        ocl, imp4 = sel_matmuls(sel_probs(sel_scores(q_ref, cr_sizes[0]), 0))
        sel_choose(ocl, imp4, 0)()

    bias_ref[...] = biasn_ref[...]
    oc = ocn_ref[...]

    qa_ref[0:DH, :] = jnp.concatenate([q_ref[h] for h in range(HPG)], axis=1)
    qa_ref[DH:KILL_LANE, :] = jnp.zeros((KILL_LANE - DH, NQ), BF16)

    def set_bias_rows(grp):
        bias_rows = bias_ref[pl.ds(pl.multiple_of(grp * BLK_PER_TILE, BLK_PER_TILE), BLK_PER_TILE), :]
        qa_ref[DH:DH + BLK_PER_TILE, :] = jnp.concatenate([bias_rows] * HPG, axis=1)

    def set_flag_rows(ref):
        r = lax.broadcasted_iota(jnp.int32, (LANES - KILL_LANE, NQ), 0)
        rows = jnp.where(r == 0, NEG, 0.0) if ref is None else jnp.where(r == 0, NEG, jnp.where(r == 1, -ref, 0.0))
        qa_ref[KILL_LANE:LANES, :] = rows.astype(BF16)

    def scores(kt, s_ref):
        k0 = pl.multiple_of(kt * TK, TK)
        set_bias_rows(kt // 2)
        s_ref[...] = jnp.dot(ks_ref[pl.ds(k0, TK), :], qa_ref[...], preferred_element_type=F32)

    def accumulate(kt, s_ref):
        k0 = pl.multiple_of(kt * TK, TK)
        m_old = m_ref[...]
        m_new = jnp.maximum(m_old, jnp.max(s_ref[...], axis=0, keepdims=True))
        p_ref[0:TK, :] = jnp.exp2(s_ref[...] - m_new).astype(BF16)
        acc_ref[...] = (jnp.exp2(m_old - m_new) * acc_ref[...]
                        + jnp.dot(vs_ref[:, pl.ds(k0, TK)], p_ref[0:TK, :], preferred_element_type=F32))
        m_ref[...] = m_new

    def mask_own_keys(s_ref):
        band = pl.ds(pl.multiple_of(s0 - n_full * TK, QB), QB)
        s_ref[band, :] = jnp.where(band_row <= band_tq, s_ref[band, :], NEG)

    n_full = s0 // TK

    m_ref[...] = jnp.full((1, NQ), NEG, F32)
    acc_ref[...] = jnp.zeros((VROWS, NQ), F32)
    set_flag_rows(None)

    def first_region(cr):
        sc = sel_scores(qn_ref, cr)
        sw = jnp.dot(kw_ref[pl.ds(pl.multiple_of(s0, QB), WIN_KEYS), :], qa_ref[...],
                     preferred_element_type=F32)
        scores(n_full, sa_ref)

        pc = sel_probs(sc, b + 1)
        ocl, imp4 = sel_matmuls(pc)
        redo_if_tied = sel_choose(ocl, imp4, b + 1)

        sw = jnp.concatenate([jnp.where(band_row > band_tq, sw[0:QB], NEG),
                              sw[QB:WINDOW],
                              jnp.where(band_row <= band_tq, sw[WINDOW:WIN_KEYS], NEG)], axis=0)
        pw = jnp.exp2(sw - jnp.max(sw, axis=0, keepdims=True)).astype(BF16)
        mask_own_keys(sa_ref)

        ow = jnp.dot(vw_ref[:, pl.ds(pl.multiple_of(s0, QB), WIN_KEYS)], pw, preferred_element_type=F32)
        owin_ref[...] = ow[0:DH, :] * (1.0 / ow[DH:DH + 1, :])
        accumulate(n_full, sa_ref)

        m = m_ref[...]
        c0 = m.astype(BF16).astype(F32)
        c_ref[...] = c0
        acc_ref[...] = acc_ref[...] * jnp.exp2(m - c0)
        rise_ref[...] = jnp.zeros((1, NQ), F32)
        redo_if_tied()

    for i, cr in enumerate(cr_sizes):
        fits = cr_need <= cr
        if i > 0:
            fits = fits & (cr_need > cr_sizes[i - 1])
        pl.when(fits)(functools.partial(first_region, cr))

    def older_keys(grp, k0, rows):
        c = c_ref[...]
        set_bias_rows(grp)
        set_flag_rows(c)
        s = jnp.dot(ks_ref[pl.ds(k0, rows), :], qa_ref[...], preferred_element_type=F32)
        mt = jnp.max(s, axis=0, keepdims=True)
        p_ref[0:rows, :] = jnp.exp2(s).astype(BF16)
        pv = jnp.dot(vs_ref[:, pl.ds(k0, rows)], p_ref[0:rows, :], preferred_element_type=F32)
        c_new = (c + jnp.maximum(mt, 0.0)).astype(BF16).astype(F32)
        acc_ref[...] = (acc_ref[...] + pv) * jnp.exp2(c - c_new)
        c_ref[...] = c_new
        rise_ref[...] = jnp.maximum(rise_ref[...], mt)

    def older_pair(jj, carry):
        older_keys(jj, pl.multiple_of(jj * TK_OLD, TK_OLD), TK_OLD)
        return carry

    lax.fori_loop(0, n_full // 2, older_pair, 0)

    @pl.when(n_full % 2 == 1)
    def _():
        older_keys(n_full // 2, pl.multiple_of((n_full - 1) * TK, TK), TK)

    @pl.when(jnp.max(rise_ref[...]) > REF_SLACK)
    def _():
        m_ref[...] = jnp.full((1, NQ), NEG, F32)
        acc_ref[...] = jnp.zeros((VROWS, NQ), F32)
        set_flag_rows(None)

        def two_pass_tile(kt, carry):
            scores(kt, sa_ref)

            @pl.when(kt == n_full)
            def _():
                mask_own_keys(sa_ref)
            accumulate(kt, sa_ref)
            return carry

        lax.fori_loop(0, n_full + 1, two_pass_tile, 0)

    acc = acc_ref[...]
    osel = acc[0:DH, :] * (1.0 / acc[DH:DH + 1, :])
    owin = owin_ref[...]

    heads = []
    for h in range(HPG):
        sl = slice(h * QB, (h + 1) * QB)
        heads.append(gt_ref[3 * h:3 * h + 1, :] * oc[:, sl]
                     + gt_ref[3 * h + 1:3 * h + 2, :] * osel[:, sl]
                     + gt_ref[3 * h + 2:3 * h + 3, :] * owin[:, sl])
    out_ref[...] = jnp.concatenate(heads, axis=0).T.astype(out_ref.dtype)


def _nsa(q_t, kcmp, vcmp_t, ks, vs_t, kw, vw_t, gt, ovl_t):
    S = ks.shape[1]
    NC, NB = S // CMP_STRIDE, S // SLC_BLOCK
    n_qb = S // QB
    grp = lambda g, b: (g, 0, 0)
    return pl.pallas_call(
        functools.partial(_nsa_kernel, S=S),
        grid=(NSA_GROUPS, n_qb),
        in_specs=[pl.BlockSpec((HPG, DH, QB), lambda g, b: (g, 0, b)),
                  pl.BlockSpec((HPG, DH, QB), lambda g, b: (g, 0, jnp.minimum(b + 1, n_qb - 1))),
                  pl.BlockSpec((None, NC, LANES), grp),
                  pl.BlockSpec((None, VROWS, NC), grp),
                  pl.BlockSpec((None, S, LANES), grp, pipeline_mode=pl.Buffered(1)),
                  pl.BlockSpec((None, VROWS, S), grp, pipeline_mode=pl.Buffered(1)),
                  pl.BlockSpec((None, WINDOW + S, LANES), grp, pipeline_mode=pl.Buffered(1)),
                  pl.BlockSpec((None, VROWS, WINDOW + S), grp, pipeline_mode=pl.Buffered(1)),
                  pl.BlockSpec((None, GATE_ROWS, QB), lambda g, b: (g, 0, b)),
                  pl.BlockSpec((NB, NC), lambda g, b: (0, 0))],
        out_specs=pl.BlockSpec((QB, HPG * DH), lambda g, b: (b, g)),
        out_shape=jax.ShapeDtypeStruct((S, NSA_HEADS * DH), BF16),
        scratch_shapes=[pltpu.VMEM((LANES, HPG * QB), BF16),
                        pltpu.VMEM((LANES, HPG * QB), BF16),
                        pltpu.VMEM((NB, QB), BF16),
                        pltpu.VMEM((NB, QB), BF16),
                        pltpu.VMEM((DH, HPG * QB), F32),
                        pltpu.VMEM((1, HPG * QB), F32),
                        pltpu.VMEM((VROWS, HPG * QB), F32),
                        pltpu.VMEM((TK, HPG * QB), F32),
                        pltpu.VMEM((TK_OLD, HPG * QB), BF16),
                        pltpu.VMEM((1, HPG * QB), F32),
                        pltpu.VMEM((1, HPG * QB), F32),
                        pltpu.VMEM((DH, HPG * QB), F32)],
        compiler_params=_params(("arbitrary", "arbitrary")), name="nsa",
    )(q_t, q_t, kcmp, vcmp_t, ks, vs_t, kw, vw_t, gt, ovl_t)


def _ret_kernel(q_ref, k_ref, v_ref, g_ref, dmat_ref, xi_ref, zeta_ref, dec_ref, o_ref, state_ref, *, tm):
    @pl.when(pl.program_id(0) == 0)
    def _():
        state_ref[...] = jnp.zeros_like(state_ref)

    for j in range(tm // RET_C):
        rows = slice(j * RET_C, (j + 1) * RET_C)
        for h in range(RET_HEADS):
            cols = slice(h * RET_D, (h + 1) * RET_D)
            qc, kc, vc = q_ref[rows, cols], k_ref[rows, cols], v_ref[rows, cols]
            st = state_ref[h]
            st_hi = st.astype(BF16)
            st_lo = (st - st_hi.astype(F32)).astype(BF16)
            sc = lax.dot_general(qc, kc, NT_DIMS, preferred_element_type=F32) * dmat_ref[h]
            inner = jnp.dot(sc.astype(BF16), vc, preferred_element_type=F32)
            cross = (jnp.dot(qc, st_hi, preferred_element_type=F32)
                     + jnp.dot(qc, st_lo, preferred_element_type=F32)) * xi_ref[h]
            kz = (kc.astype(F32) * zeta_ref[h]).astype(BF16)
            state_ref[h] = dec_ref[h] * st + lax.dot_general(kz, vc, TN_DIMS, preferred_element_type=F32)
            o = inner + cross
            o = o * lax.rsqrt(jnp.mean(o * o, axis=-1, keepdims=True) + EPS)
            gate = g_ref[rows, cols].astype(F32)
            o_ref[rows, cols] = (gate * _sigmoid(gate) * o).astype(o_ref.dtype)


def _retention(rq, rk, rv, rg, dmat, xi, zeta, dec, tm=512):
    S = rq.shape[0]
    W = RET_HEADS * RET_D
    row = lambda i: (i, 0)
    cst = lambda i: (0, 0, 0)
    blk = pl.BlockSpec((tm, W), row)
    tbl = pl.BlockSpec((RET_HEADS, RET_C, RET_D), cst)
    return pl.pallas_call(
        functools.partial(_ret_kernel, tm=tm),
        grid=(S // tm,),
        in_specs=[blk, blk, blk, blk, tbl, tbl, tbl, tbl],
        out_specs=blk,
        out_shape=jax.ShapeDtypeStruct((S, W), BF16),
        scratch_shapes=[pltpu.VMEM((RET_HEADS, RET_D, RET_D), F32)],
        compiler_params=_params(("arbitrary",)), name="retention",
    )(rq, rk, rv, rg, dmat, xi, zeta, dec)


def _mix_kernel(x_ref, g_ref, a_ref, r_ref, wg_ref, wpa_ref, wpb_ref, wo_ref, o_ref):
    x = x_ref[...]
    h = _rms(x, g_ref[...]).astype(BF16)
    ga = _sigmoid(jnp.dot(h, wg_ref[:, 0:D_MODEL], preferred_element_type=F32))
    gb = _sigmoid(jnp.dot(h, wg_ref[:, D_MODEL:2 * D_MODEL], preferred_element_type=F32))
    mix = (ga * jnp.dot(a_ref[...], wpa_ref[...], preferred_element_type=F32)
           + gb * jnp.dot(r_ref[...], wpb_ref[...], preferred_element_type=F32))
    o_ref[...] = x + jnp.dot(mix.astype(BF16), wo_ref[...], preferred_element_type=F32)


def _mix(x, g, nsa_out, ret_out, wg, wpa, wpb, wo, tm=512):
    S = x.shape[0]
    row = lambda i: (i, 0)
    cst = lambda i: (0, 0)
    return pl.pallas_call(
        _mix_kernel,
        grid=(S // tm,),
        in_specs=[pl.BlockSpec((tm, D_MODEL), row), pl.BlockSpec((1, D_MODEL), cst),
                  pl.BlockSpec((tm, NSA_HEADS * DH), row), pl.BlockSpec((tm, RET_HEADS * RET_D), row),
                  pl.BlockSpec((D_MODEL, 2 * D_MODEL), cst), pl.BlockSpec((512, D_MODEL), cst),
                  pl.BlockSpec((512, D_MODEL), cst), pl.BlockSpec((D_MODEL, D_MODEL), cst)],
        out_specs=pl.BlockSpec((tm, D_MODEL), row),
        out_shape=jax.ShapeDtypeStruct((S, D_MODEL), F32),
        compiler_params=_params(("arbitrary",)), name="mix",
    )(x, g, nsa_out, ret_out, wg, wpa, wpb, wo)


def _mlp_kernel(x_ref, g_ref, wu_ref, wd_ref, gf_ref, o_ref, *, hc):
    x = x_ref[...]
    h = _rms(x, g_ref[...]).astype(BF16)
    acc = x
    for c in range(MLP_HIDDEN // hc):
        u = jnp.maximum(jnp.dot(h, wu_ref[:, c * hc:(c + 1) * hc], preferred_element_type=F32), 0.0)
        acc = acc + jnp.dot((u * u).astype(BF16), wd_ref[c * hc:(c + 1) * hc, :], preferred_element_type=F32)
    o_ref[...] = _rms(acc, gf_ref[...])


def _mlp(x, g, wu, wd, gf, tm=512, hc=1024):
    S = x.shape[0]
    row = lambda i: (i, 0)
    cst = lambda i: (0, 0)
    return pl.pallas_call(
        functools.partial(_mlp_kernel, hc=hc),
        grid=(S // tm,),
        in_specs=[pl.BlockSpec((tm, D_MODEL), row), pl.BlockSpec((1, D_MODEL), cst),
                  pl.BlockSpec((D_MODEL, MLP_HIDDEN), cst), pl.BlockSpec((MLP_HIDDEN, D_MODEL), cst),
                  pl.BlockSpec((1, D_MODEL), cst)],
        out_specs=pl.BlockSpec((tm, D_MODEL), row),
        out_shape=jax.ShapeDtypeStruct((S, D_MODEL), F32),
        compiler_params=_params(("arbitrary",)), name="mlp",
    )(x, g, wu, wd, gf)


def _rope_freqs():
    half = ROPE_DIM // 2
    inv = jnp.exp(-math.log(ROPE_THETA) * jnp.arange(half, dtype=F32) * 2.0 / ROPE_DIM)
    hr = RET_D // 2
    inv_r = jnp.exp(-math.log(RET_BASE) * jnp.arange(hr, dtype=F32) * 2.0 / RET_D)
    return jnp.concatenate([inv_r, inv, jnp.zeros((LANES - hr - half,), F32)])[None, :]


def _decay_tables():
    H, C = RET_HEADS, RET_C
    log_g = jnp.log(1.0 - jnp.exp2(-5.0 - jnp.arange(H, dtype=F32)))
    n = jnp.arange(C, dtype=F32)
    rel = n[:, None] - n[None, :]
    dmat = jnp.where(rel >= 0, jnp.exp(log_g[:, None, None] * jnp.maximum(rel, 0.0)), 0.0)
    xi = jnp.exp(log_g[:, None] * (n + 1.0))
    zeta = jnp.exp(log_g[:, None] * (C - 1.0 - n))
    dec = jnp.exp(log_g * C)
    bc = lambda a: jnp.broadcast_to(a, (H, C, RET_D))
    return dmat, bc(xi[:, :, None]), bc(zeta[:, :, None]), bc(dec[:, None, None])


def _overlap_t(S):
    n_cmp = (S - CMP_BLOCK) // CMP_STRIDE + 1
    n_slc = S // SLC_BLOCK
    cs = np.arange(n_cmp)[:, None] * CMP_STRIDE
    ss = np.arange(n_slc)[None, :] * SLC_BLOCK
    ov = np.clip(np.minimum(cs + CMP_BLOCK, ss + SLC_BLOCK) - np.maximum(cs, ss), 0, None) / CMP_BLOCK
    out = np.zeros((n_slc, S // CMP_STRIDE), np.float32)
    out[:, :n_cmp] = ov.T
    return jnp.asarray(out, BF16)


def _v_rows(v_t):
    S = v_t.shape[1]
    return jnp.concatenate([v_t, jnp.ones((1, S), v_t.dtype), jnp.zeros((VROWS - DH - 1, S), v_t.dtype)], axis=0)


_IN_OFFSETS = np.cumsum((0, 512, 128, 128, 128, 128, 128, 128, 24, 512, 512, 512, 512, 1024, 1024))


_W_MOVES = tuple((int(_IN_OFFSETS[i]), dst, int(_IN_OFFSETS[i + 1] - _IN_OFFSETS[i])) for i, dst in
                 ((0, 0), (1, 512), (3, 640), (5, 768), (2, 896), (4, 1024), (6, 1152),
                  (8, 1408), (9, 1920), (10, 2432), (11, 2944)))


def _in_col(wi, i):
    return wi[:, _IN_OFFSETS[i]:_IN_OFFSETS[i + 1]]


def _branches(x, positions, norm_mix, w_in, cmp_pos_k, cmp_pos_v, cmp_k_w1, cmp_k_w2, cmp_v_w1, cmp_v_w2):
    B, S, _ = x.shape
    assert B == 1 and norm_mix.shape[0] == 1 and S % TK == 0 and S >= max(WIN_KEYS, SLC_COUNT * SLC_BLOCK)
    xs = x[0]
    q_t, xh, ks, kw, vs_t, vw_t, gt, rq, rk, rv, rg = _proj(
        xs, norm_mix, w_in, jnp.broadcast_to(positions[0].astype(F32)[:, None], (S, LANES)), _rope_freqs())

    w1 = jnp.stack([cmp_k_w1[0], cmp_v_w1[0]]).astype(BF16)
    pe = jnp.stack([cmp_pos_k[0], cmp_pos_v[0]]).reshape(2, 1, CMP_BLOCK * DH)
    pe = jnp.broadcast_to(pe, (2, BF16_ROWS, CMP_BLOCK * DH)).astype(BF16)
    w2 = jnp.pad(jnp.stack([cmp_k_w2[0], cmp_v_w2[0]]), ((0, 0), (0, 0), (0, LANES - DH))).astype(BF16)
    cmp = _compress(xh, w1, pe, w2)
    kcmp = cmp[0:NSA_GROUPS].astype(BF16)
    vcmp_t = jax.vmap(_v_rows)(cmp[NSA_GROUPS:, :, 0:DH].transpose(0, 2, 1).astype(BF16))

    vw_t = jnp.pad(vw_t, ((0, 0), (0, 0), (WINDOW, 0)))
    pad_keys = jnp.zeros((NSA_GROUPS, WINDOW, LANES), BF16).at[:, :, KILL_LANE].set(1.0)
    kw = jnp.concatenate([pad_keys, kw], axis=1)
    nsa_out = _nsa(q_t, kcmp, vcmp_t, ks, vs_t, kw, vw_t, gt, _overlap_t(S))

    ret_out = _retention(rq, rk, rv, rg, *_decay_tables())
    return nsa_out, ret_out


def kernel(x, positions, norm_mix, w_in, cmp_pos_k, cmp_pos_v, cmp_k_w1, cmp_k_w2, cmp_v_w1, cmp_v_w2,
           w_proj_a, w_proj_b, w_out, norm_mlp, w_up, w_down, norm_final):
    nsa_out, ret_out = _branches(x, positions, norm_mix, w_in, cmp_pos_k, cmp_pos_v,
                                 cmp_k_w1, cmp_k_w2, cmp_v_w1, cmp_v_w2)
    xs = x[0]
    w_gate = jnp.concatenate([_in_col(w_in[0], 12), _in_col(w_in[0], 13)], axis=1).astype(BF16)
    x1 = _mix(xs, norm_mix, nsa_out, ret_out, w_gate, w_proj_a[0].astype(BF16), w_proj_b[0].astype(BF16),
              w_out[0].astype(BF16))
    y = _mlp(x1, norm_mlp, w_up[0].astype(BF16), w_down[0].astype(BF16), norm_final[None, :])
    return y[None]
```

```python
import functools
import math

import numpy as np
import jax
import jax.numpy as jnp
from jax import lax
from jax.experimental import pallas as pl
from jax.experimental.pallas import tpu as pltpu

F32 = jnp.float32
BF16 = jnp.bfloat16

D_MODEL = 1024
NSA_HEADS = 8
NSA_GROUPS = 2
HPG = NSA_HEADS // NSA_GROUPS
DH = 64
CMP_BLOCK = 32
CMP_STRIDE = 16
CMP_HIDDEN = 256
SLC_BLOCK = 64
SLC_COUNT = 16
WINDOW = 512
QB = 256
ROPE_THETA = 500000.0
ROPE_DIM = DH // 4
RET_HEADS = 4
RET_D = 128
RET_C = 128
RET_BASE = 10000.0
MLP_HIDDEN = 4 * D_MODEL
EPS = 1e-6
NEG = -1e30

LANES = 128
TK = 1024
TK_OLD = 2 * TK
BLK_PER_TILE = TK_OLD // SLC_BLOCK
IMP_ROWS = 64
KILL_LANE = DH + BLK_PER_TILE
REF_LANE = KILL_LANE + 1
REF_SLACK = 64.0
LOG2E = math.log2(math.e)
VROWS = 80
BF16_ROWS = 16
GATE_ROWS = 16
SLC_SHIFT = SLC_BLOCK.bit_length() - 1
WIN_KEYS = WINDOW + QB
VMEM_LIMIT = 56 * 1024 * 1024

NT_DIMS = (((1,), (1,)), ((), ()))
TN_DIMS = (((0,), (0,)), ((), ()))


def _sigmoid(v):
    return 1.0 / (1.0 + jnp.exp(-v))


def _rms(x, g):
    return x * lax.rsqrt(jnp.mean(x * x, axis=-1, keepdims=True) + EPS) * g


def _params(sem, flags=None):
    return pltpu.CompilerParams(dimension_semantics=sem, vmem_limit_bytes=VMEM_LIMIT, flags=flags)


_C_ROT = (0, 896)
_C_V = (896, 1280)
_C_G = (1280, 1408)
_C_RQ = (1408, 1920)
_C_RK = (1920, 2432)
_C_RV = (2432, 2944)
_C_RG = (2944, 3456)
PROJ_W = 3456


def _proj_kernel(x_ref, g_ref, win_ref, pos_ref, freq_ref,
                 q_ref, xh_ref, ks_ref, kw_ref, vs_ref, vw_ref, gt_ref, rq_ref, rk_ref, rv_ref, rg_ref,
                 kv_ref, w_ref, *, tm):
    @pl.when(pl.program_id(0) == 0)
    def _():
        for src, dst, n in _W_MOVES:
            w_ref[dst:dst + n, :] = win_ref[src:src + n, :].astype(BF16)
        n_gate = 3 * HPG
        g0 = _IN_OFFSETS[7]
        zeros = lambda n: jnp.zeros((n, D_MODEL), F32)
        gates = jnp.concatenate([win_ref[g0:g0 + n_gate, :], zeros(GATE_ROWS - n_gate),
                                 win_ref[g0 + n_gate:g0 + 2 * n_gate, :], zeros(LANES - GATE_ROWS - n_gate)],
                                axis=0)
        w_ref[_C_G[0]:_C_G[1], :] = gates.astype(BF16)

    h = _rms(x_ref[...], g_ref[...]).astype(BF16)
    lane = lax.broadcasted_iota(jnp.int32, (tm, LANES), 1)
    lo = lane < DH

    ang = pos_ref[...] * freq_ref[...]
    c, s = jnp.cos(ang), jnp.sin(ang)
    half_n = ROPE_DIM // 2
    cr = jnp.where(lo, c, pltpu.roll(c, DH, 1))
    sr = jnp.where(lo, -s, pltpu.roll(s, DH, 1))
    dim = lane & (DH - 1)
    first, second = dim < half_n, (dim >= half_n) & (dim < ROPE_DIM)
    c1 = jnp.where(lo, pltpu.roll(c, DH, 1), c)
    s1 = jnp.where(lo, pltpu.roll(s, DH, 1), s)
    c2 = jnp.where(lo, pltpu.roll(c, DH + half_n, 1), pltpu.roll(c, half_n, 1))
    s2 = jnp.where(lo, pltpu.roll(s, DH + half_n, 1), pltpu.roll(s, half_n, 1))
    cn = jnp.where(first, c1, jnp.where(second, c2, 1.0))
    sa = jnp.where(first, -s1, 0.0)
    sb = jnp.where(second, s2, 0.0)

    def dot(c):
        return lax.dot_general(h, w_ref[c[0]:c[1], :], NT_DIMS, preferred_element_type=F32)

    def tile(y, i):
        return y[:, i * LANES:(i + 1) * LANES]

    def rope_n(t):
        return t * cn + pltpu.roll(t, LANES - half_n, 1) * sa + pltpu.roll(t, half_n, 1) * sb

    def rope_r(t):
        return t * cr + pltpu.roll(t, RET_D // 2, 1) * sr

    def split(t):
        return jnp.where(lo, t, 0.0), jnp.where(lo, pltpu.roll(t, DH, 1), 0.0)

    def v_rows(dst_ref, t):
        tt = t.T
        tail = jnp.where(lax.broadcasted_iota(jnp.int32, (VROWS - DH, tm), 0) == 0, 1.0, 0.0).astype(BF16)
        for g in range(NSA_GROUPS):
            dst_ref[g, 0:DH, :] = tt[g * DH:(g + 1) * DH, :].astype(BF16)
            dst_ref[g, DH:VROWS, :] = tail

    yr = dot(_C_ROT)
    for i in range(4):
        tt = (rope_n(tile(yr, i)) * (DH ** -0.5 * LOG2E)).T
        q_ref[2 * i] = tt[0:DH, :].astype(BF16)
        q_ref[2 * i + 1] = tt[DH:2 * DH, :].astype(BF16)
    row = pl.program_id(0) * tm + lax.broadcasted_iota(jnp.int32, (tm, LANES), 0)
    blk = (row >> SLC_SHIFT) & (BLK_PER_TILE - 1)
    onehot = jnp.where(((lane - DH) == blk) | (lane == REF_LANE), 1.0, 0.0)
    a, b = split(rope_n(tile(yr, 5)))
    ks_ref[0] = jnp.where(lo, a, onehot).astype(BF16)
    ks_ref[1] = jnp.where(lo, b, onehot).astype(BF16)
    a, b = split(rope_n(tile(yr, 6)))
    kw_ref[0] = a.astype(BF16)
    kw_ref[1] = b.astype(BF16)

    yv = dot(_C_V)
    v_rows(vs_ref, tile(yv, 1))
    v_rows(vw_ref, tile(yv, 2))
    gt = _sigmoid(dot(_C_G)).T
    for g in range(NSA_GROUPS):
        gt_ref[g] = gt[g * GATE_ROWS:(g + 1) * GATE_ROWS, :]

    kv_ref[0] = rope_n(tile(yr, 4))
    kv_ref[1] = tile(yv, 0)
    lo_r = lax.broadcasted_iota(jnp.int32, (tm // CMP_STRIDE, LANES), 1) < DH
    for typ in range(2):
        for u in range(CMP_STRIDE // 2):
            pa = kv_ref[typ, pl.ds(2 * u, tm // CMP_STRIDE, stride=CMP_STRIDE), :]
            pb = kv_ref[typ, pl.ds(2 * u + 1, tm // CMP_STRIDE, stride=CMP_STRIDE), :]
            cols = slice(u * LANES, (u + 1) * LANES)
            xh_ref[2 * typ, :, cols] = jnp.where(lo_r, pa, pltpu.roll(pb, DH, 1)).astype(BF16)
            xh_ref[2 * typ + 1, :, cols] = jnp.where(lo_r, pltpu.roll(pa, DH, 1), pb).astype(BF16)

    yq = dot(_C_RQ)
    yk = dot(_C_RK)
    for i in range(RET_HEADS):
        sl = slice(i * LANES, (i + 1) * LANES)
        rq_ref[:, sl] = rope_r(tile(yq, i)).astype(BF16)
        rk_ref[:, sl] = (rope_r(tile(yk, i)) * (RET_D ** -0.5)).astype(BF16)
    rv_ref[...] = dot(_C_RV).astype(BF16)
    rg_ref[...] = dot(_C_RG).astype(BF16)


def _proj(x, g, w, pos, freq, tm=512):
    S = x.shape[0]
    row = lambda i: (i, 0)
    const = lambda i: (0, 0)
    mid = lambda i: (0, i, 0)
    last = lambda i: (0, 0, i)
    W = RET_HEADS * RET_D
    out_shape = (
        jax.ShapeDtypeStruct((NSA_HEADS, DH, S), BF16),
        jax.ShapeDtypeStruct((2 * NSA_GROUPS, S // CMP_STRIDE, CMP_STRIDE * DH), BF16),
        jax.ShapeDtypeStruct((NSA_GROUPS, S, LANES), BF16),
        jax.ShapeDtypeStruct((NSA_GROUPS, S, LANES), BF16),
        jax.ShapeDtypeStruct((NSA_GROUPS, VROWS, S), BF16),
        jax.ShapeDtypeStruct((NSA_GROUPS, VROWS, S), BF16),
        jax.ShapeDtypeStruct((NSA_GROUPS, GATE_ROWS, S), F32),
        jax.ShapeDtypeStruct((S, W), BF16),
        jax.ShapeDtypeStruct((S, W), BF16),
        jax.ShapeDtypeStruct((S, W), BF16),
        jax.ShapeDtypeStruct((S, W), BF16),
    )
    out_specs = (
        pl.BlockSpec((NSA_HEADS, DH, tm), last),
        pl.BlockSpec((2 * NSA_GROUPS, tm // CMP_STRIDE, CMP_STRIDE * DH), mid),
        pl.BlockSpec((NSA_GROUPS, tm, LANES), mid),
        pl.BlockSpec((NSA_GROUPS, tm, LANES), mid),
        pl.BlockSpec((NSA_GROUPS, VROWS, tm), last),
        pl.BlockSpec((NSA_GROUPS, VROWS, tm), last),
        pl.BlockSpec((NSA_GROUPS, GATE_ROWS, tm), last),
        pl.BlockSpec((tm, W), row),
        pl.BlockSpec((tm, W), row),
        pl.BlockSpec((tm, W), row),
        pl.BlockSpec((tm, W), row),
    )
    return pl.pallas_call(
        functools.partial(_proj_kernel, tm=tm),
        grid=(S // tm,),
        in_specs=[pl.BlockSpec((tm, D_MODEL), row), pl.BlockSpec((1, D_MODEL), const),
                  pl.BlockSpec((None,) + w.shape[1:], lambda i: (0, 0, 0), pipeline_mode=pl.Buffered(1)),
                  pl.BlockSpec((tm, LANES), row),
                  pl.BlockSpec((1, LANES), const)],
        out_specs=out_specs, out_shape=out_shape,
        scratch_shapes=[pltpu.VMEM((2, tm, LANES), F32),
                        pltpu.VMEM((PROJ_W, D_MODEL), BF16)],
        compiler_params=_params(("arbitrary",)), name="proj",
    )(x, g, w, pos, freq)


def _compress_kernel(x_ref, w1_ref, pe_ref, w2_ref, o_ref):
    x = x_ref[...]
    half = CMP_STRIDE * DH
    a = jnp.dot(x, w1_ref[0:half, :], preferred_element_type=F32)
    b = jnp.dot(x, w1_ref[half:2 * half, :], preferred_element_type=F32)
    peb = jnp.dot(pe_ref[...], w1_ref[...], preferred_element_type=F32)[0:1, :]
    hid = a + pltpu.roll(b, x.shape[0] - 1, 0) + peb
    hid = hid * _sigmoid(hid)
    o_ref[...] = jnp.dot(hid.astype(BF16), w2_ref[...], preferred_element_type=F32)


def _compress(xh, w1, pe, w2):
    n_half = xh.shape[1]
    return pl.pallas_call(
        _compress_kernel,
        grid=(2 * NSA_GROUPS,),
        in_specs=[pl.BlockSpec((None, n_half, CMP_STRIDE * DH), lambda i: (i, 0, 0)),
                  pl.BlockSpec((None, CMP_BLOCK * DH, CMP_HIDDEN), lambda i: (i // NSA_GROUPS, 0, 0)),
                  pl.BlockSpec((None, BF16_ROWS, CMP_BLOCK * DH), lambda i: (i // NSA_GROUPS, 0, 0)),
                  pl.BlockSpec((None, CMP_HIDDEN, LANES), lambda i: (i // NSA_GROUPS, 0, 0))],
        out_specs=pl.BlockSpec((None, n_half, LANES), lambda i: (i, 0, 0)),
        out_shape=jax.ShapeDtypeStruct((2 * NSA_GROUPS, n_half, LANES), F32),
        compiler_params=_params(("arbitrary",)), name="compress",
    )(xh, w1, pe, w2)


def _nsa_kernel(q_ref, qn_ref, kcmp_ref, vcmp_ref, ks_ref, vs_ref, kw_ref, vw_ref, gt_ref, ovl_ref,
                out_ref, qa_ref, qs_ref, bias_ref, biasn_ref, ocn_ref, m_ref, acc_ref,
                sa_ref, p_ref, c_ref, rise_ref, owin_ref, *, S):
    NC = S // CMP_STRIDE
    NB = S // SLC_BLOCK
    NQ = HPG * QB
    b = pl.program_id(1)
    s0 = b * QB
    lane_q = lax.broadcasted_iota(jnp.int32, (1, NQ), 1) & (QB - 1)
    band_row = lax.broadcasted_iota(jnp.int32, (QB, NQ), 0)
    band_tq = lax.broadcasted_iota(jnp.int32, (QB, NQ), 1) & (QB - 1)

    def sel_scores(qsrc_ref, cr):
        qs_ref[0:DH, :] = jnp.concatenate([qsrc_ref[h] for h in range(HPG)], axis=1)
        qs_ref[DH:LANES, :] = jnp.zeros((LANES - DH, NQ), BF16)
        return jnp.dot(kcmp_ref[0:cr, :], qs_ref[...], preferred_element_type=F32)

    def sel_probs(sc, blk):
        c_end = lax.broadcasted_iota(jnp.int32, sc.shape, 0) * CMP_STRIDE + (CMP_BLOCK - 1)
        sc = jnp.where(c_end <= blk * QB + lane_q, sc, NEG)
        return jnp.exp2(sc - jnp.max(sc, axis=0, keepdims=True)).astype(BF16)

    def sel_matmuls(pc):
        cr = pc.shape[0]
        ocl = jnp.dot(vcmp_ref[:, 0:cr], pc, preferred_element_type=F32)
        ir, ic = IMP_ROWS, 4 * IMP_ROWS
        if cr % ic:
            return ocl, jnp.dot(ovl_ref[0:cr // 4, 0:cr], pc, preferred_element_type=F32)
        parts = []
        for i in range(cr // ic):
            c0 = max(i - 1, 0) * ic
            parts.append(jnp.dot(ovl_ref[i * ir:(i + 1) * ir, c0:(i + 1) * ic], pc[c0:(i + 1) * ic, :],
                                 preferred_element_type=F32))
        return ocl, jnp.concatenate(parts, axis=0)

    def sel_choose(ocl, imp4, blk):
        t_row = blk * QB + lane_q
        any_valid = jnp.where(t_row >= CMP_BLOCK - 1, 1.0, 0.0)
        rc = any_valid / ocl[DH:DH + 1, :]
        ocn_ref[...] = ocl[0:DH, :] * rc

        imp4 = imp4 * rc
        imp = imp4[:, 0:QB] + imp4[:, QB:2 * QB] + imp4[:, 2 * QB:3 * QB] + imp4[:, 3 * QB:4 * QB]
        if imp.shape[0] < NB:
            imp = jnp.concatenate([imp, jnp.zeros((NB - imp.shape[0], QB), F32)], axis=0)

        jidx = lax.broadcasted_iota(jnp.int32, (NB, QB), 0)
        cur = (blk * QB + lax.broadcasted_iota(jnp.int32, (1, QB), 1)) >> SLC_SHIFT
        forced = (jidx == 0) | (jidx == cur) | (jidx == cur - 1)
        future = jidx > cur
        cand = jnp.where(forced | future, -jnp.inf, imp)

        def select(break_ties):
            v = cand
            for _ in range(SLC_COUNT - 3):
                mx = jnp.max(v, axis=0, keepdims=True)
                hit = v == mx
                if break_ties:
                    hit = jidx == jnp.min(jnp.where(hit, jidx, NB), axis=0, keepdims=True)
                v = jnp.where(hit, -jnp.inf, v)
            return jnp.where((v == -jnp.inf) & (jidx <= cur), 0.0, NEG)

        bias_t = select(False)
        n_sel = jnp.sum(jnp.where(bias_t == 0.0, 1.0, 0.0), axis=0, keepdims=True)
        biasn_ref[...] = bias_t.astype(BF16)

        def redo_if_tied():
            @pl.when(jnp.max(n_sel) > SLC_COUNT)
            def _():
                biasn_ref[...] = select(True).astype(BF16)
        return redo_if_tied

    cr_need = jnp.minimum((b + 2) * (QB // CMP_STRIDE) - 1, NC)
    cr_sizes = (NC // 2, NC) if (NC // 2) % (4 * IMP_ROWS) == 0 else (NC,)

    @pl.when(b == 0)
    def _():
        ocl, imp4 = sel_matmuls(sel_probs(sel_scores(q_ref, cr_sizes[0]), 0))
        sel_choose(ocl, imp4, 0)()

    bias_ref[...] = biasn_ref[...]
    oc = ocn_ref[...]

    qa_ref[0:DH, :] = jnp.concatenate([q_ref[h] for h in range(HPG)], axis=1)
    qa_ref[DH:KILL_LANE, :] = jnp.zeros((KILL_LANE - DH, NQ), BF16)

    def set_bias_rows(grp):
        bias_rows = bias_ref[pl.ds(pl.multiple_of(grp * BLK_PER_TILE, BLK_PER_TILE), BLK_PER_TILE), :]
        qa_ref[DH:DH + BLK_PER_TILE, :] = jnp.concatenate([bias_rows] * HPG, axis=1)

    def set_flag_rows(ref):
        r = lax.broadcasted_iota(jnp.int32, (LANES - KILL_LANE, NQ), 0)
        rows = jnp.where(r == 0, NEG, 0.0) if ref is None else jnp.where(r == 0, NEG, jnp.where(r == 1, -ref, 0.0))
        qa_ref[KILL_LANE:LANES, :] = rows.astype(BF16)

    def scores(kt, s_ref):
        k0 = pl.multiple_of(kt * TK, TK)
        set_bias_rows(kt // 2)
        s_ref[...] = jnp.dot(ks_ref[pl.ds(k0, TK), :], qa_ref[...], preferred_element_type=F32)

    def accumulate(kt, s_ref):
        k0 = pl.multiple_of(kt * TK, TK)
        m_old = m_ref[...]
        m_new = jnp.maximum(m_old, jnp.max(s_ref[...], axis=0, keepdims=True))
        p_ref[0:TK, :] = jnp.exp2(s_ref[...] - m_new).astype(BF16)
        acc_ref[...] = (jnp.exp2(m_old - m_new) * acc_ref[...]
                        + jnp.dot(vs_ref[:, pl.ds(k0, TK)], p_ref[0:TK, :], preferred_element_type=F32))
        m_ref[...] = m_new

    def mask_own_keys(s_ref):
        band = pl.ds(pl.multiple_of(s0 - n_full * TK, QB), QB)
        s_ref[band, :] = jnp.where(band_row <= band_tq, s_ref[band, :], NEG)

    n_full = s0 // TK

    m_ref[...] = jnp.full((1, NQ), NEG, F32)
    acc_ref[...] = jnp.zeros((VROWS, NQ), F32)
    set_flag_rows(None)

    def first_region(cr):
        sc = sel_scores(qn_ref, cr)
        sw = jnp.dot(kw_ref[pl.ds(pl.multiple_of(s0, QB), WIN_KEYS), :], qa_ref[...],
                     preferred_element_type=F32)
        scores(n_full, sa_ref)

        pc = sel_probs(sc, b + 1)
        ocl, imp4 = sel_matmuls(pc)
        redo_if_tied = sel_choose(ocl, imp4, b + 1)

        sw = jnp.concatenate([jnp.where(band_row > band_tq, sw[0:QB], NEG),
                              sw[QB:WINDOW],
                              jnp.where(band_row <= band_tq, sw[WINDOW:WIN_KEYS], NEG)], axis=0)
        pw = jnp.exp2(sw - jnp.max(sw, axis=0, keepdims=True)).astype(BF16)
        mask_own_keys(sa_ref)

        ow = jnp.dot(vw_ref[:, pl.ds(pl.multiple_of(s0, QB), WIN_KEYS)], pw, preferred_element_type=F32)
        owin_ref[...] = ow[0:DH, :] * (1.0 / ow[DH:DH + 1, :])
        accumulate(n_full, sa_ref)

        m = m_ref[...]
        c0 = m.astype(BF16).astype(F32)
        c_ref[...] = c0
        acc_ref[...] = acc_ref[...] * jnp.exp2(m - c0)
        rise_ref[...] = jnp.zeros((1, NQ), F32)
        redo_if_tied()

    for i, cr in enumerate(cr_sizes):
        fits = cr_need <= cr
        if i > 0:
            fits = fits & (cr_need > cr_sizes[i - 1])
        pl.when(fits)(functools.partial(first_region, cr))

    def older_keys(grp, k0, rows):
        c = c_ref[...]
        set_bias_rows(grp)
        set_flag_rows(c)
        s = jnp.dot(ks_ref[pl.ds(k0, rows), :], qa_ref[...], preferred_element_type=F32)
        mt = jnp.max(s, axis=0, keepdims=True)
        p_ref[0:rows, :] = jnp.exp2(s).astype(BF16)
        pv = jnp.dot(vs_ref[:, pl.ds(k0, rows)], p_ref[0:rows, :], preferred_element_type=F32)
        c_new = (c + jnp.maximum(mt, 0.0)).astype(BF16).astype(F32)
        acc_ref[...] = (acc_ref[...] + pv) * jnp.exp2(c - c_new)
        c_ref[...] = c_new
        rise_ref[...] = jnp.maximum(rise_ref[...], mt)

    def older_pair(jj, carry):
        older_keys(jj, pl.multiple_of(jj * TK_OLD, TK_OLD), TK_OLD)
        return carry

    lax.fori_loop(0, n_full // 2, older_pair, 0)

    @pl.when(n_full % 2 == 1)
    def _():
        older_keys(n_full // 2, pl.multiple_of((n_full - 1) * TK, TK), TK)

    @pl.when(jnp.max(rise_ref[...]) > REF_SLACK)
    def _():
        m_ref[...] = jnp.full((1, NQ), NEG, F32)
        acc_ref[...] = jnp.zeros((VROWS, NQ), F32)
        set_flag_rows(None)

        def two_pass_tile(kt, carry):
            scores(kt, sa_ref)

            @pl.when(kt == n_full)
            def _():
                mask_own_keys(sa_ref)
            accumulate(kt, sa_ref)
            return carry

        lax.fori_loop(0, n_full + 1, two_pass_tile, 0)

    acc = acc_ref[...]
    osel = acc[0:DH, :] * (1.0 / acc[DH:DH + 1, :])
    owin = owin_ref[...]

    heads = []
    for h in range(HPG):
        sl = slice(h * QB, (h + 1) * QB)
        heads.append(gt_ref[3 * h:3 * h + 1, :] * oc[:, sl]
                     + gt_ref[3 * h + 1:3 * h + 2, :] * osel[:, sl]
                     + gt_ref[3 * h + 2:3 * h + 3, :] * owin[:, sl])
    out_ref[...] = jnp.concatenate(heads, axis=0).T.astype(out_ref.dtype)


def _nsa(q_t, kcmp, vcmp_t, ks, vs_t, kw, vw_t, gt, ovl_t):
    S = ks.shape[1]
    NC, NB = S // CMP_STRIDE, S // SLC_BLOCK
    n_qb = S // QB
    grp = lambda g, b: (g, 0, 0)
    return pl.pallas_call(
        functools.partial(_nsa_kernel, S=S),
        grid=(NSA_GROUPS, n_qb),
        in_specs=[pl.BlockSpec((HPG, DH, QB), lambda g, b: (g, 0, b)),
                  pl.BlockSpec((HPG, DH, QB), lambda g, b: (g, 0, jnp.minimum(b + 1, n_qb - 1))),
                  pl.BlockSpec((None, NC, LANES), grp),
                  pl.BlockSpec((None, VROWS, NC), grp),
                  pl.BlockSpec((None, S, LANES), grp, pipeline_mode=pl.Buffered(1)),
                  pl.BlockSpec((None, VROWS, S), grp, pipeline_mode=pl.Buffered(1)),
                  pl.BlockSpec((None, WINDOW + S, LANES), grp, pipeline_mode=pl.Buffered(1)),
                  pl.BlockSpec((None, VROWS, WINDOW + S), grp, pipeline_mode=pl.Buffered(1)),
                  pl.BlockSpec((None, GATE_ROWS, QB), lambda g, b: (g, 0, b)),
                  pl.BlockSpec((NB, NC), lambda g, b: (0, 0))],
        out_specs=pl.BlockSpec((QB, HPG * DH), lambda g, b: (b, g)),
        out_shape=jax.ShapeDtypeStruct((S, NSA_HEADS * DH), BF16),
        scratch_shapes=[pltpu.VMEM((LANES, HPG * QB), BF16),
                        pltpu.VMEM((LANES, HPG * QB), BF16),
                        pltpu.VMEM((NB, QB), BF16),
                        pltpu.VMEM((NB, QB), BF16),
                        pltpu.VMEM((DH, HPG * QB), F32),
                        pltpu.VMEM((1, HPG * QB), F32),
                        pltpu.VMEM((VROWS, HPG * QB), F32),
                        pltpu.VMEM((TK, HPG * QB), F32),
                        pltpu.VMEM((TK_OLD, HPG * QB), BF16),
                        pltpu.VMEM((1, HPG * QB), F32),
                        pltpu.VMEM((1, HPG * QB), F32),
                        pltpu.VMEM((DH, HPG * QB), F32)],
        compiler_params=_params(("arbitrary", "arbitrary")), name="nsa",
    )(q_t, q_t, kcmp, vcmp_t, ks, vs_t, kw, vw_t, gt, ovl_t)


def _ret_kernel(q_ref, k_ref, v_ref, g_ref, dmat_ref, xi_ref, zeta_ref, dec_ref, o_ref, state_ref, *, tm):
    @pl.when(pl.program_id(0) == 0)
    def _():
        state_ref[...] = jnp.zeros_like(state_ref)

    for j in range(tm // RET_C):
        rows = slice(j * RET_C, (j + 1) * RET_C)
        for h in range(RET_HEADS):
            cols = slice(h * RET_D, (h + 1) * RET_D)
            qc, kc, vc = q_ref[rows, cols], k_ref[rows, cols], v_ref[rows, cols]
            st = state_ref[h]
            st_hi = st.astype(BF16)
            st_lo = (st - st_hi.astype(F32)).astype(BF16)
            sc = lax.dot_general(qc, kc, NT_DIMS, preferred_element_type=F32) * dmat_ref[h]
            inner = jnp.dot(sc.astype(BF16), vc, preferred_element_type=F32)
            cross = (jnp.dot(qc, st_hi, preferred_element_type=F32)
                     + jnp.dot(qc, st_lo, preferred_element_type=F32)) * xi_ref[h]
            kz = (kc.astype(F32) * zeta_ref[h]).astype(BF16)
            state_ref[h] = dec_ref[h] * st + lax.dot_general(kz, vc, TN_DIMS, preferred_element_type=F32)
            o = inner + cross
            o = o * lax.rsqrt(jnp.mean(o * o, axis=-1, keepdims=True) + EPS)
            gate = g_ref[rows, cols].astype(F32)
            o_ref[rows, cols] = (gate * _sigmoid(gate) * o).astype(o_ref.dtype)


def _retention(rq, rk, rv, rg, dmat, xi, zeta, dec, tm=512):
    S = rq.shape[0]
    W = RET_HEADS * RET_D
    row = lambda i: (i, 0)
    cst = lambda i: (0, 0, 0)
    blk = pl.BlockSpec((tm, W), row)
    tbl = pl.BlockSpec((RET_HEADS, RET_C, RET_D), cst)
    return pl.pallas_call(
        functools.partial(_ret_kernel, tm=tm),
        grid=(S // tm,),
        in_specs=[blk, blk, blk, blk, tbl, tbl, tbl, tbl],
        out_specs=blk,
        out_shape=jax.ShapeDtypeStruct((S, W), BF16),
        scratch_shapes=[pltpu.VMEM((RET_HEADS, RET_D, RET_D), F32)],
        compiler_params=_params(("arbitrary",)), name="retention",
    )(rq, rk, rv, rg, dmat, xi, zeta, dec)


def _mix_kernel(x_ref, g_ref, a_ref, r_ref, wg_ref, wpa_ref, wpb_ref, wo_ref, o_ref):
    x = x_ref[...]
    h = _rms(x, g_ref[...]).astype(BF16)
    ga = _sigmoid(lax.dot_general(h, wg_ref[0:D_MODEL, :], NT_DIMS, preferred_element_type=F32))
    gb = _sigmoid(lax.dot_general(h, wg_ref[D_MODEL:2 * D_MODEL, :], NT_DIMS, preferred_element_type=F32))
    mix = (ga * jnp.dot(a_ref[...], wpa_ref[...], preferred_element_type=F32)
           + gb * jnp.dot(r_ref[...], wpb_ref[...], preferred_element_type=F32))
    o_ref[...] = x + jnp.dot(mix.astype(BF16), wo_ref[...], preferred_element_type=F32)


def _mix(x, g, nsa_out, ret_out, wg, wpa, wpb, wo, tm=512):
    S = x.shape[0]
    row = lambda i: (i, 0)
    cst = lambda i: (0, 0)
    return pl.pallas_call(
        _mix_kernel,
        grid=(S // tm,),
        in_specs=[pl.BlockSpec((tm, D_MODEL), row), pl.BlockSpec((1, D_MODEL), cst),
                  pl.BlockSpec((tm, NSA_HEADS * DH), row), pl.BlockSpec((tm, RET_HEADS * RET_D), row),
                  pl.BlockSpec((2 * D_MODEL, D_MODEL), cst), pl.BlockSpec((512, D_MODEL), cst),
                  pl.BlockSpec((512, D_MODEL), cst), pl.BlockSpec((D_MODEL, D_MODEL), cst)],
        out_specs=pl.BlockSpec((tm, D_MODEL), row),
        out_shape=jax.ShapeDtypeStruct((S, D_MODEL), F32),
        compiler_params=_params(("arbitrary",)), name="mix",
    )(x, g, nsa_out, ret_out, wg, wpa, wpb, wo)


def _mlp_kernel(x_ref, g_ref, wu_ref, wd_ref, gf_ref, o_ref, *, hc):
    x = x_ref[...]
    h = _rms(x, g_ref[...]).astype(BF16)
    acc = x
    for c in range(MLP_HIDDEN // hc):
        u = jnp.maximum(jnp.dot(h, wu_ref[:, c * hc:(c + 1) * hc], preferred_element_type=F32), 0.0)
        acc = acc + jnp.dot((u * u).astype(BF16), wd_ref[c * hc:(c + 1) * hc, :], preferred_element_type=F32)
    o_ref[...] = _rms(acc, gf_ref[...])


def _mlp(x, g, wu, wd, gf, tm=512, hc=1024):
    S = x.shape[0]
    row = lambda i: (i, 0)
    cst = lambda i: (0, 0)
    return pl.pallas_call(
        functools.partial(_mlp_kernel, hc=hc),
        grid=(S // tm,),
        in_specs=[pl.BlockSpec((tm, D_MODEL), row), pl.BlockSpec((1, D_MODEL), cst),
                  pl.BlockSpec((D_MODEL, MLP_HIDDEN), cst), pl.BlockSpec((MLP_HIDDEN, D_MODEL), cst),
                  pl.BlockSpec((1, D_MODEL), cst)],
        out_specs=pl.BlockSpec((tm, D_MODEL), row),
        out_shape=jax.ShapeDtypeStruct((S, D_MODEL), F32),
        compiler_params=_params(("arbitrary",)), name="mlp",
    )(x, g, wu, wd, gf)


def _rope_freqs():
    half = ROPE_DIM // 2
    inv = jnp.exp(-math.log(ROPE_THETA) * jnp.arange(half, dtype=F32) * 2.0 / ROPE_DIM)
    hr = RET_D // 2
    inv_r = jnp.exp(-math.log(RET_BASE) * jnp.arange(hr, dtype=F32) * 2.0 / RET_D)
    return jnp.concatenate([inv_r, inv, jnp.zeros((LANES - hr - half,), F32)])[None, :]


def _decay_tables():
    H, C = RET_HEADS, RET_C
    log_g = jnp.log(1.0 - jnp.exp2(-5.0 - jnp.arange(H, dtype=F32)))
    n = jnp.arange(C, dtype=F32)
    rel = n[:, None] - n[None, :]
    dmat = jnp.where(rel >= 0, jnp.exp(log_g[:, None, None] * jnp.maximum(rel, 0.0)), 0.0)
    xi = jnp.exp(log_g[:, None] * (n + 1.0))
    zeta = jnp.exp(log_g[:, None] * (C - 1.0 - n))
    dec = jnp.exp(log_g * C)
    bc = lambda a: jnp.broadcast_to(a, (H, C, RET_D))
    return dmat, bc(xi[:, :, None]), bc(zeta[:, :, None]), bc(dec[:, None, None])


def _overlap_t(S):
    n_cmp = (S - CMP_BLOCK) // CMP_STRIDE + 1
    n_slc = S // SLC_BLOCK
    cs = np.arange(n_cmp)[:, None] * CMP_STRIDE
    ss = np.arange(n_slc)[None, :] * SLC_BLOCK
    ov = np.clip(np.minimum(cs + CMP_BLOCK, ss + SLC_BLOCK) - np.maximum(cs, ss), 0, None) / CMP_BLOCK
    out = np.zeros((n_slc, S // CMP_STRIDE), np.float32)
    out[:, :n_cmp] = ov.T
    return jnp.asarray(out, BF16)


def _v_rows(v_t):
    S = v_t.shape[1]
    return jnp.concatenate([v_t, jnp.ones((1, S), v_t.dtype), jnp.zeros((VROWS - DH - 1, S), v_t.dtype)], axis=0)


_IN_OFFSETS = np.cumsum((0, 512, 128, 128, 128, 128, 128, 128, 24, 512, 512, 512, 512, 1024, 1024))


_W_MOVES = tuple((int(_IN_OFFSETS[i]), dst, int(_IN_OFFSETS[i + 1] - _IN_OFFSETS[i])) for i, dst in
                 ((0, 0), (1, 512), (3, 640), (5, 768), (2, 896), (4, 1024), (6, 1152),
                  (8, 1408), (9, 1920), (10, 2432), (11, 2944)))


def _in_col(wi, i):
    return wi[:, _IN_OFFSETS[i]:_IN_OFFSETS[i + 1]]


def _branches(x, positions, norm_mix, w_in, cmp_pos_k, cmp_pos_v, cmp_k_w1, cmp_k_w2, cmp_v_w1, cmp_v_w2):
    B, S, _ = x.shape
    assert B == 1 and norm_mix.shape[0] == 1 and S % TK == 0 and S >= max(WIN_KEYS, SLC_COUNT * SLC_BLOCK)
    xs = x[0]
    q_t, xh, ks, kw, vs_t, vw_t, gt, rq, rk, rv, rg = _proj(
        xs, norm_mix, jnp.swapaxes(w_in, 1, 2), jnp.broadcast_to(positions[0].astype(F32)[:, None], (S, LANES)), _rope_freqs())

    w1 = jnp.stack([cmp_k_w1[0], cmp_v_w1[0]]).astype(BF16)
    pe = jnp.stack([cmp_pos_k[0], cmp_pos_v[0]]).reshape(2, 1, CMP_BLOCK * DH)
    pe = jnp.broadcast_to(pe, (2, BF16_ROWS, CMP_BLOCK * DH)).astype(BF16)
    w2 = jnp.pad(jnp.stack([cmp_k_w2[0], cmp_v_w2[0]]), ((0, 0), (0, 0), (0, LANES - DH))).astype(BF16)
    cmp = _compress(xh, w1, pe, w2)
    kcmp = cmp[0:NSA_GROUPS].astype(BF16)
    vcmp_t = jax.vmap(_v_rows)(cmp[NSA_GROUPS:, :, 0:DH].transpose(0, 2, 1).astype(BF16))

    vw_t = jnp.pad(vw_t, ((0, 0), (0, 0), (WINDOW, 0)))
    pad_keys = jnp.zeros((NSA_GROUPS, WINDOW, LANES), BF16).at[:, :, KILL_LANE].set(1.0)
    kw = jnp.concatenate([pad_keys, kw], axis=1)
    nsa_out = _nsa(q_t, kcmp, vcmp_t, ks, vs_t, kw, vw_t, gt, _overlap_t(S))

    ret_out = _retention(rq, rk, rv, rg, *_decay_tables())
    return nsa_out, ret_out


def kernel(x, positions, norm_mix, w_in, cmp_pos_k, cmp_pos_v, cmp_k_w1, cmp_k_w2, cmp_v_w1, cmp_v_w2,
           w_proj_a, w_proj_b, w_out, norm_mlp, w_up, w_down, norm_final):
    nsa_out, ret_out = _branches(x, positions, norm_mix, w_in, cmp_pos_k, cmp_pos_v,
                                 cmp_k_w1, cmp_k_w2, cmp_v_w1, cmp_v_w2)
    xs = x[0]
    w_gate = jnp.swapaxes(w_in, 1, 2)[0, _IN_OFFSETS[12]:_IN_OFFSETS[14], :].astype(BF16)
    x1 = _mix(xs, norm_mix, nsa_out, ret_out, w_gate, w_proj_a[0].astype(BF16), w_proj_b[0].astype(BF16),
              w_out[0].astype(BF16))
    y = _mlp(x1, norm_mlp, w_up[0].astype(BF16), w_down[0].astype(BF16), norm_final[None, :])
    return y[None]
```

```python
import functools
import math

import numpy as np
import jax
import jax.numpy as jnp
from jax import lax
from jax.experimental import pallas as pl
from jax.experimental.pallas import tpu as pltpu

F32 = jnp.float32
BF16 = jnp.bfloat16

D_MODEL = 1024
NSA_HEADS = 8
NSA_GROUPS = 2
HPG = NSA_HEADS // NSA_GROUPS
DH = 64
CMP_BLOCK = 32
CMP_STRIDE = 16
CMP_HIDDEN = 256
SLC_BLOCK = 64
SLC_COUNT = 16
WINDOW = 512
QB = 256
ROPE_THETA = 500000.0
ROPE_DIM = DH // 4
RET_HEADS = 4
RET_D = 128
RET_C = 128
RET_BASE = 10000.0
MLP_HIDDEN = 4 * D_MODEL
EPS = 1e-6
NEG = -1e30

LANES = 128
TK = 1024
TK_OLD = 2 * TK
BLK_PER_TILE = TK_OLD // SLC_BLOCK
IMP_ROWS = 64
KILL_LANE = DH + BLK_PER_TILE
REF_LANE = KILL_LANE + 1
REF_SLACK = 64.0
LOG2E = math.log2(math.e)
VROWS = 80
BF16_ROWS = 16
GATE_ROWS = 16
SLC_SHIFT = SLC_BLOCK.bit_length() - 1
WIN_KEYS = WINDOW + QB
VMEM_LIMIT = 56 * 1024 * 1024

NT_DIMS = (((1,), (1,)), ((), ()))
TN_DIMS = (((0,), (0,)), ((), ()))


def _sigmoid(v):
    return 1.0 / (1.0 + jnp.exp(-v))


def _rms(x, g):
    return x * lax.rsqrt(jnp.mean(x * x, axis=-1, keepdims=True) + EPS) * g


def _params(sem, flags=None):
    return pltpu.CompilerParams(dimension_semantics=sem, vmem_limit_bytes=VMEM_LIMIT, flags=flags)


_C_ROT = (0, 896)
_C_V = (896, 1280)
_C_G = (1280, 1408)
_C_RQ = (1408, 1920)
_C_RK = (1920, 2432)
_C_RV = (2432, 2944)
_C_RG = (2944, 3456)
PROJ_W = 3456


def _proj_kernel(x_ref, g_ref, win_ref, pos_ref, freq_ref,
                 q_ref, xh_ref, ks_ref, kw_ref, vs_ref, vw_ref, gt_ref, rq_ref, rk_ref, rv_ref, rg_ref,
                 kv_ref, w_ref, *, tm):
    @pl.when(pl.program_id(0) == 0)
    def _():
        for src, dst, n in _W_MOVES:
            w_ref[dst:dst + n, :] = win_ref[src:src + n, :].astype(BF16)
        n_gate = 3 * HPG
        g0 = _IN_OFFSETS[7]
        zeros = lambda n: jnp.zeros((n, D_MODEL), F32)
        gates = jnp.concatenate([win_ref[g0:g0 + n_gate, :], zeros(GATE_ROWS - n_gate),
                                 win_ref[g0 + n_gate:g0 + 2 * n_gate, :], zeros(LANES - GATE_ROWS - n_gate)],
                                axis=0)
        w_ref[_C_G[0]:_C_G[1], :] = gates.astype(BF16)

    h = _rms(x_ref[...], g_ref[...]).astype(BF16)
    lane = lax.broadcasted_iota(jnp.int32, (tm, LANES), 1)
    lo = lane < DH

    ang = pos_ref[...] * freq_ref[...]
    c, s = jnp.cos(ang), jnp.sin(ang)
    half_n = ROPE_DIM // 2
    cr = jnp.where(lo, c, pltpu.roll(c, DH, 1))
    sr = jnp.where(lo, -s, pltpu.roll(s, DH, 1))
    dim = lane & (DH - 1)
    first, second = dim < half_n, (dim >= half_n) & (dim < ROPE_DIM)
    c1 = jnp.where(lo, pltpu.roll(c, DH, 1), c)
    s1 = jnp.where(lo, pltpu.roll(s, DH, 1), s)
    c2 = jnp.where(lo, pltpu.roll(c, DH + half_n, 1), pltpu.roll(c, half_n, 1))
    s2 = jnp.where(lo, pltpu.roll(s, DH + half_n, 1), pltpu.roll(s, half_n, 1))
    cn = jnp.where(first, c1, jnp.where(second, c2, 1.0))
    sa = jnp.where(first, -s1, 0.0)
    sb = jnp.where(second, s2, 0.0)

    def dot(c):
        return lax.dot_general(h, w_ref[c[0]:c[1], :], NT_DIMS, preferred_element_type=F32)

    def tile(y, i):
        return y[:, i * LANES:(i + 1) * LANES]

    def rope_n(t):
        return t * cn + pltpu.roll(t, LANES - half_n, 1) * sa + pltpu.roll(t, half_n, 1) * sb

    def rope_r(t):
        return t * cr + pltpu.roll(t, RET_D // 2, 1) * sr

    def split(t):
        return jnp.where(lo, t, 0.0), jnp.where(lo, pltpu.roll(t, DH, 1), 0.0)

    def v_rows(dst_ref, t):
        tt = t.T
        tail = jnp.where(lax.broadcasted_iota(jnp.int32, (VROWS - DH, tm), 0) == 0, 1.0, 0.0).astype(BF16)
        for g in range(NSA_GROUPS):
            dst_ref[g, 0:DH, :] = tt[g * DH:(g + 1) * DH, :].astype(BF16)
            dst_ref[g, DH:VROWS, :] = tail

    yr = dot(_C_ROT)
    for i in range(4):
        tt = (rope_n(tile(yr, i)) * (DH ** -0.5 * LOG2E)).T
        q_ref[2 * i] = tt[0:DH, :].astype(BF16)
        q_ref[2 * i + 1] = tt[DH:2 * DH, :].astype(BF16)
    row = pl.program_id(0) * tm + lax.broadcasted_iota(jnp.int32, (tm, LANES), 0)
    blk = (row >> SLC_SHIFT) & (BLK_PER_TILE - 1)
    onehot = jnp.where(((lane - DH) == blk) | (lane == REF_LANE), 1.0, 0.0)
    a, b = split(rope_n(tile(yr, 5)))
    ks_ref[0] = jnp.where(lo, a, onehot).astype(BF16)
    ks_ref[1] = jnp.where(lo, b, onehot).astype(BF16)
    a, b = split(rope_n(tile(yr, 6)))
    kw_ref[0] = a.astype(BF16)
    kw_ref[1] = b.astype(BF16)

    yv = dot(_C_V)
    v_rows(vs_ref, tile(yv, 1))
    v_rows(vw_ref, tile(yv, 2))
    gt = _sigmoid(dot(_C_G)).T
    for g in range(NSA_GROUPS):
        gt_ref[g] = gt[g * GATE_ROWS:(g + 1) * GATE_ROWS, :]

    kv_ref[0] = rope_n(tile(yr, 4))
    kv_ref[1] = tile(yv, 0)
    lo_r = lax.broadcasted_iota(jnp.int32, (tm // CMP_STRIDE, LANES), 1) < DH
    for typ in range(2):
        for u in range(CMP_STRIDE // 2):
            pa = kv_ref[typ, pl.ds(2 * u, tm // CMP_STRIDE, stride=CMP_STRIDE), :]
            pb = kv_ref[typ, pl.ds(2 * u + 1, tm // CMP_STRIDE, stride=CMP_STRIDE), :]
            cols = slice(u * LANES, (u + 1) * LANES)
            xh_ref[2 * typ, :, cols] = jnp.where(lo_r, pa, pltpu.roll(pb, DH, 1)).astype(BF16)
            xh_ref[2 * typ + 1, :, cols] = jnp.where(lo_r, pltpu.roll(pa, DH, 1), pb).astype(BF16)

    yq = dot(_C_RQ)
    yk = dot(_C_RK)
    for i in range(RET_HEADS):
        sl = slice(i * LANES, (i + 1) * LANES)
        rq_ref[:, sl] = rope_r(tile(yq, i)).astype(BF16)
        rk_ref[:, sl] = (rope_r(tile(yk, i)) * (RET_D ** -0.5)).astype(BF16)
    rv_ref[...] = dot(_C_RV).astype(BF16)
    rg_ref[...] = dot(_C_RG).astype(BF16)


def _proj(x, g, w, pos, freq, tm=512):
    S = x.shape[0]
    row = lambda i: (i, 0)
    const = lambda i: (0, 0)
    mid = lambda i: (0, i, 0)
    last = lambda i: (0, 0, i)
    W = RET_HEADS * RET_D
    out_shape = (
        jax.ShapeDtypeStruct((NSA_HEADS, DH, S), BF16),
        jax.ShapeDtypeStruct((2 * NSA_GROUPS, S // CMP_STRIDE, CMP_STRIDE * DH), BF16),
        jax.ShapeDtypeStruct((NSA_GROUPS, S, LANES), BF16),
        jax.ShapeDtypeStruct((NSA_GROUPS, S, LANES), BF16),
        jax.ShapeDtypeStruct((NSA_GROUPS, VROWS, S), BF16),
        jax.ShapeDtypeStruct((NSA_GROUPS, VROWS, S), BF16),
        jax.ShapeDtypeStruct((NSA_GROUPS, GATE_ROWS, S), F32),
        jax.ShapeDtypeStruct((S, W), BF16),
        jax.ShapeDtypeStruct((S, W), BF16),
        jax.ShapeDtypeStruct((S, W), BF16),
        jax.ShapeDtypeStruct((S, W), BF16),
    )
    out_specs = (
        pl.BlockSpec((NSA_HEADS, DH, tm), last),
        pl.BlockSpec((2 * NSA_GROUPS, tm // CMP_STRIDE, CMP_STRIDE * DH), mid),
        pl.BlockSpec((NSA_GROUPS, tm, LANES), mid),
        pl.BlockSpec((NSA_GROUPS, tm, LANES), mid),
        pl.BlockSpec((NSA_GROUPS, VROWS, tm), last),
        pl.BlockSpec((NSA_GROUPS, VROWS, tm), last),
        pl.BlockSpec((NSA_GROUPS, GATE_ROWS, tm), last),
        pl.BlockSpec((tm, W), row),
        pl.BlockSpec((tm, W), row),
        pl.BlockSpec((tm, W), row),
        pl.BlockSpec((tm, W), row),
    )
    return pl.pallas_call(
        functools.partial(_proj_kernel, tm=tm),
        grid=(S // tm,),
        in_specs=[pl.BlockSpec((tm, D_MODEL), row), pl.BlockSpec((1, D_MODEL), const),
                  pl.BlockSpec((None,) + w.shape[1:], lambda i: (0, 0, 0), pipeline_mode=pl.Buffered(1)),
                  pl.BlockSpec((tm, 1), row),
                  pl.BlockSpec((1, LANES), const)],
        out_specs=out_specs, out_shape=out_shape,
        scratch_shapes=[pltpu.VMEM((2, tm, LANES), F32),
                        pltpu.VMEM((PROJ_W, D_MODEL), BF16)],
        compiler_params=_params(("arbitrary",)), name="proj",
    )(x, g, w, pos, freq)


def _compress_kernel(x_ref, w1_ref, pe_ref, w2_ref, o_ref):
    x = x_ref[...]
    half = CMP_STRIDE * DH
    a = jnp.dot(x, w1_ref[0:half, :], preferred_element_type=F32)
    b = jnp.dot(x, w1_ref[half:2 * half, :], preferred_element_type=F32)
    peb = jnp.dot(pe_ref[...], w1_ref[...], preferred_element_type=F32)[0:1, :]
    hid = a + pltpu.roll(b, x.shape[0] - 1, 0) + peb
    hid = hid * _sigmoid(hid)
    o_ref[...] = jnp.dot(hid.astype(BF16), w2_ref[...], preferred_element_type=F32)


def _compress(xh, w1, pe, w2):
    n_half = xh.shape[1]
    return pl.pallas_call(
        _compress_kernel,
        grid=(2 * NSA_GROUPS,),
        in_specs=[pl.BlockSpec((None, n_half, CMP_STRIDE * DH), lambda i: (i, 0, 0)),
                  pl.BlockSpec((None, CMP_BLOCK * DH, CMP_HIDDEN), lambda i: (i // NSA_GROUPS, 0, 0)),
                  pl.BlockSpec((None, BF16_ROWS, CMP_BLOCK * DH), lambda i: (i // NSA_GROUPS, 0, 0)),
                  pl.BlockSpec((None, CMP_HIDDEN, LANES), lambda i: (i // NSA_GROUPS, 0, 0))],
        out_specs=pl.BlockSpec((None, n_half, LANES), lambda i: (i, 0, 0)),
        out_shape=jax.ShapeDtypeStruct((2 * NSA_GROUPS, n_half, LANES), F32),
        compiler_params=_params(("arbitrary",)), name="compress",
    )(xh, w1, pe, w2)


def _nsa_kernel(q_ref, qn_ref, kcmp_ref, vcmp_ref, ks_ref, vs_ref, kw_ref, vw_ref, gt_ref, ovl_ref,
                out_ref, qa_ref, qs_ref, bias_ref, biasn_ref, ocn_ref, m_ref, acc_ref,
                sa_ref, p_ref, c_ref, rise_ref, owin_ref, *, S):
    NC = S // CMP_STRIDE
    NB = S // SLC_BLOCK
    NQ = HPG * QB
    b = pl.program_id(1)
    s0 = b * QB
    lane_q = lax.broadcasted_iota(jnp.int32, (1, NQ), 1) & (QB - 1)
    band_row = lax.broadcasted_iota(jnp.int32, (QB, NQ), 0)
    band_tq = lax.broadcasted_iota(jnp.int32, (QB, NQ), 1) & (QB - 1)

    def sel_scores(qsrc_ref, cr):
        qs_ref[0:DH, :] = jnp.concatenate([qsrc_ref[h] for h in range(HPG)], axis=1)
        qs_ref[DH:LANES, :] = jnp.zeros((LANES - DH, NQ), BF16)
        return jnp.dot(kcmp_ref[0:cr, :], qs_ref[...], preferred_element_type=F32)

    def sel_probs(sc, blk):
        c_end = lax.broadcasted_iota(jnp.int32, sc.shape, 0) * CMP_STRIDE + (CMP_BLOCK - 1)
        sc = jnp.where(c_end <= blk * QB + lane_q, sc, NEG)
        return jnp.exp2(sc - jnp.max(sc, axis=0, keepdims=True)).astype(BF16)

    def sel_matmuls(pc):
        cr = pc.shape[0]
        ocl = jnp.dot(vcmp_ref[:, 0:cr], pc, preferred_element_type=F32)
        ir, ic = IMP_ROWS, 4 * IMP_ROWS
        if cr % ic:
            return ocl, jnp.dot(ovl_ref[0:cr // 4, 0:cr], pc, preferred_element_type=F32)
        parts = []
        for i in range(cr // ic):
            c0 = max(i - 1, 0) * ic
            parts.append(jnp.dot(ovl_ref[i * ir:(i + 1) * ir, c0:(i + 1) * ic], pc[c0:(i + 1) * ic, :],
                                 preferred_element_type=F32))
        return ocl, jnp.concatenate(parts, axis=0)

    def sel_choose(ocl, imp4, blk):
        t_row = blk * QB + lane_q
        any_valid = jnp.where(t_row >= CMP_BLOCK - 1, 1.0, 0.0)
        rc = any_valid / ocl[DH:DH + 1, :]
        ocn_ref[...] = ocl[0:DH, :] * rc

        imp4 = imp4 * rc
        imp = imp4[:, 0:QB] + imp4[:, QB:2 * QB] + imp4[:, 2 * QB:3 * QB] + imp4[:, 3 * QB:4 * QB]
        if imp.shape[0] < NB:
            imp = jnp.concatenate([imp, jnp.zeros((NB - imp.shape[0], QB), F32)], axis=0)

        jidx = lax.broadcasted_iota(jnp.int32, (NB, QB), 0)
        cur = (blk * QB + lax.broadcasted_iota(jnp.int32, (1, QB), 1)) >> SLC_SHIFT
        forced = (jidx == 0) | (jidx == cur) | (jidx == cur - 1)
        future = jidx > cur
        cand = jnp.where(forced | future, -jnp.inf, imp)

        def select(break_ties):
            v = cand
            for _ in range(SLC_COUNT - 3):
                mx = jnp.max(v, axis=0, keepdims=True)
                hit = v == mx
                if break_ties:
                    hit = jidx == jnp.min(jnp.where(hit, jidx, NB), axis=0, keepdims=True)
                v = jnp.where(hit, -jnp.inf, v)
            return jnp.where((v == -jnp.inf) & (jidx <= cur), 0.0, NEG)

        bias_t = select(False)
        n_sel = jnp.sum(jnp.where(bias_t == 0.0, 1.0, 0.0), axis=0, keepdims=True)
        biasn_ref[...] = bias_t.astype(BF16)

        def redo_if_tied():
            @pl.when(jnp.max(n_sel) > SLC_COUNT)
            def _():
                biasn_ref[...] = select(True).astype(BF16)
        return redo_if_tied

    cr_need = jnp.minimum((b + 2) * (QB // CMP_STRIDE) - 1, NC)
    cr_sizes = (NC // 2, NC) if (NC // 2) % (4 * IMP_ROWS) == 0 else (NC,)

    @pl.when(b == 0)
    def _():
        ocl, imp4 = sel_matmuls(sel_probs(sel_scores(q_ref, cr_sizes[0]), 0))
        sel_choose(ocl, imp4, 0)()

    bias_ref[...] = biasn_ref[...]
    oc = ocn_ref[...]

    qa_ref[0:DH, :] = jnp.concatenate([q_ref[h] for h in range(HPG)], axis=1)
    qa_ref[DH:KILL_LANE, :] = jnp.zeros((KILL_LANE - DH, NQ), BF16)

    def set_bias_rows(grp):
        bias_rows = bias_ref[pl.ds(pl.multiple_of(grp * BLK_PER_TILE, BLK_PER_TILE), BLK_PER_TILE), :]
        qa_ref[DH:DH + BLK_PER_TILE, :] = jnp.concatenate([bias_rows] * HPG, axis=1)

    def set_flag_rows(ref):
        r = lax.broadcasted_iota(jnp.int32, (LANES - KILL_LANE, NQ), 0)
        rows = jnp.where(r == 0, NEG, 0.0) if ref is None else jnp.where(r == 0, NEG, jnp.where(r == 1, -ref, 0.0))
        qa_ref[KILL_LANE:LANES, :] = rows.astype(BF16)

    def scores(kt, s_ref):
        k0 = pl.multiple_of(kt * TK, TK)
        set_bias_rows(kt // 2)
        s_ref[...] = jnp.dot(ks_ref[pl.ds(k0, TK), :], qa_ref[...], preferred_element_type=F32)

    def accumulate(kt, s_ref):
        k0 = pl.multiple_of(kt * TK, TK)
        m_old = m_ref[...]
        m_new = jnp.maximum(m_old, jnp.max(s_ref[...], axis=0, keepdims=True))
        p_ref[0:TK, :] = jnp.exp2(s_ref[...] - m_new).astype(BF16)
        acc_ref[...] = (jnp.exp2(m_old - m_new) * acc_ref[...]
                        + jnp.dot(vs_ref[:, pl.ds(k0, TK)], p_ref[0:TK, :], preferred_element_type=F32))
        m_ref[...] = m_new

    def mask_own_keys(s_ref):
        band = pl.ds(pl.multiple_of(s0 - n_full * TK, QB), QB)
        s_ref[band, :] = jnp.where(band_row <= band_tq, s_ref[band, :], NEG)

    n_full = s0 // TK

    m_ref[...] = jnp.full((1, NQ), NEG, F32)
    acc_ref[...] = jnp.zeros((VROWS, NQ), F32)
    set_flag_rows(None)

    def first_region(cr):
        sc = sel_scores(qn_ref, cr)
        sw = jnp.dot(kw_ref[pl.ds(pl.multiple_of(s0, QB), WIN_KEYS), :], qa_ref[...],
                     preferred_element_type=F32)
        scores(n_full, sa_ref)

        pc = sel_probs(sc, b + 1)
        ocl, imp4 = sel_matmuls(pc)
        redo_if_tied = sel_choose(ocl, imp4, b + 1)

        sw = jnp.concatenate([jnp.where(band_row > band_tq, sw[0:QB], NEG),
                              sw[QB:WINDOW],
                              jnp.where(band_row <= band_tq, sw[WINDOW:WIN_KEYS], NEG)], axis=0)
        pw = jnp.exp2(sw - jnp.max(sw, axis=0, keepdims=True)).astype(BF16)
        mask_own_keys(sa_ref)

        ow = jnp.dot(vw_ref[:, pl.ds(pl.multiple_of(s0, QB), WIN_KEYS)], pw, preferred_element_type=F32)
        owin_ref[...] = ow[0:DH, :] * (1.0 / ow[DH:DH + 1, :])
        accumulate(n_full, sa_ref)

        m = m_ref[...]
        c0 = m.astype(BF16).astype(F32)
        c_ref[...] = c0
        acc_ref[...] = acc_ref[...] * jnp.exp2(m - c0)
        rise_ref[...] = jnp.zeros((1, NQ), F32)
        redo_if_tied()

    for i, cr in enumerate(cr_sizes):
        fits = cr_need <= cr
        if i > 0:
            fits = fits & (cr_need > cr_sizes[i - 1])
        pl.when(fits)(functools.partial(first_region, cr))

    def older_keys(grp, k0, rows):
        c = c_ref[...]
        set_bias_rows(grp)
        set_flag_rows(c)
        s = jnp.dot(ks_ref[pl.ds(k0, rows), :], qa_ref[...], preferred_element_type=F32)
        mt = jnp.max(s, axis=0, keepdims=True)
        p_ref[0:rows, :] = jnp.exp2(s).astype(BF16)
        pv = jnp.dot(vs_ref[:, pl.ds(k0, rows)], p_ref[0:rows, :], preferred_element_type=F32)
        c_new = (c + jnp.maximum(mt, 0.0)).astype(BF16).astype(F32)
        acc_ref[...] = (acc_ref[...] + pv) * jnp.exp2(c - c_new)
        c_ref[...] = c_new
        rise_ref[...] = jnp.maximum(rise_ref[...], mt)

    def older_pair(jj, carry):
        older_keys(jj, pl.multiple_of(jj * TK_OLD, TK_OLD), TK_OLD)
        return carry

    lax.fori_loop(0, n_full // 2, older_pair, 0)

    @pl.when(n_full % 2 == 1)
    def _():
        older_keys(n_full // 2, pl.multiple_of((n_full - 1) * TK, TK), TK)

    @pl.when(jnp.max(rise_ref[...]) > REF_SLACK)
    def _():
        m_ref[...] = jnp.full((1, NQ), NEG, F32)
        acc_ref[...] = jnp.zeros((VROWS, NQ), F32)
        set_flag_rows(None)

        def two_pass_tile(kt, carry):
            scores(kt, sa_ref)

            @pl.when(kt == n_full)
            def _():
                mask_own_keys(sa_ref)
            accumulate(kt, sa_ref)
            return carry

        lax.fori_loop(0, n_full + 1, two_pass_tile, 0)

    acc = acc_ref[...]
    osel = acc[0:DH, :] * (1.0 / acc[DH:DH + 1, :])
    owin = owin_ref[...]

    heads = []
    for h in range(HPG):
        sl = slice(h * QB, (h + 1) * QB)
        heads.append(gt_ref[3 * h:3 * h + 1, :] * oc[:, sl]
                     + gt_ref[3 * h + 1:3 * h + 2, :] * osel[:, sl]
                     + gt_ref[3 * h + 2:3 * h + 3, :] * owin[:, sl])
    out_ref[...] = jnp.concatenate(heads, axis=0).T.astype(out_ref.dtype)


def _nsa(q_t, kcmp, vcmp_t, ks, vs_t, kw, vw_t, gt, ovl_t):
    S = ks.shape[1]
    NC, NB = S // CMP_STRIDE, S // SLC_BLOCK
    n_qb = S // QB
    grp = lambda g, b: (g, 0, 0)
    return pl.pallas_call(
        functools.partial(_nsa_kernel, S=S),
        grid=(NSA_GROUPS, n_qb),
        in_specs=[pl.BlockSpec((HPG, DH, QB), lambda g, b: (g, 0, b)),
                  pl.BlockSpec((HPG, DH, QB), lambda g, b: (g, 0, jnp.minimum(b + 1, n_qb - 1))),
                  pl.BlockSpec((None, NC, LANES), grp),
                  pl.BlockSpec((None, VROWS, NC), grp),
                  pl.BlockSpec((None, S, LANES), grp, pipeline_mode=pl.Buffered(1)),
                  pl.BlockSpec((None, VROWS, S), grp, pipeline_mode=pl.Buffered(1)),
                  pl.BlockSpec((None, WINDOW + S, LANES), grp, pipeline_mode=pl.Buffered(1)),
                  pl.BlockSpec((None, VROWS, WINDOW + S), grp, pipeline_mode=pl.Buffered(1)),
                  pl.BlockSpec((None, GATE_ROWS, QB), lambda g, b: (g, 0, b)),
                  pl.BlockSpec((NB, NC), lambda g, b: (0, 0))],
        out_specs=pl.BlockSpec((QB, HPG * DH), lambda g, b: (b, g)),
        out_shape=jax.ShapeDtypeStruct((S, NSA_HEADS * DH), BF16),
        scratch_shapes=[pltpu.VMEM((LANES, HPG * QB), BF16),
                        pltpu.VMEM((LANES, HPG * QB), BF16),
                        pltpu.VMEM((NB, QB), BF16),
                        pltpu.VMEM((NB, QB), BF16),
                        pltpu.VMEM((DH, HPG * QB), F32),
                        pltpu.VMEM((1, HPG * QB), F32),
                        pltpu.VMEM((VROWS, HPG * QB), F32),
                        pltpu.VMEM((TK, HPG * QB), F32),
                        pltpu.VMEM((TK_OLD, HPG * QB), BF16),
                        pltpu.VMEM((1, HPG * QB), F32),
                        pltpu.VMEM((1, HPG * QB), F32),
                        pltpu.VMEM((DH, HPG * QB), F32)],
        compiler_params=_params(("arbitrary", "arbitrary")), name="nsa",
    )(q_t, q_t, kcmp, vcmp_t, ks, vs_t, kw, vw_t, gt, ovl_t)


def _ret_kernel(q_ref, k_ref, v_ref, g_ref, dmat_ref, xi_ref, zeta_ref, dec_ref, o_ref, state_ref, *, tm):
    @pl.when(pl.program_id(0) == 0)
    def _():
        state_ref[...] = jnp.zeros_like(state_ref)

    for j in range(tm // RET_C):
        rows = slice(j * RET_C, (j + 1) * RET_C)
        for h in range(RET_HEADS):
            cols = slice(h * RET_D, (h + 1) * RET_D)
            qc, kc, vc = q_ref[rows, cols], k_ref[rows, cols], v_ref[rows, cols]
            st = state_ref[h]
            st_hi = st.astype(BF16)
            st_lo = (st - st_hi.astype(F32)).astype(BF16)
            sc = lax.dot_general(qc, kc, NT_DIMS, preferred_element_type=F32) * dmat_ref[h]
            inner = jnp.dot(sc.astype(BF16), vc, preferred_element_type=F32)
            cross = (jnp.dot(qc, st_hi, preferred_element_type=F32)
                     + jnp.dot(qc, st_lo, preferred_element_type=F32)) * xi_ref[h]
            kz = (kc.astype(F32) * zeta_ref[h]).astype(BF16)
            state_ref[h] = dec_ref[h] * st + lax.dot_general(kz, vc, TN_DIMS, preferred_element_type=F32)
            o = inner + cross
            o = o * lax.rsqrt(jnp.mean(o * o, axis=-1, keepdims=True) + EPS)
            gate = g_ref[rows, cols].astype(F32)
            o_ref[rows, cols] = (gate * _sigmoid(gate) * o).astype(o_ref.dtype)


def _retention(rq, rk, rv, rg, dmat, xi, zeta, dec, tm=512):
    S = rq.shape[0]
    W = RET_HEADS * RET_D
    row = lambda i: (i, 0)
    cst = lambda i: (0, 0, 0)
    blk = pl.BlockSpec((tm, W), row)
    tbl = pl.BlockSpec((RET_HEADS, RET_C, RET_D), cst)
    return pl.pallas_call(
        functools.partial(_ret_kernel, tm=tm),
        grid=(S // tm,),
        in_specs=[blk, blk, blk, blk, tbl, tbl, tbl, tbl],
        out_specs=blk,
        out_shape=jax.ShapeDtypeStruct((S, W), BF16),
        scratch_shapes=[pltpu.VMEM((RET_HEADS, RET_D, RET_D), F32)],
        compiler_params=_params(("arbitrary",)), name="retention",
    )(rq, rk, rv, rg, dmat, xi, zeta, dec)


def _mix_kernel(x_ref, g_ref, a_ref, r_ref, wg_ref, wpa_ref, wpb_ref, wo_ref, o_ref):
    x = x_ref[...]
    h = _rms(x, g_ref[...]).astype(BF16)
    ga = _sigmoid(lax.dot_general(h, wg_ref[0:D_MODEL, :], NT_DIMS, preferred_element_type=F32))
    gb = _sigmoid(lax.dot_general(h, wg_ref[D_MODEL:2 * D_MODEL, :], NT_DIMS, preferred_element_type=F32))
    mix = (ga * jnp.dot(a_ref[...], wpa_ref[...], preferred_element_type=F32)
           + gb * jnp.dot(r_ref[...], wpb_ref[...], preferred_element_type=F32))
    o_ref[...] = x + jnp.dot(mix.astype(BF16), wo_ref[...], preferred_element_type=F32)


def _mix(x, g, nsa_out, ret_out, wg, wpa, wpb, wo, tm=512):
    S = x.shape[0]
    row = lambda i: (i, 0)
    cst = lambda i: (0, 0)
    return pl.pallas_call(
        _mix_kernel,
        grid=(S // tm,),
        in_specs=[pl.BlockSpec((tm, D_MODEL), row), pl.BlockSpec((1, D_MODEL), cst),
                  pl.BlockSpec((tm, NSA_HEADS * DH), row), pl.BlockSpec((tm, RET_HEADS * RET_D), row),
                  pl.BlockSpec((2 * D_MODEL, D_MODEL), cst), pl.BlockSpec((512, D_MODEL), cst),
                  pl.BlockSpec((512, D_MODEL), cst), pl.BlockSpec((D_MODEL, D_MODEL), cst)],
        out_specs=pl.BlockSpec((tm, D_MODEL), row),
        out_shape=jax.ShapeDtypeStruct((S, D_MODEL), F32),
        compiler_params=_params(("arbitrary",)), name="mix",
    )(x, g, nsa_out, ret_out, wg, wpa, wpb, wo)


def _mlp_kernel(x_ref, g_ref, wu_ref, wd_ref, gf_ref, o_ref, *, hc):
    x = x_ref[...]
    h = _rms(x, g_ref[...]).astype(BF16)
    acc = x
    for c in range(MLP_HIDDEN // hc):
        u = jnp.maximum(jnp.dot(h, wu_ref[:, c * hc:(c + 1) * hc], preferred_element_type=F32), 0.0)
        acc = acc + jnp.dot((u * u).astype(BF16), wd_ref[c * hc:(c + 1) * hc, :], preferred_element_type=F32)
    o_ref[...] = _rms(acc, gf_ref[...])


def _mlp(x, g, wu, wd, gf, tm=512, hc=1024):
    S = x.shape[0]
    row = lambda i: (i, 0)
    cst = lambda i: (0, 0)
    return pl.pallas_call(
        functools.partial(_mlp_kernel, hc=hc),
        grid=(S // tm,),
        in_specs=[pl.BlockSpec((tm, D_MODEL), row), pl.BlockSpec((1, D_MODEL), cst),
                  pl.BlockSpec((D_MODEL, MLP_HIDDEN), cst), pl.BlockSpec((MLP_HIDDEN, D_MODEL), cst),
                  pl.BlockSpec((1, D_MODEL), cst)],
        out_specs=pl.BlockSpec((tm, D_MODEL), row),
        out_shape=jax.ShapeDtypeStruct((S, D_MODEL), F32),
        compiler_params=_params(("arbitrary",)), name="mlp",
    )(x, g, wu, wd, gf)


def _rope_freqs():
    half = ROPE_DIM // 2
    inv = jnp.exp(-math.log(ROPE_THETA) * jnp.arange(half, dtype=F32) * 2.0 / ROPE_DIM)
    hr = RET_D // 2
    inv_r = jnp.exp(-math.log(RET_BASE) * jnp.arange(hr, dtype=F32) * 2.0 / RET_D)
    return jnp.concatenate([inv_r, inv, jnp.zeros((LANES - hr - half,), F32)])[None, :]


def _decay_tables():
    H, C = RET_HEADS, RET_C
    log_g = jnp.log(1.0 - jnp.exp2(-5.0 - jnp.arange(H, dtype=F32)))
    n = jnp.arange(C, dtype=F32)
    rel = n[:, None] - n[None, :]
    dmat = jnp.where(rel >= 0, jnp.exp(log_g[:, None, None] * jnp.maximum(rel, 0.0)), 0.0)
    xi = jnp.exp(log_g[:, None] * (n + 1.0))
    zeta = jnp.exp(log_g[:, None] * (C - 1.0 - n))
    dec = jnp.exp(log_g * C)
    bc = lambda a: jnp.broadcast_to(a, (H, C, RET_D))
    return dmat, bc(xi[:, :, None]), bc(zeta[:, :, None]), bc(dec[:, None, None])


def _overlap_t(S):
    n_cmp = (S - CMP_BLOCK) // CMP_STRIDE + 1
    n_slc = S // SLC_BLOCK
    cs = np.arange(n_cmp)[:, None] * CMP_STRIDE
    ss = np.arange(n_slc)[None, :] * SLC_BLOCK
    ov = np.clip(np.minimum(cs + CMP_BLOCK, ss + SLC_BLOCK) - np.maximum(cs, ss), 0, None) / CMP_BLOCK
    out = np.zeros((n_slc, S // CMP_STRIDE), np.float32)
    out[:, :n_cmp] = ov.T
    return jnp.asarray(out, BF16)


def _v_rows(v_t):
    S = v_t.shape[1]
    return jnp.concatenate([v_t, jnp.ones((1, S), v_t.dtype), jnp.zeros((VROWS - DH - 1, S), v_t.dtype)], axis=0)


_IN_OFFSETS = np.cumsum((0, 512, 128, 128, 128, 128, 128, 128, 24, 512, 512, 512, 512, 1024, 1024))


_W_MOVES = tuple((int(_IN_OFFSETS[i]), dst, int(_IN_OFFSETS[i + 1] - _IN_OFFSETS[i])) for i, dst in
                 ((0, 0), (1, 512), (3, 640), (5, 768), (2, 896), (4, 1024), (6, 1152),
                  (8, 1408), (9, 1920), (10, 2432), (11, 2944)))


def _branches(x, positions, norm_mix, w_in, cmp_pos_k, cmp_pos_v, cmp_k_w1, cmp_k_w2, cmp_v_w1, cmp_v_w2):
    B, S, _ = x.shape
    assert B == 1 and norm_mix.shape[0] == 1 and S % TK == 0 and S >= max(WIN_KEYS, SLC_COUNT * SLC_BLOCK)
    xs = x[0]
    q_t, xh, ks, kw, vs_t, vw_t, gt, rq, rk, rv, rg = _proj(
        xs, norm_mix, jnp.swapaxes(w_in, 1, 2), positions[0].astype(F32)[:, None], _rope_freqs())

    w1 = jnp.stack([cmp_k_w1[0], cmp_v_w1[0]]).astype(BF16)
    pe = jnp.stack([cmp_pos_k[0], cmp_pos_v[0]]).reshape(2, 1, CMP_BLOCK * DH)
    pe = jnp.broadcast_to(pe, (2, BF16_ROWS, CMP_BLOCK * DH)).astype(BF16)
    w2 = jnp.pad(jnp.stack([cmp_k_w2[0], cmp_v_w2[0]]), ((0, 0), (0, 0), (0, LANES - DH))).astype(BF16)
    cmp = _compress(xh, w1, pe, w2)
    kcmp = cmp[0:NSA_GROUPS].astype(BF16)
    vcmp_t = jax.vmap(_v_rows)(cmp[NSA_GROUPS:, :, 0:DH].transpose(0, 2, 1).astype(BF16))

    vw_t = jnp.pad(vw_t, ((0, 0), (0, 0), (WINDOW, 0)))
    pad_keys = jnp.zeros((NSA_GROUPS, WINDOW, LANES), BF16).at[:, :, KILL_LANE].set(1.0)
    kw = jnp.concatenate([pad_keys, kw], axis=1)
    nsa_out = _nsa(q_t, kcmp, vcmp_t, ks, vs_t, kw, vw_t, gt, _overlap_t(S))

    ret_out = _retention(rq, rk, rv, rg, *_decay_tables())
    return nsa_out, ret_out


def kernel(x, positions, norm_mix, w_in, cmp_pos_k, cmp_pos_v, cmp_k_w1, cmp_k_w2, cmp_v_w1, cmp_v_w2,
           w_proj_a, w_proj_b, w_out, norm_mlp, w_up, w_down, norm_final):
    nsa_out, ret_out = _branches(x, positions, norm_mix, w_in, cmp_pos_k, cmp_pos_v,
                                 cmp_k_w1, cmp_k_w2, cmp_v_w1, cmp_v_w2)
    xs = x[0]
    w_gate = jnp.swapaxes(w_in, 1, 2)[0, _IN_OFFSETS[12]:_IN_OFFSETS[14], :].astype(BF16)
    x1 = _mix(xs, norm_mix, nsa_out, ret_out, w_gate, w_proj_a[0].astype(BF16), w_proj_b[0].astype(BF16),
              w_out[0].astype(BF16))
    y = _mlp(x1, norm_mlp, w_up[0].astype(BF16), w_down[0].astype(BF16), norm_final[None, :])
    return y[None]
```

```python
import functools
import math

import numpy as np
import jax
import jax.numpy as jnp
from jax import lax
from jax.experimental import pallas as pl
from jax.experimental.pallas import tpu as pltpu

F32 = jnp.float32
BF16 = jnp.bfloat16

D_MODEL = 1024
NSA_HEADS = 8
NSA_GROUPS = 2
HPG = NSA_HEADS // NSA_GROUPS
DH = 64
CMP_BLOCK = 32
CMP_STRIDE = 16
CMP_HIDDEN = 256
SLC_BLOCK = 64
SLC_COUNT = 16
WINDOW = 512
QB = 256
ROPE_THETA = 500000.0
ROPE_DIM = DH // 4
RET_HEADS = 4
RET_D = 128
RET_C = 128
RET_BASE = 10000.0
MLP_HIDDEN = 4 * D_MODEL
EPS = 1e-6
NEG = -1e30

LANES = 128
TK = 1024
TK_OLD = 2 * TK
BLK_PER_TILE = TK_OLD // SLC_BLOCK
IMP_ROWS = 64
KILL_LANE = DH + BLK_PER_TILE
REF_LANE = KILL_LANE + 1
REF_SLACK = 64.0
LOG2E = math.log2(math.e)
VROWS = 80
BF16_ROWS = 16
GATE_ROWS = 16
SLC_SHIFT = SLC_BLOCK.bit_length() - 1
WIN_KEYS = WINDOW + QB
VMEM_LIMIT = 56 * 1024 * 1024

NT_DIMS = (((1,), (1,)), ((), ()))
TN_DIMS = (((0,), (0,)), ((), ()))


def _sigmoid(v):
    return 1.0 / (1.0 + jnp.exp(-v))


def _rms(x, g):
    return x * lax.rsqrt(jnp.mean(x * x, axis=-1, keepdims=True) + EPS) * g


def _params(sem, flags=None):
    return pltpu.CompilerParams(dimension_semantics=sem, vmem_limit_bytes=VMEM_LIMIT, flags=flags)


_C_ROT = (0, 896)
_C_V = (896, 1280)
_C_G = (1280, 1408)
_C_RQ = (1408, 1920)
_C_RK = (1920, 2432)
_C_RV = (2432, 2944)
_C_RG = (2944, 3456)
PROJ_W = 3456


def _proj_kernel(x_ref, g_ref, win_ref, pos_ref, freq_ref,
                 q_ref, xh_ref, ks_ref, kw_ref, vs_ref, vw_ref, gt_ref, rq_ref, rk_ref, rv_ref, rg_ref,
                 kv_ref, w_ref, *, tm):
    @pl.when(pl.program_id(0) == 0)
    def _():
        for src, dst, n in _W_MOVES:
            w_ref[dst:dst + n, :] = win_ref[src:src + n, :].astype(BF16)
        n_gate = 3 * HPG
        g0 = _IN_OFFSETS[7]
        zeros = lambda n: jnp.zeros((n, D_MODEL), F32)
        gates = jnp.concatenate([win_ref[g0:g0 + n_gate, :], zeros(GATE_ROWS - n_gate),
                                 win_ref[g0 + n_gate:g0 + 2 * n_gate, :], zeros(LANES - GATE_ROWS - n_gate)],
                                axis=0)
        w_ref[_C_G[0]:_C_G[1], :] = gates.astype(BF16)

    h = _rms(x_ref[...], g_ref[...]).astype(BF16)
    lane = lax.broadcasted_iota(jnp.int32, (tm, LANES), 1)
    lo = lane < DH

    ang = pos_ref[...] * freq_ref[...]
    c, s = jnp.cos(ang), jnp.sin(ang)
    half_n = ROPE_DIM // 2
    cr = jnp.where(lo, c, pltpu.roll(c, DH, 1))
    sr = jnp.where(lo, -s, pltpu.roll(s, DH, 1))
    dim = lane & (DH - 1)
    first, second = dim < half_n, (dim >= half_n) & (dim < ROPE_DIM)
    c1 = jnp.where(lo, pltpu.roll(c, DH, 1), c)
    s1 = jnp.where(lo, pltpu.roll(s, DH, 1), s)
    c2 = jnp.where(lo, pltpu.roll(c, DH + half_n, 1), pltpu.roll(c, half_n, 1))
    s2 = jnp.where(lo, pltpu.roll(s, DH + half_n, 1), pltpu.roll(s, half_n, 1))
    cn = jnp.where(first, c1, jnp.where(second, c2, 1.0))
    sa = jnp.where(first, -s1, 0.0)
    sb = jnp.where(second, s2, 0.0)

    def dot(c):
        return lax.dot_general(h, w_ref[c[0]:c[1], :], NT_DIMS, preferred_element_type=F32)

    def tile(y, i):
        return y[:, i * LANES:(i + 1) * LANES]

    def rope_n(t):
        return t * cn + pltpu.roll(t, LANES - half_n, 1) * sa + pltpu.roll(t, half_n, 1) * sb

    def rope_r(t):
        return t * cr + pltpu.roll(t, RET_D // 2, 1) * sr

    def split(t):
        return jnp.where(lo, t, 0.0), jnp.where(lo, pltpu.roll(t, DH, 1), 0.0)

    def v_rows(dst_ref, t):
        tt = t.T
        tail = jnp.where(lax.broadcasted_iota(jnp.int32, (VROWS - DH, tm), 0) == 0, 1.0, 0.0).astype(BF16)
        for g in range(NSA_GROUPS):
            dst_ref[g, 0:DH, :] = tt[g * DH:(g + 1) * DH, :].astype(BF16)
            dst_ref[g, DH:VROWS, :] = tail

    yr = dot(_C_ROT)
    for i in range(4):
        tt = (rope_n(tile(yr, i)) * (DH ** -0.5 * LOG2E)).T
        q_ref[2 * i] = tt[0:DH, :].astype(BF16)
        q_ref[2 * i + 1] = tt[DH:2 * DH, :].astype(BF16)
    row = pl.program_id(0) * tm + lax.broadcasted_iota(jnp.int32, (tm, LANES), 0)
    blk = (row >> SLC_SHIFT) & (BLK_PER_TILE - 1)
    onehot = jnp.where(((lane - DH) == blk) | (lane == REF_LANE), 1.0, 0.0)
    a, b = split(rope_n(tile(yr, 5)))
    ks_ref[0] = jnp.where(lo, a, onehot).astype(BF16)
    ks_ref[1] = jnp.where(lo, b, onehot).astype(BF16)
    a, b = split(rope_n(tile(yr, 6)))
    kw_ref[0] = a.astype(BF16)
    kw_ref[1] = b.astype(BF16)

    yv = dot(_C_V)
    v_rows(vs_ref, tile(yv, 1))
    v_rows(vw_ref, tile(yv, 2))
    gt = _sigmoid(dot(_C_G)).T
    for g in range(NSA_GROUPS):
        gt_ref[g] = gt[g * GATE_ROWS:(g + 1) * GATE_ROWS, :]

    kv_ref[0] = rope_n(tile(yr, 4))
    kv_ref[1] = tile(yv, 0)
    lo_r = lax.broadcasted_iota(jnp.int32, (tm // CMP_STRIDE, LANES), 1) < DH
    for typ in range(2):
        for u in range(CMP_STRIDE // 2):
            pa = kv_ref[typ, pl.ds(2 * u, tm // CMP_STRIDE, stride=CMP_STRIDE), :]
            pb = kv_ref[typ, pl.ds(2 * u + 1, tm // CMP_STRIDE, stride=CMP_STRIDE), :]
            cols = slice(u * LANES, (u + 1) * LANES)
            xh_ref[2 * typ, :, cols] = jnp.where(lo_r, pa, pltpu.roll(pb, DH, 1)).astype(BF16)
            xh_ref[2 * typ + 1, :, cols] = jnp.where(lo_r, pltpu.roll(pa, DH, 1), pb).astype(BF16)

    yq = dot(_C_RQ)
    yk = dot(_C_RK)
    for i in range(RET_HEADS):
        sl = slice(i * LANES, (i + 1) * LANES)
        rq_ref[:, sl] = rope_r(tile(yq, i)).astype(BF16)
        rk_ref[:, sl] = (rope_r(tile(yk, i)) * (RET_D ** -0.5)).astype(BF16)
    rv_ref[...] = dot(_C_RV).astype(BF16)
    rg_ref[...] = dot(_C_RG).astype(BF16)


def _proj(x, g, w, pos, freq, tm=512):
    S = x.shape[0]
    row = lambda i: (i, 0)
    const = lambda i: (0, 0)
    mid = lambda i: (0, i, 0)
    last = lambda i: (0, 0, i)
    W = RET_HEADS * RET_D
    out_shape = (
        jax.ShapeDtypeStruct((NSA_HEADS, DH, S), BF16),
        jax.ShapeDtypeStruct((2 * NSA_GROUPS, S // CMP_STRIDE, CMP_STRIDE * DH), BF16),
        jax.ShapeDtypeStruct((NSA_GROUPS, S, LANES), BF16),
        jax.ShapeDtypeStruct((NSA_GROUPS, S, LANES), BF16),
        jax.ShapeDtypeStruct((NSA_GROUPS, VROWS, S), BF16),
        jax.ShapeDtypeStruct((NSA_GROUPS, VROWS, S), BF16),
        jax.ShapeDtypeStruct((NSA_GROUPS, GATE_ROWS, S), F32),
        jax.ShapeDtypeStruct((S, W), BF16),
        jax.ShapeDtypeStruct((S, W), BF16),
        jax.ShapeDtypeStruct((S, W), BF16),
        jax.ShapeDtypeStruct((S, W), BF16),
    )
    out_specs = (
        pl.BlockSpec((NSA_HEADS, DH, tm), last),
        pl.BlockSpec((2 * NSA_GROUPS, tm // CMP_STRIDE, CMP_STRIDE * DH), mid),
        pl.BlockSpec((NSA_GROUPS, tm, LANES), mid),
        pl.BlockSpec((NSA_GROUPS, tm, LANES), mid),
        pl.BlockSpec((NSA_GROUPS, VROWS, tm), last),
        pl.BlockSpec((NSA_GROUPS, VROWS, tm), last),
        pl.BlockSpec((NSA_GROUPS, GATE_ROWS, tm), last),
        pl.BlockSpec((tm, W), row),
        pl.BlockSpec((tm, W), row),
        pl.BlockSpec((tm, W), row),
        pl.BlockSpec((tm, W), row),
    )
    return pl.pallas_call(
        functools.partial(_proj_kernel, tm=tm),
        grid=(S // tm,),
        in_specs=[pl.BlockSpec((tm, D_MODEL), row), pl.BlockSpec((1, D_MODEL), const),
                  pl.BlockSpec((None,) + w.shape[1:], lambda i: (0, 0, 0), pipeline_mode=pl.Buffered(1)),
                  pl.BlockSpec((tm, 1), row),
                  pl.BlockSpec((1, LANES), const)],
        out_specs=out_specs, out_shape=out_shape,
        scratch_shapes=[pltpu.VMEM((2, tm, LANES), F32),
                        pltpu.VMEM((PROJ_W, D_MODEL), BF16)],
        compiler_params=_params(("arbitrary",)), name="proj",
    )(x, g, w, pos, freq)


def _compress_kernel(x_ref, w1_ref, pe_ref, w2_ref, o_ref):
    x = x_ref[...]
    half = CMP_STRIDE * DH
    a = jnp.dot(x, w1_ref[0:half, :], preferred_element_type=F32)
    b = jnp.dot(x, w1_ref[half:2 * half, :], preferred_element_type=F32)
    peb = jnp.dot(pe_ref[...], w1_ref[...], preferred_element_type=F32)[0:1, :]
    hid = a + pltpu.roll(b, x.shape[0] - 1, 0) + peb
    hid = hid * _sigmoid(hid)
    o_ref[...] = jnp.dot(hid.astype(BF16), w2_ref[...], preferred_element_type=F32)


def _compress(xh, w1, pe, w2):
    n_half = xh.shape[1]
    return pl.pallas_call(
        _compress_kernel,
        grid=(2 * NSA_GROUPS,),
        in_specs=[pl.BlockSpec((None, n_half, CMP_STRIDE * DH), lambda i: (i, 0, 0)),
                  pl.BlockSpec((None, CMP_BLOCK * DH, CMP_HIDDEN), lambda i: (i // NSA_GROUPS, 0, 0)),
                  pl.BlockSpec((None, BF16_ROWS, CMP_BLOCK * DH), lambda i: (i // NSA_GROUPS, 0, 0)),
                  pl.BlockSpec((None, CMP_HIDDEN, LANES), lambda i: (i // NSA_GROUPS, 0, 0))],
        out_specs=pl.BlockSpec((None, n_half, LANES), lambda i: (i, 0, 0)),
        out_shape=jax.ShapeDtypeStruct((2 * NSA_GROUPS, n_half, LANES), F32),
        compiler_params=_params(("arbitrary",)), name="compress",
    )(xh, w1, pe, w2)


def _nsa_kernel(q_ref, qn_ref, kcmp_ref, vcmp_ref, ks_ref, vs_ref, kw_ref, vw_ref, gt_ref, ovl_ref,
                out_ref, qa_ref, qs_ref, bias_ref, biasn_ref, ocn_ref, m_ref, acc_ref,
                sa_ref, p_ref, c_ref, rise_ref, owin_ref, *, S):
    NC = S // CMP_STRIDE
    NB = S // SLC_BLOCK
    NQ = HPG * QB
    b = pl.program_id(1)
    s0 = b * QB
    lane_q = lax.broadcasted_iota(jnp.int32, (1, NQ), 1) & (QB - 1)
    band_row = lax.broadcasted_iota(jnp.int32, (QB, NQ), 0)
    band_tq = lax.broadcasted_iota(jnp.int32, (QB, NQ), 1) & (QB - 1)

    def sel_scores(qsrc_ref, cr):
        qs_ref[0:DH, :] = jnp.concatenate([qsrc_ref[h] for h in range(HPG)], axis=1)
        qs_ref[DH:LANES, :] = jnp.zeros((LANES - DH, NQ), BF16)
        return jnp.dot(kcmp_ref[0:cr, :], qs_ref[...], preferred_element_type=F32)

    def sel_probs(sc, blk):
        c_end = lax.broadcasted_iota(jnp.int32, sc.shape, 0) * CMP_STRIDE + (CMP_BLOCK - 1)
        sc = jnp.where(c_end <= blk * QB + lane_q, sc, NEG)
        return jnp.exp2(sc - jnp.max(sc, axis=0, keepdims=True)).astype(BF16)

    def sel_matmuls(pc):
        cr = pc.shape[0]
        ocl = jnp.dot(vcmp_ref[:, 0:cr], pc, preferred_element_type=F32)
        ir, ic = IMP_ROWS, 4 * IMP_ROWS
        if cr % ic:
            return ocl, jnp.dot(ovl_ref[0:cr // 4, 0:cr], pc, preferred_element_type=F32)
        parts = []
        for i in range(cr // ic):
            c0 = max(i - 1, 0) * ic
            parts.append(jnp.dot(ovl_ref[i * ir:(i + 1) * ir, c0:(i + 1) * ic], pc[c0:(i + 1) * ic, :],
                                 preferred_element_type=F32))
        return ocl, jnp.concatenate(parts, axis=0)

    def sel_choose(ocl, imp4, blk):
        t_row = blk * QB + lane_q
        any_valid = jnp.where(t_row >= CMP_BLOCK - 1, 1.0, 0.0)
        rc = any_valid / ocl[DH:DH + 1, :]
        ocn_ref[...] = ocl[0:DH, :] * rc

        imp4 = imp4 * rc
        imp = imp4[:, 0:QB] + imp4[:, QB:2 * QB] + imp4[:, 2 * QB:3 * QB] + imp4[:, 3 * QB:4 * QB]
        nb = imp.shape[0]

        jidx = lax.broadcasted_iota(jnp.int32, (nb, QB), 0)
        cur = (blk * QB + lax.broadcasted_iota(jnp.int32, (1, QB), 1)) >> SLC_SHIFT
        forced = (jidx == 0) | (jidx == cur) | (jidx == cur - 1)
        future = jidx > cur
        cand = jnp.where(forced | future, -jnp.inf, imp)

        def select(break_ties):
            v = cand
            for _ in range(SLC_COUNT - 3):
                mx = jnp.max(v, axis=0, keepdims=True)
                hit = v == mx
                if break_ties:
                    hit = jidx == jnp.min(jnp.where(hit, jidx, nb), axis=0, keepdims=True)
                v = jnp.where(hit, -jnp.inf, v)
            return jnp.where((v == -jnp.inf) & (jidx <= cur), 0.0, NEG)

        bias_t = select(False)
        n_sel = jnp.sum(jnp.where(bias_t == 0.0, 1.0, 0.0), axis=0, keepdims=True)
        biasn_ref[0:nb, :] = bias_t.astype(BF16)
        if nb < NB:
            biasn_ref[nb:NB, :] = jnp.full((NB - nb, QB), NEG, BF16)

        def redo_if_tied():
            @pl.when(jnp.max(n_sel) > SLC_COUNT)
            def _():
                biasn_ref[0:nb, :] = select(True).astype(BF16)
        return redo_if_tied

    cr_need = jnp.minimum((b + 2) * (QB // CMP_STRIDE) - 1, NC)
    cr_sizes = (NC // 2, NC) if (NC // 2) % (4 * IMP_ROWS) == 0 else (NC,)

    @pl.when(b == 0)
    def _():
        ocl, imp4 = sel_matmuls(sel_probs(sel_scores(q_ref, cr_sizes[0]), 0))
        sel_choose(ocl, imp4, 0)()

    bias_ref[...] = biasn_ref[...]
    oc = ocn_ref[...]

    qa_ref[0:DH, :] = jnp.concatenate([q_ref[h] for h in range(HPG)], axis=1)
    qa_ref[DH:KILL_LANE, :] = jnp.zeros((KILL_LANE - DH, NQ), BF16)

    def set_bias_rows(grp):
        bias_rows = bias_ref[pl.ds(pl.multiple_of(grp * BLK_PER_TILE, BLK_PER_TILE), BLK_PER_TILE), :]
        qa_ref[DH:DH + BLK_PER_TILE, :] = jnp.concatenate([bias_rows] * HPG, axis=1)

    def set_flag_rows(ref):
        r = lax.broadcasted_iota(jnp.int32, (LANES - KILL_LANE, NQ), 0)
        rows = jnp.where(r == 0, NEG, 0.0) if ref is None else jnp.where(r == 0, NEG, jnp.where(r == 1, -ref, 0.0))
        qa_ref[KILL_LANE:LANES, :] = rows.astype(BF16)

    def scores(kt, s_ref):
        k0 = pl.multiple_of(kt * TK, TK)
        set_bias_rows(kt // 2)
        s_ref[...] = jnp.dot(ks_ref[pl.ds(k0, TK), :], qa_ref[...], preferred_element_type=F32)

    def accumulate(kt, s_ref):
        k0 = pl.multiple_of(kt * TK, TK)
        m_old = m_ref[...]
        m_new = jnp.maximum(m_old, jnp.max(s_ref[...], axis=0, keepdims=True))
        p_ref[0:TK, :] = jnp.exp2(s_ref[...] - m_new).astype(BF16)
        acc_ref[...] = (jnp.exp2(m_old - m_new) * acc_ref[...]
                        + jnp.dot(vs_ref[:, pl.ds(k0, TK)], p_ref[0:TK, :], preferred_element_type=F32))
        m_ref[...] = m_new

    def mask_own_keys(s_ref):
        band = pl.ds(pl.multiple_of(s0 - n_full * TK, QB), QB)
        s_ref[band, :] = jnp.where(band_row <= band_tq, s_ref[band, :], NEG)

    n_full = s0 // TK

    m_ref[...] = jnp.full((1, NQ), NEG, F32)
    acc_ref[...] = jnp.zeros((VROWS, NQ), F32)
    set_flag_rows(None)

    def first_region(cr):
        sc = sel_scores(qn_ref, cr)
        sw = jnp.dot(kw_ref[pl.ds(pl.multiple_of(s0, QB), WIN_KEYS), :], qa_ref[...],
                     preferred_element_type=F32)
        scores(n_full, sa_ref)

        pc = sel_probs(sc, b + 1)
        ocl, imp4 = sel_matmuls(pc)
        redo_if_tied = sel_choose(ocl, imp4, b + 1)

        sw = jnp.concatenate([jnp.where(band_row > band_tq, sw[0:QB], NEG),
                              sw[QB:WINDOW],
                              jnp.where(band_row <= band_tq, sw[WINDOW:WIN_KEYS], NEG)], axis=0)
        pw = jnp.exp2(sw - jnp.max(sw, axis=0, keepdims=True)).astype(BF16)
        mask_own_keys(sa_ref)

        ow = jnp.dot(vw_ref[:, pl.ds(pl.multiple_of(s0, QB), WIN_KEYS)], pw, preferred_element_type=F32)
        owin_ref[...] = ow[0:DH, :] * (1.0 / ow[DH:DH + 1, :])
        accumulate(n_full, sa_ref)

        m = m_ref[...]
        c0 = m.astype(BF16).astype(F32)
        c_ref[...] = c0
        acc_ref[...] = acc_ref[...] * jnp.exp2(m - c0)
        rise_ref[...] = jnp.zeros((1, NQ), F32)
        redo_if_tied()

    for i, cr in enumerate(cr_sizes):
        fits = cr_need <= cr
        if i > 0:
            fits = fits & (cr_need > cr_sizes[i - 1])
        pl.when(fits)(functools.partial(first_region, cr))

    def older_keys(grp, k0, rows):
        c = c_ref[...]
        set_bias_rows(grp)
        set_flag_rows(c)
        s = jnp.dot(ks_ref[pl.ds(k0, rows), :], qa_ref[...], preferred_element_type=F32)
        mt = jnp.max(s, axis=0, keepdims=True)
        p_ref[0:rows, :] = jnp.exp2(s).astype(BF16)
        pv = jnp.dot(vs_ref[:, pl.ds(k0, rows)], p_ref[0:rows, :], preferred_element_type=F32)
        c_new = (c + jnp.maximum(mt, 0.0)).astype(BF16).astype(F32)
        acc_ref[...] = (acc_ref[...] + pv) * jnp.exp2(c - c_new)
        c_ref[...] = c_new
        rise_ref[...] = jnp.maximum(rise_ref[...], mt)

    def older_pair(jj, carry):
        older_keys(jj, pl.multiple_of(jj * TK_OLD, TK_OLD), TK_OLD)
        return carry

    lax.fori_loop(0, n_full // 2, older_pair, 0)

    @pl.when(n_full % 2 == 1)
    def _():
        older_keys(n_full // 2, pl.multiple_of((n_full - 1) * TK, TK), TK)

    @pl.when(jnp.max(rise_ref[...]) > REF_SLACK)
    def _():
        m_ref[...] = jnp.full((1, NQ), NEG, F32)
        acc_ref[...] = jnp.zeros((VROWS, NQ), F32)
        set_flag_rows(None)

        def two_pass_tile(kt, carry):
            scores(kt, sa_ref)

            @pl.when(kt == n_full)
            def _():
                mask_own_keys(sa_ref)
            accumulate(kt, sa_ref)
            return carry

        lax.fori_loop(0, n_full + 1, two_pass_tile, 0)

    acc = acc_ref[...]
    osel = acc[0:DH, :] * (1.0 / acc[DH:DH + 1, :])
    owin = owin_ref[...]

    heads = []
    for h in range(HPG):
        sl = slice(h * QB, (h + 1) * QB)
        heads.append(gt_ref[3 * h:3 * h + 1, :] * oc[:, sl]
                     + gt_ref[3 * h + 1:3 * h + 2, :] * osel[:, sl]
                     + gt_ref[3 * h + 2:3 * h + 3, :] * owin[:, sl])
    out_ref[...] = jnp.concatenate(heads, axis=0).T.astype(out_ref.dtype)


def _nsa(q_t, kcmp, vcmp_t, ks, vs_t, kw, vw_t, gt, ovl_t):
    S = ks.shape[1]
    NC, NB = S // CMP_STRIDE, S // SLC_BLOCK
    n_qb = S // QB
    grp = lambda g, b: (g, 0, 0)
    return pl.pallas_call(
        functools.partial(_nsa_kernel, S=S),
        grid=(NSA_GROUPS, n_qb),
        in_specs=[pl.BlockSpec((HPG, DH, QB), lambda g, b: (g, 0, b)),
                  pl.BlockSpec((HPG, DH, QB), lambda g, b: (g, 0, jnp.minimum(b + 1, n_qb - 1))),
                  pl.BlockSpec((None, NC, LANES), grp),
                  pl.BlockSpec((None, VROWS, NC), grp),
                  pl.BlockSpec((None, S, LANES), grp, pipeline_mode=pl.Buffered(1)),
                  pl.BlockSpec((None, VROWS, S), grp, pipeline_mode=pl.Buffered(1)),
                  pl.BlockSpec((None, WINDOW + S, LANES), grp, pipeline_mode=pl.Buffered(1)),
                  pl.BlockSpec((None, VROWS, WINDOW + S), grp, pipeline_mode=pl.Buffered(1)),
                  pl.BlockSpec((None, GATE_ROWS, QB), lambda g, b: (g, 0, b)),
                  pl.BlockSpec((NB, NC), lambda g, b: (0, 0))],
        out_specs=pl.BlockSpec((QB, HPG * DH), lambda g, b: (b, g)),
        out_shape=jax.ShapeDtypeStruct((S, NSA_HEADS * DH), BF16),
        scratch_shapes=[pltpu.VMEM((LANES, HPG * QB), BF16),
                        pltpu.VMEM((LANES, HPG * QB), BF16),
                        pltpu.VMEM((NB, QB), BF16),
                        pltpu.VMEM((NB, QB), BF16),
                        pltpu.VMEM((DH, HPG * QB), F32),
                        pltpu.VMEM((1, HPG * QB), F32),
                        pltpu.VMEM((VROWS, HPG * QB), F32),
                        pltpu.VMEM((TK, HPG * QB), F32),
                        pltpu.VMEM((TK_OLD, HPG * QB), BF16),
                        pltpu.VMEM((1, HPG * QB), F32),
                        pltpu.VMEM((1, HPG * QB), F32),
                        pltpu.VMEM((DH, HPG * QB), F32)],
        compiler_params=_params(("arbitrary", "arbitrary")), name="nsa",
    )(q_t, q_t, kcmp, vcmp_t, ks, vs_t, kw, vw_t, gt, ovl_t)


def _ret_kernel(q_ref, k_ref, v_ref, g_ref, dmat_ref, xi_ref, zeta_ref, dec_ref, o_ref, state_ref, *, tm):
    @pl.when(pl.program_id(0) == 0)
    def _():
        state_ref[...] = jnp.zeros_like(state_ref)

    for j in range(tm // RET_C):
        rows = slice(j * RET_C, (j + 1) * RET_C)
        for h in range(RET_HEADS):
            cols = slice(h * RET_D, (h + 1) * RET_D)
            qc, kc, vc = q_ref[rows, cols], k_ref[rows, cols], v_ref[rows, cols]
            st = state_ref[h]
            st_hi = st.astype(BF16)
            st_lo = (st - st_hi.astype(F32)).astype(BF16)
            sc = lax.dot_general(qc, kc, NT_DIMS, preferred_element_type=F32) * dmat_ref[h]
            inner = jnp.dot(sc.astype(BF16), vc, preferred_element_type=F32)
            cross = (jnp.dot(qc, st_hi, preferred_element_type=F32)
                     + jnp.dot(qc, st_lo, preferred_element_type=F32)) * xi_ref[h]
            kz = (kc.astype(F32) * zeta_ref[h]).astype(BF16)
            state_ref[h] = dec_ref[h] * st + lax.dot_general(kz, vc, TN_DIMS, preferred_element_type=F32)
            o = inner + cross
            o = o * lax.rsqrt(jnp.mean(o * o, axis=-1, keepdims=True) + EPS)
            gate = g_ref[rows, cols].astype(F32)
            o_ref[rows, cols] = (gate * _sigmoid(gate) * o).astype(o_ref.dtype)


def _retention(rq, rk, rv, rg, dmat, xi, zeta, dec, tm=512):
    S = rq.shape[0]
    W = RET_HEADS * RET_D
    row = lambda i: (i, 0)
    cst = lambda i: (0, 0, 0)
    blk = pl.BlockSpec((tm, W), row)
    tbl = pl.BlockSpec((RET_HEADS, RET_C, RET_D), cst)
    return pl.pallas_call(
        functools.partial(_ret_kernel, tm=tm),
        grid=(S // tm,),
        in_specs=[blk, blk, blk, blk, tbl, tbl, tbl, tbl],
        out_specs=blk,
        out_shape=jax.ShapeDtypeStruct((S, W), BF16),
        scratch_shapes=[pltpu.VMEM((RET_HEADS, RET_D, RET_D), F32)],
        compiler_params=_params(("arbitrary",)), name="retention",
    )(rq, rk, rv, rg, dmat, xi, zeta, dec)


def _mix_kernel(x_ref, g_ref, a_ref, r_ref, wg_ref, wpa_ref, wpb_ref, wo_ref, o_ref):
    x = x_ref[...]
    h = _rms(x, g_ref[...]).astype(BF16)
    ga = _sigmoid(lax.dot_general(h, wg_ref[0:D_MODEL, :], NT_DIMS, preferred_element_type=F32))
    gb = _sigmoid(lax.dot_general(h, wg_ref[D_MODEL:2 * D_MODEL, :], NT_DIMS, preferred_element_type=F32))
    mix = (ga * jnp.dot(a_ref[...], wpa_ref[...], preferred_element_type=F32)
           + gb * jnp.dot(r_ref[...], wpb_ref[...], preferred_element_type=F32))
    o_ref[...] = x + jnp.dot(mix.astype(BF16), wo_ref[...], preferred_element_type=F32)


def _mix(x, g, nsa_out, ret_out, wg, wpa, wpb, wo, tm=512):
    S = x.shape[0]
    row = lambda i: (i, 0)
    cst = lambda i: (0, 0)
    return pl.pallas_call(
        _mix_kernel,
        grid=(S // tm,),
        in_specs=[pl.BlockSpec((tm, D_MODEL), row), pl.BlockSpec((1, D_MODEL), cst),
                  pl.BlockSpec((tm, NSA_HEADS * DH), row), pl.BlockSpec((tm, RET_HEADS * RET_D), row),
                  pl.BlockSpec((2 * D_MODEL, D_MODEL), cst), pl.BlockSpec((512, D_MODEL), cst),
                  pl.BlockSpec((512, D_MODEL), cst), pl.BlockSpec((D_MODEL, D_MODEL), cst)],
        out_specs=pl.BlockSpec((tm, D_MODEL), row),
        out_shape=jax.ShapeDtypeStruct((S, D_MODEL), F32),
        compiler_params=_params(("arbitrary",)), name="mix",
    )(x, g, nsa_out, ret_out, wg, wpa, wpb, wo)


def _mlp_kernel(x_ref, g_ref, wu_ref, wd_ref, gf_ref, o_ref, *, hc):
    x = x_ref[...]
    h = _rms(x, g_ref[...]).astype(BF16)
    acc = x
    for c in range(MLP_HIDDEN // hc):
        u = jnp.maximum(jnp.dot(h, wu_ref[:, c * hc:(c + 1) * hc], preferred_element_type=F32), 0.0)
        acc = acc + jnp.dot((u * u).astype(BF16), wd_ref[c * hc:(c + 1) * hc, :], preferred_element_type=F32)
    o_ref[...] = _rms(acc, gf_ref[...])


def _mlp(x, g, wu, wd, gf, tm=512, hc=1024):
    S = x.shape[0]
    row = lambda i: (i, 0)
    cst = lambda i: (0, 0)
    return pl.pallas_call(
        functools.partial(_mlp_kernel, hc=hc),
        grid=(S // tm,),
        in_specs=[pl.BlockSpec((tm, D_MODEL), row), pl.BlockSpec((1, D_MODEL), cst),
                  pl.BlockSpec((D_MODEL, MLP_HIDDEN), cst), pl.BlockSpec((MLP_HIDDEN, D_MODEL), cst),
                  pl.BlockSpec((1, D_MODEL), cst)],
        out_specs=pl.BlockSpec((tm, D_MODEL), row),
        out_shape=jax.ShapeDtypeStruct((S, D_MODEL), F32),
        compiler_params=_params(("arbitrary",)), name="mlp",
    )(x, g, wu, wd, gf)


def _rope_freqs():
    half = ROPE_DIM // 2
    inv = jnp.exp(-math.log(ROPE_THETA) * jnp.arange(half, dtype=F32) * 2.0 / ROPE_DIM)
    hr = RET_D // 2
    inv_r = jnp.exp(-math.log(RET_BASE) * jnp.arange(hr, dtype=F32) * 2.0 / RET_D)
    return jnp.concatenate([inv_r, inv, jnp.zeros((LANES - hr - half,), F32)])[None, :]


def _decay_tables():
    H, C = RET_HEADS, RET_C
    log_g = jnp.log(1.0 - jnp.exp2(-5.0 - jnp.arange(H, dtype=F32)))
    n = jnp.arange(C, dtype=F32)
    rel = n[:, None] - n[None, :]
    dmat = jnp.where(rel >= 0, jnp.exp(log_g[:, None, None] * jnp.maximum(rel, 0.0)), 0.0)
    xi = jnp.exp(log_g[:, None] * (n + 1.0))
    zeta = jnp.exp(log_g[:, None] * (C - 1.0 - n))
    dec = jnp.exp(log_g * C)
    bc = lambda a: jnp.broadcast_to(a, (H, C, RET_D))
    return dmat, bc(xi[:, :, None]), bc(zeta[:, :, None]), bc(dec[:, None, None])


def _overlap_t(S):
    n_cmp = (S - CMP_BLOCK) // CMP_STRIDE + 1
    n_slc = S // SLC_BLOCK
    cs = np.arange(n_cmp)[:, None] * CMP_STRIDE
    ss = np.arange(n_slc)[None, :] * SLC_BLOCK
    ov = np.clip(np.minimum(cs + CMP_BLOCK, ss + SLC_BLOCK) - np.maximum(cs, ss), 0, None) / CMP_BLOCK
    out = np.zeros((n_slc, S // CMP_STRIDE), np.float32)
    out[:, :n_cmp] = ov.T
    return jnp.asarray(out, BF16)


def _v_rows(v_t):
    S = v_t.shape[1]
    return jnp.concatenate([v_t, jnp.ones((1, S), v_t.dtype), jnp.zeros((VROWS - DH - 1, S), v_t.dtype)], axis=0)


_IN_OFFSETS = np.cumsum((0, 512, 128, 128, 128, 128, 128, 128, 24, 512, 512, 512, 512, 1024, 1024))


_W_MOVES = tuple((int(_IN_OFFSETS[i]), dst, int(_IN_OFFSETS[i + 1] - _IN_OFFSETS[i])) for i, dst in
                 ((0, 0), (1, 512), (3, 640), (5, 768), (2, 896), (4, 1024), (6, 1152),
                  (8, 1408), (9, 1920), (10, 2432), (11, 2944)))


def _branches(x, positions, norm_mix, w_in, cmp_pos_k, cmp_pos_v, cmp_k_w1, cmp_k_w2, cmp_v_w1, cmp_v_w2):
    B, S, _ = x.shape
    assert B == 1 and norm_mix.shape[0] == 1 and S % TK == 0 and S >= max(WIN_KEYS, SLC_COUNT * SLC_BLOCK)
    xs = x[0]
    q_t, xh, ks, kw, vs_t, vw_t, gt, rq, rk, rv, rg = _proj(
        xs, norm_mix, jnp.swapaxes(w_in, 1, 2), positions[0].astype(F32)[:, None], _rope_freqs())

    w1 = jnp.stack([cmp_k_w1[0], cmp_v_w1[0]]).astype(BF16)
    pe = jnp.stack([cmp_pos_k[0], cmp_pos_v[0]]).reshape(2, 1, CMP_BLOCK * DH)
    pe = jnp.broadcast_to(pe, (2, BF16_ROWS, CMP_BLOCK * DH)).astype(BF16)
    w2 = jnp.pad(jnp.stack([cmp_k_w2[0], cmp_v_w2[0]]), ((0, 0), (0, 0), (0, LANES - DH))).astype(BF16)
    cmp = _compress(xh, w1, pe, w2)
    kcmp = cmp[0:NSA_GROUPS].astype(BF16)
    vcmp_t = jax.vmap(_v_rows)(cmp[NSA_GROUPS:, :, 0:DH].transpose(0, 2, 1).astype(BF16))

    vw_t = jnp.pad(vw_t, ((0, 0), (0, 0), (WINDOW, 0)))
    pad_keys = jnp.zeros((NSA_GROUPS, WINDOW, LANES), BF16).at[:, :, KILL_LANE].set(1.0)
    kw = jnp.concatenate([pad_keys, kw], axis=1)
    nsa_out = _nsa(q_t, kcmp, vcmp_t, ks, vs_t, kw, vw_t, gt, _overlap_t(S))

    ret_out = _retention(rq, rk, rv, rg, *_decay_tables())
    return nsa_out, ret_out


def kernel(x, positions, norm_mix, w_in, cmp_pos_k, cmp_pos_v, cmp_k_w1, cmp_k_w2, cmp_v_w1, cmp_v_w2,
           w_proj_a, w_proj_b, w_out, norm_mlp, w_up, w_down, norm_final):
    nsa_out, ret_out = _branches(x, positions, norm_mix, w_in, cmp_pos_k, cmp_pos_v,
                                 cmp_k_w1, cmp_k_w2, cmp_v_w1, cmp_v_w2)
    xs = x[0]
    w_gate = jnp.swapaxes(w_in, 1, 2)[0, _IN_OFFSETS[12]:_IN_OFFSETS[14], :].astype(BF16)
    x1 = _mix(xs, norm_mix, nsa_out, ret_out, w_gate, w_proj_a[0].astype(BF16), w_proj_b[0].astype(BF16),
              w_out[0].astype(BF16))
    y = _mlp(x1, norm_mlp, w_up[0].astype(BF16), w_down[0].astype(BF16), norm_final[None, :])
    return y[None]
```

```python
import functools
import math

import numpy as np
import jax
import jax.numpy as jnp
from jax import lax
from jax.experimental import pallas as pl
from jax.experimental.pallas import tpu as pltpu

F32 = jnp.float32
BF16 = jnp.bfloat16

D_MODEL = 1024
NSA_HEADS = 8
NSA_GROUPS = 2
HPG = NSA_HEADS // NSA_GROUPS
DH = 64
CMP_BLOCK = 32
CMP_STRIDE = 16
CMP_HIDDEN = 256
SLC_BLOCK = 64
SLC_COUNT = 16
WINDOW = 512
QB = 256
ROPE_THETA = 500000.0
ROPE_DIM = DH // 4
RET_HEADS = 4
RET_D = 128
RET_C = 128
RET_BASE = 10000.0
MLP_HIDDEN = 4 * D_MODEL
EPS = 1e-6
NEG = -1e30

LANES = 128
TK = 1024
TK_OLD = 2 * TK
BLK_PER_TILE = TK_OLD // SLC_BLOCK
IMP_ROWS = 64
KILL_LANE = DH + BLK_PER_TILE
REF_LANE = KILL_LANE + 1
REF_SLACK = 64.0
LOG2E = math.log2(math.e)
VROWS = 80
BF16_ROWS = 16
GATE_ROWS = 16
SLC_SHIFT = SLC_BLOCK.bit_length() - 1
WIN_KEYS = WINDOW + QB
VMEM_LIMIT = 56 * 1024 * 1024

NT_DIMS = (((1,), (1,)), ((), ()))
TN_DIMS = (((0,), (0,)), ((), ()))


def _sigmoid(v):
    return 1.0 / (1.0 + jnp.exp(-v))


def _rms(x, g):
    return x * lax.rsqrt(jnp.mean(x * x, axis=-1, keepdims=True) + EPS) * g


def _params(sem, flags=None):
    return pltpu.CompilerParams(dimension_semantics=sem, vmem_limit_bytes=VMEM_LIMIT, flags=flags)


_C_ROT = (0, 896)
_C_V = (896, 1280)
_C_G = (1280, 1408)
_C_RQ = (1408, 1920)
_C_RK = (1920, 2432)
_C_RV = (2432, 2944)
_C_RG = (2944, 3456)
PROJ_W = 3456


def _proj_kernel(x_ref, g_ref, win_ref, pos_ref, freq_ref,
                 q_ref, xh_ref, ks_ref, kw_ref, vs_ref, vw_ref, gt_ref, rq_ref, rk_ref, rv_ref, rg_ref,
                 kv_ref, w_ref, *, tm):
    @pl.when(pl.program_id(0) == 0)
    def _():
        for src, dst, n in _W_MOVES:
            w_ref[dst:dst + n, :] = win_ref[src:src + n, :].astype(BF16)
        n_gate = 3 * HPG
        g0 = _IN_OFFSETS[7]
        zeros = lambda n: jnp.zeros((n, D_MODEL), F32)
        gates = jnp.concatenate([win_ref[g0:g0 + n_gate, :], zeros(GATE_ROWS - n_gate),
                                 win_ref[g0 + n_gate:g0 + 2 * n_gate, :], zeros(LANES - GATE_ROWS - n_gate)],
                                axis=0)
        w_ref[_C_G[0]:_C_G[1], :] = gates.astype(BF16)

    h = _rms(x_ref[...], g_ref[...]).astype(BF16)
    lane = lax.broadcasted_iota(jnp.int32, (tm, LANES), 1)
    lo = lane < DH

    ang = pos_ref[...] * freq_ref[...]
    c, s = jnp.cos(ang), jnp.sin(ang)
    half_n = ROPE_DIM // 2
    cr = jnp.where(lo, c, pltpu.roll(c, DH, 1))
    sr = jnp.where(lo, -s, pltpu.roll(s, DH, 1))
    dim = lane & (DH - 1)
    first, second = dim < half_n, (dim >= half_n) & (dim < ROPE_DIM)
    c1 = jnp.where(lo, pltpu.roll(c, DH, 1), c)
    s1 = jnp.where(lo, pltpu.roll(s, DH, 1), s)
    c2 = jnp.where(lo, pltpu.roll(c, DH + half_n, 1), pltpu.roll(c, half_n, 1))
    s2 = jnp.where(lo, pltpu.roll(s, DH + half_n, 1), pltpu.roll(s, half_n, 1))
    cn = jnp.where(first, c1, jnp.where(second, c2, 1.0))
    sa = jnp.where(first, -s1, 0.0)
    sb = jnp.where(second, s2, 0.0)

    def dot(c):
        return lax.dot_general(h, w_ref[c[0]:c[1], :], NT_DIMS, preferred_element_type=F32)

    def tile(y, i):
        return y[:, i * LANES:(i + 1) * LANES]

    def rope_n(t):
        return t * cn + pltpu.roll(t, LANES - half_n, 1) * sa + pltpu.roll(t, half_n, 1) * sb

    def rope_r(t):
        return t * cr + pltpu.roll(t, RET_D // 2, 1) * sr

    def split(t):
        return jnp.where(lo, t, 0.0), jnp.where(lo, pltpu.roll(t, DH, 1), 0.0)

    def v_rows(dst_ref, t):
        tt = t.T
        tail = jnp.where(lax.broadcasted_iota(jnp.int32, (VROWS - DH, tm), 0) == 0, 1.0, 0.0).astype(BF16)
        for g in range(NSA_GROUPS):
            dst_ref[g, 0:DH, :] = tt[g * DH:(g + 1) * DH, :].astype(BF16)
            dst_ref[g, DH:VROWS, :] = tail

    yr = dot(_C_ROT)
    for i in range(4):
        tt = (rope_n(tile(yr, i)) * (DH ** -0.5 * LOG2E)).T
        q_ref[2 * i] = tt[0:DH, :].astype(BF16)
        q_ref[2 * i + 1] = tt[DH:2 * DH, :].astype(BF16)
    row = pl.program_id(0) * tm + lax.broadcasted_iota(jnp.int32, (tm, LANES), 0)
    blk = (row >> SLC_SHIFT) & (BLK_PER_TILE - 1)
    onehot = jnp.where(((lane - DH) == blk) | (lane == REF_LANE), 1.0, 0.0)
    a, b = split(rope_n(tile(yr, 5)))
    ks_ref[0] = jnp.where(lo, a, onehot).astype(BF16)
    ks_ref[1] = jnp.where(lo, b, onehot).astype(BF16)
    a, b = split(rope_n(tile(yr, 6)))
    kw_ref[0] = a.astype(BF16)
    kw_ref[1] = b.astype(BF16)

    yv = dot(_C_V)
    v_rows(vs_ref, tile(yv, 1))
    v_rows(vw_ref, tile(yv, 2))
    gt = _sigmoid(dot(_C_G)).T
    for g in range(NSA_GROUPS):
        gt_ref[g] = gt[g * GATE_ROWS:(g + 1) * GATE_ROWS, :]

    kv_ref[0] = rope_n(tile(yr, 4))
    kv_ref[1] = tile(yv, 0)
    lo_r = lax.broadcasted_iota(jnp.int32, (tm // CMP_STRIDE, LANES), 1) < DH
    for typ in range(2):
        for u in range(CMP_STRIDE // 2):
            pa = kv_ref[typ, pl.ds(2 * u, tm // CMP_STRIDE, stride=CMP_STRIDE), :]
            pb = kv_ref[typ, pl.ds(2 * u + 1, tm // CMP_STRIDE, stride=CMP_STRIDE), :]
            cols = slice(u * LANES, (u + 1) * LANES)
            xh_ref[2 * typ, :, cols] = jnp.where(lo_r, pa, pltpu.roll(pb, DH, 1)).astype(BF16)
            xh_ref[2 * typ + 1, :, cols] = jnp.where(lo_r, pltpu.roll(pa, DH, 1), pb).astype(BF16)

    yq = dot(_C_RQ)
    yk = dot(_C_RK)
    for i in range(RET_HEADS):
        sl = slice(i * LANES, (i + 1) * LANES)
        rq_ref[:, sl] = rope_r(tile(yq, i)).astype(BF16)
        rk_ref[:, sl] = (rope_r(tile(yk, i)) * (RET_D ** -0.5)).astype(BF16)
    rv_ref[...] = dot(_C_RV).astype(BF16)
    rg_ref[...] = dot(_C_RG).astype(BF16)


def _proj(x, g, w, pos, freq, tm=512):
    S = x.shape[0]
    row = lambda i: (i, 0)
    const = lambda i: (0, 0)
    mid = lambda i: (0, i, 0)
    last = lambda i: (0, 0, i)
    W = RET_HEADS * RET_D
    out_shape = (
        jax.ShapeDtypeStruct((NSA_HEADS, DH, S), BF16),
        jax.ShapeDtypeStruct((2 * NSA_GROUPS, S // CMP_STRIDE, CMP_STRIDE * DH), BF16),
        jax.ShapeDtypeStruct((NSA_GROUPS, S, LANES), BF16),
        jax.ShapeDtypeStruct((NSA_GROUPS, S, LANES), BF16),
        jax.ShapeDtypeStruct((NSA_GROUPS, VROWS, S), BF16),
        jax.ShapeDtypeStruct((NSA_GROUPS, VROWS, S), BF16),
        jax.ShapeDtypeStruct((NSA_GROUPS, GATE_ROWS, S), F32),
        jax.ShapeDtypeStruct((S, W), BF16),
        jax.ShapeDtypeStruct((S, W), BF16),
        jax.ShapeDtypeStruct((S, W), BF16),
        jax.ShapeDtypeStruct((S, W), BF16),
    )
    out_specs = (
        pl.BlockSpec((NSA_HEADS, DH, tm), last),
        pl.BlockSpec((2 * NSA_GROUPS, tm // CMP_STRIDE, CMP_STRIDE * DH), mid),
        pl.BlockSpec((NSA_GROUPS, tm, LANES), mid),
        pl.BlockSpec((NSA_GROUPS, tm, LANES), mid),
        pl.BlockSpec((NSA_GROUPS, VROWS, tm), last),
        pl.BlockSpec((NSA_GROUPS, VROWS, tm), last),
        pl.BlockSpec((NSA_GROUPS, GATE_ROWS, tm), last),
        pl.BlockSpec((tm, W), row),
        pl.BlockSpec((tm, W), row),
        pl.BlockSpec((tm, W), row),
        pl.BlockSpec((tm, W), row),
    )
    return pl.pallas_call(
        functools.partial(_proj_kernel, tm=tm),
        grid=(S // tm,),
        in_specs=[pl.BlockSpec((tm, D_MODEL), row), pl.BlockSpec((1, D_MODEL), const),
                  pl.BlockSpec((None,) + w.shape[1:], lambda i: (0, 0, 0), pipeline_mode=pl.Buffered(1)),
                  pl.BlockSpec((tm, 1), row),
                  pl.BlockSpec((1, LANES), const)],
        out_specs=out_specs, out_shape=out_shape,
        scratch_shapes=[pltpu.VMEM((2, tm, LANES), F32),
                        pltpu.VMEM((PROJ_W, D_MODEL), BF16)],
        compiler_params=_params(("arbitrary",)), name="proj",
    )(x, g, w, pos, freq)


def _compress_kernel(x_ref, w1_ref, pe_ref, w2_ref, o_ref):
    x = x_ref[...]
    half = CMP_STRIDE * DH
    a = jnp.dot(x, w1_ref[0:half, :], preferred_element_type=F32)
    b = jnp.dot(x, w1_ref[half:2 * half, :], preferred_element_type=F32)
    peb = jnp.dot(pe_ref[...], w1_ref[...], preferred_element_type=F32)[0:1, :]
    hid = a + pltpu.roll(b, x.shape[0] - 1, 0) + peb
    hid = hid * _sigmoid(hid)
    o_ref[...] = jnp.dot(hid.astype(BF16), w2_ref[...], preferred_element_type=F32)


def _compress(xh, w1, pe, w2):
    n_half = xh.shape[1]
    return pl.pallas_call(
        _compress_kernel,
        grid=(2 * NSA_GROUPS,),
        in_specs=[pl.BlockSpec((None, n_half, CMP_STRIDE * DH), lambda i: (i, 0, 0)),
                  pl.BlockSpec((None, CMP_BLOCK * DH, CMP_HIDDEN), lambda i: (i // NSA_GROUPS, 0, 0)),
                  pl.BlockSpec((None, BF16_ROWS, CMP_BLOCK * DH), lambda i: (i // NSA_GROUPS, 0, 0)),
                  pl.BlockSpec((None, CMP_HIDDEN, LANES), lambda i: (i // NSA_GROUPS, 0, 0))],
        out_specs=pl.BlockSpec((None, n_half, LANES), lambda i: (i, 0, 0)),
        out_shape=jax.ShapeDtypeStruct((2 * NSA_GROUPS, n_half, LANES), F32),
        compiler_params=_params(("arbitrary",)), name="compress",
    )(xh, w1, pe, w2)


def _nsa_kernel(q_ref, qn_ref, kcmp_ref, vcmp_ref, ks_ref, vs_ref, kw_ref, vw_ref, gt_ref, ovl_ref,
                out_ref, qa_ref, qs_ref, bias_ref, biasn_ref, ocn_ref, m_ref, acc_ref,
                sa_ref, p_ref, c_ref, rise_ref, owin_ref, *, S):
    NC = S // CMP_STRIDE
    NB = S // SLC_BLOCK
    NQ = HPG * QB
    b = pl.program_id(1)
    s0 = b * QB
    lane_q = lax.broadcasted_iota(jnp.int32, (1, NQ), 1) & (QB - 1)
    band_row = lax.broadcasted_iota(jnp.int32, (QB, NQ), 0)
    band_tq = lax.broadcasted_iota(jnp.int32, (QB, NQ), 1) & (QB - 1)

    def sel_scores(qsrc_ref, cr):
        qs_ref[0:DH, :] = jnp.concatenate([qsrc_ref[h] for h in range(HPG)], axis=1)
        qs_ref[DH:LANES, :] = jnp.zeros((LANES - DH, NQ), BF16)
        return jnp.dot(kcmp_ref[0:cr, :], qs_ref[...], preferred_element_type=F32)

    def sel_probs(sc, blk):
        c_end = lax.broadcasted_iota(jnp.int32, sc.shape, 0) * CMP_STRIDE + (CMP_BLOCK - 1)
        sc = jnp.where(c_end <= blk * QB + lane_q, sc, NEG)
        return jnp.exp2(sc - jnp.max(sc, axis=0, keepdims=True)).astype(BF16)

    def sel_matmuls(pc):
        cr = pc.shape[0]
        ocl = jnp.dot(vcmp_ref[:, 0:cr], pc, preferred_element_type=F32)
        ir, ic = IMP_ROWS, 4 * IMP_ROWS
        if cr % ic:
            return ocl, jnp.dot(ovl_ref[0:cr // 4, 0:cr], pc, preferred_element_type=F32)
        parts = []
        for i in range(cr // ic):
            c0 = max(i - 1, 0) * ic
            parts.append(jnp.dot(ovl_ref[i * ir:(i + 1) * ir, c0:(i + 1) * ic], pc[c0:(i + 1) * ic, :],
                                 preferred_element_type=F32))
        return ocl, jnp.concatenate(parts, axis=0)

    def sel_choose(ocl, imp4, blk):
        t_row = blk * QB + lane_q
        any_valid = jnp.where(t_row >= CMP_BLOCK - 1, 1.0, 0.0)
        rc = any_valid / ocl[DH:DH + 1, :]
        ocn_ref[...] = ocl[0:DH, :] * rc

        imp4 = imp4 * rc
        imp = imp4[:, 0:QB] + imp4[:, QB:2 * QB] + imp4[:, 2 * QB:3 * QB] + imp4[:, 3 * QB:4 * QB]
        nb = imp.shape[0]

        jidx = lax.broadcasted_iota(jnp.int32, (nb, QB), 0)
        cur = (blk * QB + lax.broadcasted_iota(jnp.int32, (1, QB), 1)) >> SLC_SHIFT
        forced = (jidx == 0) | (jidx == cur) | (jidx == cur - 1)
        future = jidx > cur
        cand = jnp.where(forced | future, -jnp.inf, imp)

        def select(break_ties):
            v = cand
            for _ in range(SLC_COUNT - 3):
                mx = jnp.max(v, axis=0, keepdims=True)
                hit = v == mx
                if break_ties:
                    hit = jidx == jnp.min(jnp.where(hit, jidx, nb), axis=0, keepdims=True)
                v = jnp.where(hit, -jnp.inf, v)
            return jnp.where((v == -jnp.inf) & (jidx <= cur), 0.0, NEG)

        bias_t = select(False)
        n_sel = jnp.sum(jnp.where(bias_t == 0.0, 1.0, 0.0), axis=0, keepdims=True)
        biasn_ref[0:nb, :] = bias_t.astype(BF16)
        if nb < NB:
            biasn_ref[nb:NB, :] = jnp.full((NB - nb, QB), NEG, BF16)

        def redo_if_tied():
            @pl.when(jnp.max(n_sel) > SLC_COUNT)
            def _():
                biasn_ref[0:nb, :] = select(True).astype(BF16)
        return redo_if_tied

    cr_need = jnp.minimum((b + 2) * (QB // CMP_STRIDE) - 1, NC)
    cr_parts = next(n for n in (4, 2, 1) if (NC // n) % (4 * IMP_ROWS) == 0 or n == 1)
    cr_sizes = tuple(NC * (i + 1) // cr_parts for i in range(cr_parts))

    @pl.when(b == 0)
    def _():
        ocl, imp4 = sel_matmuls(sel_probs(sel_scores(q_ref, cr_sizes[0]), 0))
        sel_choose(ocl, imp4, 0)()

    bias_ref[...] = biasn_ref[...]
    oc = ocn_ref[...]

    qa_ref[0:DH, :] = jnp.concatenate([q_ref[h] for h in range(HPG)], axis=1)
    qa_ref[DH:KILL_LANE, :] = jnp.zeros((KILL_LANE - DH, NQ), BF16)

    def set_bias_rows(grp):
        bias_rows = bias_ref[pl.ds(pl.multiple_of(grp * BLK_PER_TILE, BLK_PER_TILE), BLK_PER_TILE), :]
        qa_ref[DH:DH + BLK_PER_TILE, :] = jnp.concatenate([bias_rows] * HPG, axis=1)

    def set_flag_rows(ref):
        r = lax.broadcasted_iota(jnp.int32, (LANES - KILL_LANE, NQ), 0)
        rows = jnp.where(r == 0, NEG, 0.0) if ref is None else jnp.where(r == 0, NEG, jnp.where(r == 1, -ref, 0.0))
        qa_ref[KILL_LANE:LANES, :] = rows.astype(BF16)

    def scores(kt, s_ref):
        k0 = pl.multiple_of(kt * TK, TK)
        set_bias_rows(kt // 2)
        s_ref[...] = jnp.dot(ks_ref[pl.ds(k0, TK), :], qa_ref[...], preferred_element_type=F32)

    def accumulate(kt, s_ref):
        k0 = pl.multiple_of(kt * TK, TK)
        m_old = m_ref[...]
        m_new = jnp.maximum(m_old, jnp.max(s_ref[...], axis=0, keepdims=True))
        p_ref[0:TK, :] = jnp.exp2(s_ref[...] - m_new).astype(BF16)
        acc_ref[...] = (jnp.exp2(m_old - m_new) * acc_ref[...]
                        + jnp.dot(vs_ref[:, pl.ds(k0, TK)], p_ref[0:TK, :], preferred_element_type=F32))
        m_ref[...] = m_new

    def mask_own_keys(s_ref):
        band = pl.ds(pl.multiple_of(s0 - n_full * TK, QB), QB)
        s_ref[band, :] = jnp.where(band_row <= band_tq, s_ref[band, :], NEG)

    n_full = s0 // TK

    m_ref[...] = jnp.full((1, NQ), NEG, F32)
    acc_ref[...] = jnp.zeros((VROWS, NQ), F32)
    set_flag_rows(None)

    def first_region(cr):
        sc = sel_scores(qn_ref, cr)
        sw = jnp.dot(kw_ref[pl.ds(pl.multiple_of(s0, QB), WIN_KEYS), :], qa_ref[...],
                     preferred_element_type=F32)
        scores(n_full, sa_ref)

        pc = sel_probs(sc, b + 1)
        ocl, imp4 = sel_matmuls(pc)
        redo_if_tied = sel_choose(ocl, imp4, b + 1)

        sw = jnp.concatenate([jnp.where(band_row > band_tq, sw[0:QB], NEG),
                              sw[QB:WINDOW],
                              jnp.where(band_row <= band_tq, sw[WINDOW:WIN_KEYS], NEG)], axis=0)
        pw = jnp.exp2(sw - jnp.max(sw, axis=0, keepdims=True)).astype(BF16)
        mask_own_keys(sa_ref)

        ow = jnp.dot(vw_ref[:, pl.ds(pl.multiple_of(s0, QB), WIN_KEYS)], pw, preferred_element_type=F32)
        owin_ref[...] = ow[0:DH, :] * (1.0 / ow[DH:DH + 1, :])
        accumulate(n_full, sa_ref)

        m = m_ref[...]
        c0 = m.astype(BF16).astype(F32)
        c_ref[...] = c0
        acc_ref[...] = acc_ref[...] * jnp.exp2(m - c0)
        rise_ref[...] = jnp.zeros((1, NQ), F32)
        redo_if_tied()

    for i, cr in enumerate(cr_sizes):
        fits = cr_need <= cr
        if i > 0:
            fits = fits & (cr_need > cr_sizes[i - 1])
        pl.when(fits)(functools.partial(first_region, cr))

    def older_keys(grp, k0, rows):
        c = c_ref[...]
        set_bias_rows(grp)
        set_flag_rows(c)
        s = jnp.dot(ks_ref[pl.ds(k0, rows), :], qa_ref[...], preferred_element_type=F32)
        mt = jnp.max(s, axis=0, keepdims=True)
        p_ref[0:rows, :] = jnp.exp2(s).astype(BF16)
        pv = jnp.dot(vs_ref[:, pl.ds(k0, rows)], p_ref[0:rows, :], preferred_element_type=F32)
        c_new = (c + jnp.maximum(mt, 0.0)).astype(BF16).astype(F32)
        acc_ref[...] = (acc_ref[...] + pv) * jnp.exp2(c - c_new)
        c_ref[...] = c_new
        rise_ref[...] = jnp.maximum(rise_ref[...], mt)

    def older_pair(jj, carry):
        older_keys(jj, pl.multiple_of(jj * TK_OLD, TK_OLD), TK_OLD)
        return carry

    lax.fori_loop(0, n_full // 2, older_pair, 0)

    @pl.when(n_full % 2 == 1)
    def _():
        older_keys(n_full // 2, pl.multiple_of((n_full - 1) * TK, TK), TK)

    @pl.when(jnp.max(rise_ref[...]) > REF_SLACK)
    def _():
        m_ref[...] = jnp.full((1, NQ), NEG, F32)
        acc_ref[...] = jnp.zeros((VROWS, NQ), F32)
        set_flag_rows(None)

        def two_pass_tile(kt, carry):
            scores(kt, sa_ref)

            @pl.when(kt == n_full)
            def _():
                mask_own_keys(sa_ref)
            accumulate(kt, sa_ref)
            return carry

        lax.fori_loop(0, n_full + 1, two_pass_tile, 0)

    acc = acc_ref[...]
    osel = acc[0:DH, :] * (1.0 / acc[DH:DH + 1, :])
    owin = owin_ref[...]

    heads = []
    for h in range(HPG):
        sl = slice(h * QB, (h + 1) * QB)
        heads.append(gt_ref[3 * h:3 * h + 1, :] * oc[:, sl]
                     + gt_ref[3 * h + 1:3 * h + 2, :] * osel[:, sl]
                     + gt_ref[3 * h + 2:3 * h + 3, :] * owin[:, sl])
    out_ref[...] = jnp.concatenate(heads, axis=0).T.astype(out_ref.dtype)


def _nsa(q_t, kcmp, vcmp_t, ks, vs_t, kw, vw_t, gt, ovl_t):
    S = ks.shape[1]
    NC, NB = S // CMP_STRIDE, S // SLC_BLOCK
    n_qb = S // QB
    grp = lambda g, b: (g, 0, 0)
    return pl.pallas_call(
        functools.partial(_nsa_kernel, S=S),
        grid=(NSA_GROUPS, n_qb),
        in_specs=[pl.BlockSpec((HPG, DH, QB), lambda g, b: (g, 0, b)),
                  pl.BlockSpec((HPG, DH, QB), lambda g, b: (g, 0, jnp.minimum(b + 1, n_qb - 1))),
                  pl.BlockSpec((None, NC, LANES), grp),
                  pl.BlockSpec((None, VROWS, NC), grp),
                  pl.BlockSpec((None, S, LANES), grp, pipeline_mode=pl.Buffered(1)),
                  pl.BlockSpec((None, VROWS, S), grp, pipeline_mode=pl.Buffered(1)),
                  pl.BlockSpec((None, WINDOW + S, LANES), grp, pipeline_mode=pl.Buffered(1)),
                  pl.BlockSpec((None, VROWS, WINDOW + S), grp, pipeline_mode=pl.Buffered(1)),
                  pl.BlockSpec((None, GATE_ROWS, QB), lambda g, b: (g, 0, b)),
                  pl.BlockSpec((NB, NC), lambda g, b: (0, 0))],
        out_specs=pl.BlockSpec((QB, HPG * DH), lambda g, b: (b, g)),
        out_shape=jax.ShapeDtypeStruct((S, NSA_HEADS * DH), BF16),
        scratch_shapes=[pltpu.VMEM((LANES, HPG * QB), BF16),
                        pltpu.VMEM((LANES, HPG * QB), BF16),
                        pltpu.VMEM((NB, QB), BF16),
                        pltpu.VMEM((NB, QB), BF16),
                        pltpu.VMEM((DH, HPG * QB), F32),
                        pltpu.VMEM((1, HPG * QB), F32),
                        pltpu.VMEM((VROWS, HPG * QB), F32),
                        pltpu.VMEM((TK, HPG * QB), F32),
                        pltpu.VMEM((TK_OLD, HPG * QB), BF16),
                        pltpu.VMEM((1, HPG * QB), F32),
                        pltpu.VMEM((1, HPG * QB), F32),
                        pltpu.VMEM((DH, HPG * QB), F32)],
        compiler_params=_params(("arbitrary", "arbitrary")), name="nsa",
    )(q_t, q_t, kcmp, vcmp_t, ks, vs_t, kw, vw_t, gt, ovl_t)


def _ret_kernel(q_ref, k_ref, v_ref, g_ref, dmat_ref, xi_ref, zeta_ref, dec_ref, o_ref, state_ref, *, tm):
    @pl.when(pl.program_id(0) == 0)
    def _():
        state_ref[...] = jnp.zeros_like(state_ref)

    for j in range(tm // RET_C):
        rows = slice(j * RET_C, (j + 1) * RET_C)
        for h in range(RET_HEADS):
            cols = slice(h * RET_D, (h + 1) * RET_D)
            qc, kc, vc = q_ref[rows, cols], k_ref[rows, cols], v_ref[rows, cols]
            st = state_ref[h]
            st_hi = st.astype(BF16)
            st_lo = (st - st_hi.astype(F32)).astype(BF16)
            sc = lax.dot_general(qc, kc, NT_DIMS, preferred_element_type=F32) * dmat_ref[h]
            inner = jnp.dot(sc.astype(BF16), vc, preferred_element_type=F32)
            cross = (jnp.dot(qc, st_hi, preferred_element_type=F32)
                     + jnp.dot(qc, st_lo, preferred_element_type=F32)) * xi_ref[h]
            kz = (kc.astype(F32) * zeta_ref[h]).astype(BF16)
            state_ref[h] = dec_ref[h] * st + lax.dot_general(kz, vc, TN_DIMS, preferred_element_type=F32)
            o = inner + cross
            o = o * lax.rsqrt(jnp.mean(o * o, axis=-1, keepdims=True) + EPS)
            gate = g_ref[rows, cols].astype(F32)
            o_ref[rows, cols] = (gate * _sigmoid(gate) * o).astype(o_ref.dtype)


def _retention(rq, rk, rv, rg, dmat, xi, zeta, dec, tm=512):
    S = rq.shape[0]
    W = RET_HEADS * RET_D
    row = lambda i: (i, 0)
    cst = lambda i: (0, 0, 0)
    blk = pl.BlockSpec((tm, W), row)
    tbl = pl.BlockSpec((RET_HEADS, RET_C, RET_D), cst)
    return pl.pallas_call(
        functools.partial(_ret_kernel, tm=tm),
        grid=(S // tm,),
        in_specs=[blk, blk, blk, blk, tbl, tbl, tbl, tbl],
        out_specs=blk,
        out_shape=jax.ShapeDtypeStruct((S, W), BF16),
        scratch_shapes=[pltpu.VMEM((RET_HEADS, RET_D, RET_D), F32)],
        compiler_params=_params(("arbitrary",)), name="retention",
    )(rq, rk, rv, rg, dmat, xi, zeta, dec)


def _mix_kernel(x_ref, g_ref, a_ref, r_ref, wg_ref, wpa_ref, wpb_ref, wo_ref, o_ref):
    x = x_ref[...]
    h = _rms(x, g_ref[...]).astype(BF16)
    ga = _sigmoid(lax.dot_general(h, wg_ref[0:D_MODEL, :], NT_DIMS, preferred_element_type=F32))
    gb = _sigmoid(lax.dot_general(h, wg_ref[D_MODEL:2 * D_MODEL, :], NT_DIMS, preferred_element_type=F32))
    mix = (ga * jnp.dot(a_ref[...], wpa_ref[...], preferred_element_type=F32)
           + gb * jnp.dot(r_ref[...], wpb_ref[...], preferred_element_type=F32))
    o_ref[...] = x + jnp.dot(mix.astype(BF16), wo_ref[...], preferred_element_type=F32)


def _mix(x, g, nsa_out, ret_out, wg, wpa, wpb, wo, tm=512):
    S = x.shape[0]
    row = lambda i: (i, 0)
    cst = lambda i: (0, 0)
    return pl.pallas_call(
        _mix_kernel,
        grid=(S // tm,),
        in_specs=[pl.BlockSpec((tm, D_MODEL), row), pl.BlockSpec((1, D_MODEL), cst),
                  pl.BlockSpec((tm, NSA_HEADS * DH), row), pl.BlockSpec((tm, RET_HEADS * RET_D), row),
                  pl.BlockSpec((2 * D_MODEL, D_MODEL), cst), pl.BlockSpec((512, D_MODEL), cst),
                  pl.BlockSpec((512, D_MODEL), cst), pl.BlockSpec((D_MODEL, D_MODEL), cst)],
        out_specs=pl.BlockSpec((tm, D_MODEL), row),
        out_shape=jax.ShapeDtypeStruct((S, D_MODEL), F32),
        compiler_params=_params(("arbitrary",)), name="mix",
    )(x, g, nsa_out, ret_out, wg, wpa, wpb, wo)


def _mlp_kernel(x_ref, g_ref, wu_ref, wd_ref, gf_ref, o_ref, *, hc):
    x = x_ref[...]
    h = _rms(x, g_ref[...]).astype(BF16)
    acc = x
    for c in range(MLP_HIDDEN // hc):
        u = jnp.maximum(jnp.dot(h, wu_ref[:, c * hc:(c + 1) * hc], preferred_element_type=F32), 0.0)
        acc = acc + jnp.dot((u * u).astype(BF16), wd_ref[c * hc:(c + 1) * hc, :], preferred_element_type=F32)
    o_ref[...] = _rms(acc, gf_ref[...])


def _mlp(x, g, wu, wd, gf, tm=512, hc=1024):
    S = x.shape[0]
    row = lambda i: (i, 0)
    cst = lambda i: (0, 0)
    return pl.pallas_call(
        functools.partial(_mlp_kernel, hc=hc),
        grid=(S // tm,),
        in_specs=[pl.BlockSpec((tm, D_MODEL), row), pl.BlockSpec((1, D_MODEL), cst),
                  pl.BlockSpec((D_MODEL, MLP_HIDDEN), cst), pl.BlockSpec((MLP_HIDDEN, D_MODEL), cst),
                  pl.BlockSpec((1, D_MODEL), cst)],
        out_specs=pl.BlockSpec((tm, D_MODEL), row),
        out_shape=jax.ShapeDtypeStruct((S, D_MODEL), F32),
        compiler_params=_params(("arbitrary",)), name="mlp",
    )(x, g, wu, wd, gf)


def _rope_freqs():
    half = ROPE_DIM // 2
    inv = jnp.exp(-math.log(ROPE_THETA) * jnp.arange(half, dtype=F32) * 2.0 / ROPE_DIM)
    hr = RET_D // 2
    inv_r = jnp.exp(-math.log(RET_BASE) * jnp.arange(hr, dtype=F32) * 2.0 / RET_D)
    return jnp.concatenate([inv_r, inv, jnp.zeros((LANES - hr - half,), F32)])[None, :]


def _decay_tables():
    H, C = RET_HEADS, RET_C
    log_g = jnp.log(1.0 - jnp.exp2(-5.0 - jnp.arange(H, dtype=F32)))
    n = jnp.arange(C, dtype=F32)
    rel = n[:, None] - n[None, :]
    dmat = jnp.where(rel >= 0, jnp.exp(log_g[:, None, None] * jnp.maximum(rel, 0.0)), 0.0)
    xi = jnp.exp(log_g[:, None] * (n + 1.0))
    zeta = jnp.exp(log_g[:, None] * (C - 1.0 - n))
    dec = jnp.exp(log_g * C)
    bc = lambda a: jnp.broadcast_to(a, (H, C, RET_D))
    return dmat, bc(xi[:, :, None]), bc(zeta[:, :, None]), bc(dec[:, None, None])


def _overlap_t(S):
    n_cmp = (S - CMP_BLOCK) // CMP_STRIDE + 1
    n_slc = S // SLC_BLOCK
    cs = np.arange(n_cmp)[:, None] * CMP_STRIDE
    ss = np.arange(n_slc)[None, :] * SLC_BLOCK
    ov = np.clip(np.minimum(cs + CMP_BLOCK, ss + SLC_BLOCK) - np.maximum(cs, ss), 0, None) / CMP_BLOCK
    out = np.zeros((n_slc, S // CMP_STRIDE), np.float32)
    out[:, :n_cmp] = ov.T
    return jnp.asarray(out, BF16)


def _v_rows(v_t):
    S = v_t.shape[1]
    return jnp.concatenate([v_t, jnp.ones((1, S), v_t.dtype), jnp.zeros((VROWS - DH - 1, S), v_t.dtype)], axis=0)


_IN_OFFSETS = np.cumsum((0, 512, 128, 128, 128, 128, 128, 128, 24, 512, 512, 512, 512, 1024, 1024))


_W_MOVES = tuple((int(_IN_OFFSETS[i]), dst, int(_IN_OFFSETS[i + 1] - _IN_OFFSETS[i])) for i, dst in
                 ((0, 0), (1, 512), (3, 640), (5, 768), (2, 896), (4, 1024), (6, 1152),
                  (8, 1408), (9, 1920), (10, 2432), (11, 2944)))


def _branches(x, positions, norm_mix, w_in, cmp_pos_k, cmp_pos_v, cmp_k_w1, cmp_k_w2, cmp_v_w1, cmp_v_w2):
    B, S, _ = x.shape
    assert B == 1 and norm_mix.shape[0] == 1 and S % TK == 0 and S >= max(WIN_KEYS, SLC_COUNT * SLC_BLOCK)
    xs = x[0]
    q_t, xh, ks, kw, vs_t, vw_t, gt, rq, rk, rv, rg = _proj(
        xs, norm_mix, jnp.swapaxes(w_in, 1, 2), positions[0].astype(F32)[:, None], _rope_freqs())

    w1 = jnp.stack([cmp_k_w1[0], cmp_v_w1[0]]).astype(BF16)
    pe = jnp.stack([cmp_pos_k[0], cmp_pos_v[0]]).reshape(2, 1, CMP_BLOCK * DH)
    pe = jnp.broadcast_to(pe, (2, BF16_ROWS, CMP_BLOCK * DH)).astype(BF16)
    w2 = jnp.pad(jnp.stack([cmp_k_w2[0], cmp_v_w2[0]]), ((0, 0), (0, 0), (0, LANES - DH))).astype(BF16)
    cmp = _compress(xh, w1, pe, w2)
    kcmp = cmp[0:NSA_GROUPS].astype(BF16)
    vcmp_t = jax.vmap(_v_rows)(cmp[NSA_GROUPS:, :, 0:DH].transpose(0, 2, 1).astype(BF16))

    vw_t = jnp.pad(vw_t, ((0, 0), (0, 0), (WINDOW, 0)))
    pad_keys = jnp.zeros((NSA_GROUPS, WINDOW, LANES), BF16).at[:, :, KILL_LANE].set(1.0)
    kw = jnp.concatenate([pad_keys, kw], axis=1)
    nsa_out = _nsa(q_t, kcmp, vcmp_t, ks, vs_t, kw, vw_t, gt, _overlap_t(S))

    ret_out = _retention(rq, rk, rv, rg, *_decay_tables())
    return nsa_out, ret_out


def kernel(x, positions, norm_mix, w_in, cmp_pos_k, cmp_pos_v, cmp_k_w1, cmp_k_w2, cmp_v_w1, cmp_v_w2,
           w_proj_a, w_proj_b, w_out, norm_mlp, w_up, w_down, norm_final):
    nsa_out, ret_out = _branches(x, positions, norm_mix, w_in, cmp_pos_k, cmp_pos_v,
                                 cmp_k_w1, cmp_k_w2, cmp_v_w1, cmp_v_w2)
    xs = x[0]
    w_gate = jnp.swapaxes(w_in, 1, 2)[0, _IN_OFFSETS[12]:_IN_OFFSETS[14], :].astype(BF16)
    x1 = _mix(xs, norm_mix, nsa_out, ret_out, w_gate, w_proj_a[0].astype(BF16), w_proj_b[0].astype(BF16),
              w_out[0].astype(BF16))
    y = _mlp(x1, norm_mlp, w_up[0].astype(BF16), w_down[0].astype(BF16), norm_final[None, :])
    return y[None]
```

```python
import functools
import math

import numpy as np
import jax
import jax.numpy as jnp
from jax import lax
from jax.experimental import pallas as pl
from jax.experimental.pallas import tpu as pltpu

F32 = jnp.float32
BF16 = jnp.bfloat16

D_MODEL = 1024
NSA_HEADS = 8
NSA_GROUPS = 2
HPG = NSA_HEADS // NSA_GROUPS
DH = 64
CMP_BLOCK = 32
CMP_STRIDE = 16
CMP_HIDDEN = 256
SLC_BLOCK = 64
SLC_COUNT = 16
WINDOW = 512
QB = 256
ROPE_THETA = 500000.0
ROPE_DIM = DH // 4
RET_HEADS = 4
RET_D = 128
RET_C = 128
RET_BASE = 10000.0
MLP_HIDDEN = 4 * D_MODEL
EPS = 1e-6
NEG = -1e30

LANES = 128
TK = 1024
TK_OLD = 2 * TK
BLK_PER_TILE = TK_OLD // SLC_BLOCK
IMP_ROWS = 64
KILL_LANE = DH + BLK_PER_TILE
REF_LANE = KILL_LANE + 1
REF_SLACK = 64.0
LOG2E = math.log2(math.e)
VROWS = 80
BF16_ROWS = 16
GATE_ROWS = 16
SLC_SHIFT = SLC_BLOCK.bit_length() - 1
WIN_KEYS = WINDOW + QB
VMEM_LIMIT = 56 * 1024 * 1024

NT_DIMS = (((1,), (1,)), ((), ()))
TN_DIMS = (((0,), (0,)), ((), ()))


def _sigmoid(v):
    return 1.0 / (1.0 + jnp.exp(-v))


def _rms(x, g):
    return x * lax.rsqrt(jnp.mean(x * x, axis=-1, keepdims=True) + EPS) * g


def _params(sem, flags=None):
    return pltpu.CompilerParams(dimension_semantics=sem, vmem_limit_bytes=VMEM_LIMIT, flags=flags)


_C_ROT = (0, 896)
_C_V = (896, 1280)
_C_G = (1280, 1408)
_C_RQ = (1408, 1920)
_C_RK = (1920, 2432)
_C_RV = (2432, 2944)
_C_RG = (2944, 3456)
PROJ_W = 3456


def _proj_kernel(x_ref, g_ref, win_ref, pos_ref, freq_ref,
                 q_ref, xh_ref, ks_ref, kw_ref, vs_ref, vw_ref, gt_ref, rq_ref, rk_ref, rv_ref, rg_ref,
                 kv_ref, w_ref, *, tm):
    @pl.when(pl.program_id(0) == 0)
    def _():
        for src, dst, n in _W_MOVES:
            w_ref[dst:dst + n, :] = win_ref[src:src + n, :].astype(BF16)
        n_gate = 3 * HPG
        g0 = _IN_OFFSETS[7]
        zeros = lambda n: jnp.zeros((n, D_MODEL), F32)
        gates = jnp.concatenate([win_ref[g0:g0 + n_gate, :], zeros(GATE_ROWS - n_gate),
                                 win_ref[g0 + n_gate:g0 + 2 * n_gate, :], zeros(LANES - GATE_ROWS - n_gate)],
                                axis=0)
        w_ref[_C_G[0]:_C_G[1], :] = gates.astype(BF16)

    h = _rms(x_ref[...], g_ref[...]).astype(BF16)
    lane = lax.broadcasted_iota(jnp.int32, (tm, LANES), 1)
    lo = lane < DH

    ang = pos_ref[...] * freq_ref[...]
    c, s = jnp.cos(ang), jnp.sin(ang)
    half_n = ROPE_DIM // 2
    cr = jnp.where(lo, c, pltpu.roll(c, DH, 1))
    sr = jnp.where(lo, -s, pltpu.roll(s, DH, 1))
    dim = lane & (DH - 1)
    first, second = dim < half_n, (dim >= half_n) & (dim < ROPE_DIM)
    c1 = jnp.where(lo, pltpu.roll(c, DH, 1), c)
    s1 = jnp.where(lo, pltpu.roll(s, DH, 1), s)
    c2 = jnp.where(lo, pltpu.roll(c, DH + half_n, 1), pltpu.roll(c, half_n, 1))
    s2 = jnp.where(lo, pltpu.roll(s, DH + half_n, 1), pltpu.roll(s, half_n, 1))
    cn = jnp.where(first, c1, jnp.where(second, c2, 1.0))
    sa = jnp.where(first, -s1, 0.0)
    sb = jnp.where(second, s2, 0.0)

    def dot(c):
        return lax.dot_general(h, w_ref[c[0]:c[1], :], NT_DIMS, preferred_element_type=F32)

    def tile(y, i):
        return y[:, i * LANES:(i + 1) * LANES]

    def rope_n(t):
        return t * cn + pltpu.roll(t, LANES - half_n, 1) * sa + pltpu.roll(t, half_n, 1) * sb

    def rope_r(t):
        return t * cr + pltpu.roll(t, RET_D // 2, 1) * sr

    def split(t):
        return jnp.where(lo, t, 0.0), jnp.where(lo, pltpu.roll(t, DH, 1), 0.0)

    def v_rows(dst_ref, t):
        tt = t.T
        tail = jnp.where(lax.broadcasted_iota(jnp.int32, (VROWS - DH, tm), 0) == 0, 1.0, 0.0).astype(BF16)
        for g in range(NSA_GROUPS):
            dst_ref[g, 0:DH, :] = tt[g * DH:(g + 1) * DH, :].astype(BF16)
            dst_ref[g, DH:VROWS, :] = tail

    yr = dot(_C_ROT)
    for i in range(4):
        tt = (rope_n(tile(yr, i)) * (DH ** -0.5 * LOG2E)).T
        q_ref[2 * i] = tt[0:DH, :].astype(BF16)
        q_ref[2 * i + 1] = tt[DH:2 * DH, :].astype(BF16)
    row = pl.program_id(0) * tm + lax.broadcasted_iota(jnp.int32, (tm, LANES), 0)
    blk = (row >> SLC_SHIFT) & (BLK_PER_TILE - 1)
    onehot = jnp.where(((lane - DH) == blk) | (lane == REF_LANE), 1.0, 0.0)
    a, b = split(rope_n(tile(yr, 5)))
    ks_ref[0] = jnp.where(lo, a, onehot).astype(BF16)
    ks_ref[1] = jnp.where(lo, b, onehot).astype(BF16)
    a, b = split(rope_n(tile(yr, 6)))
    kw_ref[0] = a.astype(BF16)
    kw_ref[1] = b.astype(BF16)

    yv = dot(_C_V)
    v_rows(vs_ref, tile(yv, 1))
    v_rows(vw_ref, tile(yv, 2))
    gt = _sigmoid(dot(_C_G)).T
    for g in range(NSA_GROUPS):
        gt_ref[g] = gt[g * GATE_ROWS:(g + 1) * GATE_ROWS, :]

    kv_ref[0] = rope_n(tile(yr, 4))
    kv_ref[1] = tile(yv, 0)
    lo_r = lax.broadcasted_iota(jnp.int32, (tm // CMP_STRIDE, LANES), 1) < DH
    for typ in range(2):
        for u in range(CMP_STRIDE // 2):
            pa = kv_ref[typ, pl.ds(2 * u, tm // CMP_STRIDE, stride=CMP_STRIDE), :]
            pb = kv_ref[typ, pl.ds(2 * u + 1, tm // CMP_STRIDE, stride=CMP_STRIDE), :]
            cols = slice(u * LANES, (u + 1) * LANES)
            xh_ref[2 * typ, :, cols] = jnp.where(lo_r, pa, pltpu.roll(pb, DH, 1)).astype(BF16)
            xh_ref[2 * typ + 1, :, cols] = jnp.where(lo_r, pltpu.roll(pa, DH, 1), pb).astype(BF16)

    yq = dot(_C_RQ)
    yk = dot(_C_RK)
    for i in range(RET_HEADS):
        sl = slice(i * LANES, (i + 1) * LANES)
        rq_ref[:, sl] = rope_r(tile(yq, i)).astype(BF16)
        rk_ref[:, sl] = (rope_r(tile(yk, i)) * (RET_D ** -0.5)).astype(BF16)
    rv_ref[...] = dot(_C_RV).astype(BF16)
    rg_ref[...] = dot(_C_RG).astype(BF16)


def _proj(x, g, w, pos, freq, tm=512):
    S = x.shape[0]
    row = lambda i: (i, 0)
    const = lambda i: (0, 0)
    mid = lambda i: (0, i, 0)
    last = lambda i: (0, 0, i)
    W = RET_HEADS * RET_D
    out_shape = (
        jax.ShapeDtypeStruct((NSA_HEADS, DH, S), BF16),
        jax.ShapeDtypeStruct((2 * NSA_GROUPS, S // CMP_STRIDE, CMP_STRIDE * DH), BF16),
        jax.ShapeDtypeStruct((NSA_GROUPS, S, LANES), BF16),
        jax.ShapeDtypeStruct((NSA_GROUPS, S, LANES), BF16),
        jax.ShapeDtypeStruct((NSA_GROUPS, VROWS, S), BF16),
        jax.ShapeDtypeStruct((NSA_GROUPS, VROWS, S), BF16),
        jax.ShapeDtypeStruct((NSA_GROUPS, GATE_ROWS, S), F32),
        jax.ShapeDtypeStruct((S, W), BF16),
        jax.ShapeDtypeStruct((S, W), BF16),
        jax.ShapeDtypeStruct((S, W), BF16),
        jax.ShapeDtypeStruct((S, W), BF16),
    )
    out_specs = (
        pl.BlockSpec((NSA_HEADS, DH, tm), last),
        pl.BlockSpec((2 * NSA_GROUPS, tm // CMP_STRIDE, CMP_STRIDE * DH), mid),
        pl.BlockSpec((NSA_GROUPS, tm, LANES), mid),
        pl.BlockSpec((NSA_GROUPS, tm, LANES), mid),
        pl.BlockSpec((NSA_GROUPS, VROWS, tm), last),
        pl.BlockSpec((NSA_GROUPS, VROWS, tm), last),
        pl.BlockSpec((NSA_GROUPS, GATE_ROWS, tm), last),
        pl.BlockSpec((tm, W), row),
        pl.BlockSpec((tm, W), row),
        pl.BlockSpec((tm, W), row),
        pl.BlockSpec((tm, W), row),
    )
    return pl.pallas_call(
        functools.partial(_proj_kernel, tm=tm),
        grid=(S // tm,),
        in_specs=[pl.BlockSpec((tm, D_MODEL), row), pl.BlockSpec((1, D_MODEL), const),
                  pl.BlockSpec((None,) + w.shape[1:], lambda i: (0, 0, 0), pipeline_mode=pl.Buffered(1)),
                  pl.BlockSpec((tm, 1), row),
                  pl.BlockSpec((1, LANES), const)],
        out_specs=out_specs, out_shape=out_shape,
        scratch_shapes=[pltpu.VMEM((2, tm, LANES), F32),
                        pltpu.VMEM((PROJ_W, D_MODEL), BF16)],
        compiler_params=_params(("arbitrary",)), name="proj",
    )(x, g, w, pos, freq)


def _compress_kernel(x_ref, w1_ref, pe_ref, w2_ref, o_ref):
    x = x_ref[...]
    half = CMP_STRIDE * DH
    a = jnp.dot(x, w1_ref[0:half, :], preferred_element_type=F32)
    b = jnp.dot(x, w1_ref[half:2 * half, :], preferred_element_type=F32)
    peb = jnp.dot(pe_ref[...], w1_ref[...], preferred_element_type=F32)[0:1, :]
    hid = a + pltpu.roll(b, x.shape[0] - 1, 0) + peb
    hid = hid * _sigmoid(hid)
    o_ref[...] = jnp.dot(hid.astype(BF16), w2_ref[...], preferred_element_type=F32)


def _compress(xh, w1, pe, w2):
    n_half = xh.shape[1]
    return pl.pallas_call(
        _compress_kernel,
        grid=(2 * NSA_GROUPS,),
        in_specs=[pl.BlockSpec((None, n_half, CMP_STRIDE * DH), lambda i: (i, 0, 0)),
                  pl.BlockSpec((None, CMP_BLOCK * DH, CMP_HIDDEN), lambda i: (i // NSA_GROUPS, 0, 0)),
                  pl.BlockSpec((None, BF16_ROWS, CMP_BLOCK * DH), lambda i: (i // NSA_GROUPS, 0, 0)),
                  pl.BlockSpec((None, CMP_HIDDEN, LANES), lambda i: (i // NSA_GROUPS, 0, 0))],
        out_specs=pl.BlockSpec((None, n_half, LANES), lambda i: (i, 0, 0)),
        out_shape=jax.ShapeDtypeStruct((2 * NSA_GROUPS, n_half, LANES), F32),
        compiler_params=_params(("arbitrary",)), name="compress",
    )(xh, w1, pe, w2)


def _nsa_kernel(q_ref, qn_ref, kcmp_ref, vcmp_ref, ks_ref, vs_ref, kw_ref, vw_ref, gt_ref, ovl_ref,
                out_ref, qa_ref, qs_ref, bias_ref, biasn_ref, ocn_ref, m_ref, acc_ref,
                sa_ref, p_ref, c_ref, rise_ref, owin_ref, *, S):
    NC = S // CMP_STRIDE
    NB = S // SLC_BLOCK
    NQ = HPG * QB
    b = pl.program_id(1)
    s0 = b * QB
    lane_q = lax.broadcasted_iota(jnp.int32, (1, NQ), 1) & (QB - 1)
    band_row = lax.broadcasted_iota(jnp.int32, (QB, NQ), 0)
    band_tq = lax.broadcasted_iota(jnp.int32, (QB, NQ), 1) & (QB - 1)

    def sel_scores(qsrc_ref, cr):
        qs_ref[0:DH, :] = jnp.concatenate([qsrc_ref[h] for h in range(HPG)], axis=1)
        qs_ref[DH:LANES, :] = jnp.zeros((LANES - DH, NQ), BF16)
        return jnp.dot(kcmp_ref[0:cr, :], qs_ref[...], preferred_element_type=F32)

    def sel_probs(sc, blk):
        c_end = lax.broadcasted_iota(jnp.int32, sc.shape, 0) * CMP_STRIDE + (CMP_BLOCK - 1)
        sc = jnp.where(c_end <= blk * QB + lane_q, sc, NEG)
        return jnp.exp2(sc - jnp.max(sc, axis=0, keepdims=True)).astype(BF16)

    def sel_matmuls(pc):
        cr = pc.shape[0]
        ocl = jnp.dot(vcmp_ref[:, 0:cr], pc, preferred_element_type=F32)
        ir, ic = IMP_ROWS, 4 * IMP_ROWS
        if cr % (ic // 2):
            return ocl, jnp.dot(ovl_ref[0:cr // 4, 0:cr], pc, preferred_element_type=F32)
        parts = []
        for i in range(pl.cdiv(cr, ic)):
            c0, c1 = max(i - 1, 0) * ic, min((i + 1) * ic, cr)
            parts.append(jnp.dot(ovl_ref[i * ir:c1 // 4, c0:c1], pc[c0:c1, :], preferred_element_type=F32))
        return ocl, jnp.concatenate(parts, axis=0)

    def sel_choose(ocl, imp4, blk):
        t_row = blk * QB + lane_q
        any_valid = jnp.where(t_row >= CMP_BLOCK - 1, 1.0, 0.0)
        rc = any_valid / ocl[DH:DH + 1, :]
        ocn_ref[...] = ocl[0:DH, :] * rc

        imp4 = imp4 * rc
        imp = imp4[:, 0:QB] + imp4[:, QB:2 * QB] + imp4[:, 2 * QB:3 * QB] + imp4[:, 3 * QB:4 * QB]
        nb = imp.shape[0]

        jidx = lax.broadcasted_iota(jnp.int32, (nb, QB), 0)
        cur = (blk * QB + lax.broadcasted_iota(jnp.int32, (1, QB), 1)) >> SLC_SHIFT
        forced = (jidx == 0) | (jidx == cur) | (jidx == cur - 1)
        future = jidx > cur
        cand = jnp.where(forced | future, -jnp.inf, imp)

        def select(break_ties):
            v = cand
            for _ in range(SLC_COUNT - 3):
                mx = jnp.max(v, axis=0, keepdims=True)
                hit = v == mx
                if break_ties:
                    hit = jidx == jnp.min(jnp.where(hit, jidx, nb), axis=0, keepdims=True)
                v = jnp.where(hit, -jnp.inf, v)
            return jnp.where((v == -jnp.inf) & (jidx <= cur), 0.0, NEG)

        bias_t = select(False)
        n_sel = jnp.sum(jnp.where(bias_t == 0.0, 1.0, 0.0), axis=0, keepdims=True)
        biasn_ref[0:nb, :] = bias_t.astype(BF16)
        if nb < NB:
            biasn_ref[nb:NB, :] = jnp.full((NB - nb, QB), NEG, BF16)

        def redo_if_tied():
            @pl.when(jnp.max(n_sel) > SLC_COUNT)
            def _():
                biasn_ref[0:nb, :] = select(True).astype(BF16)
        return redo_if_tied

    cr_need = jnp.minimum((b + 2) * (QB // CMP_STRIDE) - 1, NC)
    cr_parts = next(n for n in (8, 4, 2, 1) if (NC // n) % (2 * IMP_ROWS) == 0 or n == 1)
    cr_sizes = tuple(NC * (i + 1) // cr_parts for i in range(cr_parts))

    @pl.when(b == 0)
    def _():
        ocl, imp4 = sel_matmuls(sel_probs(sel_scores(q_ref, cr_sizes[0]), 0))
        sel_choose(ocl, imp4, 0)()

    bias_ref[...] = biasn_ref[...]
    oc = ocn_ref[...]

    qa_ref[0:DH, :] = jnp.concatenate([q_ref[h] for h in range(HPG)], axis=1)
    qa_ref[DH:KILL_LANE, :] = jnp.zeros((KILL_LANE - DH, NQ), BF16)

    def set_bias_rows(grp):
        bias_rows = bias_ref[pl.ds(pl.multiple_of(grp * BLK_PER_TILE, BLK_PER_TILE), BLK_PER_TILE), :]
        qa_ref[DH:DH + BLK_PER_TILE, :] = jnp.concatenate([bias_rows] * HPG, axis=1)

    def set_flag_rows(ref):
        r = lax.broadcasted_iota(jnp.int32, (LANES - KILL_LANE, NQ), 0)
        rows = jnp.where(r == 0, NEG, 0.0) if ref is None else jnp.where(r == 0, NEG, jnp.where(r == 1, -ref, 0.0))
        qa_ref[KILL_LANE:LANES, :] = rows.astype(BF16)

    def scores(kt, s_ref):
        k0 = pl.multiple_of(kt * TK, TK)
        set_bias_rows(kt // 2)
        s_ref[...] = jnp.dot(ks_ref[pl.ds(k0, TK), :], qa_ref[...], preferred_element_type=F32)

    def accumulate(kt, s_ref):
        k0 = pl.multiple_of(kt * TK, TK)
        m_old = m_ref[...]
        m_new = jnp.maximum(m_old, jnp.max(s_ref[...], axis=0, keepdims=True))
        p_ref[0:TK, :] = jnp.exp2(s_ref[...] - m_new).astype(BF16)
        acc_ref[...] = (jnp.exp2(m_old - m_new) * acc_ref[...]
                        + jnp.dot(vs_ref[:, pl.ds(k0, TK)], p_ref[0:TK, :], preferred_element_type=F32))
        m_ref[...] = m_new

    def mask_own_keys(s_ref):
        band = pl.ds(pl.multiple_of(s0 - n_full * TK, QB), QB)
        s_ref[band, :] = jnp.where(band_row <= band_tq, s_ref[band, :], NEG)

    n_full = s0 // TK

    m_ref[...] = jnp.full((1, NQ), NEG, F32)
    acc_ref[...] = jnp.zeros((VROWS, NQ), F32)
    set_flag_rows(None)

    def first_region(cr):
        sc = sel_scores(qn_ref, cr)
        sw = jnp.dot(kw_ref[pl.ds(pl.multiple_of(s0, QB), WIN_KEYS), :], qa_ref[...],
                     preferred_element_type=F32)
        scores(n_full, sa_ref)

        pc = sel_probs(sc, b + 1)
        ocl, imp4 = sel_matmuls(pc)
        redo_if_tied = sel_choose(ocl, imp4, b + 1)

        sw = jnp.concatenate([jnp.where(band_row > band_tq, sw[0:QB], NEG),
                              sw[QB:WINDOW],
                              jnp.where(band_row <= band_tq, sw[WINDOW:WIN_KEYS], NEG)], axis=0)
        pw = jnp.exp2(sw - jnp.max(sw, axis=0, keepdims=True)).astype(BF16)
        mask_own_keys(sa_ref)

        ow = jnp.dot(vw_ref[:, pl.ds(pl.multiple_of(s0, QB), WIN_KEYS)], pw, preferred_element_type=F32)
        owin_ref[...] = ow[0:DH, :] * (1.0 / ow[DH:DH + 1, :])
        accumulate(n_full, sa_ref)

        m = m_ref[...]
        c0 = m.astype(BF16).astype(F32)
        c_ref[...] = c0
        acc_ref[...] = acc_ref[...] * jnp.exp2(m - c0)
        rise_ref[...] = jnp.zeros((1, NQ), F32)
        redo_if_tied()

    for i, cr in enumerate(cr_sizes):
        fits = cr_need <= cr
        if i > 0:
            fits = fits & (cr_need > cr_sizes[i - 1])
        pl.when(fits)(functools.partial(first_region, cr))

    def older_keys(grp, k0, rows):
        c = c_ref[...]
        set_bias_rows(grp)
        set_flag_rows(c)
        s = jnp.dot(ks_ref[pl.ds(k0, rows), :], qa_ref[...], preferred_element_type=F32)
        mt = jnp.max(s, axis=0, keepdims=True)
        p_ref[0:rows, :] = jnp.exp2(s).astype(BF16)
        pv = jnp.dot(vs_ref[:, pl.ds(k0, rows)], p_ref[0:rows, :], preferred_element_type=F32)
        c_new = (c + jnp.maximum(mt, 0.0)).astype(BF16).astype(F32)
        acc_ref[...] = (acc_ref[...] + pv) * jnp.exp2(c - c_new)
        c_ref[...] = c_new
        rise_ref[...] = jnp.maximum(rise_ref[...], mt)

    def older_pair(jj, carry):
        older_keys(jj, pl.multiple_of(jj * TK_OLD, TK_OLD), TK_OLD)
        return carry

    lax.fori_loop(0, n_full // 2, older_pair, 0)

    @pl.when(n_full % 2 == 1)
    def _():
        older_keys(n_full // 2, pl.multiple_of((n_full - 1) * TK, TK), TK)

    @pl.when(jnp.max(rise_ref[...]) > REF_SLACK)
    def _():
        m_ref[...] = jnp.full((1, NQ), NEG, F32)
        acc_ref[...] = jnp.zeros((VROWS, NQ), F32)
        set_flag_rows(None)

        def two_pass_tile(kt, carry):
            scores(kt, sa_ref)

            @pl.when(kt == n_full)
            def _():
                mask_own_keys(sa_ref)
            accumulate(kt, sa_ref)
            return carry

        lax.fori_loop(0, n_full + 1, two_pass_tile, 0)

    acc = acc_ref[...]
    osel = acc[0:DH, :] * (1.0 / acc[DH:DH + 1, :])
    owin = owin_ref[...]

    heads = []
    for h in range(HPG):
        sl = slice(h * QB, (h + 1) * QB)
        heads.append(gt_ref[3 * h:3 * h + 1, :] * oc[:, sl]
                     + gt_ref[3 * h + 1:3 * h + 2, :] * osel[:, sl]
                     + gt_ref[3 * h + 2:3 * h + 3, :] * owin[:, sl])
    out_ref[...] = jnp.concatenate(heads, axis=0).T.astype(out_ref.dtype)


def _nsa(q_t, kcmp, vcmp_t, ks, vs_t, kw, vw_t, gt, ovl_t):
    S = ks.shape[1]
    NC, NB = S // CMP_STRIDE, S // SLC_BLOCK
    n_qb = S // QB
    grp = lambda g, b: (g, 0, 0)
    return pl.pallas_call(
        functools.partial(_nsa_kernel, S=S),
        grid=(NSA_GROUPS, n_qb),
        in_specs=[pl.BlockSpec((HPG, DH, QB), lambda g, b: (g, 0, b)),
                  pl.BlockSpec((HPG, DH, QB), lambda g, b: (g, 0, jnp.minimum(b + 1, n_qb - 1))),
                  pl.BlockSpec((None, NC, LANES), grp),
                  pl.BlockSpec((None, VROWS, NC), grp),
                  pl.BlockSpec((None, S, LANES), grp, pipeline_mode=pl.Buffered(1)),
                  pl.BlockSpec((None, VROWS, S), grp, pipeline_mode=pl.Buffered(1)),
                  pl.BlockSpec((None, WINDOW + S, LANES), grp, pipeline_mode=pl.Buffered(1)),
                  pl.BlockSpec((None, VROWS, WINDOW + S), grp, pipeline_mode=pl.Buffered(1)),
                  pl.BlockSpec((None, GATE_ROWS, QB), lambda g, b: (g, 0, b)),
                  pl.BlockSpec((NB, NC), lambda g, b: (0, 0))],
        out_specs=pl.BlockSpec((QB, HPG * DH), lambda g, b: (b, g)),
        out_shape=jax.ShapeDtypeStruct((S, NSA_HEADS * DH), BF16),
        scratch_shapes=[pltpu.VMEM((LANES, HPG * QB), BF16),
                        pltpu.VMEM((LANES, HPG * QB), BF16),
                        pltpu.VMEM((NB, QB), BF16),
                        pltpu.VMEM((NB, QB), BF16),
                        pltpu.VMEM((DH, HPG * QB), F32),
                        pltpu.VMEM((1, HPG * QB), F32),
                        pltpu.VMEM((VROWS, HPG * QB), F32),
                        pltpu.VMEM((TK, HPG * QB), F32),
                        pltpu.VMEM((TK_OLD, HPG * QB), BF16),
                        pltpu.VMEM((1, HPG * QB), F32),
                        pltpu.VMEM((1, HPG * QB), F32),
                        pltpu.VMEM((DH, HPG * QB), F32)],
        compiler_params=_params(("arbitrary", "arbitrary")), name="nsa",
    )(q_t, q_t, kcmp, vcmp_t, ks, vs_t, kw, vw_t, gt, ovl_t)


def _ret_kernel(q_ref, k_ref, v_ref, g_ref, dmat_ref, xi_ref, zeta_ref, dec_ref, o_ref, state_ref, *, tm):
    @pl.when(pl.program_id(0) == 0)
    def _():
        state_ref[...] = jnp.zeros_like(state_ref)

    for j in range(tm // RET_C):
        rows = slice(j * RET_C, (j + 1) * RET_C)
        for h in range(RET_HEADS):
            cols = slice(h * RET_D, (h + 1) * RET_D)
            qc, kc, vc = q_ref[rows, cols], k_ref[rows, cols], v_ref[rows, cols]
            st = state_ref[h]
            st_hi = st.astype(BF16)
            st_lo = (st - st_hi.astype(F32)).astype(BF16)
            sc = lax.dot_general(qc, kc, NT_DIMS, preferred_element_type=F32) * dmat_ref[h]
            inner = jnp.dot(sc.astype(BF16), vc, preferred_element_type=F32)
            cross = (jnp.dot(qc, st_hi, preferred_element_type=F32)
                     + jnp.dot(qc, st_lo, preferred_element_type=F32)) * xi_ref[h]
            kz = (kc.astype(F32) * zeta_ref[h]).astype(BF16)
            state_ref[h] = dec_ref[h] * st + lax.dot_general(kz, vc, TN_DIMS, preferred_element_type=F32)
            o = inner + cross
            o = o * lax.rsqrt(jnp.mean(o * o, axis=-1, keepdims=True) + EPS)
            gate = g_ref[rows, cols].astype(F32)
            o_ref[rows, cols] = (gate * _sigmoid(gate) * o).astype(o_ref.dtype)


def _retention(rq, rk, rv, rg, dmat, xi, zeta, dec, tm=512):
    S = rq.shape[0]
    W = RET_HEADS * RET_D
    row = lambda i: (i, 0)
    cst = lambda i: (0, 0, 0)
    blk = pl.BlockSpec((tm, W), row)
    tbl = pl.BlockSpec((RET_HEADS, RET_C, RET_D), cst)
    return pl.pallas_call(
        functools.partial(_ret_kernel, tm=tm),
        grid=(S // tm,),
        in_specs=[blk, blk, blk, blk, tbl, tbl, tbl, tbl],
        out_specs=blk,
        out_shape=jax.ShapeDtypeStruct((S, W), BF16),
        scratch_shapes=[pltpu.VMEM((RET_HEADS, RET_D, RET_D), F32)],
        compiler_params=_params(("arbitrary",)), name="retention",
    )(rq, rk, rv, rg, dmat, xi, zeta, dec)


def _mix_kernel(x_ref, g_ref, a_ref, r_ref, wg_ref, wpa_ref, wpb_ref, wo_ref, o_ref):
    x = x_ref[...]
    h = _rms(x, g_ref[...]).astype(BF16)
    ga = _sigmoid(lax.dot_general(h, wg_ref[0:D_MODEL, :], NT_DIMS, preferred_element_type=F32))
    gb = _sigmoid(lax.dot_general(h, wg_ref[D_MODEL:2 * D_MODEL, :], NT_DIMS, preferred_element_type=F32))
    mix = (ga * jnp.dot(a_ref[...], wpa_ref[...], preferred_element_type=F32)
           + gb * jnp.dot(r_ref[...], wpb_ref[...], preferred_element_type=F32))
    o_ref[...] = x + jnp.dot(mix.astype(BF16), wo_ref[...], preferred_element_type=F32)


def _mix(x, g, nsa_out, ret_out, wg, wpa, wpb, wo, tm=512):
    S = x.shape[0]
    row = lambda i: (i, 0)
    cst = lambda i: (0, 0)
    return pl.pallas_call(
        _mix_kernel,
        grid=(S // tm,),
        in_specs=[pl.BlockSpec((tm, D_MODEL), row), pl.BlockSpec((1, D_MODEL), cst),
                  pl.BlockSpec((tm, NSA_HEADS * DH), row), pl.BlockSpec((tm, RET_HEADS * RET_D), row),
                  pl.BlockSpec((2 * D_MODEL, D_MODEL), cst), pl.BlockSpec((512, D_MODEL), cst),
                  pl.BlockSpec((512, D_MODEL), cst), pl.BlockSpec((D_MODEL, D_MODEL), cst)],
        out_specs=pl.BlockSpec((tm, D_MODEL), row),
        out_shape=jax.ShapeDtypeStruct((S, D_MODEL), F32),
        compiler_params=_params(("arbitrary",)), name="mix",
    )(x, g, nsa_out, ret_out, wg, wpa, wpb, wo)


def _mlp_kernel(x_ref, g_ref, wu_ref, wd_ref, gf_ref, o_ref, *, hc):
    x = x_ref[...]
    h = _rms(x, g_ref[...]).astype(BF16)
    acc = x
    for c in range(MLP_HIDDEN // hc):
        u = jnp.maximum(jnp.dot(h, wu_ref[:, c * hc:(c + 1) * hc], preferred_element_type=F32), 0.0)
        acc = acc + jnp.dot((u * u).astype(BF16), wd_ref[c * hc:(c + 1) * hc, :], preferred_element_type=F32)
    o_ref[...] = _rms(acc, gf_ref[...])


def _mlp(x, g, wu, wd, gf, tm=512, hc=1024):
    S = x.shape[0]
    row = lambda i: (i, 0)
    cst = lambda i: (0, 0)
    return pl.pallas_call(
        functools.partial(_mlp_kernel, hc=hc),
        grid=(S // tm,),
        in_specs=[pl.BlockSpec((tm, D_MODEL), row), pl.BlockSpec((1, D_MODEL), cst),
                  pl.BlockSpec((D_MODEL, MLP_HIDDEN), cst), pl.BlockSpec((MLP_HIDDEN, D_MODEL), cst),
                  pl.BlockSpec((1, D_MODEL), cst)],
        out_specs=pl.BlockSpec((tm, D_MODEL), row),
        out_shape=jax.ShapeDtypeStruct((S, D_MODEL), F32),
        compiler_params=_params(("arbitrary",)), name="mlp",
    )(x, g, wu, wd, gf)


def _rope_freqs():
    half = ROPE_DIM // 2
    inv = jnp.exp(-math.log(ROPE_THETA) * jnp.arange(half, dtype=F32) * 2.0 / ROPE_DIM)
    hr = RET_D // 2
    inv_r = jnp.exp(-math.log(RET_BASE) * jnp.arange(hr, dtype=F32) * 2.0 / RET_D)
    return jnp.concatenate([inv_r, inv, jnp.zeros((LANES - hr - half,), F32)])[None, :]


def _decay_tables():
    H, C = RET_HEADS, RET_C
    log_g = jnp.log(1.0 - jnp.exp2(-5.0 - jnp.arange(H, dtype=F32)))
    n = jnp.arange(C, dtype=F32)
    rel = n[:, None] - n[None, :]
    dmat = jnp.where(rel >= 0, jnp.exp(log_g[:, None, None] * jnp.maximum(rel, 0.0)), 0.0)
    xi = jnp.exp(log_g[:, None] * (n + 1.0))
    zeta = jnp.exp(log_g[:, None] * (C - 1.0 - n))
    dec = jnp.exp(log_g * C)
    bc = lambda a: jnp.broadcast_to(a, (H, C, RET_D))
    return dmat, bc(xi[:, :, None]), bc(zeta[:, :, None]), bc(dec[:, None, None])


def _overlap_t(S):
    n_cmp = (S - CMP_BLOCK) // CMP_STRIDE + 1
    n_slc = S // SLC_BLOCK
    cs = np.arange(n_cmp)[:, None] * CMP_STRIDE
    ss = np.arange(n_slc)[None, :] * SLC_BLOCK
    ov = np.clip(np.minimum(cs + CMP_BLOCK, ss + SLC_BLOCK) - np.maximum(cs, ss), 0, None) / CMP_BLOCK
    out = np.zeros((n_slc, S // CMP_STRIDE), np.float32)
    out[:, :n_cmp] = ov.T
    return jnp.asarray(out, BF16)


def _v_rows(v_t):
    S = v_t.shape[1]
    return jnp.concatenate([v_t, jnp.ones((1, S), v_t.dtype), jnp.zeros((VROWS - DH - 1, S), v_t.dtype)], axis=0)


_IN_OFFSETS = np.cumsum((0, 512, 128, 128, 128, 128, 128, 128, 24, 512, 512, 512, 512, 1024, 1024))


_W_MOVES = tuple((int(_IN_OFFSETS[i]), dst, int(_IN_OFFSETS[i + 1] - _IN_OFFSETS[i])) for i, dst in
                 ((0, 0), (1, 512), (3, 640), (5, 768), (2, 896), (4, 1024), (6, 1152),
                  (8, 1408), (9, 1920), (10, 2432), (11, 2944)))


def _branches(x, positions, norm_mix, w_in, cmp_pos_k, cmp_pos_v, cmp_k_w1, cmp_k_w2, cmp_v_w1, cmp_v_w2):
    B, S, _ = x.shape
    assert B == 1 and norm_mix.shape[0] == 1 and S % TK == 0 and S >= max(WIN_KEYS, SLC_COUNT * SLC_BLOCK)
    xs = x[0]
    q_t, xh, ks, kw, vs_t, vw_t, gt, rq, rk, rv, rg = _proj(
        xs, norm_mix, jnp.swapaxes(w_in, 1, 2), positions[0].astype(F32)[:, None], _rope_freqs())

    w1 = jnp.stack([cmp_k_w1[0], cmp_v_w1[0]]).astype(BF16)
    pe = jnp.stack([cmp_pos_k[0], cmp_pos_v[0]]).reshape(2, 1, CMP_BLOCK * DH)
    pe = jnp.broadcast_to(pe, (2, BF16_ROWS, CMP_BLOCK * DH)).astype(BF16)
    w2 = jnp.pad(jnp.stack([cmp_k_w2[0], cmp_v_w2[0]]), ((0, 0), (0, 0), (0, LANES - DH))).astype(BF16)
    cmp = _compress(xh, w1, pe, w2)
    kcmp = cmp[0:NSA_GROUPS].astype(BF16)
    vcmp_t = jax.vmap(_v_rows)(cmp[NSA_GROUPS:, :, 0:DH].transpose(0, 2, 1).astype(BF16))

    vw_t = jnp.pad(vw_t, ((0, 0), (0, 0), (WINDOW, 0)))
    pad_keys = jnp.zeros((NSA_GROUPS, WINDOW, LANES), BF16).at[:, :, KILL_LANE].set(1.0)
    kw = jnp.concatenate([pad_keys, kw], axis=1)
    nsa_out = _nsa(q_t, kcmp, vcmp_t, ks, vs_t, kw, vw_t, gt, _overlap_t(S))

    ret_out = _retention(rq, rk, rv, rg, *_decay_tables())
    return nsa_out, ret_out


def kernel(x, positions, norm_mix, w_in, cmp_pos_k, cmp_pos_v, cmp_k_w1, cmp_k_w2, cmp_v_w1, cmp_v_w2,
           w_proj_a, w_proj_b, w_out, norm_mlp, w_up, w_down, norm_final):
    nsa_out, ret_out = _branches(x, positions, norm_mix, w_in, cmp_pos_k, cmp_pos_v,
                                 cmp_k_w1, cmp_k_w2, cmp_v_w1, cmp_v_w2)
    xs = x[0]
    w_gate = jnp.swapaxes(w_in, 1, 2)[0, _IN_OFFSETS[12]:_IN_OFFSETS[14], :].astype(BF16)
    x1 = _mix(xs, norm_mix, nsa_out, ret_out, w_gate, w_proj_a[0].astype(BF16), w_proj_b[0].astype(BF16),
              w_out[0].astype(BF16))
    y = _mlp(x1, norm_mlp, w_up[0].astype(BF16), w_down[0].astype(BF16), norm_final[None, :])
    return y[None]
```

```python
import functools
import math

import numpy as np
import jax
import jax.numpy as jnp
from jax import lax
from jax.experimental import pallas as pl
from jax.experimental.pallas import tpu as pltpu

F32 = jnp.float32
BF16 = jnp.bfloat16

D_MODEL = 1024
NSA_HEADS = 8
NSA_GROUPS = 2
HPG = NSA_HEADS // NSA_GROUPS
DH = 64
CMP_BLOCK = 32
CMP_STRIDE = 16
CMP_HIDDEN = 256
SLC_BLOCK = 64
SLC_COUNT = 16
WINDOW = 512
QB = 256
ROPE_THETA = 500000.0
ROPE_DIM = DH // 4
RET_HEADS = 4
RET_D = 128
RET_C = 128
RET_BASE = 10000.0
MLP_HIDDEN = 4 * D_MODEL
EPS = 1e-6
NEG = -1e30

LANES = 128
TK = 1024
TK_OLD = 2 * TK
BLK_PER_TILE = TK_OLD // SLC_BLOCK
IMP_ROWS = 64
KILL_LANE = DH + BLK_PER_TILE
REF_LANE = KILL_LANE + 1
REF_SLACK = 64.0
LOG2E = math.log2(math.e)
VROWS = 80
BF16_ROWS = 16
GATE_ROWS = 16
SLC_SHIFT = SLC_BLOCK.bit_length() - 1
WIN_KEYS = WINDOW + QB
VMEM_LIMIT = 56 * 1024 * 1024

NT_DIMS = (((1,), (1,)), ((), ()))
TN_DIMS = (((0,), (0,)), ((), ()))


def _sigmoid(v):
    return 1.0 / (1.0 + jnp.exp(-v))


def _rms(x, g):
    return x * lax.rsqrt(jnp.mean(x * x, axis=-1, keepdims=True) + EPS) * g


def _params(sem, flags=None):
    return pltpu.CompilerParams(dimension_semantics=sem, vmem_limit_bytes=VMEM_LIMIT, flags=flags)


_C_ROT = (0, 896)
_C_V = (896, 1280)
_C_G = (1280, 1408)
_C_RQ = (1408, 1920)
_C_RK = (1920, 2432)
_C_RV = (2432, 2944)
_C_RG = (2944, 3456)
PROJ_W = 3456


def _proj_kernel(x_ref, g_ref, win_ref, pos_ref, freq_ref,
                 q_ref, xh_ref, ks_ref, kw_ref, vs_ref, vw_ref, gt_ref, rq_ref, rk_ref, rv_ref, rg_ref,
                 kv_ref, w_ref, *, tm):
    @pl.when(pl.program_id(0) == 0)
    def _():
        for src, dst, n in _W_MOVES:
            w_ref[dst:dst + n, :] = win_ref[src:src + n, :].astype(BF16)
        n_gate = 3 * HPG
        g0 = _IN_OFFSETS[7]
        zeros = lambda n: jnp.zeros((n, D_MODEL), F32)
        gates = jnp.concatenate([win_ref[g0:g0 + n_gate, :], zeros(GATE_ROWS - n_gate),
                                 win_ref[g0 + n_gate:g0 + 2 * n_gate, :], zeros(LANES - GATE_ROWS - n_gate)],
                                axis=0)
        w_ref[_C_G[0]:_C_G[1], :] = gates.astype(BF16)

    h = _rms(x_ref[...], g_ref[...]).astype(BF16)
    lane = lax.broadcasted_iota(jnp.int32, (tm, LANES), 1)
    lo = lane < DH

    ang = pos_ref[...] * freq_ref[...]
    c, s = jnp.cos(ang), jnp.sin(ang)
    half_n = ROPE_DIM // 2
    cr = jnp.where(lo, c, pltpu.roll(c, DH, 1))
    sr = jnp.where(lo, -s, pltpu.roll(s, DH, 1))
    dim = lane & (DH - 1)
    first, second = dim < half_n, (dim >= half_n) & (dim < ROPE_DIM)
    c1 = jnp.where(lo, pltpu.roll(c, DH, 1), c)
    s1 = jnp.where(lo, pltpu.roll(s, DH, 1), s)
    c2 = jnp.where(lo, pltpu.roll(c, DH + half_n, 1), pltpu.roll(c, half_n, 1))
    s2 = jnp.where(lo, pltpu.roll(s, DH + half_n, 1), pltpu.roll(s, half_n, 1))
    cn = jnp.where(first, c1, jnp.where(second, c2, 1.0))
    sa = jnp.where(first, -s1, 0.0)
    sb = jnp.where(second, s2, 0.0)

    def dot(c):
        return lax.dot_general(h, w_ref[c[0]:c[1], :], NT_DIMS, preferred_element_type=F32)

    def tile(y, i):
        return y[:, i * LANES:(i + 1) * LANES]

    def rope_n(t):
        return t * cn + pltpu.roll(t, LANES - half_n, 1) * sa + pltpu.roll(t, half_n, 1) * sb

    def rope_r(t):
        return t * cr + pltpu.roll(t, RET_D // 2, 1) * sr

    def split(t):
        return jnp.where(lo, t, 0.0), jnp.where(lo, pltpu.roll(t, DH, 1), 0.0)

    def v_rows(dst_ref, t):
        tt = t.T
        tail = jnp.where(lax.broadcasted_iota(jnp.int32, (VROWS - DH, tm), 0) == 0, 1.0, 0.0).astype(BF16)
        for g in range(NSA_GROUPS):
            dst_ref[g, 0:DH, :] = tt[g * DH:(g + 1) * DH, :].astype(BF16)
            dst_ref[g, DH:VROWS, :] = tail

    yr = dot(_C_ROT)
    for i in range(4):
        tt = (rope_n(tile(yr, i)) * (DH ** -0.5 * LOG2E)).T
        q_ref[2 * i] = tt[0:DH, :].astype(BF16)
        q_ref[2 * i + 1] = tt[DH:2 * DH, :].astype(BF16)
    row = pl.program_id(0) * tm + lax.broadcasted_iota(jnp.int32, (tm, LANES), 0)
    blk = (row >> SLC_SHIFT) & (BLK_PER_TILE - 1)
    onehot = jnp.where(((lane - DH) == blk) | (lane == REF_LANE), 1.0, 0.0)
    a, b = split(rope_n(tile(yr, 5)))
    ks_ref[0] = jnp.where(lo, a, onehot).astype(BF16)
    ks_ref[1] = jnp.where(lo, b, onehot).astype(BF16)
    a, b = split(rope_n(tile(yr, 6)))
    kw_ref[0] = a.astype(BF16)
    kw_ref[1] = b.astype(BF16)

    yv = dot(_C_V)
    v_rows(vs_ref, tile(yv, 1))
    v_rows(vw_ref, tile(yv, 2))
    gt = _sigmoid(dot(_C_G)).T
    for g in range(NSA_GROUPS):
        gt_ref[g] = gt[g * GATE_ROWS:(g + 1) * GATE_ROWS, :]

    kv_ref[0] = rope_n(tile(yr, 4))
    kv_ref[1] = tile(yv, 0)
    lo_r = lax.broadcasted_iota(jnp.int32, (tm // CMP_STRIDE, LANES), 1) < DH
    for typ in range(2):
        for u in range(CMP_STRIDE // 2):
            pa = kv_ref[typ, pl.ds(2 * u, tm // CMP_STRIDE, stride=CMP_STRIDE), :]
            pb = kv_ref[typ, pl.ds(2 * u + 1, tm // CMP_STRIDE, stride=CMP_STRIDE), :]
            cols = slice(u * LANES, (u + 1) * LANES)
            xh_ref[2 * typ, :, cols] = jnp.where(lo_r, pa, pltpu.roll(pb, DH, 1)).astype(BF16)
            xh_ref[2 * typ + 1, :, cols] = jnp.where(lo_r, pltpu.roll(pa, DH, 1), pb).astype(BF16)

    yq = dot(_C_RQ)
    yk = dot(_C_RK)
    for i in range(RET_HEADS):
        sl = slice(i * LANES, (i + 1) * LANES)
        rq_ref[:, sl] = rope_r(tile(yq, i)).astype(BF16)
        rk_ref[:, sl] = (rope_r(tile(yk, i)) * (RET_D ** -0.5)).astype(BF16)
    rv_ref[...] = dot(_C_RV).astype(BF16)
    rg_ref[...] = dot(_C_RG).astype(BF16)


def _proj(x, g, w, pos, freq, tm=512):
    S = x.shape[0]
    row = lambda i: (i, 0)
    const = lambda i: (0, 0)
    mid = lambda i: (0, i, 0)
    last = lambda i: (0, 0, i)
    W = RET_HEADS * RET_D
    out_shape = (
        jax.ShapeDtypeStruct((NSA_HEADS, DH, S), BF16),
        jax.ShapeDtypeStruct((2 * NSA_GROUPS, S // CMP_STRIDE, CMP_STRIDE * DH), BF16),
        jax.ShapeDtypeStruct((NSA_GROUPS, S, LANES), BF16),
        jax.ShapeDtypeStruct((NSA_GROUPS, S, LANES), BF16),
        jax.ShapeDtypeStruct((NSA_GROUPS, VROWS, S), BF16),
        jax.ShapeDtypeStruct((NSA_GROUPS, VROWS, S), BF16),
        jax.ShapeDtypeStruct((NSA_GROUPS, GATE_ROWS, S), F32),
        jax.ShapeDtypeStruct((S, W), BF16),
        jax.ShapeDtypeStruct((S, W), BF16),
        jax.ShapeDtypeStruct((S, W), BF16),
        jax.ShapeDtypeStruct((S, W), BF16),
    )
    out_specs = (
        pl.BlockSpec((NSA_HEADS, DH, tm), last),
        pl.BlockSpec((2 * NSA_GROUPS, tm // CMP_STRIDE, CMP_STRIDE * DH), mid),
        pl.BlockSpec((NSA_GROUPS, tm, LANES), mid),
        pl.BlockSpec((NSA_GROUPS, tm, LANES), mid),
        pl.BlockSpec((NSA_GROUPS, VROWS, tm), last),
        pl.BlockSpec((NSA_GROUPS, VROWS, tm), last),
        pl.BlockSpec((NSA_GROUPS, GATE_ROWS, tm), last),
        pl.BlockSpec((tm, W), row),
        pl.BlockSpec((tm, W), row),
        pl.BlockSpec((tm, W), row),
        pl.BlockSpec((tm, W), row),
    )
    return pl.pallas_call(
        functools.partial(_proj_kernel, tm=tm),
        grid=(S // tm,),
        in_specs=[pl.BlockSpec((tm, D_MODEL), row), pl.BlockSpec((1, D_MODEL), const),
                  pl.BlockSpec((None,) + w.shape[1:], lambda i: (0, 0, 0), pipeline_mode=pl.Buffered(1)),
                  pl.BlockSpec((tm, 1), row),
                  pl.BlockSpec((1, LANES), const)],
        out_specs=out_specs, out_shape=out_shape,
        scratch_shapes=[pltpu.VMEM((2, tm, LANES), F32),
                        pltpu.VMEM((PROJ_W, D_MODEL), BF16)],
        compiler_params=_params(("arbitrary",)), name="proj",
    )(x, g, w, pos, freq)


def _compress_kernel(x_ref, w1_ref, pe_ref, w2_ref, o_ref):
    x = x_ref[...]
    half = CMP_STRIDE * DH
    a = jnp.dot(x, w1_ref[0:half, :], preferred_element_type=F32)
    b = jnp.dot(x, w1_ref[half:2 * half, :], preferred_element_type=F32)
    peb = jnp.dot(pe_ref[...], w1_ref[...], preferred_element_type=F32)[0:1, :]
    hid = a + pltpu.roll(b, x.shape[0] - 1, 0) + peb
    hid = hid * _sigmoid(hid)
    o_ref[...] = jnp.dot(hid.astype(BF16), w2_ref[...], preferred_element_type=F32)


def _compress(xh, w1, pe, w2):
    n_half = xh.shape[1]
    return pl.pallas_call(
        _compress_kernel,
        grid=(2 * NSA_GROUPS,),
        in_specs=[pl.BlockSpec((None, n_half, CMP_STRIDE * DH), lambda i: (i, 0, 0)),
                  pl.BlockSpec((None, CMP_BLOCK * DH, CMP_HIDDEN), lambda i: (i // NSA_GROUPS, 0, 0)),
                  pl.BlockSpec((None, BF16_ROWS, CMP_BLOCK * DH), lambda i: (i // NSA_GROUPS, 0, 0)),
                  pl.BlockSpec((None, CMP_HIDDEN, LANES), lambda i: (i // NSA_GROUPS, 0, 0))],
        out_specs=pl.BlockSpec((None, n_half, LANES), lambda i: (i, 0, 0)),
        out_shape=jax.ShapeDtypeStruct((2 * NSA_GROUPS, n_half, LANES), F32),
        compiler_params=_params(("arbitrary",)), name="compress",
    )(xh, w1, pe, w2)


def _nsa_kernel(q_ref, qn_ref, kcmp_ref, vcmp_ref, ks_ref, vs_ref, kw_ref, vw_ref, gt_ref, ovl_ref,
                out_ref, qa_ref, qs_ref, bias_ref, biasn_ref, ocn_ref, m_ref, acc_ref,
                sa_ref, p_ref, c_ref, rise_ref, owin_ref, *, S):
    NC = S // CMP_STRIDE
    NB = S // SLC_BLOCK
    NQ = HPG * QB
    b = pl.program_id(1)
    s0 = b * QB
    lane_q = lax.broadcasted_iota(jnp.int32, (1, NQ), 1) & (QB - 1)
    band_row = lax.broadcasted_iota(jnp.int32, (QB, NQ), 0)
    band_tq = lax.broadcasted_iota(jnp.int32, (QB, NQ), 1) & (QB - 1)

    def sel_scores(qsrc_ref, cr):
        qs_ref[0:DH, :] = jnp.concatenate([qsrc_ref[h] for h in range(HPG)], axis=1)
        qs_ref[DH:LANES, :] = jnp.zeros((LANES - DH, NQ), BF16)
        return jnp.dot(kcmp_ref[0:cr, :], qs_ref[...], preferred_element_type=F32)

    def sel_probs(sc, blk):
        c_end = lax.broadcasted_iota(jnp.int32, sc.shape, 0) * CMP_STRIDE + (CMP_BLOCK - 1)
        sc = jnp.where(c_end <= blk * QB + lane_q, sc, NEG)
        return jnp.exp2(sc - jnp.max(sc, axis=0, keepdims=True)).astype(BF16)

    def sel_matmuls(pc):
        cr = pc.shape[0]
        ocl = jnp.dot(vcmp_ref[:, 0:cr], pc, preferred_element_type=F32)
        ir, ic = IMP_ROWS, 4 * IMP_ROWS
        if cr % ic:
            return ocl, jnp.dot(ovl_ref[0:cr // 4, 0:cr], pc, preferred_element_type=F32)
        parts = []
        for i in range(cr // ic):
            c0 = max(i - 1, 0) * ic
            parts.append(jnp.dot(ovl_ref[i * ir:(i + 1) * ir, c0:(i + 1) * ic], pc[c0:(i + 1) * ic, :],
                                 preferred_element_type=F32))
        return ocl, jnp.concatenate(parts, axis=0)

    def sel_choose(ocl, imp4, blk):
        t_row = blk * QB + lane_q
        any_valid = jnp.where(t_row >= CMP_BLOCK - 1, 1.0, 0.0)
        rc = any_valid / ocl[DH:DH + 1, :]
        ocn_ref[...] = ocl[0:DH, :] * rc

        imp4 = imp4 * rc
        imp = imp4[:, 0:QB] + imp4[:, QB:2 * QB] + imp4[:, 2 * QB:3 * QB] + imp4[:, 3 * QB:4 * QB]
        nb = imp.shape[0]

        jidx = lax.broadcasted_iota(jnp.int32, (nb, QB), 0)
        cur = (blk * QB + lax.broadcasted_iota(jnp.int32, (1, QB), 1)) >> SLC_SHIFT
        forced = (jidx == 0) | (jidx == cur) | (jidx == cur - 1)
        future = jidx > cur
        cand = jnp.where(forced | future, -jnp.inf, imp)

        def select(break_ties):
            v = cand
            for _ in range(SLC_COUNT - 3):
                mx = jnp.max(v, axis=0, keepdims=True)
                hit = v == mx
                if break_ties:
                    hit = jidx == jnp.min(jnp.where(hit, jidx, nb), axis=0, keepdims=True)
                v = jnp.where(hit, -jnp.inf, v)
            return jnp.where((v == -jnp.inf) & (jidx <= cur), 0.0, NEG)

        bias_t = select(False)
        n_sel = jnp.sum(jnp.where(bias_t == 0.0, 1.0, 0.0), axis=0, keepdims=True)
        biasn_ref[0:nb, :] = bias_t.astype(BF16)
        if nb < NB:
            biasn_ref[nb:NB, :] = jnp.full((NB - nb, QB), NEG, BF16)

        def redo_if_tied():
            @pl.when(jnp.max(n_sel) > SLC_COUNT)
            def _():
                biasn_ref[0:nb, :] = select(True).astype(BF16)
        return redo_if_tied

    cr_need = jnp.minimum((b + 2) * (QB // CMP_STRIDE) - 1, NC)
    cr_parts = next(n for n in (4, 2, 1) if (NC // n) % (4 * IMP_ROWS) == 0 or n == 1)
    cr_sizes = tuple(NC * (i + 1) // cr_parts for i in range(cr_parts))

    @pl.when(b == 0)
    def _():
        ocl, imp4 = sel_matmuls(sel_probs(sel_scores(q_ref, cr_sizes[0]), 0))
        sel_choose(ocl, imp4, 0)()

    bias_ref[...] = biasn_ref[...]
    oc = ocn_ref[...]

    qa_ref[0:DH, :] = jnp.concatenate([q_ref[h] for h in range(HPG)], axis=1)
    qa_ref[DH:KILL_LANE, :] = jnp.zeros((KILL_LANE - DH, NQ), BF16)

    def set_bias_rows(grp):
        bias_rows = bias_ref[pl.ds(pl.multiple_of(grp * BLK_PER_TILE, BLK_PER_TILE), BLK_PER_TILE), :]
        qa_ref[DH:DH + BLK_PER_TILE, :] = jnp.concatenate([bias_rows] * HPG, axis=1)

    def set_flag_rows(ref):
        r = lax.broadcasted_iota(jnp.int32, (LANES - KILL_LANE, NQ), 0)
        rows = jnp.where(r == 0, NEG, 0.0) if ref is None else jnp.where(r == 0, NEG, jnp.where(r == 1, -ref, 0.0))
        qa_ref[KILL_LANE:LANES, :] = rows.astype(BF16)

    def scores(kt, s_ref):
        k0 = pl.multiple_of(kt * TK, TK)
        set_bias_rows(kt // 2)
        s_ref[...] = jnp.dot(ks_ref[pl.ds(k0, TK), :], qa_ref[...], preferred_element_type=F32)

    def accumulate(kt, s_ref):
        k0 = pl.multiple_of(kt * TK, TK)
        m_old = m_ref[...]
        m_new = jnp.maximum(m_old, jnp.max(s_ref[...], axis=0, keepdims=True))
        p_ref[0:TK, :] = jnp.exp2(s_ref[...] - m_new).astype(BF16)
        acc_ref[...] = (jnp.exp2(m_old - m_new) * acc_ref[...]
                        + jnp.dot(vs_ref[:, pl.ds(k0, TK)], p_ref[0:TK, :], preferred_element_type=F32))
        m_ref[...] = m_new

    def mask_own_keys(s_ref):
        band = pl.ds(pl.multiple_of(s0 - n_full * TK, QB), QB)
        s_ref[band, :] = jnp.where(band_row <= band_tq, s_ref[band, :], NEG)

    n_full = s0 // TK

    m_ref[...] = jnp.full((1, NQ), NEG, F32)
    acc_ref[...] = jnp.zeros((VROWS, NQ), F32)
    set_flag_rows(None)

    def first_region(cr):
        sc = sel_scores(qn_ref, cr)
        sw = jnp.dot(kw_ref[pl.ds(pl.multiple_of(s0, QB), WIN_KEYS), :], qa_ref[...],
                     preferred_element_type=F32)
        scores(n_full, sa_ref)

        pc = sel_probs(sc, b + 1)
        ocl, imp4 = sel_matmuls(pc)
        redo_if_tied = sel_choose(ocl, imp4, b + 1)

        sw = jnp.concatenate([jnp.where(band_row > band_tq, sw[0:QB], NEG),
                              sw[QB:WINDOW],
                              jnp.where(band_row <= band_tq, sw[WINDOW:WIN_KEYS], NEG)], axis=0)
        pw = jnp.exp2(sw - jnp.max(sw, axis=0, keepdims=True)).astype(BF16)
        mask_own_keys(sa_ref)

        ow = jnp.dot(vw_ref[:, pl.ds(pl.multiple_of(s0, QB), WIN_KEYS)], pw, preferred_element_type=F32)
        owin_ref[...] = ow[0:DH, :] * (1.0 / ow[DH:DH + 1, :])
        accumulate(n_full, sa_ref)

        m = m_ref[...]
        c0 = m.astype(BF16).astype(F32)
        c_ref[...] = c0
        acc_ref[...] = acc_ref[...] * jnp.exp2(m - c0)
        rise_ref[...] = jnp.zeros((1, NQ), F32)
        redo_if_tied()

    for i, cr in enumerate(cr_sizes):
        fits = cr_need <= cr
        if i > 0:
            fits = fits & (cr_need > cr_sizes[i - 1])
        pl.when(fits)(functools.partial(first_region, cr))

    def older_keys(grp, k0, rows):
        c = c_ref[...]
        set_bias_rows(grp)
        set_flag_rows(c)
        s = jnp.dot(ks_ref[pl.ds(k0, rows), :], qa_ref[...], preferred_element_type=F32)
        mt = jnp.max(s, axis=0, keepdims=True)
        p_ref[0:rows, :] = jnp.exp2(s).astype(BF16)
        pv = jnp.dot(vs_ref[:, pl.ds(k0, rows)], p_ref[0:rows, :], preferred_element_type=F32)
        c_new = (c + jnp.maximum(mt, 0.0)).astype(BF16).astype(F32)
        acc_ref[...] = (acc_ref[...] + pv) * jnp.exp2(c - c_new)
        c_ref[...] = c_new
        rise_ref[...] = jnp.maximum(rise_ref[...], mt)

    def older_pair(jj, carry):
        older_keys(jj, pl.multiple_of(jj * TK_OLD, TK_OLD), TK_OLD)
        return carry

    lax.fori_loop(0, n_full // 2, older_pair, 0)

    @pl.when(n_full % 2 == 1)
    def _():
        older_keys(n_full // 2, pl.multiple_of((n_full - 1) * TK, TK), TK)

    @pl.when(jnp.max(rise_ref[...]) > REF_SLACK)
    def _():
        m_ref[...] = jnp.full((1, NQ), NEG, F32)
        acc_ref[...] = jnp.zeros((VROWS, NQ), F32)
        set_flag_rows(None)

        def two_pass_tile(kt, carry):
            scores(kt, sa_ref)

            @pl.when(kt == n_full)
            def _():
                mask_own_keys(sa_ref)
            accumulate(kt, sa_ref)
            return carry

        lax.fori_loop(0, n_full + 1, two_pass_tile, 0)

    acc = acc_ref[...]
    osel = acc[0:DH, :] * (1.0 / acc[DH:DH + 1, :])
    owin = owin_ref[...]

    heads = []
    for h in range(HPG):
        sl = slice(h * QB, (h + 1) * QB)
        heads.append(gt_ref[3 * h:3 * h + 1, :] * oc[:, sl]
                     + gt_ref[3 * h + 1:3 * h + 2, :] * osel[:, sl]
                     + gt_ref[3 * h + 2:3 * h + 3, :] * owin[:, sl])
    out_ref[...] = jnp.concatenate(heads, axis=0).T.astype(out_ref.dtype)


def _nsa(q_t, kcmp, vcmp_t, ks, vs_t, kw, vw_t, gt, ovl_t):
    S = ks.shape[1]
    NC, NB = S // CMP_STRIDE, S // SLC_BLOCK
    n_qb = S // QB
    grp = lambda g, b: (g, 0, 0)
    return pl.pallas_call(
        functools.partial(_nsa_kernel, S=S),
        grid=(NSA_GROUPS, n_qb),
        in_specs=[pl.BlockSpec((HPG, DH, QB), lambda g, b: (g, 0, b)),
                  pl.BlockSpec((HPG, DH, QB), lambda g, b: (g, 0, jnp.minimum(b + 1, n_qb - 1))),
                  pl.BlockSpec((None, NC, LANES), grp),
                  pl.BlockSpec((None, VROWS, NC), grp),
                  pl.BlockSpec((None, S, LANES), grp, pipeline_mode=pl.Buffered(1)),
                  pl.BlockSpec((None, VROWS, S), grp, pipeline_mode=pl.Buffered(1)),
                  pl.BlockSpec((None, WINDOW + S, LANES), grp, pipeline_mode=pl.Buffered(1)),
                  pl.BlockSpec((None, VROWS, WINDOW + S), grp, pipeline_mode=pl.Buffered(1)),
                  pl.BlockSpec((None, GATE_ROWS, QB), lambda g, b: (g, 0, b)),
                  pl.BlockSpec((NB, NC), lambda g, b: (0, 0))],
        out_specs=pl.BlockSpec((QB, HPG * DH), lambda g, b: (b, g)),
        out_shape=jax.ShapeDtypeStruct((S, NSA_HEADS * DH), BF16),
        scratch_shapes=[pltpu.VMEM((LANES, HPG * QB), BF16),
                        pltpu.VMEM((LANES, HPG * QB), BF16),
                        pltpu.VMEM((NB, QB), BF16),
                        pltpu.VMEM((NB, QB), BF16),
                        pltpu.VMEM((DH, HPG * QB), F32),
                        pltpu.VMEM((1, HPG * QB), F32),
                        pltpu.VMEM((VROWS, HPG * QB), F32),
                        pltpu.VMEM((TK, HPG * QB), F32),
                        pltpu.VMEM((TK_OLD, HPG * QB), BF16),
                        pltpu.VMEM((1, HPG * QB), F32),
                        pltpu.VMEM((1, HPG * QB), F32),
                        pltpu.VMEM((DH, HPG * QB), F32)],
        compiler_params=_params(("arbitrary", "arbitrary")), name="nsa",
    )(q_t, q_t, kcmp, vcmp_t, ks, vs_t, kw, vw_t, gt, ovl_t)


def _ret_kernel(q_ref, k_ref, v_ref, g_ref, dmat_ref, xi_ref, zeta_ref, dec_ref, o_ref, state_ref, *, tm):
    @pl.when(pl.program_id(0) == 0)
    def _():
        state_ref[...] = jnp.zeros_like(state_ref)

    for j in range(tm // RET_C):
        rows = slice(j * RET_C, (j + 1) * RET_C)
        for h in range(RET_HEADS):
            cols = slice(h * RET_D, (h + 1) * RET_D)
            qc, kc, vc = q_ref[rows, cols], k_ref[rows, cols], v_ref[rows, cols]
            st = state_ref[h]
            st_hi = st.astype(BF16)
            st_lo = (st - st_hi.astype(F32)).astype(BF16)
            sc = lax.dot_general(qc, kc, NT_DIMS, preferred_element_type=F32) * dmat_ref[h]
            inner = jnp.dot(sc.astype(BF16), vc, preferred_element_type=F32)
            cross = (jnp.dot(qc, st_hi, preferred_element_type=F32)
                     + jnp.dot(qc, st_lo, preferred_element_type=F32)) * xi_ref[h]
            kz = (kc.astype(F32) * zeta_ref[h]).astype(BF16)
            state_ref[h] = dec_ref[h] * st + lax.dot_general(kz, vc, TN_DIMS, preferred_element_type=F32)
            o = inner + cross
            o = o * lax.rsqrt(jnp.mean(o * o, axis=-1, keepdims=True) + EPS)
            gate = g_ref[rows, cols].astype(F32)
            o_ref[rows, cols] = (gate * _sigmoid(gate) * o).astype(o_ref.dtype)


def _retention(rq, rk, rv, rg, dmat, xi, zeta, dec, tm=512):
    S = rq.shape[0]
    W = RET_HEADS * RET_D
    row = lambda i: (i, 0)
    cst = lambda i: (0, 0, 0)
    blk = pl.BlockSpec((tm, W), row)
    tbl = pl.BlockSpec((RET_HEADS, RET_C, RET_D), cst)
    return pl.pallas_call(
        functools.partial(_ret_kernel, tm=tm),
        grid=(S // tm,),
        in_specs=[blk, blk, blk, blk, tbl, tbl, tbl, tbl],
        out_specs=blk,
        out_shape=jax.ShapeDtypeStruct((S, W), BF16),
        scratch_shapes=[pltpu.VMEM((RET_HEADS, RET_D, RET_D), F32)],
        compiler_params=_params(("arbitrary",)), name="retention",
    )(rq, rk, rv, rg, dmat, xi, zeta, dec)


def _mix_kernel(x_ref, g_ref, a_ref, r_ref, wg_ref, wpa_ref, wpb_ref, wo_ref, o_ref):
    x = x_ref[...]
    h = _rms(x, g_ref[...]).astype(BF16)
    ga = _sigmoid(lax.dot_general(h, wg_ref[0:D_MODEL, :], NT_DIMS, preferred_element_type=F32))
    gb = _sigmoid(lax.dot_general(h, wg_ref[D_MODEL:2 * D_MODEL, :], NT_DIMS, preferred_element_type=F32))
    mix = (ga * jnp.dot(a_ref[...], wpa_ref[...], preferred_element_type=F32)
           + gb * jnp.dot(r_ref[...], wpb_ref[...], preferred_element_type=F32))
    o_ref[...] = x + jnp.dot(mix.astype(BF16), wo_ref[...], preferred_element_type=F32)


def _mlp_kernel(x_ref, g_ref, wu_ref, wd_ref, gf_ref, o_ref, *, hc):
    x = x_ref[...]
    h = _rms(x, g_ref[...]).astype(BF16)
    acc = x
    for c in range(MLP_HIDDEN // hc):
        u = jnp.maximum(jnp.dot(h, wu_ref[:, c * hc:(c + 1) * hc], preferred_element_type=F32), 0.0)
        acc = acc + jnp.dot((u * u).astype(BF16), wd_ref[c * hc:(c + 1) * hc, :], preferred_element_type=F32)
    o_ref[...] = _rms(acc, gf_ref[...])


def _mix_mlp_kernel(x_ref, g_ref, a_ref, r_ref, wg_ref, wpa_ref, wpb_ref, wo_ref, gm_ref, wu_ref, wd_ref, gf_ref,
                    o_ref, x1_ref, *, hc):
    _mix_kernel(x_ref, g_ref, a_ref, r_ref, wg_ref, wpa_ref, wpb_ref, wo_ref, x1_ref)
    _mlp_kernel(x1_ref, gm_ref, wu_ref, wd_ref, gf_ref, o_ref, hc=hc)


def _mix_mlp(x, g, nsa_out, ret_out, wg, wpa, wpb, wo, gm, wu, wd, gf, tm=512, hc=1024):
    S = x.shape[0]
    row = lambda i: (i, 0)
    cst = lambda i: (0, 0)
    held = lambda shape: pl.BlockSpec(shape, cst, pipeline_mode=pl.Buffered(1))
    return pl.pallas_call(
        functools.partial(_mix_mlp_kernel, hc=hc),
        grid=(S // tm,),
        in_specs=[pl.BlockSpec((tm, D_MODEL), row), pl.BlockSpec((1, D_MODEL), cst),
                  pl.BlockSpec((tm, NSA_HEADS * DH), row), pl.BlockSpec((tm, RET_HEADS * RET_D), row),
                  held((2 * D_MODEL, D_MODEL)), held((512, D_MODEL)), held((512, D_MODEL)),
                  held((D_MODEL, D_MODEL)), pl.BlockSpec((1, D_MODEL), cst),
                  held((D_MODEL, MLP_HIDDEN)), held((MLP_HIDDEN, D_MODEL)), pl.BlockSpec((1, D_MODEL), cst)],
        out_specs=pl.BlockSpec((tm, D_MODEL), row),
        out_shape=jax.ShapeDtypeStruct((S, D_MODEL), F32),
        scratch_shapes=[pltpu.VMEM((tm, D_MODEL), F32)],
        compiler_params=_params(("arbitrary",)), name="mixmlp",
    )(x, g, nsa_out, ret_out, wg, wpa, wpb, wo, gm, wu, wd, gf)


def _rope_freqs():
    half = ROPE_DIM // 2
    inv = jnp.exp(-math.log(ROPE_THETA) * jnp.arange(half, dtype=F32) * 2.0 / ROPE_DIM)
    hr = RET_D // 2
    inv_r = jnp.exp(-math.log(RET_BASE) * jnp.arange(hr, dtype=F32) * 2.0 / RET_D)
    return jnp.concatenate([inv_r, inv, jnp.zeros((LANES - hr - half,), F32)])[None, :]


def _decay_tables():
    H, C = RET_HEADS, RET_C
    log_g = jnp.log(1.0 - jnp.exp2(-5.0 - jnp.arange(H, dtype=F32)))
    n = jnp.arange(C, dtype=F32)
    rel = n[:, None] - n[None, :]
    dmat = jnp.where(rel >= 0, jnp.exp(log_g[:, None, None] * jnp.maximum(rel, 0.0)), 0.0)
    xi = jnp.exp(log_g[:, None] * (n + 1.0))
    zeta = jnp.exp(log_g[:, None] * (C - 1.0 - n))
    dec = jnp.exp(log_g * C)
    bc = lambda a: jnp.broadcast_to(a, (H, C, RET_D))
    return dmat, bc(xi[:, :, None]), bc(zeta[:, :, None]), bc(dec[:, None, None])


def _overlap_t(S):
    n_cmp = (S - CMP_BLOCK) // CMP_STRIDE + 1
    n_slc = S // SLC_BLOCK
    cs = np.arange(n_cmp)[:, None] * CMP_STRIDE
    ss = np.arange(n_slc)[None, :] * SLC_BLOCK
    ov = np.clip(np.minimum(cs + CMP_BLOCK, ss + SLC_BLOCK) - np.maximum(cs, ss), 0, None) / CMP_BLOCK
    out = np.zeros((n_slc, S // CMP_STRIDE), np.float32)
    out[:, :n_cmp] = ov.T
    return jnp.asarray(out, BF16)


def _v_rows(v_t):
    S = v_t.shape[1]
    return jnp.concatenate([v_t, jnp.ones((1, S), v_t.dtype), jnp.zeros((VROWS - DH - 1, S), v_t.dtype)], axis=0)


_IN_OFFSETS = np.cumsum((0, 512, 128, 128, 128, 128, 128, 128, 24, 512, 512, 512, 512, 1024, 1024))


_W_MOVES = tuple((int(_IN_OFFSETS[i]), dst, int(_IN_OFFSETS[i + 1] - _IN_OFFSETS[i])) for i, dst in
                 ((0, 0), (1, 512), (3, 640), (5, 768), (2, 896), (4, 1024), (6, 1152),
                  (8, 1408), (9, 1920), (10, 2432), (11, 2944)))


def _branches(x, positions, norm_mix, w_in, cmp_pos_k, cmp_pos_v, cmp_k_w1, cmp_k_w2, cmp_v_w1, cmp_v_w2):
    B, S, _ = x.shape
    assert B == 1 and norm_mix.shape[0] == 1 and S % TK == 0 and S >= max(WIN_KEYS, SLC_COUNT * SLC_BLOCK)
    xs = x[0]
    q_t, xh, ks, kw, vs_t, vw_t, gt, rq, rk, rv, rg = _proj(
        xs, norm_mix, jnp.swapaxes(w_in, 1, 2), positions[0].astype(F32)[:, None], _rope_freqs())

    w1 = jnp.stack([cmp_k_w1[0], cmp_v_w1[0]]).astype(BF16)
    pe = jnp.stack([cmp_pos_k[0], cmp_pos_v[0]]).reshape(2, 1, CMP_BLOCK * DH)
    pe = jnp.broadcast_to(pe, (2, BF16_ROWS, CMP_BLOCK * DH)).astype(BF16)
    w2 = jnp.pad(jnp.stack([cmp_k_w2[0], cmp_v_w2[0]]), ((0, 0), (0, 0), (0, LANES - DH))).astype(BF16)
    cmp = _compress(xh, w1, pe, w2)
    kcmp = cmp[0:NSA_GROUPS].astype(BF16)
    vcmp_t = jax.vmap(_v_rows)(cmp[NSA_GROUPS:, :, 0:DH].transpose(0, 2, 1).astype(BF16))

    vw_t = jnp.pad(vw_t, ((0, 0), (0, 0), (WINDOW, 0)))
    pad_keys = jnp.zeros((NSA_GROUPS, WINDOW, LANES), BF16).at[:, :, KILL_LANE].set(1.0)
    kw = jnp.concatenate([pad_keys, kw], axis=1)
    nsa_out = _nsa(q_t, kcmp, vcmp_t, ks, vs_t, kw, vw_t, gt, _overlap_t(S))

    ret_out = _retention(rq, rk, rv, rg, *_decay_tables())
    return nsa_out, ret_out


def kernel(x, positions, norm_mix, w_in, cmp_pos_k, cmp_pos_v, cmp_k_w1, cmp_k_w2, cmp_v_w1, cmp_v_w2,
           w_proj_a, w_proj_b, w_out, norm_mlp, w_up, w_down, norm_final):
    nsa_out, ret_out = _branches(x, positions, norm_mix, w_in, cmp_pos_k, cmp_pos_v,
                                 cmp_k_w1, cmp_k_w2, cmp_v_w1, cmp_v_w2)
    xs = x[0]
    w_gate = jnp.swapaxes(w_in, 1, 2)[0, _IN_OFFSETS[12]:_IN_OFFSETS[14], :].astype(BF16)
    y = _mix_mlp(xs, norm_mix, nsa_out, ret_out, w_gate, w_proj_a[0].astype(BF16), w_proj_b[0].astype(BF16),
                 w_out[0].astype(BF16), norm_mlp, w_up[0].astype(BF16), w_down[0].astype(BF16),
                 norm_final[None, :])
    return y[None]
```
